```python
import math
import jax, jax.numpy as jnp
from jax import lax
import numpy as np

D_MODEL = 1024
BATCH = 8
SEQ = 2048
DEPTH = 2
DEC_BATCH = 32
DEC_SEQ = 4
PAST_LEN = 8192
PAGE_SIZE = 128

HEAD_DIM = 64
SB_HEADS = 4
SB_WIDTH = SB_HEADS * HEAD_DIM
FOX_HEADS = 4
FOX_WIDTH = FOX_HEADS * HEAD_DIM
SSM_HEADS = 8
SSM_HEAD_DIM = 64
SSM_INNER = SSM_HEADS * SSM_HEAD_DIM
SSM_GROUPS = 2
SSM_STATE = 128
SSM_CONV = 4
SSM_CONV_DIM = SSM_INNER + 2 * SSM_GROUPS * SSM_STATE
SSM_CHUNK = 128
Q_BLOCK = 128
N_BRANCH = 3
FFN_HIDDEN = 4 * D_MODEL
RMS_EPS = 1e-6
NEG_INF = -1e30
DT_MIN = 0.001
DT_MAX = 0.1
SPLIT_SIZES = (SB_WIDTH, SB_WIDTH, SB_WIDTH, FOX_WIDTH, FOX_WIDTH, FOX_WIDTH, FOX_HEADS,
               SSM_INNER, SSM_CONV_DIM, SSM_HEADS, N_BRANCH * D_MODEL)
IN_WIDTH = sum(SPLIT_SIZES)

kernel_name = 'hybrid_stickbreak_ssd_fox_step'


def rms_norm(x, g):
    xf = x.astype(jnp.float32)
    y = xf * lax.rsqrt(jnp.mean(xf * xf, axis=-1, keepdims=True) + RMS_EPS)
    return (y * g.astype(jnp.float32)).astype(x.dtype)


def split_projection(proj):
    idx = np.cumsum(SPLIT_SIZES)[:-1].tolist()
    return jnp.split(proj, idx, axis=-1)


def to_heads(a, n):
    return a.reshape(a.shape[0], a.shape[1], n, HEAD_DIM)


def stick_breaking_core(q, k, v, q_pos, k_pos):
    z = jnp.einsum('bqhd,bkhd->bhqk', q, k).astype(jnp.float32) * (HEAD_DIM ** -0.5)
    mask = k_pos[None, :] < q_pos[:, None]
    log_beta = jax.nn.log_sigmoid(z)
    log_keep = jnp.where(mask, jax.nn.log_sigmoid(-z), 0.0)
    later = lax.cumsum(log_keep, axis=3, reverse=True) - log_keep
    w = jnp.where(mask, jnp.exp(log_beta + later), 0.0)
    return jnp.einsum('bhqk,bkhd->bqhd', w.astype(v.dtype), v)


def forgetting_core(q, k, v, fq, fk, q_pos, k_pos):
    s = jnp.einsum('bqhd,bkhd->bhqk', q, k).astype(jnp.float32) * (HEAD_DIM ** -0.5)
    s = s + jnp.moveaxis(fq, 1, 2)[:, :, :, None] - jnp.moveaxis(fk, 1, 2)[:, :, None, :]
    mask = k_pos[None, :] <= q_pos[:, None]
    p = jax.nn.softmax(jnp.where(mask, s, NEG_INF), axis=-1)
    return jnp.einsum('bhqk,bkhd->bqhd', p.astype(v.dtype), v)


def sweep_query_blocks(block_fn, q_pos, *q_arrays):
    n_blocks = q_pos.shape[0] // Q_BLOCK

    def to_blocks(a):
        return jnp.moveaxis(a.reshape(a.shape[0], n_blocks, Q_BLOCK, *a.shape[2:]), 1, 0)

    args = (q_pos.reshape(n_blocks, Q_BLOCK),) + tuple(to_blocks(a) for a in q_arrays)
    out = lax.map(lambda xs: block_fn(*xs), args)
    out = jnp.moveaxis(out, 0, 1)
    return out.reshape(out.shape[0], -1, *out.shape[3:])


def causal_conv(xbc, conv_state, w, bias):
    t = xbc.shape[1]
    xp = jnp.concatenate([conv_state.astype(xbc.dtype), xbc], axis=1)
    out = bias + xp[:, 0:t] * w[0]
    for i in range(1, SSM_CONV):
        out = out + xp[:, i:i + t] * w[i]
    return jax.nn.silu(out), xp[:, t:]


def ssd_scan(x, dt, a, b_in, c_in, h0, chunk):
    bsz, t, nh, hp = x.shape
    rep = nh // b_in.shape[2]
    bh = jnp.repeat(b_in, rep, axis=2)
    ch = jnp.repeat(c_in, rep, axis=2)
    nc = t // chunk
    x = x.reshape(bsz, nc, chunk, nh, hp)
    dt = dt.reshape(bsz, nc, chunk, nh)
    bh = bh.reshape(bsz, nc, chunk, nh, -1)
    ch = ch.reshape(bsz, nc, chunk, nh, -1)
    a_cs = jnp.cumsum(dt * a, axis=2)
    seg = a_cs[:, :, :, None, :] - a_cs[:, :, None, :, :]
    tri = jnp.tril(jnp.ones((chunk, chunk), dtype=bool))[None, None, :, :, None]
    decay = jnp.exp(jnp.where(tri, seg, -jnp.inf))
    scores = jnp.einsum('bclhn,bcshn->bclsh', ch, bh)
    xdt = x * dt[..., None]
    y_diag = jnp.einsum('bclsh,bcshp->bclhp', scores * decay, xdt)
    to_end = jnp.exp(a_cs[:, :, -1:, :] - a_cs)
    states = jnp.einsum('bclhn,bclh,bclhp->bchpn', bh, to_end * dt, x)
    chunk_decay = jnp.exp(a_cs[:, :, -1, :])

    def step(h, inp):
        st, dec = inp
        return h * dec[..., None, None] + st, h

    h_final, h_starts = lax.scan(step, h0, (jnp.moveaxis(states, 1, 0), jnp.moveaxis(chunk_decay, 1, 0)))
    h_starts = jnp.moveaxis(h_starts, 0, 1)
    y_off = jnp.einsum('bclhn,bchpn,bclh->bclhp', ch, h_starts, jnp.exp(a_cs))
    return (y_diag + y_off).reshape(bsz, t, nh, hp), h_final


def ssm_branch(z, xbc, dt_raw, conv_state, ssm_state, chunk, lp):
    bsz, t, _ = z.shape
    f32 = jnp.float32
    xbc_act, conv_new = causal_conv(xbc, conv_state, lp['conv_w'], lp['conv_b'])
    xs, b_in, c_in = jnp.split(xbc_act, [SSM_INNER, SSM_INNER + SSM_GROUPS * SSM_STATE], axis=-1)
    xs = xs.reshape(bsz, t, SSM_HEADS, SSM_HEAD_DIM).astype(f32)
    b_in = b_in.reshape(bsz, t, SSM_GROUPS, SSM_STATE).astype(f32)
    c_in = c_in.reshape(bsz, t, SSM_GROUPS, SSM_STATE).astype(f32)
    dt = jax.nn.softplus(dt_raw.astype(f32) + lp['dt_bias'].astype(f32))
    a = -jnp.exp(lp['a_log'].astype(f32))
    y, ssm_new = ssd_scan(xs, dt, a, b_in, c_in, ssm_state.astype(f32), chunk)
    y = y + lp['d_skip'].astype(f32)[:, None] * xs
    y = y.reshape(bsz, t, SSM_INNER) * jax.nn.silu(z.astype(f32))
    yg = y.reshape(bsz, t, SSM_GROUPS, -1)
    yg = yg * lax.rsqrt(jnp.mean(yg * yg, axis=-1, keepdims=True) + RMS_EPS)
    y = yg.reshape(bsz, t, SSM_INNER) * lp['ssm_norm_g'].astype(f32)
    return y.astype(z.dtype), conv_new, ssm_new


def merge_and_ffn(x, y_sb, y_ssm, y_fox, gate_logits, lp):
    bsz, t, _ = x.shape
    g = jax.nn.sigmoid(gate_logits.astype(jnp.float32)).reshape(bsz, t, N_BRANCH, D_MODEL)
    br_sb = jnp.einsum('bte,ed->btd', y_sb.reshape(bsz, t, SB_WIDTH), lp['w_sb_out'])
    br_ssm = jnp.einsum('bte,ed->btd', y_ssm, lp['w_ssm_out'])
    br_fox = jnp.einsum('bte,ed->btd', y_fox.reshape(bsz, t, FOX_WIDTH), lp['w_fox_out'])
    mixed = g[:, :, 0] * br_sb + g[:, :, 1] * br_ssm + g[:, :, 2] * br_fox
    x = x + jnp.einsum('btd,de->bte', mixed.astype(x.dtype), lp['w_o'])
    h = rms_norm(x, lp['norm2_g'])
    u = jax.nn.relu(jnp.einsum('btd,df->btf', h, lp['w_up']))
    return x + jnp.einsum('btf,fd->btd', u * u, lp['w_down'])


def mixer_inputs(x, lp):
    h = rms_norm(x, lp['norm1_g'])
    return split_projection(jnp.einsum('btd,de->bte', h, lp['w_in']))


def prompt_layer(x, lp):
    bsz, t, _ = x.shape
    q_sb, k_sb, v_sb, q_fx, k_fx, v_fx, f_fx, z, xbc, dt_raw, gates = mixer_inputs(x, lp)
    q_sb, k_sb, v_sb = to_heads(q_sb, SB_HEADS), to_heads(k_sb, SB_HEADS), to_heads(v_sb, SB_HEADS)
    q_fx, k_fx, v_fx = to_heads(q_fx, FOX_HEADS), to_heads(k_fx, FOX_HEADS), to_heads(v_fx, FOX_HEADS)
    pos = jnp.arange(t)
    y_sb = sweep_query_blocks(lambda qp, qb: stick_breaking_core(qb, k_sb, v_sb, qp, pos), pos, q_sb)
    logf = jax.nn.log_sigmoid(f_fx.astype(jnp.float32) + lp['b_forget'].astype(jnp.float32))
    cum = jnp.cumsum(logf, axis=1)
    y_fx = sweep_query_blocks(lambda qp, qb, fb: forgetting_core(qb, k_fx, v_fx, fb, cum, qp, pos), pos, q_fx, cum)
    conv0 = jnp.zeros((bsz, SSM_CONV - 1, SSM_CONV_DIM), x.dtype)
    ssm0 = jnp.zeros((bsz, SSM_HEADS, SSM_HEAD_DIM, SSM_STATE), jnp.float32)
    y_ssm, conv_new, ssm_new = ssm_branch(z, xbc, dt_raw, conv0, ssm0, min(SSM_CHUNK, t), lp)
    x = merge_and_ffn(x, y_sb, y_ssm, y_fx, gates, lp)
    return x, (k_sb, v_sb, k_fx, v_fx, logf, ssm_new, conv_new)


def sample_layer(x, l, lp, cache_sb_k, cache_sb_v, cache_fox_k, cache_fox_v, cache_fox_logf,
                 state_ssm, state_conv, page_table):
    bsz, t, _ = x.shape
    past = page_table.shape[1] * PAGE_SIZE

    def gather(pool):
        g = pool[l, page_table]
        return g.reshape(bsz, past, *g.shape[3:])

    q_sb, k_sb, v_sb, q_fx, k_fx, v_fx, f_fx, z, xbc, dt_raw, gates = mixer_inputs(x, lp)
    q_sb, k_sb, v_sb = to_heads(q_sb, SB_HEADS), to_heads(k_sb, SB_HEADS), to_heads(v_sb, SB_HEADS)
    q_fx, k_fx, v_fx = to_heads(q_fx, FOX_HEADS), to_heads(k_fx, FOX_HEADS), to_heads(v_fx, FOX_HEADS)
    q_pos = past + jnp.arange(t)
    k_pos = jnp.arange(past + t)
    k_sb_all = jnp.concatenate([gather(cache_sb_k).astype(k_sb.dtype), k_sb], axis=1)
    v_sb_all = jnp.concatenate([gather(cache_sb_v).astype(v_sb.dtype), v_sb], axis=1)
    y_sb = stick_breaking_core(q_sb, k_sb_all, v_sb_all, q_pos, k_pos)
    logf = jax.nn.log_sigmoid(f_fx.astype(jnp.float32) + lp['b_forget'].astype(jnp.float32))
    cum = jnp.cumsum(jnp.concatenate([gather(cache_fox_logf).astype(jnp.float32), logf], axis=1), axis=1)
    k_fx_all = jnp.concatenate([gather(cache_fox_k).astype(k_fx.dtype), k_fx], axis=1)
    v_fx_all = jnp.concatenate([gather(cache_fox_v).astype(v_fx.dtype), v_fx], axis=1)
    y_fx = forgetting_core(q_fx, k_fx_all, v_fx_all, cum[:, past:], cum, q_pos, k_pos)
    y_ssm, conv_new, ssm_new = ssm_branch(z, xbc, dt_raw, state_conv[l], state_ssm[l], t, lp)
    x = merge_and_ffn(x, y_sb, y_ssm, y_fx, gates, lp)
    return x, (k_sb, v_sb, k_fx, v_fx, logf, ssm_new, conv_new)


def setup_inputs(seed: int = 0) -> dict:
    key = jax.random.key(seed)
    ks = jax.random.split(key, 32)
    f32 = jnp.float32
    n_pages = PAST_LEN // PAGE_SIZE
    n_used = DEC_BATCH * n_pages
    n_pool = n_used + max(1, n_used // 4)

    def nrm(k, shape, scale=1.0):
        return jax.random.normal(k, shape, f32) * scale

    u_dt = jax.random.uniform(ks[14], (DEPTH, SSM_HEADS), f32)
    dt0 = jnp.exp(u_dt * (math.log(DT_MAX) - math.log(DT_MIN)) + math.log(DT_MIN))
    return {
        'x_prompt': nrm(ks[0], (BATCH, SEQ, D_MODEL)),
        'x_sample': nrm(ks[1], (DEC_BATCH, DEC_SEQ, D_MODEL)),
        'cache_sb_k': nrm(ks[2], (DEPTH, n_pool, PAGE_SIZE, SB_HEADS, HEAD_DIM)),
        'cache_sb_v': nrm(ks[3], (DEPTH, n_pool, PAGE_SIZE, SB_HEADS, HEAD_DIM)),
        'cache_fox_k': nrm(ks[4], (DEPTH, n_pool, PAGE_SIZE, FOX_HEADS, HEAD_DIM)),
        'cache_fox_v': nrm(ks[5], (DEPTH, n_pool, PAGE_SIZE, FOX_HEADS, HEAD_DIM)),
        'cache_fox_logf': jax.nn.log_sigmoid(3.0 + nrm(ks[6], (DEPTH, n_pool, PAGE_SIZE, FOX_HEADS), 1.5)),
        'state_ssm': nrm(ks[7], (DEPTH, DEC_BATCH, SSM_HEADS, SSM_HEAD_DIM, SSM_STATE), 0.5),
        'state_conv': nrm(ks[8], (DEPTH, DEC_BATCH, SSM_CONV - 1, SSM_CONV_DIM)),
        'page_table': jax.random.permutation(ks[9], n_pool)[:n_used].reshape(DEC_BATCH, n_pages).astype(jnp.int32),
        'norm1_g': 1.0 + nrm(ks[10], (DEPTH, D_MODEL), 0.05),
        'w_in': nrm(ks[11], (DEPTH, D_MODEL, IN_WIDTH), D_MODEL ** -0.5),
        'b_forget': 3.0 + nrm(ks[12], (DEPTH, FOX_HEADS), 1.5),
        'conv_w': nrm(ks[13], (DEPTH, SSM_CONV, SSM_CONV_DIM), SSM_CONV ** -0.5),
        'conv_b': nrm(ks[15], (DEPTH, SSM_CONV_DIM), 0.01),
        'dt_bias': dt0 + jnp.log(-jnp.expm1(-dt0)),
        'a_log': jnp.log(jax.random.uniform(ks[16], (DEPTH, SSM_HEADS), f32, minval=1.0, maxval=16.0)),
        'd_skip': 1.0 + nrm(ks[17], (DEPTH, SSM_HEADS), 0.1),
        'ssm_norm_g': 1.0 + nrm(ks[18], (DEPTH, SSM_INNER), 0.05),
        'w_sb_out': nrm(ks[19], (DEPTH, SB_WIDTH, D_MODEL), SB_WIDTH ** -0.5),
        'w_ssm_out': nrm(ks[20], (DEPTH, SSM_INNER, D_MODEL), SSM_INNER ** -0.5),
        'w_fox_out': nrm(ks[21], (DEPTH, FOX_WIDTH, D_MODEL), FOX_WIDTH ** -0.5),
        'w_o': nrm(ks[22], (DEPTH, D_MODEL, D_MODEL), D_MODEL ** -0.5),
        'norm2_g': 1.0 + nrm(ks[23], (DEPTH, D_MODEL), 0.05),
        'w_up': nrm(ks[24], (DEPTH, D_MODEL, FFN_HIDDEN), D_MODEL ** -0.5),
        'w_down': nrm(ks[25], (DEPTH, FFN_HIDDEN, D_MODEL), FFN_HIDDEN ** -0.5),
        'final_norm_g': 1.0 + nrm(ks[26], (D_MODEL,), 0.05),
    }


def reference(x_prompt, x_sample, cache_sb_k, cache_sb_v, cache_fox_k, cache_fox_v, cache_fox_logf,
              state_ssm, state_conv, page_table, norm1_g, w_in, b_forget, conv_w, conv_b, dt_bias,
              a_log, d_skip, ssm_norm_g, w_sb_out, w_ssm_out, w_fox_out, w_o, norm2_g, w_up, w_down,
              final_norm_g):
    xp, xs = x_prompt, x_sample
    prompt_states, sample_states = [], []
    for l in range(DEPTH):
        lp = {'norm1_g': norm1_g[l], 'w_in': w_in[l], 'b_forget': b_forget[l], 'conv_w': conv_w[l],
              'conv_b': conv_b[l], 'dt_bias': dt_bias[l], 'a_log': a_log[l], 'd_skip': d_skip[l],
              'ssm_norm_g': ssm_norm_g[l], 'w_sb_out': w_sb_out[l], 'w_ssm_out': w_ssm_out[l],
              'w_fox_out': w_fox_out[l], 'w_o': w_o[l], 'norm2_g': norm2_g[l], 'w_up': w_up[l],
              'w_down': w_down[l]}
        xp, st_p = prompt_layer(xp, lp)
        xs, st_s = sample_layer(xs, l, lp, cache_sb_k, cache_sb_v, cache_fox_k, cache_fox_v,
                                cache_fox_logf, state_ssm, state_conv, page_table)
        prompt_states.append(st_p)
        sample_states.append(st_s)
    y_prompt = rms_norm(xp, final_norm_g)
    y_sample = rms_norm(xs, final_norm_g)
    p_sb_k, p_sb_v, p_fox_k, p_fox_v, p_fox_logf, p_ssm, p_conv = [jnp.stack(s) for s in zip(*prompt_states)]
    s_sb_k, s_sb_v, s_fox_k, s_fox_v, s_fox_logf, s_ssm, s_conv = [jnp.stack(s) for s in zip(*sample_states)]
    return (y_prompt, y_sample, p_sb_k, p_sb_v, p_fox_k, p_fox_v, p_fox_logf, p_ssm, p_conv,
            s_sb_k, s_sb_v, s_fox_k, s_fox_v, s_fox_logf, s_ssm, s_conv)
```

```python
import functools

import jax
import jax.numpy as jnp
from jax import lax
from jax.experimental import pallas as pl
from jax.experimental.pallas import tpu as pltpu

F32 = jnp.float32
BF16 = jnp.bfloat16

D_MODEL = 1024
HEAD_DIM = 64
N_HEADS = 4
ATT_WIDTH = N_HEADS * HEAD_DIM
SSM_HEADS = 8
SSM_INNER = 512
SSM_STATE = 128
SSM_GROUPS = 2
SSM_CONV = 4
SSM_CONV_DIM = 1024
SSM_CHUNK = 128
PAGE_SIZE = 128
N_BRANCH = 3
FFN_HIDDEN = 4 * D_MODEL
RMS_EPS = 1e-6
NEG_INF = -1e30
Q_SCALE = HEAD_DIM ** -0.5

_OFF_F = 6 * ATT_WIDTH
_OFF_Z = _OFF_F + N_HEADS
_OFF_XBC = _OFF_Z + SSM_INNER
_OFF_DT = _OFF_XBC + SSM_CONV_DIM
_OFF_GATE = _OFF_DT + SSM_HEADS
IN_WIDTH = _OFF_GATE + N_BRANCH * D_MODEL
MAIN_WIDTH = 6 * ATT_WIDTH + SSM_INNER + SSM_CONV_DIM + N_BRANCH * D_MODEL
SMALL_WIDTH = 128
DT_LANE = N_HEADS

V7X_VMEM_LIMIT_BYTES = 56 * 1024 * 1024
PAGES_PER_STEP = 16
CUMSUM_SEG = 256


def _cparams(*sem):
    return pltpu.CompilerParams(dimension_semantics=sem, vmem_limit_bytes=V7X_VMEM_LIMIT_BYTES)


def _const_spec(shape):
    n = len(shape)
    return pl.BlockSpec(shape, lambda *_: (0,) * n, pipeline_mode=pl.Buffered(1))


def _rms(x, g):
    ms = jnp.mean(x * x, axis=-1, keepdims=True)
    return x * lax.rsqrt(ms + RMS_EPS) * g


def _softplus_tail(z):
    return jnp.log1p(jnp.exp(-jnp.abs(z)))


def _log_sigmoid(z):
    return jnp.minimum(z, 0.0) - _softplus_tail(z)


def _softplus(z):
    return jnp.maximum(z, 0.0) + _softplus_tail(z)


def _dot_nt(a, b):
    return lax.dot_general(a, b, (((1,), (1,)), ((), ())), preferred_element_type=F32)


def _dot(a, b):
    return jnp.dot(a, b, preferred_element_type=F32)


def _dot_exact(a, b):
    return jnp.dot(a, b, preferred_element_type=F32, precision=lax.Precision.HIGHEST)


def _inproj_kernel(x_ref, g_ref, wm_ref, ws_ref,
                   qsb_ref, qfx_ref, ksb_ref, vsb_ref, kfx_ref, vfx_ref,
                   ksbh_ref, vsbh_ref, kfxh_ref, vfxh_ref,
                   z_ref, xbc_ref, small_ref, gate_ref, *, kv_transposed):
    h = _rms(x_ref[...], g_ref[...]).astype(BF16)

    def mm(c0, width):
        return _dot(h, wm_ref[:, c0:c0 + width])

    w = ATT_WIDTH
    qsb_ref[...] = (mm(0, w) * Q_SCALE).astype(BF16)
    for i, (full_ref, half_ref) in enumerate(((ksb_ref, ksbh_ref), (vsb_ref, vsbh_ref))):
        a = mm((1 + i) * w, w)
        full_ref[...] = a.T if kv_transposed else a
        half_ref[...] = a.astype(BF16)
    qfx_ref[...] = (mm(3 * w, w) * Q_SCALE).astype(BF16)
    for i, (full_ref, half_ref) in enumerate(((kfx_ref, kfxh_ref), (vfx_ref, vfxh_ref))):
        a = mm((4 + i) * w, w)
        full_ref[...] = a.T if kv_transposed else a
        half_ref[...] = a.astype(BF16)
    z_ref[...] = mm(6 * w, SSM_INNER)
    c0 = 6 * w + SSM_INNER
    for c in range(SSM_CONV_DIM // 512):
        xbc_ref[:, c * 512:(c + 1) * 512] = mm(c0 + c * 512, 512)
    c0 += SSM_CONV_DIM
    for c in range(N_BRANCH * D_MODEL // 512):
        gate_ref[:, c * 512:(c + 1) * 512] = jax.nn.sigmoid(mm(c0 + c * 512, 512)).astype(BF16)
    small_ref[...] = _dot(h, ws_ref[...])


def _inproj(x2d, g, w_main, w_small, kv_seq=None):
    n = x2d.shape[0]
    tm = min(512, n)
    row = lambda width: pl.BlockSpec((tm, width), lambda i: (i, 0))
    rows = lambda width, dt: (row(width), jax.ShapeDtypeStruct((n, width), dt))
    if kv_seq is None:
        kv = rows(ATT_WIDTH, F32)
    else:
        bsz, t = kv_seq
        nt = t // tm
        kv = (pl.BlockSpec((None, ATT_WIDTH, tm), lambda i: (i // nt, 0, i % nt)),
              jax.ShapeDtypeStruct((bsz, ATT_WIDTH, t), F32))
    outs = ([rows(ATT_WIDTH, BF16)] * 2 + [kv] * 4 + [rows(ATT_WIDTH, BF16)] * 4
            + [rows(SSM_INNER, F32), rows(SSM_CONV_DIM, F32), rows(SMALL_WIDTH, F32),
               rows(N_BRANCH * D_MODEL, BF16)])
    return pl.pallas_call(
        functools.partial(_inproj_kernel, kv_transposed=kv_seq is not None),
        grid=(n // tm,),
        in_specs=[row(D_MODEL), _const_spec((1, D_MODEL)),
                  _const_spec((D_MODEL, MAIN_WIDTH)), _const_spec((D_MODEL, SMALL_WIDTH))],
        out_specs=[spec for spec, _ in outs],
        out_shape=[shape for _, shape in outs],
        compiler_params=_cparams("parallel"),
        name="inproj",
    )(x2d, g, w_main, w_small)


def _lane_head(width=ATT_WIDTH):
    return lax.broadcasted_iota(jnp.int32, (1, width), 1) // HEAD_DIM


def _strict_upper(n, dtype):
    r = lax.broadcasted_iota(jnp.int32, (n, n), 0)
    c = lax.broadcasted_iota(jnp.int32, (n, n), 1)
    return jnp.where(r > c, 1.0, 0.0).astype(dtype)


def _suffix_sums(x, u):
    hi = x.astype(BF16)
    lo = (x - hi.astype(F32)).astype(BF16)
    cs = _dot(hi, u) + _dot(lo, u)
    return cs, cs[:, 0:1] + x[:, 0:1]


def _stick_block(z, carry, u, mask):
    t = _softplus_tail(z)
    log_beta = jnp.minimum(z, 0.0) - t
    log_keep = -jnp.maximum(z, 0.0) - t
    if mask is not None:
        log_keep = jnp.where(mask, log_keep, 0.0)
    n = z.shape[1]
    if n == u.shape[0]:
        later, total = _suffix_sums(log_keep, u)
        later = later + carry
        carry = carry + total
    else:
        seg = u.shape[0]
        parts = [None] * (n // seg)
        for s in reversed(range(n // seg)):
            cs, total = _suffix_sums(log_keep[:, s * seg:(s + 1) * seg], u)
            parts[s] = cs + carry
            carry = carry + total
        later = jnp.concatenate(parts, axis=1)
    w = jnp.exp(log_beta + later)
    if mask is not None:
        w = jnp.where(mask, w, 0.0)
    return w, carry


def _sb_prompt_kernel(q_ref, k_ref, v_ref, o_ref, acc_ref, *, blk):
    i = pl.program_id(1)
    q = q_ref[...]
    lane_head = _lane_head()
    qh = [jnp.where(lane_head == h, q, jnp.zeros_like(q)) for h in range(N_HEADS)]
    u = _strict_upper(blk, BF16)
    row = lax.broadcasted_iota(jnp.int32, (blk, blk), 0)
    col = lax.broadcasted_iota(jnp.int32, (blk, blk), 1)
    causal = col < row
    acc_ref[...] = jnp.zeros_like(acc_ref)

    def block(j, carries, mask):
        start = pl.multiple_of(j * blk, blk)
        kb = k_ref[pl.ds(start, blk), :]
        vb = v_ref[pl.ds(start, blk), :]
        upd = jnp.zeros((blk, ATT_WIDTH), F32)
        out = []
        for h in range(N_HEADS):
            w, c = _stick_block(_dot_nt(qh[h], kb), carries[h], u, mask)
            upd = jnp.where(lane_head == h, _dot(w.astype(BF16), vb), upd)
            out.append(c)
        acc_ref[...] += upd
        return tuple(out)

    zero = jnp.zeros((blk, 1), F32)
    carries = block(i, (zero,) * N_HEADS, causal)
    lax.fori_loop(0, i, lambda jj, c: block(i - 1 - jj, c, None), carries)
    o_ref[...] = acc_ref[...].astype(o_ref.dtype)


def _sb_prompt(q, k, v, blk=256):
    b, t, w = q.shape
    qspec = pl.BlockSpec((None, blk, w), lambda bi, i: (bi, i, 0))
    kvspec = pl.BlockSpec((None, t, w), lambda bi, i: (bi, 0, 0))
    return pl.pallas_call(
        functools.partial(_sb_prompt_kernel, blk=blk),
        grid=(b, t // blk),
        in_specs=[qspec, kvspec, kvspec],
        out_specs=qspec,
        out_shape=jax.ShapeDtypeStruct((b, t, w), BF16),
        scratch_shapes=[pltpu.VMEM((blk, w), F32)],
        compiler_params=_cparams("parallel", "arbitrary"),
        name="sb_prompt",
    )(q, k, v)


def _logf_cum_kernel(b_ref, f_ref, logf_ref, cum_ref):
    h = pl.program_id(0) % N_HEADS
    logf = _log_sigmoid(f_ref[...] + b_ref[h])
    logf_ref[...] = logf
    rows = logf.shape[0]
    r = lax.broadcasted_iota(jnp.int32, (PAGE_SIZE, PAGE_SIZE), 0)
    c = lax.broadcasted_iota(jnp.int32, (PAGE_SIZE, PAGE_SIZE), 1)
    within = _dot_exact(logf, jnp.where(r <= c, 1.0, 0.0).astype(F32))
    totals = jnp.broadcast_to(within[:, PAGE_SIZE - 1:PAGE_SIZE], within.shape)
    rr = lax.broadcasted_iota(jnp.int32, (rows, rows), 0)
    cc = lax.broadcasted_iota(jnp.int32, (rows, rows), 1)
    cum_ref[...] = within + _dot_exact(jnp.where(cc < rr, 1.0, 0.0).astype(F32), totals)


def _logf_cum(f_rows, b_forget):
    n, rows, lanes = f_rows.shape
    spec = pl.BlockSpec((None, rows, lanes), lambda i: (i, 0, 0))
    return pl.pallas_call(
        _logf_cum_kernel,
        grid=(n,),
        in_specs=[pl.BlockSpec(memory_space=pltpu.SMEM), spec],
        out_specs=[spec, spec],
        out_shape=[jax.ShapeDtypeStruct(f_rows.shape, F32)] * 2,
        compiler_params=_cparams("parallel"),
        name="logf_cum",
    )(b_forget, f_rows)


def _fox_prompt_kernel(q_ref, k_ref, v_ref, cq_ref, ck_ref, o_ref, acc_ref, *, blk):
    i = pl.program_id(1)
    q = q_ref[...]
    cq = cq_ref[...]
    lane_head = _lane_head()
    qh = [jnp.where(lane_head == h, q, jnp.zeros_like(q)) for h in range(N_HEADS)]
    row = lax.broadcasted_iota(jnp.int32, (blk, blk), 0)
    col = lax.broadcasted_iota(jnp.int32, (blk, blk), 1)
    causal = col <= row
    acc_ref[...] = jnp.zeros_like(acc_ref)

    def block(j, state, mask):
        ms, ls = state
        start = pl.multiple_of(j * blk, blk)
        kb = k_ref[pl.ds(start, blk), :]
        vb = v_ref[pl.ds(start, blk), :]
        ck = ck_ref[j]
        upd = jnp.zeros((blk, ATT_WIDTH), F32)
        scale = jnp.zeros((blk, ATT_WIDTH), F32)
        new_m, new_l = [], []
        for h in range(N_HEADS):
            s = _dot_nt(qh[h], kb) + cq[:, h:h + 1] - ck[h:h + 1, :]
            if mask is not None:
                s = jnp.where(mask, s, NEG_INF)
            m = jnp.maximum(ms[h], jnp.max(s, axis=-1, keepdims=True))
            alpha = jnp.exp(ms[h] - m)
            p = jnp.exp(s - m)
            new_m.append(m)
            new_l.append(alpha * ls[h] + jnp.sum(p, axis=-1, keepdims=True))
            upd = jnp.where(lane_head == h, _dot(p.astype(BF16), vb), upd)
            scale = jnp.where(lane_head == h, alpha, scale)
        acc_ref[...] = acc_ref[...] * scale + upd
        return tuple(new_m), tuple(new_l)

    m0 = (jnp.full((blk, 1), NEG_INF, F32),) * N_HEADS
    l0 = (jnp.zeros((blk, 1), F32),) * N_HEADS
    state = block(i, (m0, l0), causal)
    _, ls = lax.fori_loop(0, i, lambda jj, st: block(i - 1 - jj, st, None), state)
    denom = jnp.zeros((blk, ATT_WIDTH), F32)
    for h in range(N_HEADS):
        denom = jnp.where(lane_head == h, ls[h], denom)
    o_ref[...] = (acc_ref[...] / denom).astype(o_ref.dtype)


def _fox_prompt(q, k, v, cum_col, cum_row, blk=256):
    b, t, w = q.shape
    qspec = pl.BlockSpec((None, blk, w), lambda bi, i: (bi, i, 0))
    kvspec = pl.BlockSpec((None, t, w), lambda bi, i: (bi, 0, 0))
    return pl.pallas_call(
        functools.partial(_fox_prompt_kernel, blk=blk),
        grid=(b, t // blk),
        in_specs=[qspec, kvspec, kvspec,
                  pl.BlockSpec((None, blk, N_HEADS), lambda bi, i: (bi, i, 0)),
                  pl.BlockSpec((None, t // blk, N_HEADS, blk), lambda bi, i: (bi, 0, 0, 0))],
        out_specs=qspec,
        out_shape=jax.ShapeDtypeStruct((b, t, w), BF16),
        scratch_shapes=[pltpu.VMEM((blk, w), F32)],
        compiler_params=_cparams("parallel", "arbitrary"),
        name="fox_prompt",
    )(q, k, v, cum_col, cum_row)


_XBUF_ROW0 = 8


def _ssm_kernel(z_ref, xbc_ref, small_ref, conv0_ref, state0_ref,
                cw_ref, cb_ref, dtb_ref, alog_ref, dskip_ref, ng_ref,
                y_ref, state_out_ref, conv_out_ref, xbuf_ref, state_ref, *, valid):
    c = pl.program_id(1)
    last = pl.num_programs(1) - 1
    L = SSM_CHUNK
    r0 = _XBUF_ROW0

    @pl.when(c == 0)
    def _():
        xbuf_ref[r0 - 3:r0, :] = conv0_ref[...]
        state_ref[...] = state0_ref[...]

    x_cur = xbc_ref[...]
    xbuf_ref[r0:r0 + L, :] = x_cur
    cw = cw_ref[...]
    conv = cb_ref[...] + x_cur * cw[3:4, :]
    for i in range(SSM_CONV - 1):
        conv = conv + xbuf_ref[r0 - 3 + i:r0 - 3 + i + L, :] * cw[i:i + 1, :]
    xbuf_ref[r0 - 3:r0, :] = x_cur[L - 3:L, :]
    act = conv * jax.nn.sigmoid(conv)
    xs = act[:, :SSM_INNER]
    b_in = act[:, SSM_INNER:SSM_INNER + SSM_GROUPS * SSM_STATE].astype(BF16)
    c_in = act[:, SSM_INNER + SSM_GROUPS * SSM_STATE:].astype(BF16)

    row = lax.broadcasted_iota(jnp.int32, (L, L), 0)
    col = lax.broadcasted_iota(jnp.int32, (L, L), 1)
    tri = row >= col
    dt = _softplus(small_ref[...] + dtb_ref[...])
    if valid < L:
        dt = jnp.where(row < valid, dt, 0.0)
    d_a = dt * (-jnp.exp(alog_ref[...]))
    a_cs = _dot_exact(jnp.where(tri, 1.0, 0.0).astype(F32), d_a)
    a_cs_t = a_cs.T
    a_last = a_cs[L - 1:L, :]
    e_cs = jnp.exp(a_cs)
    wgt = jnp.exp(a_last - a_cs) * dt
    chunk_dec = jnp.exp(a_last)

    half = lax.broadcasted_iota(jnp.int32, (1, 128), 1) // SSM_STATE_HALF
    rhalf = lax.broadcasted_iota(jnp.int32, (128, 1), 0) // SSM_STATE_HALF
    pair_cols = lambda a, p: jnp.where(half == 0, a[:, DT_LANE + 2 * p:DT_LANE + 2 * p + 1],
                                       a[:, DT_LANE + 2 * p + 1:DT_LANE + 2 * p + 2])
    scores = [_dot_nt(c_in[:, g * SSM_STATE:(g + 1) * SSM_STATE],
                      b_in[:, g * SSM_STATE:(g + 1) * SSM_STATE]) for g in range(SSM_GROUPS)]
    ys = []
    for p in range(SSM_HEADS // 2):
        g = (2 * p) // (SSM_HEADS // SSM_GROUPS)
        bg = b_in[:, g * SSM_STATE:(g + 1) * SSM_STATE]
        cg = c_in[:, g * SSM_STATE:(g + 1) * SSM_STATE]
        xs_p = xs[:, 128 * p:128 * (p + 1)]
        xdt = (xs_p * pair_cols(dt, p)).astype(BF16)
        y_diag = jnp.zeros((L, 128), F32)
        for hh in range(2):
            lane = DT_LANE + 2 * p + hh
            seg = a_cs[:, lane:lane + 1] - a_cs_t[lane:lane + 1, :]
            decay = jnp.exp(jnp.where(tri, seg, NEG_INF))
            y_h = _dot((scores[g] * decay).astype(BF16), xdt)
            y_diag = jnp.where(half == hh, y_h, y_diag)
        st = state_ref[128 * p:128 * (p + 1), :]
        y_off = _dot_nt(cg, st.astype(BF16)) * pair_cols(e_cs, p)
        ys.append(y_diag + y_off + dskip_ref[:, 128 * p:128 * (p + 1)] * xs_p)
        xw_t = (xs_p * pair_cols(wgt, p)).T.astype(BF16)
        lane = DT_LANE + 2 * p
        dec = jnp.where(rhalf == 0,
                        jnp.broadcast_to(chunk_dec[:, lane:lane + 1], (128, SSM_STATE)),
                        jnp.broadcast_to(chunk_dec[:, lane + 1:lane + 2], (128, SSM_STATE)))
        state_ref[128 * p:128 * (p + 1), :] = st * dec + _dot(xw_t, bg)

    z = z_ref[...]
    y = jnp.concatenate(ys, axis=1) * (z * jax.nn.sigmoid(z))
    gw = SSM_INNER // SSM_GROUPS
    parts = []
    for g in range(SSM_GROUPS):
        yg = y[:, g * gw:(g + 1) * gw]
        parts.append(yg * lax.rsqrt(jnp.mean(yg * yg, axis=-1, keepdims=True) + RMS_EPS))
    y_ref[...] = (jnp.concatenate(parts, axis=1) * ng_ref[...]).astype(y_ref.dtype)

    @pl.when(c == last)
    def _():
        state_out_ref[...] = state_ref[...]
        conv_out_ref[...] = x_cur[valid - 3:valid, :]


SSM_STATE_HALF = 64


def _ssm(z, xbc, small, conv0, state0, cw, cb, dtb, alog, dskip, ng, valid):
    b, t, _ = z.shape
    nc = t // SSM_CHUNK
    chunk = lambda width: pl.BlockSpec((None, SSM_CHUNK, width), lambda bi, c: (bi, c, 0))
    per_b = lambda shape: pl.BlockSpec((None,) + shape, lambda bi, c: (bi,) + (0,) * len(shape))
    const = lambda shape: pl.BlockSpec(shape, lambda bi, c: (0,) * len(shape))
    state_rows = SSM_INNER
    return pl.pallas_call(
        functools.partial(_ssm_kernel, valid=valid),
        grid=(b, nc),
        in_specs=[chunk(SSM_INNER), chunk(SSM_CONV_DIM), chunk(SMALL_WIDTH),
                  per_b((SSM_CONV - 1, SSM_CONV_DIM)), per_b((state_rows, SSM_STATE)),
                  const((SSM_CONV, SSM_CONV_DIM)), const((1, SSM_CONV_DIM)),
                  const((1, SMALL_WIDTH)), const((1, SMALL_WIDTH)),
                  const((1, SSM_INNER)), const((1, SSM_INNER))],
        out_specs=[chunk(SSM_INNER), per_b((state_rows, SSM_STATE)),
                   per_b((SSM_CONV - 1, SSM_CONV_DIM))],
        out_shape=[jax.ShapeDtypeStruct((b, t, SSM_INNER), BF16),
                   jax.ShapeDtypeStruct((b, state_rows, SSM_STATE), F32),
                   jax.ShapeDtypeStruct((b, SSM_CONV - 1, SSM_CONV_DIM), F32)],
        scratch_shapes=[pltpu.VMEM((_XBUF_ROW0 + SSM_CHUNK, SSM_CONV_DIM), F32),
                        pltpu.VMEM((state_rows, SSM_STATE), F32)],
        compiler_params=_cparams("parallel", "arbitrary"),
        name="ssm",
    )(z, xbc, small, conv0, state0, cw, cb, dtb, alog, dskip, ng)


def _merge_ffn_kernel(x_ref, ysb_ref, yssm_ref, yfx_ref, gate_ref,
                      wsb_ref, wssm_ref, wfx_ref, wo_ref, g2_ref, wup_ref, wdn_ref, gf_ref,
                      o_ref, *, final):
    d = D_MODEL
    mixed = gate_ref[:, 0:d].astype(F32) * _dot(ysb_ref[...].astype(BF16), wsb_ref[...])
    mixed = mixed + gate_ref[:, d:2 * d].astype(F32) * _dot(yssm_ref[...].astype(BF16), wssm_ref[...])
    mixed = mixed + gate_ref[:, 2 * d:3 * d].astype(F32) * _dot(yfx_ref[...].astype(BF16), wfx_ref[...])
    x = x_ref[...] + _dot(mixed.astype(BF16), wo_ref[...])
    h = _rms(x, g2_ref[...]).astype(BF16)
    hc = FFN_HIDDEN // 2
    for c in range(2):
        u = jnp.maximum(_dot(h, wup_ref[:, c * hc:(c + 1) * hc]), 0.0)
        x = x + _dot((u * u).astype(BF16), wdn_ref[c * hc:(c + 1) * hc, :])
    if final:
        x = _rms(x, gf_ref[...])
    o_ref[...] = x


def _merge_ffn(x2d, y_sb, y_ssm, y_fx, gates, lw, gf, final):
    n = x2d.shape[0]
    tm = min(512, n)
    row = lambda width: pl.BlockSpec((tm, width), lambda i: (i, 0))
    weights = [lw['w_sb_out'], lw['w_ssm_out'], lw['w_fox_out'], lw['w_o'], lw['norm2_g'],
               lw['w_up'], lw['w_down'], gf]
    return pl.pallas_call(
        functools.partial(_merge_ffn_kernel, final=final),
        grid=(n // tm,),
        in_specs=[row(D_MODEL), row(ATT_WIDTH), row(SSM_INNER), row(ATT_WIDTH),
                  row(N_BRANCH * D_MODEL)] + [_const_spec(w.shape) for w in weights],
        out_specs=row(D_MODEL),
        out_shape=jax.ShapeDtypeStruct((n, D_MODEL), F32),
        compiler_params=_cparams("parallel"),
        name="merge_ffn",
    )(x2d, y_sb, y_ssm, y_fx, gates, *weights)


def _block_diag_q(q_ref):
    q = q_ref[...]
    rows = q.shape[0]
    row_head = lax.broadcasted_iota(jnp.int32, (rows, 1), 0) // (rows // N_HEADS)
    return jnp.where(row_head == _lane_head(), q, jnp.zeros_like(q))


def _fold_heads(acc, steps):
    lane_head = _lane_head()
    out = jnp.zeros((steps, ATT_WIDTH), F32)
    for h in range(N_HEADS):
        out = jnp.where(lane_head == h, acc[h * steps:(h + 1) * steps, :], out)
    return out


def _gather_pages(page_refs, buf_ref):
    for p, ref in enumerate(page_refs):
        buf_ref[:, p * PAGE_SIZE:(p + 1) * PAGE_SIZE] = ref[...].astype(BF16)


def _sb_sample_kernel(pt_ref, q_ref, kn_ref, vn_ref, *rest, steps, n_pages):
    k_refs = rest[:n_pages]
    v_refs = rest[n_pages:2 * n_pages]
    o_ref, acc_ref, carry_ref, kbuf_ref, vbuf_ref = rest[2 * n_pages:]
    c = pl.program_id(1)
    rows = N_HEADS * steps
    qbd = _block_diag_q(q_ref)
    u = _strict_upper(CUMSUM_SEG, BF16)

    @pl.when(c == 0)
    def _():
        step = lax.broadcasted_iota(jnp.int32, (rows, PAGE_SIZE), 0) % steps
        col = lax.broadcasted_iota(jnp.int32, (rows, PAGE_SIZE), 1)
        w, carry = _stick_block(_dot_nt(qbd, kn_ref[...]), jnp.zeros((rows, 1), F32),
                                _strict_upper(PAGE_SIZE, BF16), col < step)
        acc_ref[...] = _dot(w.astype(BF16), vn_ref[...])
        carry_ref[...] = jnp.broadcast_to(carry, carry_ref.shape)

    _gather_pages(k_refs, kbuf_ref)
    _gather_pages(v_refs, vbuf_ref)
    w, carry = _stick_block(_dot(qbd, kbuf_ref[...]), carry_ref[:, 0:1], u, None)
    acc_ref[...] += _dot_nt(w.astype(BF16), vbuf_ref[...])
    carry_ref[...] = jnp.broadcast_to(carry, carry_ref.shape)

    @pl.when(c == pl.num_programs(1) - 1)
    def _():
        o_ref[...] = _fold_heads(acc_ref[...], steps)


def _page_specs(layer, n_chunks, n_pages, block):
    def spec(p):
        def index(b, c, pt):
            return (layer, pt[b, (n_chunks - 1 - c) * n_pages + p]) + (0,) * (len(block) - 2)
        return pl.BlockSpec(block, index)
    return [spec(p) for p in range(n_pages)]


def _sb_sample(layer, page_table, q_rows, k_new, v_new, cache_k, cache_v, steps):
    b, rows, w = q_rows.shape
    n_pages = PAGES_PER_STEP
    n_chunks = page_table.shape[1] // n_pages
    page_block = (None, None, w, PAGE_SIZE)
    per_b = lambda r: pl.BlockSpec((None, r, w), lambda bi, c, pt: (bi, 0, 0))
    grid_spec = pltpu.PrefetchScalarGridSpec(
        num_scalar_prefetch=1,
        grid=(b, n_chunks),
        in_specs=[per_b(rows), per_b(PAGE_SIZE), per_b(PAGE_SIZE)]
                 + _page_specs(layer, n_chunks, n_pages, page_block) * 2,
        out_specs=per_b(steps),
        scratch_shapes=[pltpu.VMEM((rows, w), F32), pltpu.VMEM((rows, 128), F32),
                        pltpu.VMEM((w, n_pages * PAGE_SIZE), BF16),
                        pltpu.VMEM((w, n_pages * PAGE_SIZE), BF16)],
    )
    return pl.pallas_call(
        functools.partial(_sb_sample_kernel, steps=steps, n_pages=n_pages),
        grid_spec=grid_spec,
        out_shape=jax.ShapeDtypeStruct((b, steps, w), F32),
        compiler_params=_cparams("parallel", "arbitrary"),
        name="sb_sample",
    )(page_table, q_rows, k_new, v_new, *([cache_k] * n_pages), *([cache_v] * n_pages))


def _fox_sample_kernel(pt_ref, q_ref, kn_ref, vn_ref, f_ref, b_ref, *rest, steps, n_pages):
    k_refs = rest[:n_pages]
    v_refs = rest[n_pages:2 * n_pages]
    lf_refs = rest[2 * n_pages:3 * n_pages]
    (o_ref, lf_out_ref, acc_ref, m_ref, l_ref, ncum_ref, rcarry_ref,
     kbuf_ref, vbuf_ref, lbuf_ref) = rest[3 * n_pages:]
    c = pl.program_id(1)
    rows = N_HEADS * steps
    qbd = _block_diag_q(q_ref)
    bcast = lambda a, ref: jnp.broadcast_to(a, ref.shape)

    @pl.when(c == 0)
    def _():
        lf = _log_sigmoid(f_ref[...] + b_ref[...])
        lf_out_ref[...] = lf
        step = lax.broadcasted_iota(jnp.int32, (rows, PAGE_SIZE), 0) % steps
        col = lax.broadcasted_iota(jnp.int32, (rows, PAGE_SIZE), 1)
        lf = jnp.where(col < steps, lf, 0.0)
        cum = lf
        shift = 1
        while shift < steps:
            cum = cum + jnp.where(col >= shift, pltpu.roll(cum, shift, 1), 0.0)
            shift *= 2
        visible = col <= step
        ncum = jnp.sum(jnp.where(visible, lf, 0.0), axis=-1, keepdims=True)
        s = jnp.where(visible, _dot_nt(qbd, kn_ref[...]) + ncum - cum, NEG_INF)
        m = jnp.max(s, axis=-1, keepdims=True)
        p = jnp.exp(s - m)
        m_ref[...] = bcast(m, m_ref)
        l_ref[...] = bcast(jnp.sum(p, axis=-1, keepdims=True), l_ref)
        acc_ref[...] = _dot(p.astype(BF16), vn_ref[...])
        ncum_ref[...] = bcast(ncum, ncum_ref)
        rcarry_ref[...] = jnp.zeros_like(rcarry_ref)
        lbuf_ref[...] = jnp.zeros_like(lbuf_ref)

    _gather_pages(k_refs, kbuf_ref)
    _gather_pages(v_refs, vbuf_ref)
    for p, ref in enumerate(lf_refs):
        lbuf_ref[0:N_HEADS, p * PAGE_SIZE:(p + 1) * PAGE_SIZE] = ref[...]
    u = _strict_upper(CUMSUM_SEG, F32)
    n_seg = n_pages * PAGE_SIZE // CUMSUM_SEG
    carry = rcarry_ref[:, 0:1]
    parts = [None] * n_seg
    for sgm in reversed(range(n_seg)):
        x = lbuf_ref[:, sgm * CUMSUM_SEG:(sgm + 1) * CUMSUM_SEG]
        cs = _dot_exact(x, u)
        parts[sgm] = cs + carry
        carry = carry + cs[:, 0:1] + x[:, 0:1]
    rcarry_ref[...] = bcast(carry, rcarry_ref)
    suffix = jnp.concatenate(parts, axis=1)
    bias = jnp.concatenate(
        [jnp.broadcast_to(suffix[h:h + 1, :], (steps, suffix.shape[1])) for h in range(N_HEADS)], axis=0)
    s = _dot(qbd, kbuf_ref[...]) + bias + ncum_ref[:, 0:1]
    m_old = m_ref[:, 0:1]
    m = jnp.maximum(m_old, jnp.max(s, axis=-1, keepdims=True))
    alpha = jnp.exp(m_old - m)
    p = jnp.exp(s - m)
    l = alpha * l_ref[:, 0:1] + jnp.sum(p, axis=-1, keepdims=True)
    acc = acc_ref[...] * alpha + _dot_nt(p.astype(BF16), vbuf_ref[...])
    m_ref[...] = bcast(m, m_ref)
    l_ref[...] = bcast(l, l_ref)
    acc_ref[...] = acc

    @pl.when(c == pl.num_programs(1) - 1)
    def _():
        o_ref[...] = _fold_heads(acc / l, steps)


def _fox_sample(layer, page_table, q_rows, k_new, v_new, f_rows, b_rows,
                cache_k, cache_v, cache_lf_t, steps):
    b, rows, w = q_rows.shape
    n_pages = PAGES_PER_STEP
    n_chunks = page_table.shape[1] // n_pages
    per_b = lambda r, width=w: pl.BlockSpec((None, r, width), lambda bi, c, pt: (bi, 0, 0))
    stat = pltpu.VMEM((rows, 128), F32)
    grid_spec = pltpu.PrefetchScalarGridSpec(
        num_scalar_prefetch=1,
        grid=(b, n_chunks),
        in_specs=[per_b(rows), per_b(PAGE_SIZE), per_b(PAGE_SIZE), per_b(rows, PAGE_SIZE),
                  pl.BlockSpec((rows, PAGE_SIZE), lambda bi, c, pt: (0, 0))]
                 + _page_specs(layer, n_chunks, n_pages, (None, None, w, PAGE_SIZE)) * 2
                 + _page_specs(layer, n_chunks, n_pages, (None, None, N_HEADS, PAGE_SIZE)),
        out_specs=[per_b(steps), per_b(rows, PAGE_SIZE)],
        scratch_shapes=[pltpu.VMEM((rows, w), F32), stat, stat, stat, pltpu.VMEM((8, 128), F32),
                        pltpu.VMEM((w, n_pages * PAGE_SIZE), BF16),
                        pltpu.VMEM((w, n_pages * PAGE_SIZE), BF16),
                        pltpu.VMEM((8, n_pages * PAGE_SIZE), F32)],
    )
    return pl.pallas_call(
        functools.partial(_fox_sample_kernel, steps=steps, n_pages=n_pages),
        grid_spec=grid_spec,
        out_shape=[jax.ShapeDtypeStruct((b, steps, w), F32),
                   jax.ShapeDtypeStruct((b, rows, PAGE_SIZE), F32)],
        compiler_params=_cparams("parallel", "arbitrary"),
        name="fox_sample",
    )(page_table, q_rows, k_new, v_new, f_rows, b_rows,
      *([cache_k] * n_pages), *([cache_v] * n_pages), *([cache_lf_t] * n_pages))


def _layer_weights(l, norm1_g, w_in, b_forget, conv_w, conv_b, dt_bias, a_log, d_skip, ssm_norm_g,
                   w_sb_out, w_ssm_out, w_fox_out, w_o, norm2_g, w_up, w_down):
    w = w_in[l]
    w_main = jnp.concatenate([w[:, :_OFF_F], w[:, _OFF_Z:_OFF_DT], w[:, _OFF_GATE:]], axis=1).astype(BF16)
    w_small = jnp.concatenate([w[:, _OFF_F:_OFF_Z], w[:, _OFF_DT:_OFF_GATE]], axis=1)
    w_small = jnp.pad(w_small, ((0, 0), (0, SMALL_WIDTH - w_small.shape[1]))).astype(BF16)
    pad_dt = lambda a: jnp.pad(a, (DT_LANE, SMALL_WIDTH - DT_LANE - SSM_HEADS)).reshape(1, SMALL_WIDTH)
    return {
        'norm1_g': norm1_g[l].reshape(1, D_MODEL), 'w_main': w_main, 'w_small': w_small,
        'b_forget': b_forget[l], 'conv_w': conv_w[l], 'conv_b': conv_b[l].reshape(1, SSM_CONV_DIM),
        'dt_bias': pad_dt(dt_bias[l]), 'a_log': pad_dt(a_log[l]),
        'd_skip': jnp.repeat(d_skip[l], SSM_INNER // SSM_HEADS).reshape(1, SSM_INNER),
        'ssm_norm_g': ssm_norm_g[l].reshape(1, SSM_INNER),
        'w_sb_out': w_sb_out[l].astype(BF16), 'w_ssm_out': w_ssm_out[l].astype(BF16),
        'w_fox_out': w_fox_out[l].astype(BF16), 'w_o': w_o[l].astype(BF16),
        'norm2_g': norm2_g[l].reshape(1, D_MODEL),
        'w_up': w_up[l].astype(BF16), 'w_down': w_down[l].astype(BF16),
    }


def _to_heads(a, bsz, t):
    return a.reshape(bsz, t, N_HEADS, HEAD_DIM)


def _prompt_layer(x, lw, gf, final):
    bsz, t, d = x.shape
    n = bsz * t
    (q_sb, q_fx, k_sb, v_sb, k_fx, v_fx, k_sb_h, v_sb_h, k_fx_h, v_fx_h,
     z, xbc, small, gates) = _inproj(x.reshape(n, d), lw['norm1_g'], lw['w_main'], lw['w_small'], (bsz, t))
    b3 = lambda a: a.reshape(bsz, t, a.shape[-1])
    y_sb = _sb_prompt(b3(q_sb), b3(k_sb_h), b3(v_sb_h))

    f_rows = small[:, :N_HEADS].reshape(bsz, t, N_HEADS).transpose(0, 2, 1)
    f_rows = f_rows.reshape(bsz * N_HEADS, t // PAGE_SIZE, PAGE_SIZE)
    logf, cum = _logf_cum(f_rows, lw['b_forget'])
    logf = logf.reshape(bsz, N_HEADS, t).transpose(0, 2, 1)
    blk = 256
    cum_col = cum.reshape(bsz, N_HEADS, t).transpose(0, 2, 1)
    cum_row = cum.reshape(bsz, N_HEADS, t // blk, blk).transpose(0, 2, 1, 3)
    y_fx = _fox_prompt(b3(q_fx), b3(k_fx_h), b3(v_fx_h), cum_col, cum_row, blk)

    conv0 = jnp.zeros((bsz, SSM_CONV - 1, SSM_CONV_DIM), F32)
    ssm0 = jnp.zeros((bsz, SSM_INNER, SSM_STATE), F32)
    y_ssm, ssm_new, conv_new = _ssm(b3(z), b3(xbc), b3(small), conv0, ssm0, lw['conv_w'], lw['conv_b'],
                                    lw['dt_bias'], lw['a_log'], lw['d_skip'], lw['ssm_norm_g'], SSM_CHUNK)
    x_new = _merge_ffn(x.reshape(n, d), y_sb.reshape(n, -1), y_ssm.reshape(n, -1), y_fx.reshape(n, -1),
                       gates, lw, gf, final)
    from_t = lambda a: a.reshape(bsz, N_HEADS, HEAD_DIM, t).transpose(0, 3, 1, 2)
    states = (from_t(k_sb), from_t(v_sb), from_t(k_fx), from_t(v_fx), logf,
              ssm_new.reshape(bsz, SSM_HEADS, SSM_INNER // SSM_HEADS, SSM_STATE), conv_new)
    return x_new.reshape(bsz, t, d), states


def _sample_layer(x, l, lw, gf, final, caches, state_ssm, state_conv, page_table):
    bsz, t, d = x.shape
    n = bsz * t
    cache_sb_k, cache_sb_v, cache_fox_k, cache_fox_v, cache_lf_t = caches
    (q_sb, q_fx, k_sb, v_sb, k_fx, v_fx, k_sb_h, v_sb_h, k_fx_h, v_fx_h,
     z, xbc, small, gates) = _inproj(x.reshape(n, d), lw['norm1_g'], lw['w_main'], lw['w_small'])
    b3 = lambda a: a.reshape(bsz, t, a.shape[-1])
    q_rows = lambda q: jnp.tile(b3(q), (1, N_HEADS, 1))
    pad_keys = lambda a: jnp.pad(b3(a), ((0, 0), (0, PAGE_SIZE - t), (0, 0)))
    y_sb = _sb_sample(l, page_table, q_rows(q_sb), pad_keys(k_sb_h), pad_keys(v_sb_h),
                      cache_sb_k, cache_sb_v, t)

    f_new = small[:, :N_HEADS].reshape(bsz, t, N_HEADS).transpose(0, 2, 1)
    f_rows = jnp.pad(jnp.repeat(f_new, t, axis=1), ((0, 0), (0, 0), (0, PAGE_SIZE - t)))
    b_rows = jnp.broadcast_to(jnp.repeat(lw['b_forget'], t)[:, None], (N_HEADS * t, PAGE_SIZE))
    y_fx, lf_rows = _fox_sample(l, page_table, q_rows(q_fx), pad_keys(k_fx_h), pad_keys(v_fx_h),
                                f_rows, b_rows, cache_fox_k, cache_fox_v, cache_lf_t, t)
    logf = lf_rows[:, ::t, :t].transpose(0, 2, 1)

    pad_rows = lambda a: jnp.pad(b3(a), ((0, 0), (0, SSM_CHUNK - t), (0, 0)))
    y_ssm, ssm_new, conv_new = _ssm(pad_rows(z), pad_rows(xbc), pad_rows(small), state_conv[l],
                                    state_ssm[l].reshape(bsz, SSM_INNER, SSM_STATE),
                                    lw['conv_w'], lw['conv_b'], lw['dt_bias'], lw['a_log'],
                                    lw['d_skip'], lw['ssm_norm_g'], t)
    y_ssm = y_ssm[:, :t]
    x_new = _merge_ffn(x.reshape(n, d), y_sb.reshape(n, -1), y_ssm.reshape(n, -1), y_fx.reshape(n, -1),
                       gates, lw, gf, final)
    states = (_to_heads(k_sb, bsz, t), _to_heads(v_sb, bsz, t), _to_heads(k_fx, bsz, t),
              _to_heads(v_fx, bsz, t), logf,
              ssm_new.reshape(bsz, SSM_HEADS, SSM_INNER // SSM_HEADS, SSM_STATE), conv_new)
    return x_new.reshape(bsz, t, d), states


def kernel(x_prompt, x_sample, cache_sb_k, cache_sb_v, cache_fox_k, cache_fox_v, cache_fox_logf, state_ssm, state_conv, page_table, norm1_g, w_in, b_forget, conv_w, conv_b, dt_bias, a_log, d_skip, ssm_norm_g, w_sb_out, w_ssm_out, w_fox_out, w_o, norm2_g, w_up, w_down, final_norm_g):
    depth = w_in.shape[0]
    assert page_table.shape[1] % PAGES_PER_STEP == 0
    assert x_sample.shape[1] >= SSM_CONV - 1 and x_prompt.shape[1] % 256 == 0
    flat = lambda c: c.transpose(0, 1, 3, 4, 2).reshape(c.shape[0], c.shape[1], ATT_WIDTH, PAGE_SIZE)
    caches = (flat(cache_sb_k), flat(cache_sb_v), flat(cache_fox_k), flat(cache_fox_v),
              cache_fox_logf.transpose(0, 1, 3, 2))
    gf = final_norm_g.reshape(1, D_MODEL)
    xp, xs = x_prompt, x_sample
    prompt_states, sample_states = [], []
    for l in range(depth):
        lw = _layer_weights(l, norm1_g, w_in, b_forget, conv_w, conv_b, dt_bias, a_log, d_skip,
                            ssm_norm_g, w_sb_out, w_ssm_out, w_fox_out, w_o, norm2_g, w_up, w_down)
        final = l == depth - 1
        xp, st_p = _prompt_layer(xp, lw, gf, final)
        xs, st_s = _sample_layer(xs, l, lw, gf, final, caches, state_ssm, state_conv, page_table)
        prompt_states.append(st_p)
        sample_states.append(st_s)
    stacked_p = [jnp.stack(s) for s in zip(*prompt_states)]
    stacked_s = [jnp.stack(s) for s in zip(*sample_states)]
    return (xp, xs, *stacked_p, *stacked_s)
```

```python
import functools

import jax
import jax.numpy as jnp
from jax import lax
from jax.experimental import pallas as pl
from jax.experimental.pallas import tpu as pltpu

F32 = jnp.float32
BF16 = jnp.bfloat16

D_MODEL = 1024
HEAD_DIM = 64
N_HEADS = 4
ATT_WIDTH = N_HEADS * HEAD_DIM
SSM_HEADS = 8
SSM_INNER = 512
SSM_STATE = 128
SSM_GROUPS = 2
SSM_CONV = 4
SSM_CONV_DIM = 1024
SSM_CHUNK = 128
PAGE_SIZE = 128
N_BRANCH = 3
FFN_HIDDEN = 4 * D_MODEL
RMS_EPS = 1e-6
NEG_INF = -1e30
Q_SCALE = HEAD_DIM ** -0.5
LOG2_E = 1.4426950408889634

_OFF_F = 6 * ATT_WIDTH
_OFF_Z = _OFF_F + N_HEADS
_OFF_XBC = _OFF_Z + SSM_INNER
_OFF_DT = _OFF_XBC + SSM_CONV_DIM
_OFF_GATE = _OFF_DT + SSM_HEADS
IN_WIDTH = _OFF_GATE + N_BRANCH * D_MODEL
MAIN_WIDTH = 6 * ATT_WIDTH + SSM_INNER + SSM_CONV_DIM + N_BRANCH * D_MODEL
SMALL_WIDTH = 128
DT_LANE = N_HEADS

V7X_VMEM_LIMIT_BYTES = 56 * 1024 * 1024
PAGES_PER_STEP = 16
CUMSUM_SEG = 256


def _cparams(*sem):
    return pltpu.CompilerParams(dimension_semantics=sem, vmem_limit_bytes=V7X_VMEM_LIMIT_BYTES)


def _const_spec(shape):
    n = len(shape)
    return pl.BlockSpec(shape, lambda *_: (0,) * n, pipeline_mode=pl.Buffered(1))


def _rms(x, g):
    ms = jnp.mean(x * x, axis=-1, keepdims=True)
    return x * lax.rsqrt(ms + RMS_EPS) * g


def _softplus_tail(z):
    return jnp.log1p(jnp.exp(-jnp.abs(z)))


def _log_sigmoid(z):
    return jnp.minimum(z, 0.0) - _softplus_tail(z)


def _softplus(z):
    return jnp.maximum(z, 0.0) + _softplus_tail(z)


def _dot_nt(a, b):
    return lax.dot_general(a, b, (((1,), (1,)), ((), ())), preferred_element_type=F32)


def _dot(a, b):
    return jnp.dot(a, b, preferred_element_type=F32)


def _dot_exact(a, b):
    return jnp.dot(a, b, preferred_element_type=F32, precision=lax.Precision.HIGHEST)


def _inproj_kernel(x_ref, g_ref, wm_ref, ws_ref,
                   qsb_ref, qfx_ref, ksb_ref, vsb_ref, kfx_ref, vfx_ref,
                   ksbh_ref, vsbh_ref, kfxh_ref, vfxh_ref,
                   z_ref, xbc_ref, small_ref, gate_ref, *, kv_transposed):
    h = _rms(x_ref[...], g_ref[...]).astype(BF16)

    def mm(c0, width):
        return _dot_nt(h, wm_ref[c0:c0 + width, :])

    w = ATT_WIDTH
    qsb_ref[...] = (mm(0, w) * Q_SCALE).astype(BF16)
    for i, (full_ref, half_ref) in enumerate(((ksb_ref, ksbh_ref), (vsb_ref, vsbh_ref))):
        a = mm((1 + i) * w, w)
        full_ref[...] = a.T if kv_transposed else a
        half_ref[...] = a.astype(BF16)
    qfx_ref[...] = (mm(3 * w, w) * Q_SCALE).astype(BF16)
    for i, (full_ref, half_ref) in enumerate(((kfx_ref, kfxh_ref), (vfx_ref, vfxh_ref))):
        a = mm((4 + i) * w, w)
        full_ref[...] = a.T if kv_transposed else a
        half_ref[...] = a.astype(BF16)
    z_ref[...] = mm(6 * w, SSM_INNER)
    c0 = 6 * w + SSM_INNER
    for c in range(SSM_CONV_DIM // 512):
        xbc_ref[:, c * 512:(c + 1) * 512] = mm(c0 + c * 512, 512)
    c0 += SSM_CONV_DIM
    for c in range(N_BRANCH * D_MODEL // 512):
        gate_ref[:, c * 512:(c + 1) * 512] = jax.nn.sigmoid(mm(c0 + c * 512, 512)).astype(BF16)
    small_ref[...] = _dot_nt(h, ws_ref[...])


def _inproj(x2d, g, w_main, w_small, kv_seq=None):
    n = x2d.shape[0]
    tm = min(512, n)
    row = lambda width: pl.BlockSpec((tm, width), lambda i: (i, 0))
    rows = lambda width, dt: (row(width), jax.ShapeDtypeStruct((n, width), dt))
    if kv_seq is None:
        kv = rows(ATT_WIDTH, F32)
    else:
        bsz, t = kv_seq
        nt = t // tm
        kv = (pl.BlockSpec((None, ATT_WIDTH, tm), lambda i: (i // nt, 0, i % nt)),
              jax.ShapeDtypeStruct((bsz, ATT_WIDTH, t), F32))
    outs = ([rows(ATT_WIDTH, BF16)] * 2 + [kv] * 4 + [rows(ATT_WIDTH, BF16)] * 4
            + [rows(SSM_INNER, F32), rows(SSM_CONV_DIM, F32), rows(SMALL_WIDTH, F32),
               rows(N_BRANCH * D_MODEL, BF16)])
    return pl.pallas_call(
        functools.partial(_inproj_kernel, kv_transposed=kv_seq is not None),
        grid=(n // tm,),
        in_specs=[row(D_MODEL), _const_spec((1, D_MODEL)),
                  _const_spec((MAIN_WIDTH, D_MODEL)), _const_spec((SMALL_WIDTH, D_MODEL))],
        out_specs=[spec for spec, _ in outs],
        out_shape=[shape for _, shape in outs],
        compiler_params=_cparams("parallel"),
        name="inproj",
    )(x2d, g, w_main, w_small)


def _lane_head(width=ATT_WIDTH):
    return lax.broadcasted_iota(jnp.int32, (1, width), 1) // HEAD_DIM


def _suffix_matrix(seg, terms):
    r = lax.broadcasted_iota(jnp.int32, (terms * seg, seg), 0) % seg
    c = lax.broadcasted_iota(jnp.int32, (terms * seg, seg), 1)
    return jnp.where(r >= c, 1.0, 0.0).astype(BF16)


def _split_bf16(x, terms):
    out = []
    for _ in range(terms - 1):
        head = lax.bitcast_convert_type(
            lax.bitcast_convert_type(x, jnp.uint32) & jnp.uint32(0xFFFF0000), F32)
        out.append(head.astype(BF16))
        x = x - head
    out.append(x.astype(BF16))
    return jnp.concatenate(out, axis=1)


def _suffix_sums(x, u, carry):
    m, n = x.shape
    seg = u.shape[1]
    n_seg = n // seg
    terms = u.shape[0] // seg
    if n_seg == 1:
        cs = _dot(_split_bf16(x, terms), u)
        return cs + carry, carry + cs[:, 0:1]
    stacked = jnp.concatenate([x[:, s * seg:(s + 1) * seg] for s in range(n_seg)], axis=0)
    cs = _dot(_split_bf16(stacked, terms), u)
    parts = [None] * n_seg
    for s in reversed(range(n_seg)):
        part = cs[s * m:(s + 1) * m, :]
        parts[s] = part + carry
        carry = carry + part[:, 0:1]
    return jnp.concatenate(parts, axis=1), carry


def _stick_block(z, carry, u, mask):
    z2 = z * LOG2_E
    minus_abs = lax.bitcast_convert_type(
        lax.bitcast_convert_type(z2, jnp.uint32) | jnp.uint32(0x80000000), F32)
    drop = jnp.maximum(z2, 0.0) + jnp.log2(1.0 + jnp.exp2(minus_abs))
    if mask is not None:
        drop = jnp.where(mask, drop, 0.0)
    later, carry = _suffix_sums(drop, u, carry)
    w = jnp.exp2(z2 - later)
    if mask is not None:
        w = jnp.where(mask, w, 0.0)
    return w, carry


def _stack_heads(q):
    lane_head = _lane_head()
    return jnp.concatenate([jnp.where(lane_head == h, q, jnp.zeros_like(q)) for h in range(N_HEADS)], axis=0)


def _fold_heads(acc, rows):
    lane_head = _lane_head()
    out = jnp.zeros((rows, ATT_WIDTH), F32)
    for h in range(N_HEADS):
        out = jnp.where(lane_head == h, acc[h * rows:(h + 1) * rows, :], out)
    return out


def _sb_prompt_kernel(q_ref, k_ref, v_ref, o_ref, acc_ref, *, blk):
    i = pl.program_id(1)
    rows = N_HEADS * blk
    qs = _stack_heads(q_ref[...])
    u = _suffix_matrix(blk, 2)
    qpos = lax.broadcasted_iota(jnp.int32, (rows, blk), 0) % blk
    causal = lax.broadcasted_iota(jnp.int32, (rows, blk), 1) < qpos
    acc_ref[...] = jnp.zeros_like(acc_ref)

    def block(j, carry, mask):
        start = pl.multiple_of(j * blk, blk)
        w, carry = _stick_block(_dot_nt(qs, k_ref[pl.ds(start, blk), :]), carry, u, mask)
        acc_ref[...] += _dot(w.astype(BF16), v_ref[pl.ds(start, blk), :])
        return carry

    carry = block(i, jnp.zeros((rows, 1), F32), causal)
    lax.fori_loop(0, i, lambda jj, c: block(i - 1 - jj, c, None), carry)
    o_ref[...] = _fold_heads(acc_ref[...], blk).astype(o_ref.dtype)


def _sb_prompt(q, k, v, blk=256):
    b, t, w = q.shape
    qspec = pl.BlockSpec((None, blk, w), lambda bi, i: (bi, i, 0))
    kvspec = pl.BlockSpec((None, t, w), lambda bi, i: (bi, 0, 0))
    return pl.pallas_call(
        functools.partial(_sb_prompt_kernel, blk=blk),
        grid=(b, t // blk),
        in_specs=[qspec, kvspec, kvspec],
        out_specs=qspec,
        out_shape=jax.ShapeDtypeStruct((b, t, w), BF16),
        scratch_shapes=[pltpu.VMEM((N_HEADS * blk, w), F32)],
        compiler_params=_cparams("parallel", "arbitrary"),
        name="sb_prompt",
    )(q, k, v)


def _logf_cum_kernel(b_ref, f_ref, logf_ref, cum_ref):
    h = pl.program_id(0) % N_HEADS
    logf = _log_sigmoid(f_ref[...] + b_ref[h])
    logf_ref[...] = logf
    rows = logf.shape[0]
    r = lax.broadcasted_iota(jnp.int32, (PAGE_SIZE, PAGE_SIZE), 0)
    c = lax.broadcasted_iota(jnp.int32, (PAGE_SIZE, PAGE_SIZE), 1)
    within = _dot_exact(logf, jnp.where(r <= c, 1.0, 0.0).astype(F32))
    totals = jnp.broadcast_to(within[:, PAGE_SIZE - 1:PAGE_SIZE], within.shape)
    rr = lax.broadcasted_iota(jnp.int32, (rows, rows), 0)
    cc = lax.broadcasted_iota(jnp.int32, (rows, rows), 1)
    cum_ref[...] = within + _dot_exact(jnp.where(cc < rr, 1.0, 0.0).astype(F32), totals)


def _logf_cum(f_rows, b_forget):
    n, rows, lanes = f_rows.shape
    spec = pl.BlockSpec((None, rows, lanes), lambda i: (i, 0, 0))
    return pl.pallas_call(
        _logf_cum_kernel,
        grid=(n,),
        in_specs=[pl.BlockSpec(memory_space=pltpu.SMEM), spec],
        out_specs=[spec, spec],
        out_shape=[jax.ShapeDtypeStruct(f_rows.shape, F32)] * 2,
        compiler_params=_cparams("parallel"),
        name="logf_cum",
    )(b_forget, f_rows)


def _fox_prompt_kernel(q_ref, k_ref, v_ref, cq_ref, ck_ref, o_ref, acc_ref, *, blk):
    i = pl.program_id(1)
    rows = N_HEADS * blk
    qs = _stack_heads(q_ref[...])
    cq = cq_ref[...]
    cqs = jnp.concatenate([cq[:, h:h + 1] for h in range(N_HEADS)], axis=0)
    qpos = lax.broadcasted_iota(jnp.int32, (rows, blk), 0) % blk
    causal = lax.broadcasted_iota(jnp.int32, (rows, blk), 1) <= qpos
    acc_ref[...] = jnp.zeros_like(acc_ref)

    def block(j, state, mask):
        m_old, l_old = state
        start = pl.multiple_of(j * blk, blk)
        s = _dot_nt(qs, k_ref[pl.ds(start, blk), :]) + cqs
        ck = ck_ref[j]
        s = jnp.concatenate([s[h * blk:(h + 1) * blk, :] - ck[h:h + 1, :] for h in range(N_HEADS)], axis=0)
        if mask is not None:
            s = jnp.where(mask, s, NEG_INF)
        m = jnp.maximum(m_old, jnp.max(s, axis=-1, keepdims=True))
        alpha = jnp.exp(m_old - m)
        p = jnp.exp(s - m)
        acc_ref[...] = acc_ref[...] * alpha + _dot(p.astype(BF16), v_ref[pl.ds(start, blk), :])
        return m, alpha * l_old + jnp.sum(p, axis=-1, keepdims=True)

    state = block(i, (jnp.full((rows, 1), NEG_INF, F32), jnp.zeros((rows, 1), F32)), causal)
    _, l = lax.fori_loop(0, i, lambda jj, st: block(i - 1 - jj, st, None), state)
    o_ref[...] = _fold_heads(acc_ref[...] / l, blk).astype(o_ref.dtype)


def _fox_prompt(q, k, v, cum_col, cum_row, blk=256):
    b, t, w = q.shape
    qspec = pl.BlockSpec((None, blk, w), lambda bi, i: (bi, i, 0))
    kvspec = pl.BlockSpec((None, t, w), lambda bi, i: (bi, 0, 0))
    return pl.pallas_call(
        functools.partial(_fox_prompt_kernel, blk=blk),
        grid=(b, t // blk),
        in_specs=[qspec, kvspec, kvspec,
                  pl.BlockSpec((None, blk, N_HEADS), lambda bi, i: (bi, i, 0)),
                  pl.BlockSpec((None, t // blk, N_HEADS, blk), lambda bi, i: (bi, 0, 0, 0))],
        out_specs=qspec,
        out_shape=jax.ShapeDtypeStruct((b, t, w), BF16),
        scratch_shapes=[pltpu.VMEM((N_HEADS * blk, w), F32)],
        compiler_params=_cparams("parallel", "arbitrary"),
        name="fox_prompt",
    )(q, k, v, cum_col, cum_row)


_XBUF_ROW0 = 8


def _ssm_kernel(z_ref, xbc_ref, small_ref, conv0_ref, state0_ref,
                cw_ref, cb_ref, dtb_ref, alog_ref, dskip_ref, ng_ref,
                y_ref, state_out_ref, conv_out_ref, xbuf_ref, state_ref, *, valid):
    c = pl.program_id(1)
    last = pl.num_programs(1) - 1
    L = SSM_CHUNK
    r0 = _XBUF_ROW0

    @pl.when(c == 0)
    def _():
        xbuf_ref[r0 - 3:r0, :] = conv0_ref[...]
        state_ref[...] = state0_ref[...]

    x_cur = xbc_ref[...]
    xbuf_ref[r0:r0 + L, :] = x_cur
    cw = cw_ref[...]
    conv = cb_ref[...] + x_cur * cw[3:4, :]
    for i in range(SSM_CONV - 1):
        conv = conv + xbuf_ref[r0 - 3 + i:r0 - 3 + i + L, :] * cw[i:i + 1, :]
    xbuf_ref[r0 - 3:r0, :] = x_cur[L - 3:L, :]
    act = conv * jax.nn.sigmoid(conv)
    xs = act[:, :SSM_INNER]
    b_in = act[:, SSM_INNER:SSM_INNER + SSM_GROUPS * SSM_STATE].astype(BF16)
    c_in = act[:, SSM_INNER + SSM_GROUPS * SSM_STATE:].astype(BF16)

    row = lax.broadcasted_iota(jnp.int32, (L, L), 0)
    col = lax.broadcasted_iota(jnp.int32, (L, L), 1)
    tri = row >= col
    dt = _softplus(small_ref[...] + dtb_ref[...])
    if valid < L:
        dt = jnp.where(row < valid, dt, 0.0)
    d_a = dt * (-jnp.exp(alog_ref[...]))
    a_cs = _dot_exact(jnp.where(tri, 1.0, 0.0).astype(F32), d_a)
    a_cs_t = a_cs.T
    a_last = a_cs[L - 1:L, :]
    e_cs = jnp.exp(a_cs)
    wgt = jnp.exp(a_last - a_cs) * dt
    chunk_dec = jnp.exp(a_last)

    half = lax.broadcasted_iota(jnp.int32, (1, 128), 1) // SSM_STATE_HALF
    rhalf = lax.broadcasted_iota(jnp.int32, (128, 1), 0) // SSM_STATE_HALF
    pair_cols = lambda a, p: jnp.where(half == 0, a[:, DT_LANE + 2 * p:DT_LANE + 2 * p + 1],
                                       a[:, DT_LANE + 2 * p + 1:DT_LANE + 2 * p + 2])
    scores = [_dot_nt(c_in[:, g * SSM_STATE:(g + 1) * SSM_STATE],
                      b_in[:, g * SSM_STATE:(g + 1) * SSM_STATE]) for g in range(SSM_GROUPS)]
    ys = []
    for p in range(SSM_HEADS // 2):
        g = (2 * p) // (SSM_HEADS // SSM_GROUPS)
        bg = b_in[:, g * SSM_STATE:(g + 1) * SSM_STATE]
        cg = c_in[:, g * SSM_STATE:(g + 1) * SSM_STATE]
        xs_p = xs[:, 128 * p:128 * (p + 1)]
        xdt = (xs_p * pair_cols(dt, p)).astype(BF16)
        y_diag = jnp.zeros((L, 128), F32)
        for hh in range(2):
            lane = DT_LANE + 2 * p + hh
            seg = a_cs[:, lane:lane + 1] - a_cs_t[lane:lane + 1, :]
            decay = jnp.exp(jnp.where(tri, seg, NEG_INF))
            y_h = _dot((scores[g] * decay).astype(BF16), xdt)
            y_diag = jnp.where(half == hh, y_h, y_diag)
        st = state_ref[128 * p:128 * (p + 1), :]
        y_off = _dot_nt(cg, st.astype(BF16)) * pair_cols(e_cs, p)
        ys.append(y_diag + y_off + dskip_ref[:, 128 * p:128 * (p + 1)] * xs_p)
        xw_t = (xs_p * pair_cols(wgt, p)).T.astype(BF16)
        lane = DT_LANE + 2 * p
        dec = jnp.where(rhalf == 0,
                        jnp.broadcast_to(chunk_dec[:, lane:lane + 1], (128, SSM_STATE)),
                        jnp.broadcast_to(chunk_dec[:, lane + 1:lane + 2], (128, SSM_STATE)))
        state_ref[128 * p:128 * (p + 1), :] = st * dec + _dot(xw_t, bg)

    z = z_ref[...]
    y = jnp.concatenate(ys, axis=1) * (z * jax.nn.sigmoid(z))
    gw = SSM_INNER // SSM_GROUPS
    parts = []
    for g in range(SSM_GROUPS):
        yg = y[:, g * gw:(g + 1) * gw]
        parts.append(yg * lax.rsqrt(jnp.mean(yg * yg, axis=-1, keepdims=True) + RMS_EPS))
    y_ref[...] = (jnp.concatenate(parts, axis=1) * ng_ref[...]).astype(y_ref.dtype)

    @pl.when(c == last)
    def _():
        state_out_ref[...] = state_ref[...]
        conv_out_ref[...] = x_cur[valid - 3:valid, :]


SSM_STATE_HALF = 64


def _ssm(z, xbc, small, conv0, state0, cw, cb, dtb, alog, dskip, ng, valid):
    b, t, _ = z.shape
    nc = t // SSM_CHUNK
    chunk = lambda width: pl.BlockSpec((None, SSM_CHUNK, width), lambda bi, c: (bi, c, 0))
    per_b = lambda shape: pl.BlockSpec((None,) + shape, lambda bi, c: (bi,) + (0,) * len(shape))
    const = lambda shape: pl.BlockSpec(shape, lambda bi, c: (0,) * len(shape))
    state_rows = SSM_INNER
    return pl.pallas_call(
        functools.partial(_ssm_kernel, valid=valid),
        grid=(b, nc),
        in_specs=[chunk(SSM_INNER), chunk(SSM_CONV_DIM), chunk(SMALL_WIDTH),
                  per_b((SSM_CONV - 1, SSM_CONV_DIM)), per_b((state_rows, SSM_STATE)),
                  const((SSM_CONV, SSM_CONV_DIM)), const((1, SSM_CONV_DIM)),
                  const((1, SMALL_WIDTH)), const((1, SMALL_WIDTH)),
                  const((1, SSM_INNER)), const((1, SSM_INNER))],
        out_specs=[chunk(SSM_INNER), per_b((state_rows, SSM_STATE)),
                   per_b((SSM_CONV - 1, SSM_CONV_DIM))],
        out_shape=[jax.ShapeDtypeStruct((b, t, SSM_INNER), BF16),
                   jax.ShapeDtypeStruct((b, state_rows, SSM_STATE), F32),
                   jax.ShapeDtypeStruct((b, SSM_CONV - 1, SSM_CONV_DIM), F32)],
        scratch_shapes=[pltpu.VMEM((_XBUF_ROW0 + SSM_CHUNK, SSM_CONV_DIM), F32),
                        pltpu.VMEM((state_rows, SSM_STATE), F32)],
        compiler_params=_cparams("parallel", "arbitrary"),
        name="ssm",
    )(z, xbc, small, conv0, state0, cw, cb, dtb, alog, dskip, ng)


def _merge_ffn_kernel(x_ref, ysb_ref, yssm_ref, yfx_ref, gate_ref,
                      wsb_ref, wssm_ref, wfx_ref, wo_ref, g2_ref, wup_ref, wdn_ref, gf_ref,
                      o_ref, *, final):
    d = D_MODEL
    mixed = gate_ref[:, 0:d].astype(F32) * _dot(ysb_ref[...].astype(BF16), wsb_ref[...])
    mixed = mixed + gate_ref[:, d:2 * d].astype(F32) * _dot(yssm_ref[...].astype(BF16), wssm_ref[...])
    mixed = mixed + gate_ref[:, 2 * d:3 * d].astype(F32) * _dot(yfx_ref[...].astype(BF16), wfx_ref[...])
    x = x_ref[...] + _dot(mixed.astype(BF16), wo_ref[...])
    h = _rms(x, g2_ref[...]).astype(BF16)
    hc = FFN_HIDDEN // 2
    for c in range(2):
        u = jnp.maximum(_dot(h, wup_ref[:, c * hc:(c + 1) * hc]), 0.0)
        x = x + _dot((u * u).astype(BF16), wdn_ref[c * hc:(c + 1) * hc, :])
    if final:
        x = _rms(x, gf_ref[...])
    o_ref[...] = x


def _merge_ffn(x2d, y_sb, y_ssm, y_fx, gates, lw, gf, final):
    n = x2d.shape[0]
    tm = min(512, n)
    row = lambda width: pl.BlockSpec((tm, width), lambda i: (i, 0))
    weights = [lw['w_sb_out'], lw['w_ssm_out'], lw['w_fox_out'], lw['w_o'], lw['norm2_g'],
               lw['w_up'], lw['w_down'], gf]
    return pl.pallas_call(
        functools.partial(_merge_ffn_kernel, final=final),
        grid=(n // tm,),
        in_specs=[row(D_MODEL), row(ATT_WIDTH), row(SSM_INNER), row(ATT_WIDTH),
                  row(N_BRANCH * D_MODEL)] + [_const_spec(w.shape) for w in weights],
        out_specs=row(D_MODEL),
        out_shape=jax.ShapeDtypeStruct((n, D_MODEL), F32),
        compiler_params=_cparams("parallel"),
        name="merge_ffn",
    )(x2d, y_sb, y_ssm, y_fx, gates, *weights)


def _block_diag_q(q_ref):
    q = q_ref[...]
    rows = q.shape[0]
    row_head = lax.broadcasted_iota(jnp.int32, (rows, 1), 0) // (rows // N_HEADS)
    return jnp.where(row_head == _lane_head(), q, jnp.zeros_like(q))


def _page_copy(cache_ref, stage_ref, sem_ref, layer, page, slot, p):
    return pltpu.make_async_copy(cache_ref.at[layer, page], stage_ref.at[slot, p], sem_ref.at[slot])


def _prefetch_pages(pt_ref, caches, stages, sems, layer):
    n_pages = stages[0].shape[1]
    b, c = pl.program_id(0), pl.program_id(1)
    n_chunks = pl.num_programs(1)
    step = b * n_chunks + c
    slot = step % 2

    def start(bb, cc, sl):
        base = (n_chunks - 1 - cc) * n_pages
        for p in range(n_pages):
            page = pt_ref[bb, base + p]
            for cache, stage, sem in zip(caches, stages, sems):
                _page_copy(cache, stage, sem, layer, page, sl, p).start()

    @pl.when(step == 0)
    def _():
        start(b, c, slot)

    @pl.when(step + 1 < pl.num_programs(0) * n_chunks)
    def _():
        wrap = c + 1 == n_chunks
        start(jnp.where(wrap, b + 1, b), jnp.where(wrap, 0, c + 1), 1 - slot)

    for p in range(n_pages):
        for cache, stage, sem in zip(caches, stages, sems):
            _page_copy(cache, stage, sem, layer, 0, slot, p).wait()
    return slot


def _pages_to_bf16(stage_ref, slot, buf_ref):
    for p in range(stage_ref.shape[1]):
        buf_ref[:, p * PAGE_SIZE:(p + 1) * PAGE_SIZE] = stage_ref[slot, p].astype(BF16)


def _sb_sample_kernel(pt_ref, q_ref, kn_ref, vn_ref, ck_ref, cv_ref, o_ref,
                      acc_ref, carry_ref, kstage_ref, vstage_ref, ksem, vsem, kbuf_ref, vbuf_ref,
                      *, steps, layer):
    slot = _prefetch_pages(pt_ref, (ck_ref, cv_ref), (kstage_ref, vstage_ref), (ksem, vsem), layer)
    c = pl.program_id(1)
    rows = N_HEADS * steps
    qbd = _block_diag_q(q_ref)
    u = _suffix_matrix(CUMSUM_SEG, 2)

    @pl.when(c == 0)
    def _():
        step = lax.broadcasted_iota(jnp.int32, (rows, PAGE_SIZE), 0) % steps
        col = lax.broadcasted_iota(jnp.int32, (rows, PAGE_SIZE), 1)
        w, carry = _stick_block(_dot_nt(qbd, kn_ref[...]), jnp.zeros((rows, 1), F32),
                                _suffix_matrix(PAGE_SIZE, 2), col < step)
        acc_ref[...] = _dot(w.astype(BF16), vn_ref[...])
        carry_ref[...] = jnp.broadcast_to(carry, carry_ref.shape)

    _pages_to_bf16(kstage_ref, slot, kbuf_ref)
    _pages_to_bf16(vstage_ref, slot, vbuf_ref)
    w, carry = _stick_block(_dot(qbd, kbuf_ref[...]), carry_ref[:, 0:1], u, None)
    acc_ref[...] += _dot_nt(w.astype(BF16), vbuf_ref[...])
    carry_ref[...] = jnp.broadcast_to(carry, carry_ref.shape)

    @pl.when(c == pl.num_programs(1) - 1)
    def _():
        o_ref[...] = _fold_heads(acc_ref[...], steps)


def _page_staging(n_pages, rows, lanes):
    return [pltpu.VMEM((2, n_pages, rows, lanes), F32)], [pltpu.SemaphoreType.DMA((2,))]


_HBM_SPEC = pl.BlockSpec(memory_space=pl.ANY)


def _sb_sample(layer, page_table, q_rows, k_new, v_new, cache_k, cache_v, steps):
    b, rows, w = q_rows.shape
    n_pages = PAGES_PER_STEP
    n_chunks = page_table.shape[1] // n_pages
    per_b = lambda r: pl.BlockSpec((None, r, w), lambda bi, c, pt: (bi, 0, 0))
    stage, sem = _page_staging(n_pages, w, PAGE_SIZE)
    grid_spec = pltpu.PrefetchScalarGridSpec(
        num_scalar_prefetch=1,
        grid=(b, n_chunks),
        in_specs=[per_b(rows), per_b(PAGE_SIZE), per_b(PAGE_SIZE), _HBM_SPEC, _HBM_SPEC],
        out_specs=per_b(steps),
        scratch_shapes=[pltpu.VMEM((rows, w), F32), pltpu.VMEM((rows, 128), F32)]
                       + stage * 2 + sem * 2
                       + [pltpu.VMEM((w, n_pages * PAGE_SIZE), BF16)] * 2,
    )
    return pl.pallas_call(
        functools.partial(_sb_sample_kernel, steps=steps, layer=layer),
        grid_spec=grid_spec,
        out_shape=jax.ShapeDtypeStruct((b, steps, w), F32),
        compiler_params=_cparams("arbitrary", "arbitrary"),
        name="sb_sample",
    )(page_table, q_rows, k_new, v_new, cache_k, cache_v)


def _fox_sample_kernel(pt_ref, q_ref, kn_ref, vn_ref, f_ref, b_ref, ck_ref, cv_ref, clf_ref,
                       o_ref, lf_out_ref, acc_ref, m_ref, l_ref, ncum_ref, rcarry_ref,
                       kstage_ref, vstage_ref, lstage_ref, ksem, vsem, lsem,
                       kbuf_ref, vbuf_ref, lbuf_ref, *, steps, layer):
    slot = _prefetch_pages(pt_ref, (ck_ref, cv_ref, clf_ref), (kstage_ref, vstage_ref, lstage_ref),
                           (ksem, vsem, lsem), layer)
    c = pl.program_id(1)
    rows = N_HEADS * steps
    qbd = _block_diag_q(q_ref)
    bcast = lambda a, ref: jnp.broadcast_to(a, ref.shape)

    @pl.when(c == 0)
    def _():
        lf = _log_sigmoid(f_ref[...] + b_ref[...])
        lf_out_ref[...] = lf
        step = lax.broadcasted_iota(jnp.int32, (rows, PAGE_SIZE), 0) % steps
        col = lax.broadcasted_iota(jnp.int32, (rows, PAGE_SIZE), 1)
        lf = jnp.where(col < steps, lf, 0.0)
        cum = lf
        shift = 1
        while shift < steps:
            cum = cum + jnp.where(col >= shift, pltpu.roll(cum, shift, 1), 0.0)
            shift *= 2
        visible = col <= step
        ncum = jnp.sum(jnp.where(visible, lf, 0.0), axis=-1, keepdims=True)
        s = jnp.where(visible, _dot_nt(qbd, kn_ref[...]) + ncum - cum, NEG_INF)
        m = jnp.max(s, axis=-1, keepdims=True)
        p = jnp.exp(s - m)
        m_ref[...] = bcast(m, m_ref)
        l_ref[...] = bcast(jnp.sum(p, axis=-1, keepdims=True), l_ref)
        acc_ref[...] = _dot(p.astype(BF16), vn_ref[...])
        ncum_ref[...] = bcast(ncum, ncum_ref)
        rcarry_ref[...] = jnp.zeros_like(rcarry_ref)
        lbuf_ref[...] = jnp.zeros_like(lbuf_ref)

    _pages_to_bf16(kstage_ref, slot, kbuf_ref)
    _pages_to_bf16(vstage_ref, slot, vbuf_ref)
    for p in range(lstage_ref.shape[1]):
        lbuf_ref[0:N_HEADS, p * PAGE_SIZE:(p + 1) * PAGE_SIZE] = lstage_ref[slot, p]
    lf_pages = lbuf_ref[...]
    incl, carry = _suffix_sums(lf_pages, _suffix_matrix(CUMSUM_SEG, 3), rcarry_ref[:, 0:1])
    suffix = incl - lf_pages
    rcarry_ref[...] = bcast(carry, rcarry_ref)
    bias = jnp.concatenate(
        [jnp.broadcast_to(suffix[h:h + 1, :], (steps, suffix.shape[1])) for h in range(N_HEADS)], axis=0)
    s = _dot(qbd, kbuf_ref[...]) + bias + ncum_ref[:, 0:1]
    m_old = m_ref[:, 0:1]
    m = jnp.maximum(m_old, jnp.max(s, axis=-1, keepdims=True))
    alpha = jnp.exp(m_old - m)
    p = jnp.exp(s - m)
    l = alpha * l_ref[:, 0:1] + jnp.sum(p, axis=-1, keepdims=True)
    acc = acc_ref[...] * alpha + _dot_nt(p.astype(BF16), vbuf_ref[...])
    m_ref[...] = bcast(m, m_ref)
    l_ref[...] = bcast(l, l_ref)
    acc_ref[...] = acc

    @pl.when(c == pl.num_programs(1) - 1)
    def _():
        o_ref[...] = _fold_heads(acc / l, steps)


def _fox_sample(layer, page_table, q_rows, k_new, v_new, f_rows, b_rows,
                cache_k, cache_v, cache_lf_t, steps):
    b, rows, w = q_rows.shape
    n_pages = PAGES_PER_STEP
    n_chunks = page_table.shape[1] // n_pages
    per_b = lambda r, width=w: pl.BlockSpec((None, r, width), lambda bi, c, pt: (bi, 0, 0))
    stat = pltpu.VMEM((rows, 128), F32)
    kv_stage, kv_sem = _page_staging(n_pages, w, PAGE_SIZE)
    lf_stage, lf_sem = _page_staging(n_pages, N_HEADS, PAGE_SIZE)
    grid_spec = pltpu.PrefetchScalarGridSpec(
        num_scalar_prefetch=1,
        grid=(b, n_chunks),
        in_specs=[per_b(rows), per_b(PAGE_SIZE), per_b(PAGE_SIZE), per_b(rows, PAGE_SIZE),
                  pl.BlockSpec((rows, PAGE_SIZE), lambda bi, c, pt: (0, 0)),
                  _HBM_SPEC, _HBM_SPEC, _HBM_SPEC],
        out_specs=[per_b(steps), per_b(rows, PAGE_SIZE)],
        scratch_shapes=[pltpu.VMEM((rows, w), F32), stat, stat, stat, pltpu.VMEM((8, 128), F32)]
                       + kv_stage * 2 + lf_stage + kv_sem * 2 + lf_sem
                       + [pltpu.VMEM((w, n_pages * PAGE_SIZE), BF16)] * 2
                       + [pltpu.VMEM((8, n_pages * PAGE_SIZE), F32)],
    )
    return pl.pallas_call(
        functools.partial(_fox_sample_kernel, steps=steps, layer=layer),
        grid_spec=grid_spec,
        out_shape=[jax.ShapeDtypeStruct((b, steps, w), F32),
                   jax.ShapeDtypeStruct((b, rows, PAGE_SIZE), F32)],
        compiler_params=_cparams("arbitrary", "arbitrary"),
        name="fox_sample",
    )(page_table, q_rows, k_new, v_new, f_rows, b_rows, cache_k, cache_v, cache_lf_t)


def _layer_weights(l, norm1_g, w_in, b_forget, conv_w, conv_b, dt_bias, a_log, d_skip, ssm_norm_g,
                   w_sb_out, w_ssm_out, w_fox_out, w_o, norm2_g, w_up, w_down):
    w = jnp.transpose(w_in, (2, 0, 1))[:, l, :]
    w_main = jnp.concatenate([w[:_OFF_F], w[_OFF_Z:_OFF_DT], w[_OFF_GATE:]], axis=0).astype(BF16)
    w_small = jnp.concatenate([w[_OFF_F:_OFF_Z], w[_OFF_DT:_OFF_GATE]], axis=0)
    w_small = jnp.pad(w_small, ((0, SMALL_WIDTH - w_small.shape[0]), (0, 0))).astype(BF16)
    pad_dt = lambda a: jnp.pad(a, (DT_LANE, SMALL_WIDTH - DT_LANE - SSM_HEADS)).reshape(1, SMALL_WIDTH)
    return {
        'norm1_g': norm1_g[l].reshape(1, D_MODEL), 'w_main': w_main, 'w_small': w_small,
        'b_forget': b_forget[l], 'conv_w': conv_w[l], 'conv_b': conv_b[l].reshape(1, SSM_CONV_DIM),
        'dt_bias': pad_dt(dt_bias[l]), 'a_log': pad_dt(a_log[l]),
        'd_skip': jnp.repeat(d_skip[l], SSM_INNER // SSM_HEADS).reshape(1, SSM_INNER),
        'ssm_norm_g': ssm_norm_g[l].reshape(1, SSM_INNER),
        'w_sb_out': w_sb_out[l].astype(BF16), 'w_ssm_out': w_ssm_out[l].astype(BF16),
        'w_fox_out': w_fox_out[l].astype(BF16), 'w_o': w_o[l].astype(BF16),
        'norm2_g': norm2_g[l].reshape(1, D_MODEL),
        'w_up': w_up[l].astype(BF16), 'w_down': w_down[l].astype(BF16),
    }


def _to_heads(a, bsz, t):
    return a.reshape(bsz, t, N_HEADS, HEAD_DIM)


def _prompt_layer(x, lw, gf, final):
    bsz, t, d = x.shape
    n = bsz * t
    (q_sb, q_fx, k_sb, v_sb, k_fx, v_fx, k_sb_h, v_sb_h, k_fx_h, v_fx_h,
     z, xbc, small, gates) = _inproj(x.reshape(n, d), lw['norm1_g'], lw['w_main'], lw['w_small'], (bsz, t))
    b3 = lambda a: a.reshape(bsz, t, a.shape[-1])
    y_sb = _sb_prompt(b3(q_sb), b3(k_sb_h), b3(v_sb_h))

    f_rows = small[:, :N_HEADS].reshape(bsz, t, N_HEADS).transpose(0, 2, 1)
    f_rows = f_rows.reshape(bsz * N_HEADS, t // PAGE_SIZE, PAGE_SIZE)
    logf, cum = _logf_cum(f_rows, lw['b_forget'])
    logf = logf.reshape(bsz, N_HEADS, t).transpose(0, 2, 1)
    blk = 256
    cum_col = cum.reshape(bsz, N_HEADS, t).transpose(0, 2, 1)
    cum_row = cum.reshape(bsz, N_HEADS, t // blk, blk).transpose(0, 2, 1, 3)
    y_fx = _fox_prompt(b3(q_fx), b3(k_fx_h), b3(v_fx_h), cum_col, cum_row, blk)

    conv0 = jnp.zeros((bsz, SSM_CONV - 1, SSM_CONV_DIM), F32)
    ssm0 = jnp.zeros((bsz, SSM_INNER, SSM_STATE), F32)
    y_ssm, ssm_new, conv_new = _ssm(b3(z), b3(xbc), b3(small), conv0, ssm0, lw['conv_w'], lw['conv_b'],
                                    lw['dt_bias'], lw['a_log'], lw['d_skip'], lw['ssm_norm_g'], SSM_CHUNK)
    x_new = _merge_ffn(x.reshape(n, d), y_sb.reshape(n, -1), y_ssm.reshape(n, -1), y_fx.reshape(n, -1),
                       gates, lw, gf, final)
    from_t = lambda a: a.reshape(bsz, N_HEADS, HEAD_DIM, t).transpose(0, 3, 1, 2)
    states = (from_t(k_sb), from_t(v_sb), from_t(k_fx), from_t(v_fx), logf,
              ssm_new.reshape(bsz, SSM_HEADS, SSM_INNER // SSM_HEADS, SSM_STATE), conv_new)
    return x_new.reshape(bsz, t, d), states


def _sample_layer(x, l, lw, gf, final, caches, state_ssm, state_conv, page_table):
    bsz, t, d = x.shape
    n = bsz * t
    cache_sb_k, cache_sb_v, cache_fox_k, cache_fox_v, cache_lf_t = caches
    (q_sb, q_fx, k_sb, v_sb, k_fx, v_fx, k_sb_h, v_sb_h, k_fx_h, v_fx_h,
     z, xbc, small, gates) = _inproj(x.reshape(n, d), lw['norm1_g'], lw['w_main'], lw['w_small'])
    b3 = lambda a: a.reshape(bsz, t, a.shape[-1])
    q_rows = lambda q: jnp.tile(b3(q), (1, N_HEADS, 1))
    pad_keys = lambda a: jnp.pad(b3(a), ((0, 0), (0, PAGE_SIZE - t), (0, 0)))
    y_sb = _sb_sample(l, page_table, q_rows(q_sb), pad_keys(k_sb_h), pad_keys(v_sb_h),
                      cache_sb_k, cache_sb_v, t)

    f_new = small[:, :N_HEADS].reshape(bsz, t, N_HEADS).transpose(0, 2, 1)
    f_rows = jnp.pad(jnp.repeat(f_new, t, axis=1), ((0, 0), (0, 0), (0, PAGE_SIZE - t)))
    b_rows = jnp.broadcast_to(jnp.repeat(lw['b_forget'], t)[:, None], (N_HEADS * t, PAGE_SIZE))
    y_fx, lf_rows = _fox_sample(l, page_table, q_rows(q_fx), pad_keys(k_fx_h), pad_keys(v_fx_h),
                                f_rows, b_rows, cache_fox_k, cache_fox_v, cache_lf_t, t)
    logf = lf_rows[:, ::t, :t].transpose(0, 2, 1)

    pad_rows = lambda a: jnp.pad(b3(a), ((0, 0), (0, SSM_CHUNK - t), (0, 0)))
    y_ssm, ssm_new, conv_new = _ssm(pad_rows(z), pad_rows(xbc), pad_rows(small), state_conv[l],
                                    state_ssm[l].reshape(bsz, SSM_INNER, SSM_STATE),
                                    lw['conv_w'], lw['conv_b'], lw['dt_bias'], lw['a_log'],
                                    lw['d_skip'], lw['ssm_norm_g'], t)
    y_ssm = y_ssm[:, :t]
    x_new = _merge_ffn(x.reshape(n, d), y_sb.reshape(n, -1), y_ssm.reshape(n, -1), y_fx.reshape(n, -1),
                       gates, lw, gf, final)
    states = (_to_heads(k_sb, bsz, t), _to_heads(v_sb, bsz, t), _to_heads(k_fx, bsz, t),
              _to_heads(v_fx, bsz, t), logf,
              ssm_new.reshape(bsz, SSM_HEADS, SSM_INNER // SSM_HEADS, SSM_STATE), conv_new)
    return x_new.reshape(bsz, t, d), states


def kernel(x_prompt, x_sample, cache_sb_k, cache_sb_v, cache_fox_k, cache_fox_v, cache_fox_logf, state_ssm, state_conv, page_table, norm1_g, w_in, b_forget, conv_w, conv_b, dt_bias, a_log, d_skip, ssm_norm_g, w_sb_out, w_ssm_out, w_fox_out, w_o, norm2_g, w_up, w_down, final_norm_g):
    depth = w_in.shape[0]
    assert page_table.shape[1] % PAGES_PER_STEP == 0
    assert x_sample.shape[1] >= SSM_CONV - 1 and x_prompt.shape[1] % 256 == 0
    flat = lambda c: c.transpose(0, 1, 3, 4, 2).reshape(c.shape[0], c.shape[1], ATT_WIDTH, PAGE_SIZE)
    caches = (flat(cache_sb_k), flat(cache_sb_v), flat(cache_fox_k), flat(cache_fox_v),
              cache_fox_logf.transpose(0, 1, 3, 2))
    gf = final_norm_g.reshape(1, D_MODEL)
    xp, xs = x_prompt, x_sample
    prompt_states, sample_states = [], []
    for l in range(depth):
        lw = _layer_weights(l, norm1_g, w_in, b_forget, conv_w, conv_b, dt_bias, a_log, d_skip,
                            ssm_norm_g, w_sb_out, w_ssm_out, w_fox_out, w_o, norm2_g, w_up, w_down)
        final = l == depth - 1
        xp, st_p = _prompt_layer(xp, lw, gf, final)
        xs, st_s = _sample_layer(xs, l, lw, gf, final, caches, state_ssm, state_conv, page_table)
        prompt_states.append(st_p)
        sample_states.append(st_s)
    stacked_p = [jnp.stack(s) for s in zip(*prompt_states)]
    stacked_s = [jnp.stack(s) for s in zip(*sample_states)]
    return (xp, xs, *stacked_p, *stacked_s)
```

```python
import functools

import jax
import jax.numpy as jnp
from jax import lax
from jax.experimental import pallas as pl
from jax.experimental.pallas import tpu as pltpu

F32 = jnp.float32
BF16 = jnp.bfloat16

D_MODEL = 1024
HEAD_DIM = 64
N_HEADS = 4
ATT_WIDTH = N_HEADS * HEAD_DIM
SSM_HEADS = 8
SSM_INNER = 512
SSM_STATE = 128
SSM_GROUPS = 2
SSM_CONV = 4
SSM_CONV_DIM = 1024
SSM_CHUNK = 128
PAGE_SIZE = 128
N_BRANCH = 3
FFN_HIDDEN = 4 * D_MODEL
RMS_EPS = 1e-6
NEG_INF = -1e30
Q_SCALE = HEAD_DIM ** -0.5
LOG2_E = 1.4426950408889634

_OFF_F = 6 * ATT_WIDTH
_OFF_Z = _OFF_F + N_HEADS
_OFF_XBC = _OFF_Z + SSM_INNER
_OFF_DT = _OFF_XBC + SSM_CONV_DIM
_OFF_GATE = _OFF_DT + SSM_HEADS
IN_WIDTH = _OFF_GATE + N_BRANCH * D_MODEL
MAIN_WIDTH = 6 * ATT_WIDTH + SSM_INNER + SSM_CONV_DIM + N_BRANCH * D_MODEL
SMALL_WIDTH = 128
DT_LANE = N_HEADS

V7X_VMEM_LIMIT_BYTES = 56 * 1024 * 1024
PAGES_PER_STEP = 16
CUMSUM_SEG = 256


def _cparams(*sem):
    return pltpu.CompilerParams(dimension_semantics=sem, vmem_limit_bytes=V7X_VMEM_LIMIT_BYTES)


def _const_spec(shape):
    n = len(shape)
    return pl.BlockSpec(shape, lambda *_: (0,) * n, pipeline_mode=pl.Buffered(1))


def _rms(x, g):
    ms = jnp.mean(x * x, axis=-1, keepdims=True)
    return x * lax.rsqrt(ms + RMS_EPS) * g


def _softplus_tail(z):
    return jnp.log1p(jnp.exp(-jnp.abs(z)))


def _log_sigmoid(z):
    return jnp.minimum(z, 0.0) - _softplus_tail(z)


def _softplus(z):
    return jnp.maximum(z, 0.0) + _softplus_tail(z)


def _dot_nt(a, b):
    return lax.dot_general(a, b, (((1,), (1,)), ((), ())), preferred_element_type=F32)


def _dot(a, b):
    return jnp.dot(a, b, preferred_element_type=F32)


def _dot_exact(a, b):
    return jnp.dot(a, b, preferred_element_type=F32, precision=lax.Precision.HIGHEST)


def _inproj_kernel(x_ref, g_ref, wm_ref, ws_ref,
                   qsb_ref, qfx_ref, ksb_ref, vsb_ref, kfx_ref, vfx_ref,
                   ksbh_ref, vsbh_ref, kfxh_ref, vfxh_ref,
                   z_ref, xbc_ref, small_ref, gate_ref, *, kv_transposed):
    h = _rms(x_ref[...], g_ref[...]).astype(BF16)

    def mm(c0, width):
        return _dot_nt(h, wm_ref[c0:c0 + width, :])

    w = ATT_WIDTH
    qsb_ref[...] = (mm(0, w) * Q_SCALE).astype(BF16)
    for i, (full_ref, half_ref) in enumerate(((ksb_ref, ksbh_ref), (vsb_ref, vsbh_ref))):
        a = mm((1 + i) * w, w)
        full_ref[...] = a.T if kv_transposed else a
        half_ref[...] = a.astype(BF16)
    qfx_ref[...] = (mm(3 * w, w) * Q_SCALE).astype(BF16)
    for i, (full_ref, half_ref) in enumerate(((kfx_ref, kfxh_ref), (vfx_ref, vfxh_ref))):
        a = mm((4 + i) * w, w)
        full_ref[...] = a.T if kv_transposed else a
        half_ref[...] = a.astype(BF16)
    z_ref[...] = mm(6 * w, SSM_INNER)
    c0 = 6 * w + SSM_INNER
    for c in range(SSM_CONV_DIM // 512):
        xbc_ref[:, c * 512:(c + 1) * 512] = mm(c0 + c * 512, 512)
    c0 += SSM_CONV_DIM
    for c in range(N_BRANCH * D_MODEL // 512):
        gate_ref[:, c * 512:(c + 1) * 512] = jax.nn.sigmoid(mm(c0 + c * 512, 512)).astype(BF16)
    small_ref[...] = _dot_nt(h, ws_ref[...])


def _inproj(x2d, g, w_main, w_small, kv_seq=None):
    n = x2d.shape[0]
    tm = min(512, n)
    row = lambda width: pl.BlockSpec((tm, width), lambda i: (i, 0))
    rows = lambda width, dt: (row(width), jax.ShapeDtypeStruct((n, width), dt))
    if kv_seq is None:
        kv = rows(ATT_WIDTH, F32)
    else:
        bsz, t = kv_seq
        nt = t // tm
        kv = (pl.BlockSpec((None, ATT_WIDTH, tm), lambda i: (i // nt, 0, i % nt)),
              jax.ShapeDtypeStruct((bsz, ATT_WIDTH, t), F32))
    outs = ([rows(ATT_WIDTH, BF16)] * 2 + [kv] * 4 + [rows(ATT_WIDTH, BF16)] * 4
            + [rows(SSM_INNER, F32), rows(SSM_CONV_DIM, F32), rows(SMALL_WIDTH, F32),
               rows(N_BRANCH * D_MODEL, BF16)])
    return pl.pallas_call(
        functools.partial(_inproj_kernel, kv_transposed=kv_seq is not None),
        grid=(n // tm,),
        in_specs=[row(D_MODEL), _const_spec((1, D_MODEL)),
                  _const_spec((MAIN_WIDTH, D_MODEL)), _const_spec((SMALL_WIDTH, D_MODEL))],
        out_specs=[spec for spec, _ in outs],
        out_shape=[shape for _, shape in outs],
        compiler_params=_cparams("parallel"),
        name="inproj",
    )(x2d, g, w_main, w_small)


def _lane_head(width=ATT_WIDTH):
    return lax.broadcasted_iota(jnp.int32, (1, width), 1) // HEAD_DIM


def _suffix_matrix(seg, terms):
    r = lax.broadcasted_iota(jnp.int32, (terms * seg, seg), 0) % seg
    c = lax.broadcasted_iota(jnp.int32, (terms * seg, seg), 1)
    return jnp.where(r >= c, 1.0, 0.0).astype(BF16)


def _split_bf16(x, terms):
    out = []
    for _ in range(terms - 1):
        head = x.astype(BF16)
        out.append(head)
        x = x - head.astype(F32)
    out.append(x.astype(BF16))
    return jnp.concatenate(out, axis=1)


def _suffix_sums(x, u, carry):
    m, n = x.shape
    seg = u.shape[1]
    n_seg = n // seg
    terms = u.shape[0] // seg
    if n_seg == 1:
        cs = _dot(_split_bf16(x, terms), u)
        return cs + carry, carry + cs[:, 0:1]
    stacked = jnp.concatenate([x[:, s * seg:(s + 1) * seg] for s in range(n_seg)], axis=0)
    cs = _dot(_split_bf16(stacked, terms), u)
    parts = [None] * n_seg
    for s in reversed(range(n_seg)):
        part = cs[s * m:(s + 1) * m, :]
        parts[s] = part + carry
        carry = carry + part[:, 0:1]
    return jnp.concatenate(parts, axis=1), carry


def _stick_block(z, carry, u, mask):
    z2 = z * LOG2_E
    drop = jnp.maximum(z2, 0.0) + jnp.log2(1.0 + jnp.exp2(-jnp.abs(z2)))
    if mask is not None:
        drop = jnp.where(mask, drop, 0.0)
    later, carry = _suffix_sums(drop, u, carry)
    w = jnp.exp2(z2 - later)
    if mask is not None:
        w = jnp.where(mask, w, 0.0)
    return w, carry


def _stack_heads(q):
    lane_head = _lane_head()
    return jnp.concatenate([jnp.where(lane_head == h, q, jnp.zeros_like(q)) for h in range(N_HEADS)], axis=0)


def _fold_heads(acc, rows):
    lane_head = _lane_head()
    out = jnp.zeros((rows, ATT_WIDTH), F32)
    for h in range(N_HEADS):
        out = jnp.where(lane_head == h, acc[h * rows:(h + 1) * rows, :], out)
    return out


def _sb_prompt_kernel(q_ref, k_ref, v_ref, o_ref, acc_ref, *, blk):
    i = pl.program_id(1)
    rows = N_HEADS * blk
    qs = _stack_heads(q_ref[...])
    u = _suffix_matrix(blk, 2)
    qpos = lax.broadcasted_iota(jnp.int32, (rows, blk), 0) % blk
    causal = lax.broadcasted_iota(jnp.int32, (rows, blk), 1) < qpos
    acc_ref[...] = jnp.zeros_like(acc_ref)

    def block(j, carry, mask):
        start = pl.multiple_of(j * blk, blk)
        w, carry = _stick_block(_dot_nt(qs, k_ref[pl.ds(start, blk), :]), carry, u, mask)
        acc_ref[...] += _dot(w.astype(BF16), v_ref[pl.ds(start, blk), :])
        return carry

    carry = block(i, jnp.zeros((rows, 1), F32), causal)
    lax.fori_loop(0, i, lambda jj, c: block(i - 1 - jj, c, None), carry)
    o_ref[...] = _fold_heads(acc_ref[...], blk).astype(o_ref.dtype)


def _sb_prompt(q, k, v, blk=256):
    b, t, w = q.shape
    qspec = pl.BlockSpec((None, blk, w), lambda bi, i: (bi, i, 0))
    kvspec = pl.BlockSpec((None, t, w), lambda bi, i: (bi, 0, 0))
    return pl.pallas_call(
        functools.partial(_sb_prompt_kernel, blk=blk),
        grid=(b, t // blk),
        in_specs=[qspec, kvspec, kvspec],
        out_specs=qspec,
        out_shape=jax.ShapeDtypeStruct((b, t, w), BF16),
        scratch_shapes=[pltpu.VMEM((N_HEADS * blk, w), F32)],
        compiler_params=_cparams("parallel", "arbitrary"),
        name="sb_prompt",
    )(q, k, v)


def _logf_cum_kernel(b_ref, f_ref, logf_ref, cum_ref):
    h = pl.program_id(0) % N_HEADS
    logf = _log_sigmoid(f_ref[...] + b_ref[h])
    logf_ref[...] = logf
    rows = logf.shape[0]
    r = lax.broadcasted_iota(jnp.int32, (PAGE_SIZE, PAGE_SIZE), 0)
    c = lax.broadcasted_iota(jnp.int32, (PAGE_SIZE, PAGE_SIZE), 1)
    within = _dot_exact(logf, jnp.where(r <= c, 1.0, 0.0).astype(F32))
    totals = jnp.broadcast_to(within[:, PAGE_SIZE - 1:PAGE_SIZE], within.shape)
    rr = lax.broadcasted_iota(jnp.int32, (rows, rows), 0)
    cc = lax.broadcasted_iota(jnp.int32, (rows, rows), 1)
    cum_ref[...] = within + _dot_exact(jnp.where(cc < rr, 1.0, 0.0).astype(F32), totals)


def _logf_cum(f_rows, b_forget):
    n, rows, lanes = f_rows.shape
    spec = pl.BlockSpec((None, rows, lanes), lambda i: (i, 0, 0))
    return pl.pallas_call(
        _logf_cum_kernel,
        grid=(n,),
        in_specs=[pl.BlockSpec(memory_space=pltpu.SMEM), spec],
        out_specs=[spec, spec],
        out_shape=[jax.ShapeDtypeStruct(f_rows.shape, F32)] * 2,
        compiler_params=_cparams("parallel"),
        name="logf_cum",
    )(b_forget, f_rows)


def _fox_prompt_kernel(q_ref, k_ref, v_ref, cq_ref, ck_ref, o_ref, acc_ref, *, blk):
    i = pl.program_id(1)
    rows = N_HEADS * blk
    qs = _stack_heads(q_ref[...])
    cq = cq_ref[...]
    cqs = jnp.concatenate([cq[:, h:h + 1] for h in range(N_HEADS)], axis=0)
    qpos = lax.broadcasted_iota(jnp.int32, (rows, blk), 0) % blk
    causal = lax.broadcasted_iota(jnp.int32, (rows, blk), 1) <= qpos
    acc_ref[...] = jnp.zeros_like(acc_ref)

    def block(j, state, mask):
        m_old, l_old = state
        start = pl.multiple_of(j * blk, blk)
        s = _dot_nt(qs, k_ref[pl.ds(start, blk), :]) + cqs
        ck = ck_ref[j]
        s = jnp.concatenate([s[h * blk:(h + 1) * blk, :] - ck[h:h + 1, :] for h in range(N_HEADS)], axis=0)
        if mask is not None:
            s = jnp.where(mask, s, NEG_INF)
        m = jnp.maximum(m_old, jnp.max(s, axis=-1, keepdims=True))
        alpha = jnp.exp(m_old - m)
        p = jnp.exp(s - m)
        acc_ref[...] = acc_ref[...] * alpha + _dot(p.astype(BF16), v_ref[pl.ds(start, blk), :])
        return m, alpha * l_old + jnp.sum(p, axis=-1, keepdims=True)

    state = block(i, (jnp.full((rows, 1), NEG_INF, F32), jnp.zeros((rows, 1), F32)), causal)
    _, l = lax.fori_loop(0, i, lambda jj, st: block(i - 1 - jj, st, None), state)
    o_ref[...] = _fold_heads(acc_ref[...] / l, blk).astype(o_ref.dtype)


def _fox_prompt(q, k, v, cum_col, cum_row, blk=256):
    b, t, w = q.shape
    qspec = pl.BlockSpec((None, blk, w), lambda bi, i: (bi, i, 0))
    kvspec = pl.BlockSpec((None, t, w), lambda bi, i: (bi, 0, 0))
    return pl.pallas_call(
        functools.partial(_fox_prompt_kernel, blk=blk),
        grid=(b, t // blk),
        in_specs=[qspec, kvspec, kvspec,
                  pl.BlockSpec((None, blk, N_HEADS), lambda bi, i: (bi, i, 0)),
                  pl.BlockSpec((None, t // blk, N_HEADS, blk), lambda bi, i: (bi, 0, 0, 0))],
        out_specs=qspec,
        out_shape=jax.ShapeDtypeStruct((b, t, w), BF16),
        scratch_shapes=[pltpu.VMEM((N_HEADS * blk, w), F32)],
        compiler_params=_cparams("parallel", "arbitrary"),
        name="fox_prompt",
    )(q, k, v, cum_col, cum_row)


_XBUF_ROW0 = 8


def _ssm_kernel(z_ref, xbc_ref, small_ref, conv0_ref, state0_ref,
                cw_ref, cb_ref, dtb_ref, alog_ref, dskip_ref, ng_ref,
                y_ref, state_out_ref, conv_out_ref, xbuf_ref, state_ref, *, valid):
    c = pl.program_id(1)
    last = pl.num_programs(1) - 1
    L = SSM_CHUNK
    r0 = _XBUF_ROW0

    @pl.when(c == 0)
    def _():
        xbuf_ref[...] = jnp.zeros_like(xbuf_ref)
        xbuf_ref[r0 - 3:r0, :] = conv0_ref[...]
        state_ref[...] = state0_ref[...]

    x_cur = xbc_ref[...]
    window = jnp.concatenate([xbuf_ref[...], x_cur], axis=0)
    cw = cw_ref[...]
    conv = cb_ref[...] + x_cur * cw[3:4, :]
    for i in range(SSM_CONV - 1):
        conv = conv + pltpu.roll(window, 3 - i, 0)[r0:, :] * cw[i:i + 1, :]
    xbuf_ref[...] = x_cur[L - r0:L, :]
    act = conv * jax.nn.sigmoid(conv)
    xs = act[:, :SSM_INNER]
    b_in = act[:, SSM_INNER:SSM_INNER + SSM_GROUPS * SSM_STATE].astype(BF16)
    c_in = act[:, SSM_INNER + SSM_GROUPS * SSM_STATE:].astype(BF16)

    row = lax.broadcasted_iota(jnp.int32, (L, L), 0)
    col = lax.broadcasted_iota(jnp.int32, (L, L), 1)
    tri = row >= col
    dt = _softplus(small_ref[...] + dtb_ref[...])
    if valid < L:
        dt = jnp.where(row < valid, dt, 0.0)
    d_a = dt * (-jnp.exp(alog_ref[...]))
    a_cs = _dot_exact(jnp.where(tri, 1.0, 0.0).astype(F32), d_a)
    a_cs_t = a_cs.T
    a_last = a_cs[L - 1:L, :]
    e_cs = jnp.exp(a_cs)
    wgt = jnp.exp(a_last - a_cs) * dt
    chunk_dec = jnp.exp(a_last)

    half = lax.broadcasted_iota(jnp.int32, (1, 128), 1) // SSM_STATE_HALF
    rhalf = lax.broadcasted_iota(jnp.int32, (128, 1), 0) // SSM_STATE_HALF
    pair_cols = lambda a, p: jnp.where(half == 0, a[:, DT_LANE + 2 * p:DT_LANE + 2 * p + 1],
                                       a[:, DT_LANE + 2 * p + 1:DT_LANE + 2 * p + 2])
    scores = [_dot_nt(c_in[:, g * SSM_STATE:(g + 1) * SSM_STATE],
                      b_in[:, g * SSM_STATE:(g + 1) * SSM_STATE]) for g in range(SSM_GROUPS)]
    ys = []
    for p in range(SSM_HEADS // 2):
        g = (2 * p) // (SSM_HEADS // SSM_GROUPS)
        bg = b_in[:, g * SSM_STATE:(g + 1) * SSM_STATE]
        cg = c_in[:, g * SSM_STATE:(g + 1) * SSM_STATE]
        xs_p = xs[:, 128 * p:128 * (p + 1)]
        xdt = (xs_p * pair_cols(dt, p)).astype(BF16)
        y_diag = jnp.zeros((L, 128), F32)
        for hh in range(2):
            lane = DT_LANE + 2 * p + hh
            seg = a_cs[:, lane:lane + 1] - a_cs_t[lane:lane + 1, :]
            decay = jnp.exp(jnp.where(tri, seg, NEG_INF))
            y_h = _dot((scores[g] * decay).astype(BF16), xdt)
            y_diag = jnp.where(half == hh, y_h, y_diag)
        st = state_ref[128 * p:128 * (p + 1), :]
        y_off = _dot_nt(cg, st.astype(BF16)) * pair_cols(e_cs, p)
        ys.append(y_diag + y_off + dskip_ref[:, 128 * p:128 * (p + 1)] * xs_p)
        xw_t = (xs_p * pair_cols(wgt, p)).T.astype(BF16)
        lane = DT_LANE + 2 * p
        dec = jnp.where(rhalf == 0,
                        jnp.broadcast_to(chunk_dec[:, lane:lane + 1], (128, SSM_STATE)),
                        jnp.broadcast_to(chunk_dec[:, lane + 1:lane + 2], (128, SSM_STATE)))
        state_ref[128 * p:128 * (p + 1), :] = st * dec + _dot(xw_t, bg)

    z = z_ref[...]
    y = jnp.concatenate(ys, axis=1) * (z * jax.nn.sigmoid(z))
    gw = SSM_INNER // SSM_GROUPS
    parts = []
    for g in range(SSM_GROUPS):
        yg = y[:, g * gw:(g + 1) * gw]
        parts.append(yg * lax.rsqrt(jnp.mean(yg * yg, axis=-1, keepdims=True) + RMS_EPS))
    y_ref[...] = (jnp.concatenate(parts, axis=1) * ng_ref[...]).astype(y_ref.dtype)

    @pl.when(c == last)
    def _():
        state_out_ref[...] = state_ref[...]
        conv_out_ref[...] = x_cur[valid - 3:valid, :]


SSM_STATE_HALF = 64


def _ssm(z, xbc, small, conv0, state0, cw, cb, dtb, alog, dskip, ng, valid):
    b, t, _ = z.shape
    nc = t // SSM_CHUNK
    chunk = lambda width: pl.BlockSpec((None, SSM_CHUNK, width), lambda bi, c: (bi, c, 0))
    per_b = lambda shape: pl.BlockSpec((None,) + shape, lambda bi, c: (bi,) + (0,) * len(shape))
    const = lambda shape: pl.BlockSpec(shape, lambda bi, c: (0,) * len(shape))
    state_rows = SSM_INNER
    return pl.pallas_call(
        functools.partial(_ssm_kernel, valid=valid),
        grid=(b, nc),
        in_specs=[chunk(SSM_INNER), chunk(SSM_CONV_DIM), chunk(SMALL_WIDTH),
                  per_b((SSM_CONV - 1, SSM_CONV_DIM)), per_b((state_rows, SSM_STATE)),
                  const((SSM_CONV, SSM_CONV_DIM)), const((1, SSM_CONV_DIM)),
                  const((1, SMALL_WIDTH)), const((1, SMALL_WIDTH)),
                  const((1, SSM_INNER)), const((1, SSM_INNER))],
        out_specs=[chunk(SSM_INNER), per_b((state_rows, SSM_STATE)),
                   per_b((SSM_CONV - 1, SSM_CONV_DIM))],
        out_shape=[jax.ShapeDtypeStruct((b, t, SSM_INNER), BF16),
                   jax.ShapeDtypeStruct((b, state_rows, SSM_STATE), F32),
                   jax.ShapeDtypeStruct((b, SSM_CONV - 1, SSM_CONV_DIM), F32)],
        scratch_shapes=[pltpu.VMEM((_XBUF_ROW0, SSM_CONV_DIM), F32),
                        pltpu.VMEM((state_rows, SSM_STATE), F32)],
        compiler_params=_cparams("parallel", "arbitrary"),
        name="ssm",
    )(z, xbc, small, conv0, state0, cw, cb, dtb, alog, dskip, ng)


def _merge_ffn_kernel(x_ref, ysb_ref, yssm_ref, yfx_ref, gate_ref,
                      wsb_ref, wssm_ref, wfx_ref, wo_ref, g2_ref, wup_ref, wdn_ref, gf_ref,
                      o_ref, *, final):
    d = D_MODEL
    mixed = gate_ref[:, 0:d].astype(F32) * _dot(ysb_ref[...].astype(BF16), wsb_ref[...])
    mixed = mixed + gate_ref[:, d:2 * d].astype(F32) * _dot(yssm_ref[...].astype(BF16), wssm_ref[...])
    mixed = mixed + gate_ref[:, 2 * d:3 * d].astype(F32) * _dot(yfx_ref[...].astype(BF16), wfx_ref[...])
    x = x_ref[...] + _dot(mixed.astype(BF16), wo_ref[...])
    h = _rms(x, g2_ref[...]).astype(BF16)
    hc = FFN_HIDDEN // 2
    for c in range(2):
        u = jnp.maximum(_dot(h, wup_ref[:, c * hc:(c + 1) * hc]), 0.0)
        x = x + _dot((u * u).astype(BF16), wdn_ref[c * hc:(c + 1) * hc, :])
    if final:
        x = _rms(x, gf_ref[...])
    o_ref[...] = x


def _merge_ffn(x2d, y_sb, y_ssm, y_fx, gates, lw, gf, final):
    n = x2d.shape[0]
    tm = min(512, n)
    row = lambda width: pl.BlockSpec((tm, width), lambda i: (i, 0))
    weights = [lw['w_sb_out'], lw['w_ssm_out'], lw['w_fox_out'], lw['w_o'], lw['norm2_g'],
               lw['w_up'], lw['w_down'], gf]
    return pl.pallas_call(
        functools.partial(_merge_ffn_kernel, final=final),
        grid=(n // tm,),
        in_specs=[row(D_MODEL), row(ATT_WIDTH), row(SSM_INNER), row(ATT_WIDTH),
                  row(N_BRANCH * D_MODEL)] + [_const_spec(w.shape) for w in weights],
        out_specs=row(D_MODEL),
        out_shape=jax.ShapeDtypeStruct((n, D_MODEL), F32),
        compiler_params=_cparams("parallel"),
        name="merge_ffn",
    )(x2d, y_sb, y_ssm, y_fx, gates, *weights)


def _block_diag_q(q_ref):
    q = q_ref[...]
    rows = q.shape[0]
    row_head = lax.broadcasted_iota(jnp.int32, (rows, 1), 0) // (rows // N_HEADS)
    return jnp.where(row_head == _lane_head(), q, jnp.zeros_like(q))


def _page_copy(cache_ref, stage_ref, sem_ref, layer, page, slot, p):
    return pltpu.make_async_copy(cache_ref.at[layer, page], stage_ref.at[slot, p], sem_ref.at[slot])


def _prefetch_pages(pt_ref, caches, stages, sems, layer):
    n_pages = stages[0].shape[1]
    b, c = pl.program_id(0), pl.program_id(1)
    n_chunks = pl.num_programs(1)
    step = b * n_chunks + c
    slot = step % 2

    def start(bb, cc, sl):
        base = (n_chunks - 1 - cc) * n_pages
        for p in range(n_pages):
            page = pt_ref[bb, base + p]
            for i, (cache, stage, sem) in enumerate(zip(caches, stages, sems)):
                _page_copy(cache, stage, sem, layer, page, sl, p).start(priority=i % 2)

    @pl.when(step == 0)
    def _():
        start(b, c, slot)

    @pl.when(step + 1 < pl.num_programs(0) * n_chunks)
    def _():
        wrap = c + 1 == n_chunks
        start(jnp.where(wrap, b + 1, b), jnp.where(wrap, 0, c + 1), 1 - slot)

    for p in range(n_pages):
        for cache, stage, sem in zip(caches, stages, sems):
            _page_copy(cache, stage, sem, layer, 0, slot, p).wait()
    return slot


def _pages_to_bf16(stage_ref, slot, buf_ref):
    for p in range(stage_ref.shape[1]):
        buf_ref[:, p * PAGE_SIZE:(p + 1) * PAGE_SIZE] = stage_ref[slot, p].astype(BF16)


def _sb_sample_kernel(pt_ref, q_ref, kn_ref, vn_ref, ck_ref, cv_ref, o_ref,
                      acc_ref, carry_ref, kstage_ref, vstage_ref, ksem, vsem, kbuf_ref, vbuf_ref,
                      *, steps, layer):
    slot = _prefetch_pages(pt_ref, (ck_ref, cv_ref), (kstage_ref, vstage_ref), (ksem, vsem), layer)
    c = pl.program_id(1)
    rows = N_HEADS * steps
    qbd = _block_diag_q(q_ref)
    u = _suffix_matrix(CUMSUM_SEG, 2)

    @pl.when(c == 0)
    def _():
        step = lax.broadcasted_iota(jnp.int32, (rows, PAGE_SIZE), 0) % steps
        col = lax.broadcasted_iota(jnp.int32, (rows, PAGE_SIZE), 1)
        w, carry = _stick_block(_dot_nt(qbd, kn_ref[...]), jnp.zeros((rows, 1), F32),
                                _suffix_matrix(PAGE_SIZE, 2), col < step)
        acc_ref[...] = _dot(w.astype(BF16), vn_ref[...])
        carry_ref[...] = jnp.broadcast_to(carry, carry_ref.shape)

    _pages_to_bf16(kstage_ref, slot, kbuf_ref)
    _pages_to_bf16(vstage_ref, slot, vbuf_ref)
    w, carry = _stick_block(_dot(qbd, kbuf_ref[...]), carry_ref[:, 0:1], u, None)
    acc_ref[...] += _dot_nt(w.astype(BF16), vbuf_ref[...])
    carry_ref[...] = jnp.broadcast_to(carry, carry_ref.shape)

    @pl.when(c == pl.num_programs(1) - 1)
    def _():
        o_ref[...] = _fold_heads(acc_ref[...], steps)


def _page_staging(n_pages, rows, lanes):
    return [pltpu.VMEM((2, n_pages, rows, lanes), F32)], [pltpu.SemaphoreType.DMA((2,))]


_HBM_SPEC = pl.BlockSpec(memory_space=pl.ANY)


def _sb_sample(layer, page_table, q_rows, k_new, v_new, cache_k, cache_v, steps):
    b, rows, w = q_rows.shape
    n_pages = PAGES_PER_STEP
    n_chunks = page_table.shape[1] // n_pages
    per_b = lambda r: pl.BlockSpec((None, r, w), lambda bi, c, pt: (bi, 0, 0))
    stage, sem = _page_staging(n_pages, w, PAGE_SIZE)
    grid_spec = pltpu.PrefetchScalarGridSpec(
        num_scalar_prefetch=1,
        grid=(b, n_chunks),
        in_specs=[per_b(rows), per_b(PAGE_SIZE), per_b(PAGE_SIZE), _HBM_SPEC, _HBM_SPEC],
        out_specs=per_b(steps),
        scratch_shapes=[pltpu.VMEM((rows, w), F32), pltpu.VMEM((rows, 128), F32)]
                       + stage * 2 + sem * 2
                       + [pltpu.VMEM((w, n_pages * PAGE_SIZE), BF16)] * 2,
    )
    return pl.pallas_call(
        functools.partial(_sb_sample_kernel, steps=steps, layer=layer),
        grid_spec=grid_spec,
        out_shape=jax.ShapeDtypeStruct((b, steps, w), F32),
        compiler_params=_cparams("arbitrary", "arbitrary"),
        name="sb_sample",
    )(page_table, q_rows, k_new, v_new, cache_k, cache_v)


def _fox_sample_kernel(pt_ref, q_ref, kn_ref, vn_ref, f_ref, b_ref, ck_ref, cv_ref, clf_ref,
                       o_ref, lf_out_ref, acc_ref, m_ref, l_ref, ncum_ref, rcarry_ref,
                       kstage_ref, vstage_ref, lstage_ref, ksem, vsem, lsem,
                       kbuf_ref, vbuf_ref, lbuf_ref, *, steps, layer):
    slot = _prefetch_pages(pt_ref, (ck_ref, cv_ref, clf_ref), (kstage_ref, vstage_ref, lstage_ref),
                           (ksem, vsem, lsem), layer)
    c = pl.program_id(1)
    rows = N_HEADS * steps
    qbd = _block_diag_q(q_ref)
    bcast = lambda a, ref: jnp.broadcast_to(a, ref.shape)

    @pl.when(c == 0)
    def _():
        lf = _log_sigmoid(f_ref[...] + b_ref[...])
        lf_out_ref[...] = lf
        step = lax.broadcasted_iota(jnp.int32, (rows, PAGE_SIZE), 0) % steps
        col = lax.broadcasted_iota(jnp.int32, (rows, PAGE_SIZE), 1)
        lf = jnp.where(col < steps, lf, 0.0)
        cum = lf
        shift = 1
        while shift < steps:
            cum = cum + jnp.where(col >= shift, pltpu.roll(cum, shift, 1), 0.0)
            shift *= 2
        visible = col <= step
        ncum = jnp.sum(jnp.where(visible, lf, 0.0), axis=-1, keepdims=True)
        s = jnp.where(visible, _dot_nt(qbd, kn_ref[...]) + ncum - cum, NEG_INF)
        m = jnp.max(s, axis=-1, keepdims=True)
        p = jnp.exp(s - m)
        m_ref[...] = bcast(m, m_ref)
        l_ref[...] = bcast(jnp.sum(p, axis=-1, keepdims=True), l_ref)
        acc_ref[...] = _dot(p.astype(BF16), vn_ref[...])
        ncum_ref[...] = bcast(ncum, ncum_ref)
        rcarry_ref[...] = jnp.zeros_like(rcarry_ref)
        lbuf_ref[...] = jnp.zeros_like(lbuf_ref)

    _pages_to_bf16(kstage_ref, slot, kbuf_ref)
    _pages_to_bf16(vstage_ref, slot, vbuf_ref)
    for p in range(lstage_ref.shape[1]):
        lbuf_ref[0:N_HEADS, p * PAGE_SIZE:(p + 1) * PAGE_SIZE] = lstage_ref[slot, p]
    lf_pages = lbuf_ref[...]
    incl, carry = _suffix_sums(lf_pages, _suffix_matrix(CUMSUM_SEG, 3), rcarry_ref[:, 0:1])
    suffix = incl - lf_pages
    rcarry_ref[...] = bcast(carry, rcarry_ref)
    bias = jnp.concatenate(
        [jnp.broadcast_to(suffix[h:h + 1, :], (steps, suffix.shape[1])) for h in range(N_HEADS)], axis=0)
    s = _dot(qbd, kbuf_ref[...]) + bias + ncum_ref[:, 0:1]
    m_old = m_ref[:, 0:1]
    m = jnp.maximum(m_old, jnp.max(s, axis=-1, keepdims=True))
    alpha = jnp.exp(m_old - m)
    p = jnp.exp(s - m)
    l = alpha * l_ref[:, 0:1] + jnp.sum(p, axis=-1, keepdims=True)
    acc = acc_ref[...] * alpha + _dot_nt(p.astype(BF16), vbuf_ref[...])
    m_ref[...] = bcast(m, m_ref)
    l_ref[...] = bcast(l, l_ref)
    acc_ref[...] = acc

    @pl.when(c == pl.num_programs(1) - 1)
    def _():
        o_ref[...] = _fold_heads(acc / l, steps)


def _fox_sample(layer, page_table, q_rows, k_new, v_new, f_rows, b_rows,
                cache_k, cache_v, cache_lf_t, steps):
    b, rows, w = q_rows.shape
    n_pages = PAGES_PER_STEP
    n_chunks = page_table.shape[1] // n_pages
    per_b = lambda r, width=w: pl.BlockSpec((None, r, width), lambda bi, c, pt: (bi, 0, 0))
    stat = pltpu.VMEM((rows, 128), F32)
    kv_stage, kv_sem = _page_staging(n_pages, w, PAGE_SIZE)
    lf_stage, lf_sem = _page_staging(n_pages, N_HEADS, PAGE_SIZE)
    grid_spec = pltpu.PrefetchScalarGridSpec(
        num_scalar_prefetch=1,
        grid=(b, n_chunks),
        in_specs=[per_b(rows), per_b(PAGE_SIZE), per_b(PAGE_SIZE), per_b(rows, PAGE_SIZE),
                  pl.BlockSpec((rows, PAGE_SIZE), lambda bi, c, pt: (0, 0)),
                  _HBM_SPEC, _HBM_SPEC, _HBM_SPEC],
        out_specs=[per_b(steps), per_b(rows, PAGE_SIZE)],
        scratch_shapes=[pltpu.VMEM((rows, w), F32), stat, stat, stat, pltpu.VMEM((8, 128), F32)]
                       + kv_stage * 2 + lf_stage + kv_sem * 2 + lf_sem
                       + [pltpu.VMEM((w, n_pages * PAGE_SIZE), BF16)] * 2
                       + [pltpu.VMEM((8, n_pages * PAGE_SIZE), F32)],
    )
    return pl.pallas_call(
        functools.partial(_fox_sample_kernel, steps=steps, layer=layer),
        grid_spec=grid_spec,
        out_shape=[jax.ShapeDtypeStruct((b, steps, w), F32),
                   jax.ShapeDtypeStruct((b, rows, PAGE_SIZE), F32)],
        compiler_params=_cparams("arbitrary", "arbitrary"),
        name="fox_sample",
    )(page_table, q_rows, k_new, v_new, f_rows, b_rows, cache_k, cache_v, cache_lf_t)


def _layer_weights(l, norm1_g, w_in, b_forget, conv_w, conv_b, dt_bias, a_log, d_skip, ssm_norm_g,
                   w_sb_out, w_ssm_out, w_fox_out, w_o, norm2_g, w_up, w_down):
    w = jnp.transpose(w_in, (2, 0, 1))[:, l, :]
    w_main = jnp.concatenate([w[:_OFF_F], w[_OFF_Z:_OFF_DT], w[_OFF_GATE:]], axis=0).astype(BF16)
    w_small = jnp.concatenate([w[_OFF_F:_OFF_Z], w[_OFF_DT:_OFF_GATE]], axis=0)
    w_small = jnp.pad(w_small, ((0, SMALL_WIDTH - w_small.shape[0]), (0, 0))).astype(BF16)
    pad_dt = lambda a: jnp.pad(a, (DT_LANE, SMALL_WIDTH - DT_LANE - SSM_HEADS)).reshape(1, SMALL_WIDTH)
    return {
        'norm1_g': norm1_g[l].reshape(1, D_MODEL), 'w_main': w_main, 'w_small': w_small,
        'b_forget': b_forget[l], 'conv_w': conv_w[l], 'conv_b': conv_b[l].reshape(1, SSM_CONV_DIM),
        'dt_bias': pad_dt(dt_bias[l]), 'a_log': pad_dt(a_log[l]),
        'd_skip': jnp.repeat(d_skip[l], SSM_INNER // SSM_HEADS).reshape(1, SSM_INNER),
        'ssm_norm_g': ssm_norm_g[l].reshape(1, SSM_INNER),
        'w_sb_out': w_sb_out[l].astype(BF16), 'w_ssm_out': w_ssm_out[l].astype(BF16),
        'w_fox_out': w_fox_out[l].astype(BF16), 'w_o': w_o[l].astype(BF16),
        'norm2_g': norm2_g[l].reshape(1, D_MODEL),
        'w_up': w_up[l].astype(BF16), 'w_down': w_down[l].astype(BF16),
    }


def _to_heads(a, bsz, t):
    return a.reshape(bsz, t, N_HEADS, HEAD_DIM)


def _prompt_layer(x, lw, gf, final):
    bsz, t, d = x.shape
    n = bsz * t
    (q_sb, q_fx, k_sb, v_sb, k_fx, v_fx, k_sb_h, v_sb_h, k_fx_h, v_fx_h,
     z, xbc, small, gates) = _inproj(x.reshape(n, d), lw['norm1_g'], lw['w_main'], lw['w_small'], (bsz, t))
    b3 = lambda a: a.reshape(bsz, t, a.shape[-1])
    y_sb = _sb_prompt(b3(q_sb), b3(k_sb_h), b3(v_sb_h))

    f_rows = small[:, :N_HEADS].reshape(bsz, t, N_HEADS).transpose(0, 2, 1)
    f_rows = f_rows.reshape(bsz * N_HEADS, t // PAGE_SIZE, PAGE_SIZE)
    logf, cum = _logf_cum(f_rows, lw['b_forget'])
    logf = logf.reshape(bsz, N_HEADS, t).transpose(0, 2, 1)
    blk = 256
    cum_col = cum.reshape(bsz, N_HEADS, t).transpose(0, 2, 1)
    cum_row = cum.reshape(bsz, N_HEADS, t // blk, blk).transpose(0, 2, 1, 3)
    y_fx = _fox_prompt(b3(q_fx), b3(k_fx_h), b3(v_fx_h), cum_col, cum_row, blk)

    conv0 = jnp.zeros((bsz, SSM_CONV - 1, SSM_CONV_DIM), F32)
    ssm0 = jnp.zeros((bsz, SSM_INNER, SSM_STATE), F32)
    y_ssm, ssm_new, conv_new = _ssm(b3(z), b3(xbc), b3(small), conv0, ssm0, lw['conv_w'], lw['conv_b'],
                                    lw['dt_bias'], lw['a_log'], lw['d_skip'], lw['ssm_norm_g'], SSM_CHUNK)
    x_new = _merge_ffn(x.reshape(n, d), y_sb.reshape(n, -1), y_ssm.reshape(n, -1), y_fx.reshape(n, -1),
                       gates, lw, gf, final)
    from_t = lambda a: a.reshape(bsz, N_HEADS, HEAD_DIM, t).transpose(0, 3, 1, 2)
    states = (from_t(k_sb), from_t(v_sb), from_t(k_fx), from_t(v_fx), logf,
              ssm_new.reshape(bsz, SSM_HEADS, SSM_INNER // SSM_HEADS, SSM_STATE), conv_new)
    return x_new.reshape(bsz, t, d), states


def _sample_layer(x, l, lw, gf, final, caches, state_ssm, state_conv, page_table):
    bsz, t, d = x.shape
    n = bsz * t
    cache_sb_k, cache_sb_v, cache_fox_k, cache_fox_v, cache_lf_t = caches
    (q_sb, q_fx, k_sb, v_sb, k_fx, v_fx, k_sb_h, v_sb_h, k_fx_h, v_fx_h,
     z, xbc, small, gates) = _inproj(x.reshape(n, d), lw['norm1_g'], lw['w_main'], lw['w_small'])
    b3 = lambda a: a.reshape(bsz, t, a.shape[-1])
    q_rows = lambda q: jnp.tile(b3(q), (1, N_HEADS, 1))
    pad_keys = lambda a: jnp.pad(b3(a), ((0, 0), (0, PAGE_SIZE - t), (0, 0)))
    y_sb = _sb_sample(l, page_table, q_rows(q_sb), pad_keys(k_sb_h), pad_keys(v_sb_h),
                      cache_sb_k, cache_sb_v, t)

    f_new = small[:, :N_HEADS].reshape(bsz, t, N_HEADS).transpose(0, 2, 1)
    f_rows = jnp.pad(jnp.repeat(f_new, t, axis=1), ((0, 0), (0, 0), (0, PAGE_SIZE - t)))
    b_rows = jnp.broadcast_to(jnp.repeat(lw['b_forget'], t)[:, None], (N_HEADS * t, PAGE_SIZE))
    y_fx, lf_rows = _fox_sample(l, page_table, q_rows(q_fx), pad_keys(k_fx_h), pad_keys(v_fx_h),
                                f_rows, b_rows, cache_fox_k, cache_fox_v, cache_lf_t, t)
    logf = lf_rows[:, ::t, :t].transpose(0, 2, 1)

    pad_rows = lambda a: jnp.pad(b3(a), ((0, 0), (0, SSM_CHUNK - t), (0, 0)))
    y_ssm, ssm_new, conv_new = _ssm(pad_rows(z), pad_rows(xbc), pad_rows(small), state_conv[l],
                                    state_ssm[l].reshape(bsz, SSM_INNER, SSM_STATE),
                                    lw['conv_w'], lw['conv_b'], lw['dt_bias'], lw['a_log'],
                                    lw['d_skip'], lw['ssm_norm_g'], t)
    y_ssm = y_ssm[:, :t]
    x_new = _merge_ffn(x.reshape(n, d), y_sb.reshape(n, -1), y_ssm.reshape(n, -1), y_fx.reshape(n, -1),
                       gates, lw, gf, final)
    states = (_to_heads(k_sb, bsz, t), _to_heads(v_sb, bsz, t), _to_heads(k_fx, bsz, t),
              _to_heads(v_fx, bsz, t), logf,
              ssm_new.reshape(bsz, SSM_HEADS, SSM_INNER // SSM_HEADS, SSM_STATE), conv_new)
    return x_new.reshape(bsz, t, d), states


def kernel(x_prompt, x_sample, cache_sb_k, cache_sb_v, cache_fox_k, cache_fox_v, cache_fox_logf, state_ssm, state_conv, page_table, norm1_g, w_in, b_forget, conv_w, conv_b, dt_bias, a_log, d_skip, ssm_norm_g, w_sb_out, w_ssm_out, w_fox_out, w_o, norm2_g, w_up, w_down, final_norm_g):
    depth = w_in.shape[0]
    assert page_table.shape[1] % PAGES_PER_STEP == 0
    assert x_sample.shape[1] >= SSM_CONV - 1 and x_prompt.shape[1] % 256 == 0
    flat = lambda c: c.transpose(0, 1, 3, 4, 2).reshape(c.shape[0], c.shape[1], ATT_WIDTH, PAGE_SIZE)
    caches = (flat(cache_sb_k), flat(cache_sb_v), flat(cache_fox_k), flat(cache_fox_v),
              cache_fox_logf.transpose(0, 1, 3, 2))
    gf = final_norm_g.reshape(1, D_MODEL)
    xp, xs = x_prompt, x_sample
    prompt_states, sample_states = [], []
    for l in range(depth):
        lw = _layer_weights(l, norm1_g, w_in, b_forget, conv_w, conv_b, dt_bias, a_log, d_skip,
                            ssm_norm_g, w_sb_out, w_ssm_out, w_fox_out, w_o, norm2_g, w_up, w_down)
        final = l == depth - 1
        xp, st_p = _prompt_layer(xp, lw, gf, final)
        xs, st_s = _sample_layer(xs, l, lw, gf, final, caches, state_ssm, state_conv, page_table)
        prompt_states.append(st_p)
        sample_states.append(st_s)
    stacked_p = [jnp.stack(s) for s in zip(*prompt_states)]
    stacked_s = [jnp.stack(s) for s in zip(*sample_states)]
    return (xp, xs, *stacked_p, *stacked_s)
```

```python
import functools

import jax
import jax.numpy as jnp
from jax import lax
from jax.experimental import pallas as pl
from jax.experimental.pallas import tpu as pltpu

F32 = jnp.float32
BF16 = jnp.bfloat16

D_MODEL = 1024
HEAD_DIM = 64
N_HEADS = 4
ATT_WIDTH = N_HEADS * HEAD_DIM
SSM_HEADS = 8
SSM_INNER = 512
SSM_STATE = 128
SSM_GROUPS = 2
SSM_CONV = 4
SSM_CONV_DIM = 1024
SSM_CHUNK = 128
PAGE_SIZE = 128
N_BRANCH = 3
FFN_HIDDEN = 4 * D_MODEL
RMS_EPS = 1e-6
NEG_INF = -1e30
Q_SCALE = HEAD_DIM ** -0.5
LOG2_E = 1.4426950408889634

_OFF_F = 6 * ATT_WIDTH
_OFF_Z = _OFF_F + N_HEADS
_OFF_XBC = _OFF_Z + SSM_INNER
_OFF_DT = _OFF_XBC + SSM_CONV_DIM
_OFF_GATE = _OFF_DT + SSM_HEADS
IN_WIDTH = _OFF_GATE + N_BRANCH * D_MODEL
MAIN_WIDTH = 6 * ATT_WIDTH + SSM_INNER + SSM_CONV_DIM + N_BRANCH * D_MODEL
SMALL_WIDTH = 128
DT_LANE = N_HEADS

V7X_VMEM_LIMIT_BYTES = 56 * 1024 * 1024
PAGES_PER_STEP = 16
SEQS_PER_STEP = 4
CUMSUM_SEG = 256


def _cparams(*sem):
    return pltpu.CompilerParams(dimension_semantics=sem, vmem_limit_bytes=V7X_VMEM_LIMIT_BYTES)


def _const_spec(shape):
    n = len(shape)
    return pl.BlockSpec(shape, lambda *_: (0,) * n, pipeline_mode=pl.Buffered(1))


def _rms(x, g):
    ms = jnp.mean(x * x, axis=-1, keepdims=True)
    return x * lax.rsqrt(ms + RMS_EPS) * g


def _softplus_tail(z):
    return jnp.log1p(jnp.exp(-jnp.abs(z)))


def _log_sigmoid(z):
    return jnp.minimum(z, 0.0) - _softplus_tail(z)


def _softplus(z):
    return jnp.maximum(z, 0.0) + _softplus_tail(z)


def _dot_nt(a, b):
    return lax.dot_general(a, b, (((1,), (1,)), ((), ())), preferred_element_type=F32)


def _dot(a, b):
    return jnp.dot(a, b, preferred_element_type=F32)


def _dot_exact(a, b):
    return jnp.dot(a, b, preferred_element_type=F32, precision=lax.Precision.HIGHEST)


def _inproj_kernel(x_ref, g_ref, wm_ref, ws_ref,
                   qsb_ref, qfx_ref, ksb_ref, vsb_ref, kfx_ref, vfx_ref,
                   ksbh_ref, vsbh_ref, kfxh_ref, vfxh_ref,
                   z_ref, xbc_ref, small_ref, gate_ref, *, kv_transposed):
    h = _rms(x_ref[...], g_ref[...]).astype(BF16)

    def mm(c0, width):
        return _dot_nt(h, wm_ref[c0:c0 + width, :])

    w = ATT_WIDTH
    qsb_ref[...] = (mm(0, w) * Q_SCALE).astype(BF16)
    for i, (full_ref, half_ref) in enumerate(((ksb_ref, ksbh_ref), (vsb_ref, vsbh_ref))):
        a = mm((1 + i) * w, w)
        full_ref[...] = a.T if kv_transposed else a
        half_ref[...] = a.astype(BF16)
    qfx_ref[...] = (mm(3 * w, w) * Q_SCALE).astype(BF16)
    for i, (full_ref, half_ref) in enumerate(((kfx_ref, kfxh_ref), (vfx_ref, vfxh_ref))):
        a = mm((4 + i) * w, w)
        full_ref[...] = a.T if kv_transposed else a
        half_ref[...] = a.astype(BF16)
    z_ref[...] = mm(6 * w, SSM_INNER)
    c0 = 6 * w + SSM_INNER
    for c in range(SSM_CONV_DIM // 512):
        xbc_ref[:, c * 512:(c + 1) * 512] = mm(c0 + c * 512, 512)
    c0 += SSM_CONV_DIM
    for c in range(N_BRANCH * D_MODEL // 512):
        gate_ref[:, c * 512:(c + 1) * 512] = jax.nn.sigmoid(mm(c0 + c * 512, 512)).astype(BF16)
    small_ref[...] = _dot_nt(h, ws_ref[...])


def _inproj(x2d, g, w_main, w_small, kv_seq=None):
    n = x2d.shape[0]
    tm = min(512, n)
    row = lambda width: pl.BlockSpec((tm, width), lambda i: (i, 0))
    rows = lambda width, dt: (row(width), jax.ShapeDtypeStruct((n, width), dt))
    if kv_seq is None:
        kv = rows(ATT_WIDTH, F32)
    else:
        bsz, t = kv_seq
        nt = t // tm
        kv = (pl.BlockSpec((None, ATT_WIDTH, tm), lambda i: (i // nt, 0, i % nt)),
              jax.ShapeDtypeStruct((bsz, ATT_WIDTH, t), F32))
    outs = ([rows(ATT_WIDTH, BF16)] * 2 + [kv] * 4 + [rows(ATT_WIDTH, BF16)] * 4
            + [rows(SSM_INNER, F32), rows(SSM_CONV_DIM, F32), rows(SMALL_WIDTH, F32),
               rows(N_BRANCH * D_MODEL, BF16)])
    return pl.pallas_call(
        functools.partial(_inproj_kernel, kv_transposed=kv_seq is not None),
        grid=(n // tm,),
        in_specs=[row(D_MODEL), _const_spec((1, D_MODEL)),
                  _const_spec((MAIN_WIDTH, D_MODEL)), _const_spec((SMALL_WIDTH, D_MODEL))],
        out_specs=[spec for spec, _ in outs],
        out_shape=[shape for _, shape in outs],
        compiler_params=_cparams("parallel"),
        name="inproj",
    )(x2d, g, w_main, w_small)


def _lane_head(width=ATT_WIDTH):
    return lax.broadcasted_iota(jnp.int32, (1, width), 1) // HEAD_DIM


def _suffix_matrix(seg, terms):
    r = lax.broadcasted_iota(jnp.int32, (terms * seg, seg), 0) % seg
    c = lax.broadcasted_iota(jnp.int32, (terms * seg, seg), 1)
    return jnp.where(r >= c, 1.0, 0.0).astype(BF16)


def _split_bf16(x, terms):
    out = []
    for _ in range(terms - 1):
        head = x.astype(BF16)
        out.append(head)
        x = x - head.astype(F32)
    out.append(x.astype(BF16))
    return jnp.concatenate(out, axis=1)


def _suffix_sums(x, u, carry):
    m, n = x.shape
    seg = u.shape[1]
    n_seg = n // seg
    terms = u.shape[0] // seg
    if n_seg == 1:
        cs = _dot(_split_bf16(x, terms), u)
        return cs + carry, carry + cs[:, 0:1]
    stacked = jnp.concatenate([x[:, s * seg:(s + 1) * seg] for s in range(n_seg)], axis=0)
    cs = _dot(_split_bf16(stacked, terms), u)
    parts = [None] * n_seg
    for s in reversed(range(n_seg)):
        part = cs[s * m:(s + 1) * m, :]
        parts[s] = part + carry
        carry = carry + part[:, 0:1]
    return jnp.concatenate(parts, axis=1), carry


def _stick_block(z, carry, u, mask):
    z2 = z * LOG2_E
    drop = jnp.maximum(z2, 0.0) + jnp.log2(1.0 + jnp.exp2(-jnp.abs(z2)))
    if mask is not None:
        drop = jnp.where(mask, drop, 0.0)
    later, carry = _suffix_sums(drop, u, carry)
    w = jnp.exp2(z2 - later)
    if mask is not None:
        w = jnp.where(mask, w, 0.0)
    return w, carry


def _stack_heads(q):
    lane_head = _lane_head()
    return jnp.concatenate([jnp.where(lane_head == h, q, jnp.zeros_like(q)) for h in range(N_HEADS)], axis=0)


def _fold_heads(acc, rows):
    lane_head = _lane_head()
    out = jnp.zeros((rows, ATT_WIDTH), F32)
    for h in range(N_HEADS):
        out = jnp.where(lane_head == h, acc[h * rows:(h + 1) * rows, :], out)
    return out


def _causal_blocks(bq, bk):
    q0 = pl.program_id(1) * bq
    diag = q0 // bk
    return diag, q0 - diag * bk


def _sb_prompt_kernel(q_ref, k_ref, v_ref, o_ref, acc_ref, *, bq, bk):
    diag, q_off = _causal_blocks(bq, bk)
    rows = N_HEADS * bq
    qs = _stack_heads(q_ref[...])
    u = _suffix_matrix(bk, 2)
    qpos = lax.broadcasted_iota(jnp.int32, (rows, bk), 0) % bq + q_off
    causal = lax.broadcasted_iota(jnp.int32, (rows, bk), 1) < qpos
    acc_ref[...] = jnp.zeros_like(acc_ref)

    def block(j, carry, mask):
        start = pl.multiple_of(j * bk, bk)
        w, carry = _stick_block(_dot_nt(qs, k_ref[pl.ds(start, bk), :]), carry, u, mask)
        acc_ref[...] += _dot(w.astype(BF16), v_ref[pl.ds(start, bk), :])
        return carry

    carry = block(diag, jnp.zeros((rows, 1), F32), causal)
    lax.fori_loop(0, diag, lambda jj, c: block(diag - 1 - jj, c, None), carry)
    o_ref[...] = _fold_heads(acc_ref[...], bq).astype(o_ref.dtype)


def _sb_prompt(q, k, v, bq=256, bk=256):
    b, t, w = q.shape
    qspec = pl.BlockSpec((None, bq, w), lambda bi, i: (bi, i, 0))
    kvspec = pl.BlockSpec((None, t, w), lambda bi, i: (bi, 0, 0))
    return pl.pallas_call(
        functools.partial(_sb_prompt_kernel, bq=bq, bk=bk),
        grid=(b, t // bq),
        in_specs=[qspec, kvspec, kvspec],
        out_specs=qspec,
        out_shape=jax.ShapeDtypeStruct((b, t, w), BF16),
        scratch_shapes=[pltpu.VMEM((N_HEADS * bq, w), F32)],
        compiler_params=_cparams("parallel", "arbitrary"),
        name="sb_prompt",
    )(q, k, v)


def _logf_cum_kernel(b_ref, f_ref, logf_ref, cum_ref):
    h = pl.program_id(0) % N_HEADS
    logf = _log_sigmoid(f_ref[...] + b_ref[h])
    logf_ref[...] = logf
    rows = logf.shape[0]
    r = lax.broadcasted_iota(jnp.int32, (PAGE_SIZE, PAGE_SIZE), 0)
    c = lax.broadcasted_iota(jnp.int32, (PAGE_SIZE, PAGE_SIZE), 1)
    within = _dot_exact(logf, jnp.where(r <= c, 1.0, 0.0).astype(F32))
    totals = jnp.broadcast_to(within[:, PAGE_SIZE - 1:PAGE_SIZE], within.shape)
    rr = lax.broadcasted_iota(jnp.int32, (rows, rows), 0)
    cc = lax.broadcasted_iota(jnp.int32, (rows, rows), 1)
    cum_ref[...] = within + _dot_exact(jnp.where(cc < rr, 1.0, 0.0).astype(F32), totals)


def _logf_cum(f_rows, b_forget):
    n, rows, lanes = f_rows.shape
    spec = pl.BlockSpec((None, rows, lanes), lambda i: (i, 0, 0))
    return pl.pallas_call(
        _logf_cum_kernel,
        grid=(n,),
        in_specs=[pl.BlockSpec(memory_space=pltpu.SMEM), spec],
        out_specs=[spec, spec],
        out_shape=[jax.ShapeDtypeStruct(f_rows.shape, F32)] * 2,
        compiler_params=_cparams("parallel"),
        name="logf_cum",
    )(b_forget, f_rows)


def _fox_prompt_kernel(q_ref, k_ref, v_ref, cq_ref, ck_ref, o_ref, acc_ref, *, blk):
    i = pl.program_id(1)
    rows = N_HEADS * blk
    qs = _stack_heads(q_ref[...])
    cq = cq_ref[...]
    cqs = jnp.concatenate([cq[:, h:h + 1] for h in range(N_HEADS)], axis=0)
    qpos = lax.broadcasted_iota(jnp.int32, (rows, blk), 0) % blk
    causal = lax.broadcasted_iota(jnp.int32, (rows, blk), 1) <= qpos
    acc_ref[...] = jnp.zeros_like(acc_ref)

    def block(j, state, mask):
        m_old, l_old = state
        start = pl.multiple_of(j * blk, blk)
        s = _dot_nt(qs, k_ref[pl.ds(start, blk), :]) + cqs
        ck = ck_ref[j]
        s = jnp.concatenate([s[h * blk:(h + 1) * blk, :] - ck[h:h + 1, :] for h in range(N_HEADS)], axis=0)
        if mask is not None:
            s = jnp.where(mask, s, NEG_INF)
        m = jnp.maximum(m_old, jnp.max(s, axis=-1, keepdims=True))
        alpha = jnp.exp(m_old - m)
        p = jnp.exp(s - m)
        acc_ref[...] = acc_ref[...] * alpha + _dot(p.astype(BF16), v_ref[pl.ds(start, blk), :])
        return m, alpha * l_old + jnp.sum(p, axis=-1, keepdims=True)

    state = block(i, (jnp.full((rows, 1), NEG_INF, F32), jnp.zeros((rows, 1), F32)), causal)
    _, l = lax.fori_loop(0, i, lambda jj, st: block(i - 1 - jj, st, None), state)
    o_ref[...] = _fold_heads(acc_ref[...] / l, blk).astype(o_ref.dtype)


def _fox_prompt(q, k, v, cum_col, cum_row, blk=256):
    b, t, w = q.shape
    qspec = pl.BlockSpec((None, blk, w), lambda bi, i: (bi, i, 0))
    kvspec = pl.BlockSpec((None, t, w), lambda bi, i: (bi, 0, 0))
    return pl.pallas_call(
        functools.partial(_fox_prompt_kernel, blk=blk),
        grid=(b, t // blk),
        in_specs=[qspec, kvspec, kvspec,
                  pl.BlockSpec((None, blk, N_HEADS), lambda bi, i: (bi, i, 0)),
                  pl.BlockSpec((None, t // blk, N_HEADS, blk), lambda bi, i: (bi, 0, 0, 0))],
        out_specs=qspec,
        out_shape=jax.ShapeDtypeStruct((b, t, w), BF16),
        scratch_shapes=[pltpu.VMEM((N_HEADS * blk, w), F32)],
        compiler_params=_cparams("parallel", "arbitrary"),
        name="fox_prompt",
    )(q, k, v, cum_col, cum_row)


_XBUF_ROW0 = 8


def _ssm_kernel(z_ref, xbc_ref, small_ref, conv0_ref, state0_ref,
                cw_ref, cb_ref, dtb_ref, alog_ref, dskip_ref, ng_ref,
                y_ref, state_out_ref, conv_out_ref, xbuf_ref, state_ref, *, valid):
    c = pl.program_id(1)
    last = pl.num_programs(1) - 1
    L = SSM_CHUNK
    r0 = _XBUF_ROW0

    @pl.when(c == 0)
    def _():
        xbuf_ref[...] = jnp.zeros_like(xbuf_ref)
        xbuf_ref[r0 - 3:r0, :] = conv0_ref[...]
        state_ref[...] = state0_ref[...]

    x_cur = xbc_ref[...]
    window = jnp.concatenate([xbuf_ref[...], x_cur], axis=0)
    cw = cw_ref[...]
    conv = cb_ref[...] + x_cur * cw[3:4, :]
    for i in range(SSM_CONV - 1):
        conv = conv + pltpu.roll(window, 3 - i, 0)[r0:, :] * cw[i:i + 1, :]
    xbuf_ref[...] = x_cur[L - r0:L, :]
    act = conv * jax.nn.sigmoid(conv)
    xs = act[:, :SSM_INNER]
    b_in = act[:, SSM_INNER:SSM_INNER + SSM_GROUPS * SSM_STATE].astype(BF16)
    c_in = act[:, SSM_INNER + SSM_GROUPS * SSM_STATE:].astype(BF16)

    row = lax.broadcasted_iota(jnp.int32, (L, L), 0)
    col = lax.broadcasted_iota(jnp.int32, (L, L), 1)
    tri = row >= col
    dt = _softplus(small_ref[...] + dtb_ref[...])
    if valid < L:
        dt = jnp.where(row < valid, dt, 0.0)
    d_a = dt * (-jnp.exp(alog_ref[...]))
    a_cs = _dot_exact(jnp.where(tri, 1.0, 0.0).astype(F32), d_a)
    a_cs_t = a_cs.T
    a_last = a_cs[L - 1:L, :]
    e_cs = jnp.exp(a_cs)
    wgt = jnp.exp(a_last - a_cs) * dt
    chunk_dec = jnp.exp(a_last)

    half = lax.broadcasted_iota(jnp.int32, (1, 128), 1) // SSM_STATE_HALF
    rhalf = lax.broadcasted_iota(jnp.int32, (128, 1), 0) // SSM_STATE_HALF
    pair_cols = lambda a, p: jnp.where(half == 0, a[:, DT_LANE + 2 * p:DT_LANE + 2 * p + 1],
                                       a[:, DT_LANE + 2 * p + 1:DT_LANE + 2 * p + 2])
    scores = [_dot_nt(c_in[:, g * SSM_STATE:(g + 1) * SSM_STATE],
                      b_in[:, g * SSM_STATE:(g + 1) * SSM_STATE]) for g in range(SSM_GROUPS)]
    ys = []
    for p in range(SSM_HEADS // 2):
        g = (2 * p) // (SSM_HEADS // SSM_GROUPS)
        bg = b_in[:, g * SSM_STATE:(g + 1) * SSM_STATE]
        cg = c_in[:, g * SSM_STATE:(g + 1) * SSM_STATE]
        xs_p = xs[:, 128 * p:128 * (p + 1)]
        xdt = (xs_p * pair_cols(dt, p)).astype(BF16)
        y_diag = jnp.zeros((L, 128), F32)
        for hh in range(2):
            lane = DT_LANE + 2 * p + hh
            seg = a_cs[:, lane:lane + 1] - a_cs_t[lane:lane + 1, :]
            decay = jnp.exp(jnp.where(tri, seg, NEG_INF))
            y_h = _dot((scores[g] * decay).astype(BF16), xdt)
            y_diag = jnp.where(half == hh, y_h, y_diag)
        st = state_ref[128 * p:128 * (p + 1), :]
        y_off = _dot_nt(cg, st.astype(BF16)) * pair_cols(e_cs, p)
        ys.append(y_diag + y_off + dskip_ref[:, 128 * p:128 * (p + 1)] * xs_p)
        xw_t = (xs_p * pair_cols(wgt, p)).T.astype(BF16)
        lane = DT_LANE + 2 * p
        dec = jnp.where(rhalf == 0,
                        jnp.broadcast_to(chunk_dec[:, lane:lane + 1], (128, SSM_STATE)),
                        jnp.broadcast_to(chunk_dec[:, lane + 1:lane + 2], (128, SSM_STATE)))
        state_ref[128 * p:128 * (p + 1), :] = st * dec + _dot(xw_t, bg)

    z = z_ref[...]
    y = jnp.concatenate(ys, axis=1) * (z * jax.nn.sigmoid(z))
    gw = SSM_INNER // SSM_GROUPS
    parts = []
    for g in range(SSM_GROUPS):
        yg = y[:, g * gw:(g + 1) * gw]
        parts.append(yg * lax.rsqrt(jnp.mean(yg * yg, axis=-1, keepdims=True) + RMS_EPS))
    y_ref[...] = (jnp.concatenate(parts, axis=1) * ng_ref[...]).astype(y_ref.dtype)

    @pl.when(c == last)
    def _():
        state_out_ref[...] = state_ref[...]
        conv_out_ref[...] = x_cur[valid - 3:valid, :]


SSM_STATE_HALF = 64


def _ssm(z, xbc, small, conv0, state0, cw, cb, dtb, alog, dskip, ng, valid):
    b, t, _ = z.shape
    nc = t // SSM_CHUNK
    chunk = lambda width: pl.BlockSpec((None, SSM_CHUNK, width), lambda bi, c: (bi, c, 0))
    per_b = lambda shape: pl.BlockSpec((None,) + shape, lambda bi, c: (bi,) + (0,) * len(shape))
    const = lambda shape: pl.BlockSpec(shape, lambda bi, c: (0,) * len(shape))
    state_rows = SSM_INNER
    return pl.pallas_call(
        functools.partial(_ssm_kernel, valid=valid),
        grid=(b, nc),
        in_specs=[chunk(SSM_INNER), chunk(SSM_CONV_DIM), chunk(SMALL_WIDTH),
                  per_b((SSM_CONV - 1, SSM_CONV_DIM)), per_b((state_rows, SSM_STATE)),
                  const((SSM_CONV, SSM_CONV_DIM)), const((1, SSM_CONV_DIM)),
                  const((1, SMALL_WIDTH)), const((1, SMALL_WIDTH)),
                  const((1, SSM_INNER)), const((1, SSM_INNER))],
        out_specs=[chunk(SSM_INNER), per_b((state_rows, SSM_STATE)),
                   per_b((SSM_CONV - 1, SSM_CONV_DIM))],
        out_shape=[jax.ShapeDtypeStruct((b, t, SSM_INNER), BF16),
                   jax.ShapeDtypeStruct((b, state_rows, SSM_STATE), F32),
                   jax.ShapeDtypeStruct((b, SSM_CONV - 1, SSM_CONV_DIM), F32)],
        scratch_shapes=[pltpu.VMEM((_XBUF_ROW0, SSM_CONV_DIM), F32),
                        pltpu.VMEM((state_rows, SSM_STATE), F32)],
        compiler_params=_cparams("parallel", "arbitrary"),
        name="ssm",
    )(z, xbc, small, conv0, state0, cw, cb, dtb, alog, dskip, ng)


def _merge_ffn_kernel(x_ref, ysb_ref, yssm_ref, yfx_ref, gate_ref,
                      wsb_ref, wssm_ref, wfx_ref, wo_ref, g2_ref, wup_ref, wdn_ref, gf_ref,
                      o_ref, *, final):
    d = D_MODEL
    mixed = gate_ref[:, 0:d].astype(F32) * _dot(ysb_ref[...].astype(BF16), wsb_ref[...])
    mixed = mixed + gate_ref[:, d:2 * d].astype(F32) * _dot(yssm_ref[...].astype(BF16), wssm_ref[...])
    mixed = mixed + gate_ref[:, 2 * d:3 * d].astype(F32) * _dot(yfx_ref[...].astype(BF16), wfx_ref[...])
    x = x_ref[...] + _dot(mixed.astype(BF16), wo_ref[...])
    h = _rms(x, g2_ref[...]).astype(BF16)
    hc = FFN_HIDDEN // 2
    for c in range(2):
        u = jnp.maximum(_dot(h, wup_ref[:, c * hc:(c + 1) * hc]), 0.0)
        x = x + _dot((u * u).astype(BF16), wdn_ref[c * hc:(c + 1) * hc, :])
    if final:
        x = _rms(x, gf_ref[...])
    o_ref[...] = x


def _merge_ffn(x2d, y_sb, y_ssm, y_fx, gates, lw, gf, final):
    n = x2d.shape[0]
    tm = min(512, n)
    row = lambda width: pl.BlockSpec((tm, width), lambda i: (i, 0))
    weights = [lw['w_sb_out'], lw['w_ssm_out'], lw['w_fox_out'], lw['w_o'], lw['norm2_g'],
               lw['w_up'], lw['w_down'], gf]
    return pl.pallas_call(
        functools.partial(_merge_ffn_kernel, final=final),
        grid=(n // tm,),
        in_specs=[row(D_MODEL), row(ATT_WIDTH), row(SSM_INNER), row(ATT_WIDTH),
                  row(N_BRANCH * D_MODEL)] + [_const_spec(w.shape) for w in weights],
        out_specs=row(D_MODEL),
        out_shape=jax.ShapeDtypeStruct((n, D_MODEL), F32),
        compiler_params=_cparams("parallel"),
        name="merge_ffn",
    )(x2d, y_sb, y_ssm, y_fx, gates, *weights)


def _block_diag_q(q_ref):
    q = q_ref[...]
    rows = q.shape[0]
    row_head = lax.broadcasted_iota(jnp.int32, (rows, 1), 0) // (rows // N_HEADS)
    return jnp.where(row_head == _lane_head(), q, jnp.zeros_like(q))


def _page_copy(cache_ref, stage_ref, sem_ref, layer, page, slot, p):
    return pltpu.make_async_copy(cache_ref.at[layer, page], stage_ref.at[slot, p], sem_ref.at[slot])


def _prefetch_pages(pt_ref, caches, stages, sems, layer, group):
    n_pages = stages[0].shape[1]
    per_seq = n_pages // group
    b, c = pl.program_id(0), pl.program_id(1)
    n_chunks = pl.num_programs(1)
    step = b * n_chunks + c
    slot = step % 2

    def start(bb, cc, sl):
        base = (n_chunks - 1 - cc) * per_seq
        for p in range(n_pages):
            page = pt_ref[bb * group + p // per_seq, base + p % per_seq]
            for i, (cache, stage, sem) in enumerate(zip(caches, stages, sems)):
                _page_copy(cache, stage, sem, layer, page, sl, p).start(priority=i % 2)

    @pl.when(step == 0)
    def _():
        start(b, c, slot)

    @pl.when(step + 1 < pl.num_programs(0) * n_chunks)
    def _():
        wrap = c + 1 == n_chunks
        start(jnp.where(wrap, b + 1, b), jnp.where(wrap, 0, c + 1), 1 - slot)

    for p in range(n_pages):
        for cache, stage, sem in zip(caches, stages, sems):
            _page_copy(cache, stage, sem, layer, 0, slot, p).wait()
    return slot


def _pages_to_bf16(stage_ref, slot, g, buf_ref):
    per_seq = buf_ref.shape[-1] // PAGE_SIZE
    for p in range(per_seq):
        buf_ref[g, :, p * PAGE_SIZE:(p + 1) * PAGE_SIZE] = stage_ref[slot, g * per_seq + p].astype(BF16)


def _sb_sample_kernel(pt_ref, q_ref, kn_ref, vn_ref, ck_ref, cv_ref, o_ref,
                      acc_ref, carry_ref, kstage_ref, vstage_ref, ksem, vsem, kbuf_ref, vbuf_ref,
                      *, steps, layer):
    group = q_ref.shape[0]
    slot = _prefetch_pages(pt_ref, (ck_ref, cv_ref), (kstage_ref, vstage_ref), (ksem, vsem), layer, group)
    c = pl.program_id(1)
    rows = N_HEADS * steps
    qbd = [_block_diag_q(q_ref.at[g]) for g in range(group)]
    u = _suffix_matrix(CUMSUM_SEG, 2)

    @pl.when(c == 0)
    def _():
        step = lax.broadcasted_iota(jnp.int32, (rows, PAGE_SIZE), 0) % steps
        col = lax.broadcasted_iota(jnp.int32, (rows, PAGE_SIZE), 1)
        for g in range(group):
            w, carry = _stick_block(_dot_nt(qbd[g], kn_ref[g]), jnp.zeros((rows, 1), F32),
                                    _suffix_matrix(PAGE_SIZE, 2), col < step)
            acc_ref[g] = _dot(w.astype(BF16), vn_ref[g])
            carry_ref[g] = jnp.broadcast_to(carry, carry_ref.shape[1:])

    for g in range(group):
        _pages_to_bf16(kstage_ref, slot, g, kbuf_ref)
        _pages_to_bf16(vstage_ref, slot, g, vbuf_ref)
        w, carry = _stick_block(_dot(qbd[g], kbuf_ref[g]), carry_ref[g, :, 0:1], u, None)
        acc_ref[g] += _dot_nt(w.astype(BF16), vbuf_ref[g])
        carry_ref[g] = jnp.broadcast_to(carry, carry_ref.shape[1:])

    @pl.when(c == pl.num_programs(1) - 1)
    def _():
        for g in range(group):
            o_ref[g] = _fold_heads(acc_ref[g], steps)


def _page_staging(n_pages, rows, lanes):
    return [pltpu.VMEM((2, n_pages, rows, lanes), F32)], [pltpu.SemaphoreType.DMA((2,))]


_HBM_SPEC = pl.BlockSpec(memory_space=pl.ANY)


def _sb_sample(layer, page_table, q_rows, k_new, v_new, cache_k, cache_v, steps):
    b, rows, w = q_rows.shape
    g, n_pages = SEQS_PER_STEP, PAGES_PER_STEP
    n_chunks = page_table.shape[1] // n_pages
    per_b = lambda r: pl.BlockSpec((g, r, w), lambda bi, c, pt: (bi, 0, 0))
    stage, sem = _page_staging(g * n_pages, w, PAGE_SIZE)
    grid_spec = pltpu.PrefetchScalarGridSpec(
        num_scalar_prefetch=1,
        grid=(b // g, n_chunks),
        in_specs=[per_b(rows), per_b(PAGE_SIZE), per_b(PAGE_SIZE), _HBM_SPEC, _HBM_SPEC],
        out_specs=per_b(steps),
        scratch_shapes=[pltpu.VMEM((g, rows, w), F32), pltpu.VMEM((g, rows, 128), F32)]
                       + stage * 2 + sem * 2
                       + [pltpu.VMEM((g, w, n_pages * PAGE_SIZE), BF16)] * 2,
    )
    return pl.pallas_call(
        functools.partial(_sb_sample_kernel, steps=steps, layer=layer),
        grid_spec=grid_spec,
        out_shape=jax.ShapeDtypeStruct((b, steps, w), F32),
        compiler_params=_cparams("arbitrary", "arbitrary"),
        name="sb_sample",
    )(page_table, q_rows, k_new, v_new, cache_k, cache_v)


def _fox_sample_kernel(pt_ref, q_ref, kn_ref, vn_ref, f_ref, b_ref, ck_ref, cv_ref, clf_ref,
                       o_ref, lf_out_ref, acc_ref, m_ref, l_ref, ncum_ref, rcarry_ref,
                       kstage_ref, vstage_ref, lstage_ref, ksem, vsem, lsem,
                       kbuf_ref, vbuf_ref, lbuf_ref, *, steps, layer):
    group = q_ref.shape[0]
    slot = _prefetch_pages(pt_ref, (ck_ref, cv_ref, clf_ref), (kstage_ref, vstage_ref, lstage_ref),
                           (ksem, vsem, lsem), layer, group)
    c = pl.program_id(1)
    rows = N_HEADS * steps
    qbd = [_block_diag_q(q_ref.at[g]) for g in range(group)]
    stat = lambda a: jnp.broadcast_to(a, m_ref.shape[1:])

    @pl.when(c == 0)
    def _():
        step = lax.broadcasted_iota(jnp.int32, (rows, PAGE_SIZE), 0) % steps
        col = lax.broadcasted_iota(jnp.int32, (rows, PAGE_SIZE), 1)
        visible = col <= step
        lbuf_ref[...] = jnp.zeros_like(lbuf_ref)
        rcarry_ref[...] = jnp.zeros_like(rcarry_ref)
        for g in range(group):
            lf = _log_sigmoid(f_ref[g] + b_ref[...])
            lf_out_ref[g] = lf
            lf = jnp.where(col < steps, lf, 0.0)
            cum = lf
            shift = 1
            while shift < steps:
                cum = cum + jnp.where(col >= shift, pltpu.roll(cum, shift, 1), 0.0)
                shift *= 2
            ncum = jnp.sum(jnp.where(visible, lf, 0.0), axis=-1, keepdims=True)
            s = jnp.where(visible, _dot_nt(qbd[g], kn_ref[g]) + ncum - cum, NEG_INF)
            m = jnp.max(s, axis=-1, keepdims=True)
            p = jnp.exp(s - m)
            m_ref[g] = stat(m)
            l_ref[g] = stat(jnp.sum(p, axis=-1, keepdims=True))
            acc_ref[g] = _dot(p.astype(BF16), vn_ref[g])
            ncum_ref[g] = stat(ncum)

    per_seq = kbuf_ref.shape[-1] // PAGE_SIZE
    u3 = _suffix_matrix(CUMSUM_SEG, 3)
    for g in range(group):
        _pages_to_bf16(kstage_ref, slot, g, kbuf_ref)
        _pages_to_bf16(vstage_ref, slot, g, vbuf_ref)
        for p in range(per_seq):
            lbuf_ref[g, 0:N_HEADS, p * PAGE_SIZE:(p + 1) * PAGE_SIZE] = lstage_ref[slot, g * per_seq + p]
        lf_pages = lbuf_ref[g]
        incl, carry = _suffix_sums(lf_pages, u3, rcarry_ref[g, :, 0:1])
        suffix = incl - lf_pages
        rcarry_ref[g] = jnp.broadcast_to(carry, rcarry_ref.shape[1:])
        bias = jnp.concatenate(
            [jnp.broadcast_to(suffix[h:h + 1, :], (steps, suffix.shape[1])) for h in range(N_HEADS)], axis=0)
        s = _dot(qbd[g], kbuf_ref[g]) + bias + ncum_ref[g, :, 0:1]
        m_old = m_ref[g, :, 0:1]
        m = jnp.maximum(m_old, jnp.max(s, axis=-1, keepdims=True))
        alpha = jnp.exp(m_old - m)
        p = jnp.exp(s - m)
        l_ref[g] = stat(alpha * l_ref[g, :, 0:1] + jnp.sum(p, axis=-1, keepdims=True))
        acc_ref[g] = acc_ref[g] * alpha + _dot_nt(p.astype(BF16), vbuf_ref[g])
        m_ref[g] = stat(m)

    @pl.when(c == pl.num_programs(1) - 1)
    def _():
        for g in range(group):
            o_ref[g] = _fold_heads(acc_ref[g] / l_ref[g, :, 0:1], steps)


def _fox_sample(layer, page_table, q_rows, k_new, v_new, f_rows, b_rows,
                cache_k, cache_v, cache_lf_t, steps):
    b, rows, w = q_rows.shape
    g, n_pages = SEQS_PER_STEP, PAGES_PER_STEP
    n_chunks = page_table.shape[1] // n_pages
    per_b = lambda r, width=w: pl.BlockSpec((g, r, width), lambda bi, c, pt: (bi, 0, 0))
    stat = pltpu.VMEM((g, rows, 128), F32)
    kv_stage, kv_sem = _page_staging(g * n_pages, w, PAGE_SIZE)
    lf_stage, lf_sem = _page_staging(g * n_pages, N_HEADS, PAGE_SIZE)
    grid_spec = pltpu.PrefetchScalarGridSpec(
        num_scalar_prefetch=1,
        grid=(b // g, n_chunks),
        in_specs=[per_b(rows), per_b(PAGE_SIZE), per_b(PAGE_SIZE), per_b(rows, PAGE_SIZE),
                  pl.BlockSpec((rows, PAGE_SIZE), lambda bi, c, pt: (0, 0)),
                  _HBM_SPEC, _HBM_SPEC, _HBM_SPEC],
        out_specs=[per_b(steps), per_b(rows, PAGE_SIZE)],
        scratch_shapes=[pltpu.VMEM((g, rows, w), F32), stat, stat, stat, pltpu.VMEM((g, 8, 128), F32)]
                       + kv_stage * 2 + lf_stage + kv_sem * 2 + lf_sem
                       + [pltpu.VMEM((g, w, n_pages * PAGE_SIZE), BF16)] * 2
                       + [pltpu.VMEM((g, 8, n_pages * PAGE_SIZE), F32)],
    )
    return pl.pallas_call(
        functools.partial(_fox_sample_kernel, steps=steps, layer=layer),
        grid_spec=grid_spec,
        out_shape=[jax.ShapeDtypeStruct((b, steps, w), F32),
                   jax.ShapeDtypeStruct((b, rows, PAGE_SIZE), F32)],
        compiler_params=_cparams("arbitrary", "arbitrary"),
        name="fox_sample",
    )(page_table, q_rows, k_new, v_new, f_rows, b_rows, cache_k, cache_v, cache_lf_t)


def _layer_weights(l, norm1_g, w_in, b_forget, conv_w, conv_b, dt_bias, a_log, d_skip, ssm_norm_g,
                   w_sb_out, w_ssm_out, w_fox_out, w_o, norm2_g, w_up, w_down):
    w = jnp.transpose(w_in, (2, 0, 1))[:, l, :]
    w_main = jnp.concatenate([w[:_OFF_F], w[_OFF_Z:_OFF_DT], w[_OFF_GATE:]], axis=0).astype(BF16)
    w_small = jnp.concatenate([w[_OFF_F:_OFF_Z], w[_OFF_DT:_OFF_GATE]], axis=0)
    w_small = jnp.pad(w_small, ((0, SMALL_WIDTH - w_small.shape[0]), (0, 0))).astype(BF16)
    pad_dt = lambda a: jnp.pad(a, (DT_LANE, SMALL_WIDTH - DT_LANE - SSM_HEADS)).reshape(1, SMALL_WIDTH)
    return {
        'norm1_g': norm1_g[l].reshape(1, D_MODEL), 'w_main': w_main, 'w_small': w_small,
        'b_forget': b_forget[l], 'conv_w': conv_w[l], 'conv_b': conv_b[l].reshape(1, SSM_CONV_DIM),
        'dt_bias': pad_dt(dt_bias[l]), 'a_log': pad_dt(a_log[l]),
        'd_skip': jnp.repeat(d_skip[l], SSM_INNER // SSM_HEADS).reshape(1, SSM_INNER),
        'ssm_norm_g': ssm_norm_g[l].reshape(1, SSM_INNER),
        'w_sb_out': w_sb_out[l].astype(BF16), 'w_ssm_out': w_ssm_out[l].astype(BF16),
        'w_fox_out': w_fox_out[l].astype(BF16), 'w_o': w_o[l].astype(BF16),
        'norm2_g': norm2_g[l].reshape(1, D_MODEL),
        'w_up': w_up[l].astype(BF16), 'w_down': w_down[l].astype(BF16),
    }


def _to_heads(a, bsz, t):
    return a.reshape(bsz, t, N_HEADS, HEAD_DIM)


def _prompt_layer(x, lw, gf, final):
    bsz, t, d = x.shape
    n = bsz * t
    (q_sb, q_fx, k_sb, v_sb, k_fx, v_fx, k_sb_h, v_sb_h, k_fx_h, v_fx_h,
     z, xbc, small, gates) = _inproj(x.reshape(n, d), lw['norm1_g'], lw['w_main'], lw['w_small'], (bsz, t))
    b3 = lambda a: a.reshape(bsz, t, a.shape[-1])
    y_sb = _sb_prompt(b3(q_sb), b3(k_sb_h), b3(v_sb_h))

    f_rows = small[:, :N_HEADS].reshape(bsz, t, N_HEADS).transpose(0, 2, 1)
    f_rows = f_rows.reshape(bsz * N_HEADS, t // PAGE_SIZE, PAGE_SIZE)
    logf, cum = _logf_cum(f_rows, lw['b_forget'])
    logf = logf.reshape(bsz, N_HEADS, t).transpose(0, 2, 1)
    blk = 256
    cum_col = cum.reshape(bsz, N_HEADS, t).transpose(0, 2, 1)
    cum_row = cum.reshape(bsz, N_HEADS, t // blk, blk).transpose(0, 2, 1, 3)
    y_fx = _fox_prompt(b3(q_fx), b3(k_fx_h), b3(v_fx_h), cum_col, cum_row, blk)

    conv0 = jnp.zeros((bsz, SSM_CONV - 1, SSM_CONV_DIM), F32)
    ssm0 = jnp.zeros((bsz, SSM_INNER, SSM_STATE), F32)
    y_ssm, ssm_new, conv_new = _ssm(b3(z), b3(xbc), b3(small), conv0, ssm0, lw['conv_w'], lw['conv_b'],
                                    lw['dt_bias'], lw['a_log'], lw['d_skip'], lw['ssm_norm_g'], SSM_CHUNK)
    x_new = _merge_ffn(x.reshape(n, d), y_sb.reshape(n, -1), y_ssm.reshape(n, -1), y_fx.reshape(n, -1),
                       gates, lw, gf, final)
    from_t = lambda a: a.reshape(bsz, N_HEADS, HEAD_DIM, t).transpose(0, 3, 1, 2)
    states = (from_t(k_sb), from_t(v_sb), from_t(k_fx), from_t(v_fx), logf,
              ssm_new.reshape(bsz, SSM_HEADS, SSM_INNER // SSM_HEADS, SSM_STATE), conv_new)
    return x_new.reshape(bsz, t, d), states


def _sample_layer(x, l, lw, gf, final, caches, state_ssm, state_conv, page_table):
    bsz, t, d = x.shape
    n = bsz * t
    cache_sb_k, cache_sb_v, cache_fox_k, cache_fox_v, cache_lf_t = caches
    (q_sb, q_fx, k_sb, v_sb, k_fx, v_fx, k_sb_h, v_sb_h, k_fx_h, v_fx_h,
     z, xbc, small, gates) = _inproj(x.reshape(n, d), lw['norm1_g'], lw['w_main'], lw['w_small'])
    b3 = lambda a: a.reshape(bsz, t, a.shape[-1])
    q_rows = lambda q: jnp.tile(b3(q), (1, N_HEADS, 1))
    pad_keys = lambda a: jnp.pad(b3(a), ((0, 0), (0, PAGE_SIZE - t), (0, 0)))
    y_sb = _sb_sample(l, page_table, q_rows(q_sb), pad_keys(k_sb_h), pad_keys(v_sb_h),
                      cache_sb_k, cache_sb_v, t)

    f_new = small[:, :N_HEADS].reshape(bsz, t, N_HEADS).transpose(0, 2, 1)
    f_rows = jnp.pad(jnp.repeat(f_new, t, axis=1), ((0, 0), (0, 0), (0, PAGE_SIZE - t)))
    b_rows = jnp.broadcast_to(jnp.repeat(lw['b_forget'], t)[:, None], (N_HEADS * t, PAGE_SIZE))
    y_fx, lf_rows = _fox_sample(l, page_table, q_rows(q_fx), pad_keys(k_fx_h), pad_keys(v_fx_h),
                                f_rows, b_rows, cache_fox_k, cache_fox_v, cache_lf_t, t)
    logf = lf_rows[:, ::t, :t].transpose(0, 2, 1)

    pad_rows = lambda a: jnp.pad(b3(a), ((0, 0), (0, SSM_CHUNK - t), (0, 0)))
    y_ssm, ssm_new, conv_new = _ssm(pad_rows(z), pad_rows(xbc), pad_rows(small), state_conv[l],
                                    state_ssm[l].reshape(bsz, SSM_INNER, SSM_STATE),
                                    lw['conv_w'], lw['conv_b'], lw['dt_bias'], lw['a_log'],
                                    lw['d_skip'], lw['ssm_norm_g'], t)
    y_ssm = y_ssm[:, :t]
    x_new = _merge_ffn(x.reshape(n, d), y_sb.reshape(n, -1), y_ssm.reshape(n, -1), y_fx.reshape(n, -1),
                       gates, lw, gf, final)
    states = (_to_heads(k_sb, bsz, t), _to_heads(v_sb, bsz, t), _to_heads(k_fx, bsz, t),
              _to_heads(v_fx, bsz, t), logf,
              ssm_new.reshape(bsz, SSM_HEADS, SSM_INNER // SSM_HEADS, SSM_STATE), conv_new)
    return x_new.reshape(bsz, t, d), states


def kernel(x_prompt, x_sample, cache_sb_k, cache_sb_v, cache_fox_k, cache_fox_v, cache_fox_logf, state_ssm, state_conv, page_table, norm1_g, w_in, b_forget, conv_w, conv_b, dt_bias, a_log, d_skip, ssm_norm_g, w_sb_out, w_ssm_out, w_fox_out, w_o, norm2_g, w_up, w_down, final_norm_g):
    depth = w_in.shape[0]
    assert page_table.shape[1] % PAGES_PER_STEP == 0 and x_sample.shape[0] % SEQS_PER_STEP == 0
    assert x_sample.shape[1] >= SSM_CONV - 1 and x_prompt.shape[1] % 256 == 0
    flat = lambda c: c.transpose(0, 1, 3, 4, 2).reshape(c.shape[0], c.shape[1], ATT_WIDTH, PAGE_SIZE)
    caches = (flat(cache_sb_k), flat(cache_sb_v), flat(cache_fox_k), flat(cache_fox_v),
              cache_fox_logf.transpose(0, 1, 3, 2))
    gf = final_norm_g.reshape(1, D_MODEL)
    xp, xs = x_prompt, x_sample
    prompt_states, sample_states = [], []
    for l in range(depth):
        lw = _layer_weights(l, norm1_g, w_in, b_forget, conv_w, conv_b, dt_bias, a_log, d_skip,
                            ssm_norm_g, w_sb_out, w_ssm_out, w_fox_out, w_o, norm2_g, w_up, w_down)
        final = l == depth - 1
        xp, st_p = _prompt_layer(xp, lw, gf, final)
        xs, st_s = _sample_layer(xs, l, lw, gf, final, caches, state_ssm, state_conv, page_table)
        prompt_states.append(st_p)
        sample_states.append(st_s)
    stacked_p = [jnp.stack(s) for s in zip(*prompt_states)]
    stacked_s = [jnp.stack(s) for s in zip(*sample_states)]
    return (xp, xs, *stacked_p, *stacked_s)
```

```python
import functools

import jax
import jax.numpy as jnp
from jax import lax
from jax.experimental import pallas as pl
from jax.experimental.pallas import tpu as pltpu

F32 = jnp.float32
BF16 = jnp.bfloat16

D_MODEL = 1024
HEAD_DIM = 64
N_HEADS = 4
ATT_WIDTH = N_HEADS * HEAD_DIM
SSM_HEADS = 8
SSM_INNER = 512
SSM_STATE = 128
SSM_GROUPS = 2
SSM_CONV = 4
SSM_CONV_DIM = 1024
SSM_CHUNK = 128
PAGE_SIZE = 128
N_BRANCH = 3
FFN_HIDDEN = 4 * D_MODEL
RMS_EPS = 1e-6
NEG_INF = -1e30
Q_SCALE = HEAD_DIM ** -0.5
LOG2_E = 1.4426950408889634

_OFF_F = 6 * ATT_WIDTH
_OFF_Z = _OFF_F + N_HEADS
_OFF_XBC = _OFF_Z + SSM_INNER
_OFF_DT = _OFF_XBC + SSM_CONV_DIM
_OFF_GATE = _OFF_DT + SSM_HEADS
IN_WIDTH = _OFF_GATE + N_BRANCH * D_MODEL
MAIN_WIDTH = 6 * ATT_WIDTH + SSM_INNER + SSM_CONV_DIM + N_BRANCH * D_MODEL
SMALL_WIDTH = 128
DT_LANE = N_HEADS

V7X_VMEM_LIMIT_BYTES = 56 * 1024 * 1024
PAGES_PER_STEP = 16
ATT_BLOCK = 256
CUMSUM_SEG = 256


def _cparams(*sem):
    return pltpu.CompilerParams(dimension_semantics=sem, vmem_limit_bytes=V7X_VMEM_LIMIT_BYTES)


def _const_spec(shape):
    n = len(shape)
    return pl.BlockSpec(shape, lambda *_: (0,) * n, pipeline_mode=pl.Buffered(1))


def _rms(x, g):
    ms = jnp.mean(x * x, axis=-1, keepdims=True)
    return x * lax.rsqrt(ms + RMS_EPS) * g


def _softplus_tail(z):
    return jnp.log1p(jnp.exp(-jnp.abs(z)))


def _log_sigmoid(z):
    return jnp.minimum(z, 0.0) - _softplus_tail(z)


def _softplus(z):
    return jnp.maximum(z, 0.0) + _softplus_tail(z)


def _dot_nt(a, b):
    return lax.dot_general(a, b, (((1,), (1,)), ((), ())), preferred_element_type=F32)


def _dot(a, b):
    return jnp.dot(a, b, preferred_element_type=F32)


def _dot_exact(a, b):
    return jnp.dot(a, b, preferred_element_type=F32, precision=lax.Precision.HIGHEST)


def _store_kv(ref, a, transposed):
    if not transposed:
        ref[...] = a
    elif len(ref.shape) == 2:
        ref[...] = a.T
    else:
        ref[0] = a.T
        ref[1:] = jnp.zeros((ref.shape[0] - 1,) + ref.shape[1:], ref.dtype)


def _inproj_kernel(x_ref, g_ref, wm_ref, ws_ref, *rest, kv_transposed, n_aliased):
    (qsb_ref, qfx_ref, ksb_ref, vsb_ref, kfx_ref, vfx_ref, ksbh_ref, vsbh_ref, kfxh_ref, vfxh_ref,
     z_ref, xbc_ref, small_ref, gate_ref) = rest[n_aliased:]
    h = _rms(x_ref[...], g_ref[...]).astype(BF16)

    def mm(c0, width):
        return _dot_nt(h, wm_ref[c0:c0 + width, :])

    w = ATT_WIDTH
    qsb_ref[...] = (mm(0, w) * Q_SCALE).astype(BF16)
    for i, (full_ref, half_ref) in enumerate(((ksb_ref, ksbh_ref), (vsb_ref, vsbh_ref))):
        a = mm((1 + i) * w, w)
        _store_kv(full_ref, a, kv_transposed)
        half_ref[...] = a.astype(BF16)
    qfx_ref[...] = (mm(3 * w, w) * Q_SCALE).astype(BF16)
    for i, (full_ref, half_ref) in enumerate(((kfx_ref, kfxh_ref), (vfx_ref, vfxh_ref))):
        a = mm((4 + i) * w, w)
        _store_kv(full_ref, a, kv_transposed)
        half_ref[...] = a.astype(BF16)
    z_ref[...] = mm(6 * w, SSM_INNER)
    c0 = 6 * w + SSM_INNER
    for c in range(SSM_CONV_DIM // 512):
        xbc_ref[:, c * 512:(c + 1) * 512] = mm(c0 + c * 512, 512)
    c0 += SSM_CONV_DIM
    for c in range(N_BRANCH * D_MODEL // 512):
        gate_ref[:, c * 512:(c + 1) * 512] = jax.nn.sigmoid(mm(c0 + c * 512, 512)).astype(BF16)
    small_ref[...] = _dot_nt(h, ws_ref[...])


_HBM_SPEC = pl.BlockSpec(memory_space=pl.ANY)


def _inproj(x2d, g, w_main, w_small, kv_stack=None):
    n = x2d.shape[0]
    tm = min(512, n)
    row = lambda width: pl.BlockSpec((tm, width), lambda i: (i, 0))
    rows = lambda width, dt: (row(width), jax.ShapeDtypeStruct((n, width), dt))
    prev = ()
    if kv_stack is None:
        kv = rows(ATT_WIDTH, F32)
    else:
        bsz, t, layer, depth, prev = kv_stack
        prev = () if prev is None else tuple(prev)
        nt = t // tm
        if prev or depth == 1:
            block = pl.BlockSpec((None, None, ATT_WIDTH, tm), lambda i: (layer, i // nt, 0, i % nt))
        else:
            block = pl.BlockSpec((depth, None, ATT_WIDTH, tm), lambda i: (0, i // nt, 0, i % nt))
        kv = (block, jax.ShapeDtypeStruct((depth, bsz, ATT_WIDTH, t), F32))
    outs = ([rows(ATT_WIDTH, BF16)] * 2 + [kv] * 4 + [rows(ATT_WIDTH, BF16)] * 4
            + [rows(SSM_INNER, F32), rows(SSM_CONV_DIM, F32), rows(SMALL_WIDTH, F32),
               rows(N_BRANCH * D_MODEL, BF16)])
    n_in = 4
    return pl.pallas_call(
        functools.partial(_inproj_kernel, kv_transposed=kv_stack is not None, n_aliased=len(prev)),
        grid=(n // tm,),
        in_specs=[row(D_MODEL), _const_spec((1, D_MODEL)),
                  _const_spec((MAIN_WIDTH, D_MODEL)), _const_spec((SMALL_WIDTH, D_MODEL))]
                 + [_HBM_SPEC] * len(prev),
        out_specs=[spec for spec, _ in outs],
        out_shape=[shape for _, shape in outs],
        input_output_aliases={n_in + j: 2 + j for j in range(len(prev))},
        compiler_params=_cparams("parallel"),
        name="inproj",
    )(x2d, g, w_main, w_small, *prev)


def _lane_head(width=ATT_WIDTH):
    return lax.broadcasted_iota(jnp.int32, (1, width), 1) // HEAD_DIM


def _suffix_matrix(seg, terms):
    r = lax.broadcasted_iota(jnp.int32, (terms * seg, seg), 0) % seg
    c = lax.broadcasted_iota(jnp.int32, (terms * seg, seg), 1)
    return jnp.where(r >= c, 1.0, 0.0).astype(BF16)


def _split_bf16(x, terms):
    out = []
    for _ in range(terms - 1):
        head = x.astype(BF16)
        out.append(head)
        x = x - head.astype(F32)
    out.append(x.astype(BF16))
    return jnp.concatenate(out, axis=1)


def _suffix_sums(x, u, carry):
    m, n = x.shape
    seg = u.shape[1]
    n_seg = n // seg
    terms = u.shape[0] // seg
    if n_seg == 1:
        cs = _dot(_split_bf16(x, terms), u)
        return cs + carry, carry + cs[:, 0:1]
    stacked = jnp.concatenate([x[:, s * seg:(s + 1) * seg] for s in range(n_seg)], axis=0)
    cs = _dot(_split_bf16(stacked, terms), u)
    parts = [None] * n_seg
    for s in reversed(range(n_seg)):
        part = cs[s * m:(s + 1) * m, :]
        parts[s] = part + carry
        carry = carry + part[:, 0:1]
    return jnp.concatenate(parts, axis=1), carry


def _stick_block(z, carry, u, mask):
    z2 = z * LOG2_E
    drop = jnp.maximum(z2, 0.0) + jnp.log2(1.0 + jnp.exp2(-jnp.abs(z2)))
    if mask is not None:
        drop = jnp.where(mask, drop, 0.0)
    later, carry = _suffix_sums(drop, u, carry)
    w = jnp.exp2(z2 - later)
    if mask is not None:
        w = jnp.where(mask, w, 0.0)
    return w, carry


def _stack_heads(q):
    lane_head = _lane_head()
    return jnp.concatenate([jnp.where(lane_head == h, q, jnp.zeros_like(q)) for h in range(N_HEADS)], axis=0)


def _fold_heads(acc, rows):
    lane_head = _lane_head()
    out = jnp.zeros((rows, ATT_WIDTH), F32)
    for h in range(N_HEADS):
        out = jnp.where(lane_head == h, acc[h * rows:(h + 1) * rows, :], out)
    return out


def _causal_blocks(bq, bk):
    q0 = pl.program_id(1) * bq
    diag = q0 // bk
    return diag, q0 - diag * bk


def _sb_prompt_kernel(q_ref, k_ref, v_ref, o_ref, acc_ref, *, bq, bk):
    diag, q_off = _causal_blocks(bq, bk)
    rows = N_HEADS * bq
    qs = _stack_heads(q_ref[...])
    u = _suffix_matrix(bk, 2)
    qpos = lax.broadcasted_iota(jnp.int32, (rows, bk), 0) % bq + q_off
    causal = lax.broadcasted_iota(jnp.int32, (rows, bk), 1) < qpos
    acc_ref[...] = jnp.zeros_like(acc_ref)

    def block(j, carry, mask):
        start = pl.multiple_of(j * bk, bk)
        w, carry = _stick_block(_dot_nt(qs, k_ref[pl.ds(start, bk), :]), carry, u, mask)
        acc_ref[...] += _dot(w.astype(BF16), v_ref[pl.ds(start, bk), :])
        return carry

    carry = block(diag, jnp.zeros((rows, 1), F32), causal)
    lax.fori_loop(0, diag, lambda jj, c: block(diag - 1 - jj, c, None), carry)
    o_ref[...] = _fold_heads(acc_ref[...], bq).astype(o_ref.dtype)


def _logf_cum_kernel(b_ref, f_ref, logf_ref, cum_ref):
    h = pl.program_id(0) % N_HEADS
    logf = _log_sigmoid(f_ref[...] + b_ref[h])
    logf_ref[...] = logf
    rows = logf.shape[0]
    r = lax.broadcasted_iota(jnp.int32, (PAGE_SIZE, PAGE_SIZE), 0)
    c = lax.broadcasted_iota(jnp.int32, (PAGE_SIZE, PAGE_SIZE), 1)
    within = _dot_exact(logf, jnp.where(r <= c, 1.0, 0.0).astype(F32))
    totals = jnp.broadcast_to(within[:, PAGE_SIZE - 1:PAGE_SIZE], within.shape)
    rr = lax.broadcasted_iota(jnp.int32, (rows, rows), 0)
    cc = lax.broadcasted_iota(jnp.int32, (rows, rows), 1)
    cum_ref[...] = within + _dot_exact(jnp.where(cc < rr, 1.0, 0.0).astype(F32), totals)


def _logf_cum(f_rows, b_forget):
    n, rows, lanes = f_rows.shape
    spec = pl.BlockSpec((None, rows, lanes), lambda i: (i, 0, 0))
    return pl.pallas_call(
        _logf_cum_kernel,
        grid=(n,),
        in_specs=[pl.BlockSpec(memory_space=pltpu.SMEM), spec],
        out_specs=[spec, spec],
        out_shape=[jax.ShapeDtypeStruct(f_rows.shape, F32)] * 2,
        compiler_params=_cparams("parallel"),
        name="logf_cum",
    )(b_forget, f_rows)


def _fox_prompt_kernel(q_ref, k_ref, v_ref, cq_ref, ck_ref, o_ref, acc_ref, *, bq, bk):
    assert bq == bk
    blk = bq
    i = pl.program_id(1)
    rows = N_HEADS * blk
    qs = _stack_heads(q_ref[...])
    cq = cq_ref[...]
    cqs = jnp.concatenate([cq[:, h:h + 1] for h in range(N_HEADS)], axis=0)
    qpos = lax.broadcasted_iota(jnp.int32, (rows, blk), 0) % blk
    causal = lax.broadcasted_iota(jnp.int32, (rows, blk), 1) <= qpos
    acc_ref[...] = jnp.zeros_like(acc_ref)

    def block(j, state, mask):
        m_old, l_old = state
        start = pl.multiple_of(j * blk, blk)
        s = _dot_nt(qs, k_ref[pl.ds(start, blk), :]) + cqs
        ck = ck_ref[j]
        s = jnp.concatenate([s[h * blk:(h + 1) * blk, :] - ck[h:h + 1, :] for h in range(N_HEADS)], axis=0)
        if mask is not None:
            s = jnp.where(mask, s, NEG_INF)
        m = jnp.maximum(m_old, jnp.max(s, axis=-1, keepdims=True))
        alpha = jnp.exp(m_old - m)
        p = jnp.exp(s - m)
        acc_ref[...] = acc_ref[...] * alpha + _dot(p.astype(BF16), v_ref[pl.ds(start, blk), :])
        return m, alpha * l_old + jnp.sum(p, axis=-1, keepdims=True)

    state = block(i, (jnp.full((rows, 1), NEG_INF, F32), jnp.zeros((rows, 1), F32)), causal)
    _, l = lax.fori_loop(0, i, lambda jj, st: block(i - 1 - jj, st, None), state)
    o_ref[...] = _fold_heads(acc_ref[...] / l, blk).astype(o_ref.dtype)


_XBUF_ROW0 = 8


def _ssm_kernel(z_ref, xbc_ref, small_ref, conv0_ref, state0_ref,
                cw_ref, cb_ref, dtb_ref, alog_ref, dskip_ref, ng_ref,
                y_ref, state_out_ref, conv_out_ref, xbuf_ref, state_ref, *, valid):
    c = pl.program_id(1)
    last = pl.num_programs(1) - 1
    L = SSM_CHUNK
    r0 = _XBUF_ROW0

    @pl.when(c == 0)
    def _():
        xbuf_ref[...] = jnp.zeros_like(xbuf_ref)
        xbuf_ref[r0 - 3:r0, :] = conv0_ref[...]
        state_ref[...] = state0_ref[...]

    x_cur = xbc_ref[...]
    window = jnp.concatenate([xbuf_ref[...], x_cur], axis=0)
    cw = cw_ref[...]
    conv = cb_ref[...] + x_cur * cw[3:4, :]
    for i in range(SSM_CONV - 1):
        conv = conv + pltpu.roll(window, 3 - i, 0)[r0:, :] * cw[i:i + 1, :]
    xbuf_ref[...] = x_cur[L - r0:L, :]
    act = conv * jax.nn.sigmoid(conv)
    xs = act[:, :SSM_INNER]
    b_in = act[:, SSM_INNER:SSM_INNER + SSM_GROUPS * SSM_STATE].astype(BF16)
    c_in = act[:, SSM_INNER + SSM_GROUPS * SSM_STATE:].astype(BF16)

    row = lax.broadcasted_iota(jnp.int32, (L, L), 0)
    col = lax.broadcasted_iota(jnp.int32, (L, L), 1)
    tri = row >= col
    dt = _softplus(small_ref[...] + dtb_ref[...])
    if valid < L:
        dt = jnp.where(row < valid, dt, 0.0)
    d_a = dt * (-jnp.exp(alog_ref[...]))
    a_cs = _dot_exact(jnp.where(tri, 1.0, 0.0).astype(F32), d_a)
    a_cs_t = a_cs.T
    a_last = a_cs[L - 1:L, :]
    e_cs = jnp.exp(a_cs)
    wgt = jnp.exp(a_last - a_cs) * dt
    chunk_dec = jnp.exp(a_last)

    half = lax.broadcasted_iota(jnp.int32, (1, 128), 1) // SSM_STATE_HALF
    rhalf = lax.broadcasted_iota(jnp.int32, (128, 1), 0) // SSM_STATE_HALF
    pair_cols = lambda a, p: jnp.where(half == 0, a[:, DT_LANE + 2 * p:DT_LANE + 2 * p + 1],
                                       a[:, DT_LANE + 2 * p + 1:DT_LANE + 2 * p + 2])
    scores = [_dot_nt(c_in[:, g * SSM_STATE:(g + 1) * SSM_STATE],
                      b_in[:, g * SSM_STATE:(g + 1) * SSM_STATE]) for g in range(SSM_GROUPS)]
    ys = []
    for p in range(SSM_HEADS // 2):
        g = (2 * p) // (SSM_HEADS // SSM_GROUPS)
        bg = b_in[:, g * SSM_STATE:(g + 1) * SSM_STATE]
        cg = c_in[:, g * SSM_STATE:(g + 1) * SSM_STATE]
        xs_p = xs[:, 128 * p:128 * (p + 1)]
        xdt = (xs_p * pair_cols(dt, p)).astype(BF16)
        y_diag = jnp.zeros((L, 128), F32)
        for hh in range(2):
            lane = DT_LANE + 2 * p + hh
            seg = a_cs[:, lane:lane + 1] - a_cs_t[lane:lane + 1, :]
            decay = jnp.exp(jnp.where(tri, seg, NEG_INF))
            y_h = _dot((scores[g] * decay).astype(BF16), xdt)
            y_diag = jnp.where(half == hh, y_h, y_diag)
        st = state_ref[128 * p:128 * (p + 1), :]
        y_off = _dot_nt(cg, st.astype(BF16)) * pair_cols(e_cs, p)
        ys.append(y_diag + y_off + dskip_ref[:, 128 * p:128 * (p + 1)] * xs_p)
        xw_t = (xs_p * pair_cols(wgt, p)).T.astype(BF16)
        lane = DT_LANE + 2 * p
        dec = jnp.where(rhalf == 0,
                        jnp.broadcast_to(chunk_dec[:, lane:lane + 1], (128, SSM_STATE)),
                        jnp.broadcast_to(chunk_dec[:, lane + 1:lane + 2], (128, SSM_STATE)))
        state_ref[128 * p:128 * (p + 1), :] = st * dec + _dot(xw_t, bg)

    z = z_ref[...]
    y = jnp.concatenate(ys, axis=1) * (z * jax.nn.sigmoid(z))
    gw = SSM_INNER // SSM_GROUPS
    parts = []
    for g in range(SSM_GROUPS):
        yg = y[:, g * gw:(g + 1) * gw]
        parts.append(yg * lax.rsqrt(jnp.mean(yg * yg, axis=-1, keepdims=True) + RMS_EPS))
    y_ref[...] = (jnp.concatenate(parts, axis=1) * ng_ref[...]).astype(y_ref.dtype)

    @pl.when(c == last)
    def _():
        state_out_ref[...] = state_ref[...]
        conv_out_ref[...] = x_cur[valid - 3:valid, :]


SSM_STATE_HALF = 64


def _ssm(z, xbc, small, conv0, state0, cw, cb, dtb, alog, dskip, ng, valid):
    b, t, _ = z.shape
    nc = t // SSM_CHUNK
    chunk = lambda width: pl.BlockSpec((None, SSM_CHUNK, width), lambda bi, c: (bi, c, 0))
    per_b = lambda shape: pl.BlockSpec((None,) + shape, lambda bi, c: (bi,) + (0,) * len(shape))
    const = lambda shape: pl.BlockSpec(shape, lambda bi, c: (0,) * len(shape))
    state_rows = SSM_INNER
    return pl.pallas_call(
        functools.partial(_ssm_kernel, valid=valid),
        grid=(b, nc),
        in_specs=[chunk(SSM_INNER), chunk(SSM_CONV_DIM), chunk(SMALL_WIDTH),
                  per_b((SSM_CONV - 1, SSM_CONV_DIM)), per_b((state_rows, SSM_STATE)),
                  const((SSM_CONV, SSM_CONV_DIM)), const((1, SSM_CONV_DIM)),
                  const((1, SMALL_WIDTH)), const((1, SMALL_WIDTH)),
                  const((1, SSM_INNER)), const((1, SSM_INNER))],
        out_specs=[chunk(SSM_INNER), per_b((state_rows, SSM_STATE)),
                   per_b((SSM_CONV - 1, SSM_CONV_DIM))],
        out_shape=[jax.ShapeDtypeStruct((b, t, SSM_INNER), BF16),
                   jax.ShapeDtypeStruct((b, state_rows, SSM_STATE), F32),
                   jax.ShapeDtypeStruct((b, SSM_CONV - 1, SSM_CONV_DIM), F32)],
        scratch_shapes=[pltpu.VMEM((_XBUF_ROW0, SSM_CONV_DIM), F32),
                        pltpu.VMEM((state_rows, SSM_STATE), F32)],
        compiler_params=_cparams("parallel", "arbitrary"),
        name="ssm",
    )(z, xbc, small, conv0, state0, cw, cb, dtb, alog, dskip, ng)


def _merge_ffn_kernel(x_ref, ysb_ref, yssm_ref, yfx_ref, gate_ref,
                      wsb_ref, wssm_ref, wfx_ref, wo_ref, g2_ref, wup_ref, wdn_ref, gf_ref,
                      o_ref, *, final):
    d = D_MODEL
    mixed = gate_ref[:, 0:d].astype(F32) * _dot(ysb_ref[...].astype(BF16), wsb_ref[...])
    mixed = mixed + gate_ref[:, d:2 * d].astype(F32) * _dot(yssm_ref[...].astype(BF16), wssm_ref[...])
    mixed = mixed + gate_ref[:, 2 * d:3 * d].astype(F32) * _dot(yfx_ref[...].astype(BF16), wfx_ref[...])
    x = x_ref[...] + _dot(mixed.astype(BF16), wo_ref[...])
    h = _rms(x, g2_ref[...]).astype(BF16)
    hc = FFN_HIDDEN // 2
    for c in range(2):
        u = jnp.maximum(_dot(h, wup_ref[:, c * hc:(c + 1) * hc]), 0.0)
        x = x + _dot((u * u).astype(BF16), wdn_ref[c * hc:(c + 1) * hc, :])
    if final:
        x = _rms(x, gf_ref[...])
    o_ref[...] = x


def _merge_ffn(x2d, y_sb, y_ssm, y_fx, gates, lw, gf, final):
    n = x2d.shape[0]
    tm = min(512, n)
    row = lambda width: pl.BlockSpec((tm, width), lambda i: (i, 0))
    weights = [lw['w_sb_out'], lw['w_ssm_out'], lw['w_fox_out'], lw['w_o'], lw['norm2_g'],
               lw['w_up'], lw['w_down'], gf]
    return pl.pallas_call(
        functools.partial(_merge_ffn_kernel, final=final),
        grid=(n // tm,),
        in_specs=[row(D_MODEL), row(ATT_WIDTH), row(SSM_INNER), row(ATT_WIDTH),
                  row(N_BRANCH * D_MODEL)] + [_const_spec(w.shape) for w in weights],
        out_specs=row(D_MODEL),
        out_shape=jax.ShapeDtypeStruct((n, D_MODEL), F32),
        compiler_params=_cparams("parallel"),
        name="merge_ffn",
    )(x2d, y_sb, y_ssm, y_fx, gates, *weights)


def _block_diag_q(q_ref):
    q = q_ref[...]
    rows = q.shape[0]
    row_head = lax.broadcasted_iota(jnp.int32, (rows, 1), 0) // (rows // N_HEADS)
    return jnp.where(row_head == _lane_head(), q, jnp.zeros_like(q))


def _page_copy(cache_ref, stage_ref, sem_ref, layer, page, slot, p):
    return pltpu.make_async_copy(cache_ref.at[layer, page], stage_ref.at[slot, p], sem_ref.at[slot])


def _grid_step():
    return (pl.program_id(0) * pl.num_programs(1) + pl.program_id(1),
            pl.num_programs(0) * pl.num_programs(1))


def _prefetch_pages(pt_ref, caches, stages, sems, layer, group, n_chunks):
    n_pages = stages[0].shape[1]
    per_seq = n_pages // group
    step, n_steps = _grid_step()
    b, c = step // n_chunks, step % n_chunks
    slot = step % 2

    def start(bb, cc, sl):
        base = (n_chunks - 1 - cc) * per_seq
        for p in range(n_pages):
            page = pt_ref[bb * group + p // per_seq, base + p % per_seq]
            for i, (cache, stage, sem) in enumerate(zip(caches, stages, sems)):
                _page_copy(cache, stage, sem, layer, page, sl, p).start(priority=i % 2)

    @pl.when(step == 0)
    def _():
        start(b, c, slot)

    @pl.when(step + 1 < n_steps)
    def _():
        wrap = c + 1 == n_chunks
        start(jnp.where(wrap, b + 1, b), jnp.where(wrap, 0, c + 1), 1 - slot)

    for p in range(n_pages):
        for cache, stage, sem in zip(caches, stages, sems):
            _page_copy(cache, stage, sem, layer, 0, slot, p).wait()
    return slot


def _pages_to_bf16(stage_ref, slot, g, buf_ref):
    per_seq = buf_ref.shape[-1] // PAGE_SIZE
    for p in range(per_seq):
        buf_ref[g, :, p * PAGE_SIZE:(p + 1) * PAGE_SIZE] = stage_ref[slot, g * per_seq + p].astype(BF16)


def _sb_sample_kernel(pt_ref, q_ref, kn_ref, vn_ref, ck_ref, cv_ref, o_ref,
                      acc_ref, carry_ref, kstage_ref, vstage_ref, ksem, vsem, kbuf_ref, vbuf_ref,
                      *, steps, layer, n_chunks):
    group = q_ref.shape[0]
    slot = _prefetch_pages(pt_ref, (ck_ref, cv_ref), (kstage_ref, vstage_ref), (ksem, vsem), layer, group,
                           n_chunks)
    c = _grid_step()[0] % n_chunks
    rows = N_HEADS * steps
    qbd = [_block_diag_q(q_ref.at[g]) for g in range(group)]
    u = _suffix_matrix(CUMSUM_SEG, 2)

    @pl.when(c == 0)
    def _():
        step = lax.broadcasted_iota(jnp.int32, (rows, PAGE_SIZE), 0) % steps
        col = lax.broadcasted_iota(jnp.int32, (rows, PAGE_SIZE), 1)
        for g in range(group):
            w, carry = _stick_block(_dot_nt(qbd[g], kn_ref[g]), jnp.zeros((rows, 1), F32),
                                    _suffix_matrix(PAGE_SIZE, 2), col < step)
            acc_ref[g] = _dot(w.astype(BF16), vn_ref[g])
            carry_ref[g] = jnp.broadcast_to(carry, carry_ref.shape[1:])

    for g in range(group):
        _pages_to_bf16(kstage_ref, slot, g, kbuf_ref)
        _pages_to_bf16(vstage_ref, slot, g, vbuf_ref)
        w, carry = _stick_block(_dot(qbd[g], kbuf_ref[g]), carry_ref[g, :, 0:1], u, None)
        acc_ref[g] += _dot_nt(w.astype(BF16), vbuf_ref[g])
        carry_ref[g] = jnp.broadcast_to(carry, carry_ref.shape[1:])

    @pl.when(c == n_chunks - 1)
    def _():
        for g in range(group):
            o_ref[g] = _fold_heads(acc_ref[g], steps)


def _page_staging(n_pages, rows, lanes):
    return [pltpu.VMEM((2, n_pages, rows, lanes), F32)], [pltpu.SemaphoreType.DMA((2,))]


def _fox_sample_kernel(pt_ref, q_ref, kn_ref, vn_ref, f_ref, b_ref, ck_ref, cv_ref, clf_ref,
                       o_ref, lf_out_ref, acc_ref, m_ref, l_ref, ncum_ref, rcarry_ref,
                       kstage_ref, vstage_ref, lstage_ref, ksem, vsem, lsem,
                       kbuf_ref, vbuf_ref, lbuf_ref, *, steps, layer, n_chunks):
    group = q_ref.shape[0]
    slot = _prefetch_pages(pt_ref, (ck_ref, cv_ref, clf_ref), (kstage_ref, vstage_ref, lstage_ref),
                           (ksem, vsem, lsem), layer, group, n_chunks)
    c = _grid_step()[0] % n_chunks
    rows = N_HEADS * steps
    qbd = [_block_diag_q(q_ref.at[g]) for g in range(group)]
    stat = lambda a: jnp.broadcast_to(a, m_ref.shape[1:])

    @pl.when(c == 0)
    def _():
        step = lax.broadcasted_iota(jnp.int32, (rows, PAGE_SIZE), 0) % steps
        col = lax.broadcasted_iota(jnp.int32, (rows, PAGE_SIZE), 1)
        visible = col <= step
        lbuf_ref[...] = jnp.zeros_like(lbuf_ref)
        rcarry_ref[...] = jnp.zeros_like(rcarry_ref)
        for g in range(group):
            lf = _log_sigmoid(f_ref[g] + b_ref[...])
            lf_out_ref[g] = lf
            lf = jnp.where(col < steps, lf, 0.0)
            cum = lf
            shift = 1
            while shift < steps:
                cum = cum + jnp.where(col >= shift, pltpu.roll(cum, shift, 1), 0.0)
                shift *= 2
            ncum = jnp.sum(jnp.where(visible, lf, 0.0), axis=-1, keepdims=True)
            s = jnp.where(visible, _dot_nt(qbd[g], kn_ref[g]) + ncum - cum, NEG_INF)
            m = jnp.max(s, axis=-1, keepdims=True)
            p = jnp.exp(s - m)
            m_ref[g] = stat(m)
            l_ref[g] = stat(jnp.sum(p, axis=-1, keepdims=True))
            acc_ref[g] = _dot(p.astype(BF16), vn_ref[g])
            ncum_ref[g] = stat(ncum)

    per_seq = kbuf_ref.shape[-1] // PAGE_SIZE
    u3 = _suffix_matrix(CUMSUM_SEG, 3)
    for g in range(group):
        _pages_to_bf16(kstage_ref, slot, g, kbuf_ref)
        _pages_to_bf16(vstage_ref, slot, g, vbuf_ref)
        for p in range(per_seq):
            lbuf_ref[g, 0:N_HEADS, p * PAGE_SIZE:(p + 1) * PAGE_SIZE] = lstage_ref[slot, g * per_seq + p]
        lf_pages = lbuf_ref[g]
        incl, carry = _suffix_sums(lf_pages, u3, rcarry_ref[g, :, 0:1])
        suffix = incl - lf_pages
        rcarry_ref[g] = jnp.broadcast_to(carry, rcarry_ref.shape[1:])
        bias = jnp.concatenate(
            [jnp.broadcast_to(suffix[h:h + 1, :], (steps, suffix.shape[1])) for h in range(N_HEADS)], axis=0)
        s = _dot(qbd[g], kbuf_ref[g]) + bias + ncum_ref[g, :, 0:1]
        m_old = m_ref[g, :, 0:1]
        m = jnp.maximum(m_old, jnp.max(s, axis=-1, keepdims=True))
        alpha = jnp.exp(m_old - m)
        p = jnp.exp(s - m)
        l_ref[g] = stat(alpha * l_ref[g, :, 0:1] + jnp.sum(p, axis=-1, keepdims=True))
        acc_ref[g] = acc_ref[g] * alpha + _dot_nt(p.astype(BF16), vbuf_ref[g])
        m_ref[g] = stat(m)

    @pl.when(c == n_chunks - 1)
    def _():
        for g in range(group):
            o_ref[g] = _fold_heads(acc_ref[g] / l_ref[g, :, 0:1], steps)


def _both_groups_kernel(pt_ref, *refs, prompt_kernel, sample_kernel, counts):
    parts, at = [], 0
    for n in counts:
        parts.append(refs[at:at + n])
        at += n
    p_in, s_in, p_out, s_out, p_scr, s_scr = parts
    sample_kernel(pt_ref, *s_in, *s_out, *s_scr)
    prompt_kernel(*p_in, *p_out, *p_scr)


def _attention_both_groups(name, layer, page_table, prompt_kernel, prompt_in, prompt_extra_specs,
                           sample_kernel, sample_in, sample_extra_specs, caches, sample_extra_out,
                           sample_extra_scratch, steps):
    q, k, v = prompt_in[:3]
    b, t, w = q.shape
    blk = ATT_BLOCK
    nq = t // blk
    q_rows = sample_in[0]
    n_seq, rows, _ = q_rows.shape
    n_pages = PAGES_PER_STEP
    n_chunks = page_table.shape[1] // n_pages
    g = n_seq * n_chunks // (b * nq)
    assert g >= 1 and g * b * nq == n_seq * n_chunks, "the two groups must split into equally many grid steps"
    seq_group = lambda bi, i: (bi * nq + i) // n_chunks

    qspec = pl.BlockSpec((None, blk, w), lambda bi, i, pt: (bi, i, 0))
    kvspec = pl.BlockSpec((None, t, w), lambda bi, i, pt: (bi, 0, 0))
    per_g = lambda r, width=w: pl.BlockSpec((g, r, width), lambda bi, i, pt: (seq_group(bi, i), 0, 0))
    prompt_specs = [qspec, kvspec, kvspec] + prompt_extra_specs(b, t, blk)
    sample_specs = ([per_g(rows), per_g(PAGE_SIZE), per_g(PAGE_SIZE)] + sample_extra_specs(per_g, rows)
                    + [_HBM_SPEC] * len(caches))
    stat = pltpu.VMEM((g, rows, 128), F32)
    stages, sems = [], []
    for cache in caches:
        st, se = _page_staging(g * n_pages, cache.shape[2], PAGE_SIZE)
        stages += st
        sems += se
    prompt_scratch = [pltpu.VMEM((N_HEADS * blk, w), F32)]
    sample_scratch = ([pltpu.VMEM((g, rows, w), F32)] + sample_extra_scratch(g, stat) + stages + sems
                      + [pltpu.VMEM((g, w, n_pages * PAGE_SIZE), BF16)] * 2)
    if len(caches) == 3:
        sample_scratch.append(pltpu.VMEM((g, 8, n_pages * PAGE_SIZE), F32))
    out_specs = [qspec, per_g(steps)] + [per_g(rows, PAGE_SIZE)] * len(sample_extra_out)
    out_shape = ([jax.ShapeDtypeStruct((b, t, w), BF16), jax.ShapeDtypeStruct((n_seq, steps, w), F32)]
                 + list(sample_extra_out))
    counts = (len(prompt_specs), len(sample_specs), 1, len(out_specs) - 1,
              len(prompt_scratch), len(sample_scratch))
    grid_spec = pltpu.PrefetchScalarGridSpec(
        num_scalar_prefetch=1,
        grid=(b, nq),
        in_specs=prompt_specs + sample_specs,
        out_specs=out_specs,
        scratch_shapes=prompt_scratch + sample_scratch,
    )
    return pl.pallas_call(
        functools.partial(_both_groups_kernel, counts=counts,
                          prompt_kernel=functools.partial(prompt_kernel, bq=blk, bk=blk),
                          sample_kernel=functools.partial(sample_kernel, steps=steps, layer=layer,
                                                          n_chunks=n_chunks)),
        grid_spec=grid_spec,
        out_shape=out_shape,
        compiler_params=_cparams("arbitrary", "arbitrary"),
        name=name,
    )(page_table, *prompt_in, *sample_in, *caches)


def _sb_attention(layer, page_table, prompt_in, sample_in, caches, steps):
    none = lambda *_: []
    return _attention_both_groups(
        "sb_attention", layer, page_table, _sb_prompt_kernel, prompt_in, none,
        _sb_sample_kernel, sample_in, none, caches, [], lambda g, stat: [stat], steps)


def _fox_attention(layer, page_table, prompt_in, sample_in, caches, steps):
    n_seq, rows, _ = sample_in[0].shape
    prompt_extra = lambda b, t, blk: [
        pl.BlockSpec((None, blk, N_HEADS), lambda bi, i, pt: (bi, i, 0)),
        pl.BlockSpec((None, t // blk, N_HEADS, blk), lambda bi, i, pt: (bi, 0, 0, 0))]
    sample_extra = lambda per_g, rows: [per_g(rows, PAGE_SIZE),
                                        pl.BlockSpec((rows, PAGE_SIZE), lambda bi, i, pt: (0, 0))]
    return _attention_both_groups(
        "fox_attention", layer, page_table, _fox_prompt_kernel, prompt_in, prompt_extra,
        _fox_sample_kernel, sample_in, sample_extra, caches,
        [jax.ShapeDtypeStruct((n_seq, rows, PAGE_SIZE), F32)],
        lambda g, stat: [stat, stat, stat, pltpu.VMEM((g, 8, 128), F32)], steps)


def _layer_weights(l, norm1_g, w_in, b_forget, conv_w, conv_b, dt_bias, a_log, d_skip, ssm_norm_g,
                   w_sb_out, w_ssm_out, w_fox_out, w_o, norm2_g, w_up, w_down):
    w = jnp.transpose(w_in, (2, 0, 1))[:, l, :]
    w_main = jnp.concatenate([w[:_OFF_F], w[_OFF_Z:_OFF_DT], w[_OFF_GATE:]], axis=0).astype(BF16)
    w_small = jnp.concatenate([w[_OFF_F:_OFF_Z], w[_OFF_DT:_OFF_GATE]], axis=0)
    w_small = jnp.pad(w_small, ((0, SMALL_WIDTH - w_small.shape[0]), (0, 0))).astype(BF16)
    pad_dt = lambda a: jnp.pad(a, (DT_LANE, SMALL_WIDTH - DT_LANE - SSM_HEADS)).reshape(1, SMALL_WIDTH)
    return {
        'norm1_g': norm1_g[l].reshape(1, D_MODEL), 'w_main': w_main, 'w_small': w_small,
        'b_forget': b_forget[l], 'conv_w': conv_w[l], 'conv_b': conv_b[l].reshape(1, SSM_CONV_DIM),
        'dt_bias': pad_dt(dt_bias[l]), 'a_log': pad_dt(a_log[l]),
        'd_skip': jnp.repeat(d_skip[l], SSM_INNER // SSM_HEADS).reshape(1, SSM_INNER),
        'ssm_norm_g': ssm_norm_g[l].reshape(1, SSM_INNER),
        'w_sb_out': w_sb_out[l].astype(BF16), 'w_ssm_out': w_ssm_out[l].astype(BF16),
        'w_fox_out': w_fox_out[l].astype(BF16), 'w_o': w_o[l].astype(BF16),
        'norm2_g': norm2_g[l].reshape(1, D_MODEL),
        'w_up': w_up[l].astype(BF16), 'w_down': w_down[l].astype(BF16),
    }


def _to_heads(a, bsz, t):
    return a.reshape(bsz, t, N_HEADS, HEAD_DIM)


def _layer(xp, xs, l, depth, lw, gf, kv_prev, caches, state_ssm, state_conv, page_table):
    final = l == depth - 1
    cache_sb_k, cache_sb_v, cache_fox_k, cache_fox_v, cache_lf_t = caches
    bsz, t, d = xp.shape
    n = bsz * t
    (q_sb, q_fx, k_sb, v_sb, k_fx, v_fx, k_sb_h, v_sb_h, k_fx_h, v_fx_h,
     z, xbc, small, gates) = _inproj(xp.reshape(n, d), lw['norm1_g'], lw['w_main'], lw['w_small'],
                                     (bsz, t, l, depth, kv_prev))
    sn, st, _ = xs.shape
    m = sn * st
    (sq_sb, sq_fx, sk_sb, sv_sb, sk_fx, sv_fx, sk_sb_h, sv_sb_h, sk_fx_h, sv_fx_h,
     sz, sxbc, ssmall, sgates) = _inproj(xs.reshape(m, d), lw['norm1_g'], lw['w_main'], lw['w_small'])

    b3 = lambda a: a.reshape(bsz, t, a.shape[-1])
    s3 = lambda a: a.reshape(sn, st, a.shape[-1])
    q_rows = lambda q: jnp.tile(s3(q), (1, N_HEADS, 1))
    pad_keys = lambda a: jnp.pad(s3(a), ((0, 0), (0, PAGE_SIZE - st), (0, 0)))
    y_sb, sy_sb = _sb_attention(l, page_table, (b3(q_sb), b3(k_sb_h), b3(v_sb_h)),
                                (q_rows(sq_sb), pad_keys(sk_sb_h), pad_keys(sv_sb_h)),
                                (cache_sb_k, cache_sb_v), st)

    f_rows = small[:, :N_HEADS].reshape(bsz, t, N_HEADS).transpose(0, 2, 1)
    f_rows = f_rows.reshape(bsz * N_HEADS, t // PAGE_SIZE, PAGE_SIZE)
    logf, cum = _logf_cum(f_rows, lw['b_forget'])
    logf = logf.reshape(bsz, N_HEADS, t).transpose(0, 2, 1)
    blk = ATT_BLOCK
    cum_col = cum.reshape(bsz, N_HEADS, t).transpose(0, 2, 1)
    cum_row = cum.reshape(bsz, N_HEADS, t // blk, blk).transpose(0, 2, 1, 3)
    f_new = ssmall[:, :N_HEADS].reshape(sn, st, N_HEADS).transpose(0, 2, 1)
    sf_rows = jnp.pad(jnp.repeat(f_new, st, axis=1), ((0, 0), (0, 0), (0, PAGE_SIZE - st)))
    b_rows = jnp.broadcast_to(jnp.repeat(lw['b_forget'], st)[:, None], (N_HEADS * st, PAGE_SIZE))
    y_fx, sy_fx, lf_rows = _fox_attention(
        l, page_table, (b3(q_fx), b3(k_fx_h), b3(v_fx_h), cum_col, cum_row),
        (q_rows(sq_fx), pad_keys(sk_fx_h), pad_keys(sv_fx_h), sf_rows, b_rows),
        (cache_fox_k, cache_fox_v, cache_lf_t), st)
    slogf = lf_rows[:, ::st, :st].transpose(0, 2, 1)

    conv0 = jnp.zeros((bsz, SSM_CONV - 1, SSM_CONV_DIM), F32)
    ssm0 = jnp.zeros((bsz, SSM_INNER, SSM_STATE), F32)
    ssm_w = (lw['conv_w'], lw['conv_b'], lw['dt_bias'], lw['a_log'], lw['d_skip'], lw['ssm_norm_g'])
    y_ssm, ssm_new, conv_new = _ssm(b3(z), b3(xbc), b3(small), conv0, ssm0, *ssm_w, SSM_CHUNK)
    pad_rows = lambda a: jnp.pad(s3(a), ((0, 0), (0, SSM_CHUNK - st), (0, 0)))
    sy_ssm, sssm_new, sconv_new = _ssm(pad_rows(sz), pad_rows(sxbc), pad_rows(ssmall), state_conv[l],
                                       state_ssm[l].reshape(sn, SSM_INNER, SSM_STATE), *ssm_w, st)
    sy_ssm = sy_ssm[:, :st]

    xp_new = _merge_ffn(xp.reshape(n, d), y_sb.reshape(n, -1), y_ssm.reshape(n, -1), y_fx.reshape(n, -1),
                        gates, lw, gf, final)
    xs_new = _merge_ffn(xs.reshape(m, d), sy_sb.reshape(m, -1), sy_ssm.reshape(m, -1),
                        sy_fx.reshape(m, -1), sgates, lw, gf, final)
    ssm_shape = (SSM_HEADS, SSM_INNER // SSM_HEADS, SSM_STATE)
    prompt_states = (logf, ssm_new.reshape(bsz, *ssm_shape), conv_new)
    sample_states = (_to_heads(sk_sb, sn, st), _to_heads(sv_sb, sn, st), _to_heads(sk_fx, sn, st),
                     _to_heads(sv_fx, sn, st), slogf, sssm_new.reshape(sn, *ssm_shape), sconv_new)
    return (xp_new.reshape(bsz, t, d), xs_new.reshape(sn, st, d), (k_sb, v_sb, k_fx, v_fx),
            prompt_states, sample_states)


def kernel(x_prompt, x_sample, cache_sb_k, cache_sb_v, cache_fox_k, cache_fox_v, cache_fox_logf, state_ssm, state_conv, page_table, norm1_g, w_in, b_forget, conv_w, conv_b, dt_bias, a_log, d_skip, ssm_norm_g, w_sb_out, w_ssm_out, w_fox_out, w_o, norm2_g, w_up, w_down, final_norm_g):
    depth = w_in.shape[0]
    assert page_table.shape[1] % PAGES_PER_STEP == 0
    assert x_sample.shape[1] >= SSM_CONV - 1 and x_prompt.shape[1] % ATT_BLOCK == 0
    flat = lambda c: c.transpose(0, 1, 3, 4, 2).reshape(c.shape[0], c.shape[1], ATT_WIDTH, PAGE_SIZE)
    caches = (flat(cache_sb_k), flat(cache_sb_v), flat(cache_fox_k), flat(cache_fox_v),
              cache_fox_logf.transpose(0, 1, 3, 2))
    gf = final_norm_g.reshape(1, D_MODEL)
    xp, xs = x_prompt, x_sample
    prompt_states, sample_states = [], []
    kv_stacks = None
    for l in range(depth):
        lw = _layer_weights(l, norm1_g, w_in, b_forget, conv_w, conv_b, dt_bias, a_log, d_skip,
                            ssm_norm_g, w_sb_out, w_ssm_out, w_fox_out, w_o, norm2_g, w_up, w_down)
        xp, xs, kv_stacks, st_p, st_s = _layer(xp, xs, l, depth, lw, gf, kv_stacks, caches,
                                               state_ssm, state_conv, page_table)
        prompt_states.append(st_p)
        sample_states.append(st_s)
    bsz, t = x_prompt.shape[:2]
    from_t = lambda a: a.reshape(depth, bsz, N_HEADS, HEAD_DIM, t).transpose(0, 1, 4, 2, 3)
    stacked_p = [from_t(a) for a in kv_stacks] + [jnp.stack(s) for s in zip(*prompt_states)]
    stacked_s = [jnp.stack(s) for s in zip(*sample_states)]
    return (xp, xs, *stacked_p, *stacked_s)
```

```python
import functools

import jax
import jax.numpy as jnp
from jax import lax
from jax.experimental import pallas as pl
from jax.experimental.pallas import tpu as pltpu

F32 = jnp.float32
BF16 = jnp.bfloat16

D_MODEL = 1024
HEAD_DIM = 64
N_HEADS = 4
ATT_WIDTH = N_HEADS * HEAD_DIM
SSM_HEADS = 8
SSM_INNER = 512
SSM_STATE = 128
SSM_GROUPS = 2
SSM_CONV = 4
SSM_CONV_DIM = 1024
SSM_CHUNK = 128
PAGE_SIZE = 128
N_BRANCH = 3
FFN_HIDDEN = 4 * D_MODEL
RMS_EPS = 1e-6
NEG_INF = -1e30
Q_SCALE = HEAD_DIM ** -0.5
LOG2_E = 1.4426950408889634

_OFF_F = 6 * ATT_WIDTH
_OFF_Z = _OFF_F + N_HEADS
_OFF_XBC = _OFF_Z + SSM_INNER
_OFF_DT = _OFF_XBC + SSM_CONV_DIM
_OFF_GATE = _OFF_DT + SSM_HEADS
IN_WIDTH = _OFF_GATE + N_BRANCH * D_MODEL
MAIN_WIDTH = 6 * ATT_WIDTH + SSM_INNER + SSM_CONV_DIM + N_BRANCH * D_MODEL
SMALL_WIDTH = 128
DT_LANE = N_HEADS

V7X_VMEM_LIMIT_BYTES = 56 * 1024 * 1024
PAGES_PER_STEP = 16
ATT_BLOCK = 256
SB_SPLIT_TERMS = 1
CUMSUM_SEG = 256


def _cparams(*sem):
    return pltpu.CompilerParams(dimension_semantics=sem, vmem_limit_bytes=V7X_VMEM_LIMIT_BYTES)


def _const_spec(shape):
    n = len(shape)
    return pl.BlockSpec(shape, lambda *_: (0,) * n, pipeline_mode=pl.Buffered(1))


def _rms(x, g):
    ms = jnp.mean(x * x, axis=-1, keepdims=True)
    return x * lax.rsqrt(ms + RMS_EPS) * g


def _softplus_tail(z):
    return jnp.log1p(jnp.exp(-jnp.abs(z)))


def _log_sigmoid(z):
    return jnp.minimum(z, 0.0) - _softplus_tail(z)


def _softplus(z):
    return jnp.maximum(z, 0.0) + _softplus_tail(z)


def _dot_nt(a, b):
    return lax.dot_general(a, b, (((1,), (1,)), ((), ())), preferred_element_type=F32)


def _dot(a, b):
    return jnp.dot(a, b, preferred_element_type=F32)


def _dot_exact(a, b):
    return jnp.dot(a, b, preferred_element_type=F32, precision=lax.Precision.HIGHEST)


def _store_kv(ref, a, transposed):
    if not transposed:
        ref[...] = a
    elif len(ref.shape) == 2:
        ref[...] = a.T
    else:
        ref[0] = a.T
        ref[1:] = jnp.zeros((ref.shape[0] - 1,) + ref.shape[1:], ref.dtype)


def _inproj_kernel(x_ref, g_ref, wm_ref, ws_ref, *rest, kv_transposed, n_aliased):
    (qsb_ref, qfx_ref, ksb_ref, vsb_ref, kfx_ref, vfx_ref, ksbh_ref, vsbh_ref, kfxh_ref, vfxh_ref,
     z_ref, xbc_ref, small_ref, gate_ref) = rest[n_aliased:]
    h = _rms(x_ref[...], g_ref[...]).astype(BF16)

    def mm(c0, width):
        return _dot_nt(h, wm_ref[c0:c0 + width, :])

    w = ATT_WIDTH
    qsb_ref[...] = (mm(0, w) * (Q_SCALE * LOG2_E)).astype(BF16)
    for i, (full_ref, half_ref) in enumerate(((ksb_ref, ksbh_ref), (vsb_ref, vsbh_ref))):
        a = mm((1 + i) * w, w)
        _store_kv(full_ref, a, kv_transposed)
        half_ref[...] = a.astype(BF16)
    qfx_ref[...] = (mm(3 * w, w) * Q_SCALE).astype(BF16)
    for i, (full_ref, half_ref) in enumerate(((kfx_ref, kfxh_ref), (vfx_ref, vfxh_ref))):
        a = mm((4 + i) * w, w)
        _store_kv(full_ref, a, kv_transposed)
        half_ref[...] = a.astype(BF16)
    z_ref[...] = mm(6 * w, SSM_INNER)
    c0 = 6 * w + SSM_INNER
    for c in range(SSM_CONV_DIM // 512):
        xbc_ref[:, c * 512:(c + 1) * 512] = mm(c0 + c * 512, 512)
    c0 += SSM_CONV_DIM
    for c in range(N_BRANCH * D_MODEL // 512):
        gate_ref[:, c * 512:(c + 1) * 512] = jax.nn.sigmoid(mm(c0 + c * 512, 512)).astype(BF16)
    small_ref[...] = _dot_nt(h, ws_ref[...])


_HBM_SPEC = pl.BlockSpec(memory_space=pl.ANY)


def _inproj(x2d, g, w_main, w_small, kv_stack=None):
    n = x2d.shape[0]
    tm = min(512, n)
    row = lambda width: pl.BlockSpec((tm, width), lambda i: (i, 0))
    rows = lambda width, dt: (row(width), jax.ShapeDtypeStruct((n, width), dt))
    prev = ()
    if kv_stack is None:
        kv = rows(ATT_WIDTH, F32)
    else:
        bsz, t, layer, depth, prev = kv_stack
        prev = () if prev is None else tuple(prev)
        nt = t // tm
        if prev or depth == 1:
            block = pl.BlockSpec((None, None, ATT_WIDTH, tm), lambda i: (layer, i // nt, 0, i % nt))
        else:
            block = pl.BlockSpec((depth, None, ATT_WIDTH, tm), lambda i: (0, i // nt, 0, i % nt))
        kv = (block, jax.ShapeDtypeStruct((depth, bsz, ATT_WIDTH, t), F32))
    outs = ([rows(ATT_WIDTH, BF16)] * 2 + [kv] * 4 + [rows(ATT_WIDTH, BF16)] * 4
            + [rows(SSM_INNER, F32), rows(SSM_CONV_DIM, F32), rows(SMALL_WIDTH, F32),
               rows(N_BRANCH * D_MODEL, BF16)])
    n_in = 4
    return pl.pallas_call(
        functools.partial(_inproj_kernel, kv_transposed=kv_stack is not None, n_aliased=len(prev)),
        grid=(n // tm,),
        in_specs=[row(D_MODEL), _const_spec((1, D_MODEL)),
                  _const_spec((MAIN_WIDTH, D_MODEL)), _const_spec((SMALL_WIDTH, D_MODEL))]
                 + [_HBM_SPEC] * len(prev),
        out_specs=[spec for spec, _ in outs],
        out_shape=[shape for _, shape in outs],
        input_output_aliases={n_in + j: 2 + j for j in range(len(prev))},
        compiler_params=_cparams("parallel"),
        name="inproj",
    )(x2d, g, w_main, w_small, *prev)


def _lane_head(width=ATT_WIDTH):
    return lax.broadcasted_iota(jnp.int32, (1, width), 1) // HEAD_DIM


def _suffix_matrix(seg, terms):
    r = lax.broadcasted_iota(jnp.int32, (terms * seg, seg), 0) % seg
    c = lax.broadcasted_iota(jnp.int32, (terms * seg, seg), 1)
    return jnp.where(r >= c, 1.0, 0.0).astype(BF16)


def _split_bf16(x, terms):
    out = []
    for _ in range(terms - 1):
        head = x.astype(BF16)
        out.append(head)
        x = x - head.astype(F32)
    out.append(x.astype(BF16))
    return jnp.concatenate(out, axis=1)


def _suffix_sums(x, u, carry):
    m, n = x.shape
    seg = u.shape[1]
    n_seg = n // seg
    terms = u.shape[0] // seg
    if n_seg == 1:
        cs = _dot(_split_bf16(x, terms), u)
        return cs + carry, carry + cs[:, 0:1]
    stacked = jnp.concatenate([x[:, s * seg:(s + 1) * seg] for s in range(n_seg)], axis=0)
    cs = _dot(_split_bf16(stacked, terms), u)
    parts = [None] * n_seg
    for s in reversed(range(n_seg)):
        part = cs[s * m:(s + 1) * m, :]
        parts[s] = part + carry
        carry = carry + part[:, 0:1]
    return jnp.concatenate(parts, axis=1), carry


def _stick_block(z2, carry, u, mask):
    drop = jnp.maximum(z2, 0.0) + jnp.log2(1.0 + jnp.exp2(-jnp.abs(z2)))
    if mask is not None:
        drop = jnp.where(mask, drop, 0.0)
    later, carry = _suffix_sums(drop, u, carry)
    w = jnp.exp2(z2 - later)
    if mask is not None:
        w = jnp.where(mask, w, 0.0)
    return w, carry


def _stack_heads(q):
    lane_head = _lane_head()
    return jnp.concatenate([jnp.where(lane_head == h, q, jnp.zeros_like(q)) for h in range(N_HEADS)], axis=0)


def _fold_heads(acc, rows):
    lane_head = _lane_head()
    out = jnp.zeros((rows, ATT_WIDTH), F32)
    for h in range(N_HEADS):
        out = jnp.where(lane_head == h, acc[h * rows:(h + 1) * rows, :], out)
    return out


def _causal_blocks(bq, bk):
    q0 = pl.program_id(1) * bq
    diag = q0 // bk
    return diag, q0 - diag * bk


def _sb_prompt_kernel(q_ref, k_ref, v_ref, o_ref, acc_ref, *, bq, bk):
    diag, q_off = _causal_blocks(bq, bk)
    rows = N_HEADS * bq
    qs = _stack_heads(q_ref[...])
    u = _suffix_matrix(bk, SB_SPLIT_TERMS)
    qpos = lax.broadcasted_iota(jnp.int32, (rows, bk), 0) % bq + q_off
    causal = lax.broadcasted_iota(jnp.int32, (rows, bk), 1) < qpos
    acc_ref[...] = jnp.zeros_like(acc_ref)

    def block(j, carry, mask):
        start = pl.multiple_of(j * bk, bk)
        w, carry = _stick_block(_dot_nt(qs, k_ref[pl.ds(start, bk), :]), carry, u, mask)
        acc_ref[...] += _dot(w.astype(BF16), v_ref[pl.ds(start, bk), :])
        return carry

    carry = block(diag, jnp.zeros((rows, 1), F32), causal)
    lax.fori_loop(0, diag, lambda jj, c: block(diag - 1 - jj, c, None), carry)
    o_ref[...] = _fold_heads(acc_ref[...], bq).astype(o_ref.dtype)


def _logf_cum_kernel(b_ref, f_ref, logf_ref, cum_ref):
    h = pl.program_id(0) % N_HEADS
    logf = _log_sigmoid(f_ref[...] + b_ref[h])
    logf_ref[...] = logf
    rows = logf.shape[0]
    r = lax.broadcasted_iota(jnp.int32, (PAGE_SIZE, PAGE_SIZE), 0)
    c = lax.broadcasted_iota(jnp.int32, (PAGE_SIZE, PAGE_SIZE), 1)
    within = _dot_exact(logf, jnp.where(r <= c, 1.0, 0.0).astype(F32))
    totals = jnp.broadcast_to(within[:, PAGE_SIZE - 1:PAGE_SIZE], within.shape)
    rr = lax.broadcasted_iota(jnp.int32, (rows, rows), 0)
    cc = lax.broadcasted_iota(jnp.int32, (rows, rows), 1)
    cum_ref[...] = within + _dot_exact(jnp.where(cc < rr, 1.0, 0.0).astype(F32), totals)


def _logf_cum(f_rows, b_forget):
    n, rows, lanes = f_rows.shape
    spec = pl.BlockSpec((None, rows, lanes), lambda i: (i, 0, 0))
    return pl.pallas_call(
        _logf_cum_kernel,
        grid=(n,),
        in_specs=[pl.BlockSpec(memory_space=pltpu.SMEM), spec],
        out_specs=[spec, spec],
        out_shape=[jax.ShapeDtypeStruct(f_rows.shape, F32)] * 2,
        compiler_params=_cparams("parallel"),
        name="logf_cum",
    )(b_forget, f_rows)


def _fox_prompt_kernel(q_ref, k_ref, v_ref, cq_ref, ck_ref, o_ref, acc_ref, *, bq, bk):
    assert bq == bk
    blk = bq
    i = pl.program_id(1)
    rows = N_HEADS * blk
    qs = _stack_heads(q_ref[...])
    cq = cq_ref[...]
    cqs = jnp.concatenate([cq[:, h:h + 1] for h in range(N_HEADS)], axis=0)
    qpos = lax.broadcasted_iota(jnp.int32, (rows, blk), 0) % blk
    causal = lax.broadcasted_iota(jnp.int32, (rows, blk), 1) <= qpos
    acc_ref[...] = jnp.zeros_like(acc_ref)

    def block(j, state, mask):
        m_old, l_old = state
        start = pl.multiple_of(j * blk, blk)
        s = _dot_nt(qs, k_ref[pl.ds(start, blk), :]) + cqs
        ck = ck_ref[j]
        s = jnp.concatenate([s[h * blk:(h + 1) * blk, :] - ck[h:h + 1, :] for h in range(N_HEADS)], axis=0)
        if mask is not None:
            s = jnp.where(mask, s, NEG_INF)
        m = jnp.maximum(m_old, jnp.max(s, axis=-1, keepdims=True))
        alpha = jnp.exp(m_old - m)
        p = jnp.exp(s - m)
        acc_ref[...] = acc_ref[...] * alpha + _dot(p.astype(BF16), v_ref[pl.ds(start, blk), :])
        return m, alpha * l_old + jnp.sum(p, axis=-1, keepdims=True)

    state = block(i, (jnp.full((rows, 1), NEG_INF, F32), jnp.zeros((rows, 1), F32)), causal)
    _, l = lax.fori_loop(0, i, lambda jj, st: block(i - 1 - jj, st, None), state)
    o_ref[...] = _fold_heads(acc_ref[...] / l, blk).astype(o_ref.dtype)


_XBUF_ROW0 = 8


def _ssm_kernel(z_ref, xbc_ref, small_ref, conv0_ref, state0_ref,
                cw_ref, cb_ref, dtb_ref, alog_ref, dskip_ref, ng_ref,
                y_ref, state_out_ref, conv_out_ref, xbuf_ref, state_ref, *, valid):
    c = pl.program_id(1)
    last = pl.num_programs(1) - 1
    L = SSM_CHUNK
    r0 = _XBUF_ROW0

    @pl.when(c == 0)
    def _():
        xbuf_ref[...] = jnp.zeros_like(xbuf_ref)
        xbuf_ref[r0 - 3:r0, :] = conv0_ref[...]
        state_ref[...] = state0_ref[...]

    x_cur = xbc_ref[...]
    window = jnp.concatenate([xbuf_ref[...], x_cur], axis=0)
    cw = cw_ref[...]
    conv = cb_ref[...] + x_cur * cw[3:4, :]
    for i in range(SSM_CONV - 1):
        conv = conv + pltpu.roll(window, 3 - i, 0)[r0:, :] * cw[i:i + 1, :]
    xbuf_ref[...] = x_cur[L - r0:L, :]
    act = conv * jax.nn.sigmoid(conv)
    xs = act[:, :SSM_INNER]
    b_in = act[:, SSM_INNER:SSM_INNER + SSM_GROUPS * SSM_STATE].astype(BF16)
    c_in = act[:, SSM_INNER + SSM_GROUPS * SSM_STATE:].astype(BF16)

    row = lax.broadcasted_iota(jnp.int32, (L, L), 0)
    col = lax.broadcasted_iota(jnp.int32, (L, L), 1)
    tri = row >= col
    dt = _softplus(small_ref[...] + dtb_ref[...])
    if valid < L:
        dt = jnp.where(row < valid, dt, 0.0)
    d_a = dt * (-jnp.exp(alog_ref[...]))
    a_cs = _dot_exact(jnp.where(tri, 1.0, 0.0).astype(F32), d_a)
    a_cs_t = a_cs.T
    a_last = a_cs[L - 1:L, :]
    e_cs = jnp.exp(a_cs)
    wgt = jnp.exp(a_last - a_cs) * dt
    chunk_dec = jnp.exp(a_last)

    half = lax.broadcasted_iota(jnp.int32, (1, 128), 1) // SSM_STATE_HALF
    rhalf = lax.broadcasted_iota(jnp.int32, (128, 1), 0) // SSM_STATE_HALF
    pair_cols = lambda a, p: jnp.where(half == 0, a[:, DT_LANE + 2 * p:DT_LANE + 2 * p + 1],
                                       a[:, DT_LANE + 2 * p + 1:DT_LANE + 2 * p + 2])
    scores = [_dot_nt(c_in[:, g * SSM_STATE:(g + 1) * SSM_STATE],
                      b_in[:, g * SSM_STATE:(g + 1) * SSM_STATE]) for g in range(SSM_GROUPS)]
    ys = []
    for p in range(SSM_HEADS // 2):
        g = (2 * p) // (SSM_HEADS // SSM_GROUPS)
        bg = b_in[:, g * SSM_STATE:(g + 1) * SSM_STATE]
        cg = c_in[:, g * SSM_STATE:(g + 1) * SSM_STATE]
        xs_p = xs[:, 128 * p:128 * (p + 1)]
        xdt = (xs_p * pair_cols(dt, p)).astype(BF16)
        y_diag = jnp.zeros((L, 128), F32)
        for hh in range(2):
            lane = DT_LANE + 2 * p + hh
            seg = a_cs[:, lane:lane + 1] - a_cs_t[lane:lane + 1, :]
            decay = jnp.exp(jnp.where(tri, seg, NEG_INF))
            y_h = _dot((scores[g] * decay).astype(BF16), xdt)
            y_diag = jnp.where(half == hh, y_h, y_diag)
        st = state_ref[128 * p:128 * (p + 1), :]
        y_off = _dot_nt(cg, st.astype(BF16)) * pair_cols(e_cs, p)
        ys.append(y_diag + y_off + dskip_ref[:, 128 * p:128 * (p + 1)] * xs_p)
        xw_t = (xs_p * pair_cols(wgt, p)).T.astype(BF16)
        lane = DT_LANE + 2 * p
        dec = jnp.where(rhalf == 0,
                        jnp.broadcast_to(chunk_dec[:, lane:lane + 1], (128, SSM_STATE)),
                        jnp.broadcast_to(chunk_dec[:, lane + 1:lane + 2], (128, SSM_STATE)))
        state_ref[128 * p:128 * (p + 1), :] = st * dec + _dot(xw_t, bg)

    z = z_ref[...]
    y = jnp.concatenate(ys, axis=1) * (z * jax.nn.sigmoid(z))
    gw = SSM_INNER // SSM_GROUPS
    parts = []
    for g in range(SSM_GROUPS):
        yg = y[:, g * gw:(g + 1) * gw]
        parts.append(yg * lax.rsqrt(jnp.mean(yg * yg, axis=-1, keepdims=True) + RMS_EPS))
    y_ref[...] = (jnp.concatenate(parts, axis=1) * ng_ref[...]).astype(y_ref.dtype)

    @pl.when(c == last)
    def _():
        state_out_ref[...] = state_ref[...]
        conv_out_ref[...] = x_cur[valid - 3:valid, :]


SSM_STATE_HALF = 64


def _ssm(z, xbc, small, conv0, state0, cw, cb, dtb, alog, dskip, ng, valid):
    b, t, _ = z.shape
    nc = t // SSM_CHUNK
    chunk = lambda width: pl.BlockSpec((None, SSM_CHUNK, width), lambda bi, c: (bi, c, 0))
    per_b = lambda shape: pl.BlockSpec((None,) + shape, lambda bi, c: (bi,) + (0,) * len(shape))
    const = lambda shape: pl.BlockSpec(shape, lambda bi, c: (0,) * len(shape))
    state_rows = SSM_INNER
    return pl.pallas_call(
        functools.partial(_ssm_kernel, valid=valid),
        grid=(b, nc),
        in_specs=[chunk(SSM_INNER), chunk(SSM_CONV_DIM), chunk(SMALL_WIDTH),
                  per_b((SSM_CONV - 1, SSM_CONV_DIM)), per_b((state_rows, SSM_STATE)),
                  const((SSM_CONV, SSM_CONV_DIM)), const((1, SSM_CONV_DIM)),
                  const((1, SMALL_WIDTH)), const((1, SMALL_WIDTH)),
                  const((1, SSM_INNER)), const((1, SSM_INNER))],
        out_specs=[chunk(SSM_INNER), per_b((state_rows, SSM_STATE)),
                   per_b((SSM_CONV - 1, SSM_CONV_DIM))],
        out_shape=[jax.ShapeDtypeStruct((b, t, SSM_INNER), BF16),
                   jax.ShapeDtypeStruct((b, state_rows, SSM_STATE), F32),
                   jax.ShapeDtypeStruct((b, SSM_CONV - 1, SSM_CONV_DIM), F32)],
        scratch_shapes=[pltpu.VMEM((_XBUF_ROW0, SSM_CONV_DIM), F32),
                        pltpu.VMEM((state_rows, SSM_STATE), F32)],
        compiler_params=_cparams("parallel", "arbitrary"),
        name="ssm",
    )(z, xbc, small, conv0, state0, cw, cb, dtb, alog, dskip, ng)


def _merge_ffn_kernel(x_ref, ysb_ref, yssm_ref, yfx_ref, gate_ref,
                      wsb_ref, wssm_ref, wfx_ref, wo_ref, g2_ref, wup_ref, wdn_ref, gf_ref,
                      o_ref, *, final):
    d = D_MODEL
    mixed = gate_ref[:, 0:d].astype(F32) * _dot(ysb_ref[...].astype(BF16), wsb_ref[...])
    mixed = mixed + gate_ref[:, d:2 * d].astype(F32) * _dot(yssm_ref[...].astype(BF16), wssm_ref[...])
    mixed = mixed + gate_ref[:, 2 * d:3 * d].astype(F32) * _dot(yfx_ref[...].astype(BF16), wfx_ref[...])
    x = x_ref[...] + _dot(mixed.astype(BF16), wo_ref[...])
    h = _rms(x, g2_ref[...]).astype(BF16)
    hc = FFN_HIDDEN // 2
    for c in range(2):
        u = jnp.maximum(_dot(h, wup_ref[:, c * hc:(c + 1) * hc]), 0.0)
        x = x + _dot((u * u).astype(BF16), wdn_ref[c * hc:(c + 1) * hc, :])
    if final:
        x = _rms(x, gf_ref[...])
    o_ref[...] = x


def _merge_ffn(x2d, y_sb, y_ssm, y_fx, gates, lw, gf, final):
    n = x2d.shape[0]
    tm = min(512, n)
    row = lambda width: pl.BlockSpec((tm, width), lambda i: (i, 0))
    weights = [lw['w_sb_out'], lw['w_ssm_out'], lw['w_fox_out'], lw['w_o'], lw['norm2_g'],
               lw['w_up'], lw['w_down'], gf]
    return pl.pallas_call(
        functools.partial(_merge_ffn_kernel, final=final),
        grid=(n // tm,),
        in_specs=[row(D_MODEL), row(ATT_WIDTH), row(SSM_INNER), row(ATT_WIDTH),
                  row(N_BRANCH * D_MODEL)] + [_const_spec(w.shape) for w in weights],
        out_specs=row(D_MODEL),
        out_shape=jax.ShapeDtypeStruct((n, D_MODEL), F32),
        compiler_params=_cparams("parallel"),
        name="merge_ffn",
    )(x2d, y_sb, y_ssm, y_fx, gates, *weights)


def _block_diag_q(q_ref):
    q = q_ref[...]
    rows = q.shape[0]
    row_head = lax.broadcasted_iota(jnp.int32, (rows, 1), 0) // (rows // N_HEADS)
    return jnp.where(row_head == _lane_head(), q, jnp.zeros_like(q))


def _page_copy(cache_ref, stage_ref, sem_ref, layer, page, slot, p):
    return pltpu.make_async_copy(cache_ref.at[layer, page], stage_ref.at[slot, p], sem_ref.at[slot])


def _grid_step():
    return (pl.program_id(0) * pl.num_programs(1) + pl.program_id(1),
            pl.num_programs(0) * pl.num_programs(1))


def _prefetch_pages(pt_ref, caches, stages, sems, layer, group, n_chunks):
    n_pages = stages[0].shape[1]
    per_seq = n_pages // group
    step, n_steps = _grid_step()
    b, c = step // n_chunks, step % n_chunks
    slot = step % 2

    def start(bb, cc, sl):
        base = (n_chunks - 1 - cc) * per_seq
        for p in range(n_pages):
            page = pt_ref[bb * group + p // per_seq, base + p % per_seq]
            for i, (cache, stage, sem) in enumerate(zip(caches, stages, sems)):
                _page_copy(cache, stage, sem, layer, page, sl, p).start(priority=i % 2)

    @pl.when(step == 0)
    def _():
        start(b, c, slot)

    @pl.when(step + 1 < n_steps)
    def _():
        wrap = c + 1 == n_chunks
        start(jnp.where(wrap, b + 1, b), jnp.where(wrap, 0, c + 1), 1 - slot)

    for p in range(n_pages):
        for cache, stage, sem in zip(caches, stages, sems):
            _page_copy(cache, stage, sem, layer, 0, slot, p).wait()
    return slot


def _pages_to_bf16(stage_ref, slot, g, buf_ref):
    per_seq = buf_ref.shape[-1] // PAGE_SIZE
    for p in range(per_seq):
        buf_ref[g, :, p * PAGE_SIZE:(p + 1) * PAGE_SIZE] = stage_ref[slot, g * per_seq + p].astype(BF16)


def _sb_sample_kernel(pt_ref, q_ref, kn_ref, vn_ref, ck_ref, cv_ref, o_ref,
                      acc_ref, carry_ref, kstage_ref, vstage_ref, ksem, vsem, kbuf_ref, vbuf_ref,
                      *, steps, layer, n_chunks):
    group = q_ref.shape[0]
    slot = _prefetch_pages(pt_ref, (ck_ref, cv_ref), (kstage_ref, vstage_ref), (ksem, vsem), layer, group,
                           n_chunks)
    c = _grid_step()[0] % n_chunks
    rows = N_HEADS * steps
    qbd = [_block_diag_q(q_ref.at[g]) for g in range(group)]
    u = _suffix_matrix(CUMSUM_SEG, SB_SPLIT_TERMS)

    @pl.when(c == 0)
    def _():
        step = lax.broadcasted_iota(jnp.int32, (rows, PAGE_SIZE), 0) % steps
        col = lax.broadcasted_iota(jnp.int32, (rows, PAGE_SIZE), 1)
        for g in range(group):
            w, carry = _stick_block(_dot_nt(qbd[g], kn_ref[g]), jnp.zeros((rows, 1), F32),
                                    _suffix_matrix(PAGE_SIZE, SB_SPLIT_TERMS), col < step)
            acc_ref[g] = _dot(w.astype(BF16), vn_ref[g])
            carry_ref[g] = jnp.broadcast_to(carry, carry_ref.shape[1:])

    for g in range(group):
        _pages_to_bf16(kstage_ref, slot, g, kbuf_ref)
        _pages_to_bf16(vstage_ref, slot, g, vbuf_ref)
        w, carry = _stick_block(_dot(qbd[g], kbuf_ref[g]), carry_ref[g, :, 0:1], u, None)
        acc_ref[g] += _dot_nt(w.astype(BF16), vbuf_ref[g])
        carry_ref[g] = jnp.broadcast_to(carry, carry_ref.shape[1:])

    def finish():
        @pl.when(c == n_chunks - 1)
        def _():
            for g in range(group):
                o_ref[g] = _fold_heads(acc_ref[g], steps)
    return finish


def _page_staging(n_pages, rows, lanes):
    return [pltpu.VMEM((2, n_pages, rows, lanes), F32)], [pltpu.SemaphoreType.DMA((2,))]


def _fox_sample_kernel(pt_ref, q_ref, kn_ref, vn_ref, f_ref, b_ref, ck_ref, cv_ref, clf_ref,
                       o_ref, lf_out_ref, acc_ref, m_ref, l_ref, ncum_ref, rcarry_ref,
                       kstage_ref, vstage_ref, lstage_ref, ksem, vsem, lsem,
                       kbuf_ref, vbuf_ref, lbuf_ref, *, steps, layer, n_chunks):
    group = q_ref.shape[0]
    slot = _prefetch_pages(pt_ref, (ck_ref, cv_ref, clf_ref), (kstage_ref, vstage_ref, lstage_ref),
                           (ksem, vsem, lsem), layer, group, n_chunks)
    c = _grid_step()[0] % n_chunks
    rows = N_HEADS * steps
    qbd = [_block_diag_q(q_ref.at[g]) for g in range(group)]
    stat = lambda a: jnp.broadcast_to(a, m_ref.shape[1:])

    @pl.when(c == 0)
    def _():
        step = lax.broadcasted_iota(jnp.int32, (rows, PAGE_SIZE), 0) % steps
        col = lax.broadcasted_iota(jnp.int32, (rows, PAGE_SIZE), 1)
        visible = col <= step
        lbuf_ref[...] = jnp.zeros_like(lbuf_ref)
        rcarry_ref[...] = jnp.zeros_like(rcarry_ref)
        for g in range(group):
            lf = _log_sigmoid(f_ref[g] + b_ref[...])
            lf_out_ref[g] = lf
            lf = jnp.where(col < steps, lf, 0.0)
            cum = lf
            shift = 1
            while shift < steps:
                cum = cum + jnp.where(col >= shift, pltpu.roll(cum, shift, 1), 0.0)
                shift *= 2
            ncum = jnp.sum(jnp.where(visible, lf, 0.0), axis=-1, keepdims=True)
            s = jnp.where(visible, _dot_nt(qbd[g], kn_ref[g]) + ncum - cum, NEG_INF)
            m = jnp.max(s, axis=-1, keepdims=True)
            p = jnp.exp(s - m)
            m_ref[g] = stat(m)
            l_ref[g] = stat(jnp.sum(p, axis=-1, keepdims=True))
            acc_ref[g] = _dot(p.astype(BF16), vn_ref[g])
            ncum_ref[g] = stat(ncum)

    per_seq = kbuf_ref.shape[-1] // PAGE_SIZE
    u3 = _suffix_matrix(CUMSUM_SEG, 3)
    for g in range(group):
        _pages_to_bf16(kstage_ref, slot, g, kbuf_ref)
        _pages_to_bf16(vstage_ref, slot, g, vbuf_ref)
        for p in range(per_seq):
            lbuf_ref[g, 0:N_HEADS, p * PAGE_SIZE:(p + 1) * PAGE_SIZE] = lstage_ref[slot, g * per_seq + p]
        lf_pages = lbuf_ref[g]
        incl, carry = _suffix_sums(lf_pages, u3, rcarry_ref[g, :, 0:1])
        suffix = incl - lf_pages
        rcarry_ref[g] = jnp.broadcast_to(carry, rcarry_ref.shape[1:])
        bias = jnp.concatenate(
            [jnp.broadcast_to(suffix[h:h + 1, :], (steps, suffix.shape[1])) for h in range(N_HEADS)], axis=0)
        s = _dot(qbd[g], kbuf_ref[g]) + bias + ncum_ref[g, :, 0:1]
        m_old = m_ref[g, :, 0:1]
        m = jnp.maximum(m_old, jnp.max(s, axis=-1, keepdims=True))
        alpha = jnp.exp(m_old - m)
        p = jnp.exp(s - m)
        l_ref[g] = stat(alpha * l_ref[g, :, 0:1] + jnp.sum(p, axis=-1, keepdims=True))
        acc_ref[g] = acc_ref[g] * alpha + _dot_nt(p.astype(BF16), vbuf_ref[g])
        m_ref[g] = stat(m)

    def finish():
        @pl.when(c == n_chunks - 1)
        def _():
            for g in range(group):
                o_ref[g] = _fold_heads(acc_ref[g] / l_ref[g, :, 0:1], steps)
    return finish


def _both_groups_kernel(pt_ref, *refs, prompt_kernel, sample_kernel, counts):
    parts, at = [], 0
    for n in counts:
        parts.append(refs[at:at + n])
        at += n
    p_in, s_in, p_out, s_out, p_scr, s_scr = parts
    finish_sample = sample_kernel(pt_ref, *s_in, *s_out, *s_scr)
    prompt_kernel(*p_in, *p_out, *p_scr)
    finish_sample()


def _attention_both_groups(name, layer, page_table, prompt_kernel, prompt_in, prompt_extra_specs,
                           sample_kernel, sample_in, sample_extra_specs, caches, sample_extra_out,
                           sample_extra_scratch, steps):
    q, k, v = prompt_in[:3]
    b, t, w = q.shape
    blk = ATT_BLOCK
    nq = t // blk
    q_rows = sample_in[0]
    n_seq, rows, _ = q_rows.shape
    n_pages = PAGES_PER_STEP
    n_chunks = page_table.shape[1] // n_pages
    g = n_seq * n_chunks // (b * nq)
    assert g >= 1 and g * b * nq == n_seq * n_chunks, "the two groups must split into equally many grid steps"
    seq_group = lambda bi, i: (bi * nq + i) // n_chunks

    qspec = pl.BlockSpec((None, blk, w), lambda bi, i, pt: (bi, i, 0))
    kvspec = pl.BlockSpec((None, t, w), lambda bi, i, pt: (bi, 0, 0))
    per_g = lambda r, width=w: pl.BlockSpec((g, r, width), lambda bi, i, pt: (seq_group(bi, i), 0, 0))
    prompt_specs = [qspec, kvspec, kvspec] + prompt_extra_specs(b, t, blk)
    sample_specs = ([per_g(rows), per_g(PAGE_SIZE), per_g(PAGE_SIZE)] + sample_extra_specs(per_g, rows)
                    + [_HBM_SPEC] * len(caches))
    stat = pltpu.VMEM((g, rows, 128), F32)
    stages, sems = [], []
    for cache in caches:
        st, se = _page_staging(g * n_pages, cache.shape[2], PAGE_SIZE)
        stages += st
        sems += se
    prompt_scratch = [pltpu.VMEM((N_HEADS * blk, w), F32)]
    sample_scratch = ([pltpu.VMEM((g, rows, w), F32)] + sample_extra_scratch(g, stat) + stages + sems
                      + [pltpu.VMEM((g, w, n_pages * PAGE_SIZE), BF16)] * 2)
    if len(caches) == 3:
        sample_scratch.append(pltpu.VMEM((g, 8, n_pages * PAGE_SIZE), F32))
    out_specs = [qspec, per_g(steps)] + [per_g(rows, PAGE_SIZE)] * len(sample_extra_out)
    out_shape = ([jax.ShapeDtypeStruct((b, t, w), BF16), jax.ShapeDtypeStruct((n_seq, steps, w), F32)]
                 + list(sample_extra_out))
    counts = (len(prompt_specs), len(sample_specs), 1, len(out_specs) - 1,
              len(prompt_scratch), len(sample_scratch))
    grid_spec = pltpu.PrefetchScalarGridSpec(
        num_scalar_prefetch=1,
        grid=(b, nq),
        in_specs=prompt_specs + sample_specs,
        out_specs=out_specs,
        scratch_shapes=prompt_scratch + sample_scratch,
    )
    return pl.pallas_call(
        functools.partial(_both_groups_kernel, counts=counts,
                          prompt_kernel=functools.partial(prompt_kernel, bq=blk, bk=blk),
                          sample_kernel=functools.partial(sample_kernel, steps=steps, layer=layer,
                                                          n_chunks=n_chunks)),
        grid_spec=grid_spec,
        out_shape=out_shape,
        compiler_params=_cparams("arbitrary", "arbitrary"),
        name=name,
    )(page_table, *prompt_in, *sample_in, *caches)


def _sb_attention(layer, page_table, prompt_in, sample_in, caches, steps):
    none = lambda *_: []
    return _attention_both_groups(
        "sb_attention", layer, page_table, _sb_prompt_kernel, prompt_in, none,
        _sb_sample_kernel, sample_in, none, caches, [], lambda g, stat: [stat], steps)


def _fox_attention(layer, page_table, prompt_in, sample_in, caches, steps):
    n_seq, rows, _ = sample_in[0].shape
    prompt_extra = lambda b, t, blk: [
        pl.BlockSpec((None, blk, N_HEADS), lambda bi, i, pt: (bi, i, 0)),
        pl.BlockSpec((None, t // blk, N_HEADS, blk), lambda bi, i, pt: (bi, 0, 0, 0))]
    sample_extra = lambda per_g, rows: [per_g(rows, PAGE_SIZE),
                                        pl.BlockSpec((rows, PAGE_SIZE), lambda bi, i, pt: (0, 0))]
    return _attention_both_groups(
        "fox_attention", layer, page_table, _fox_prompt_kernel, prompt_in, prompt_extra,
        _fox_sample_kernel, sample_in, sample_extra, caches,
        [jax.ShapeDtypeStruct((n_seq, rows, PAGE_SIZE), F32)],
        lambda g, stat: [stat, stat, stat, pltpu.VMEM((g, 8, 128), F32)], steps)


def _layer_weights(l, norm1_g, w_in, b_forget, conv_w, conv_b, dt_bias, a_log, d_skip, ssm_norm_g,
                   w_sb_out, w_ssm_out, w_fox_out, w_o, norm2_g, w_up, w_down):
    w = jnp.transpose(w_in, (2, 0, 1))[:, l, :]
    w_main = jnp.concatenate([w[:_OFF_F], w[_OFF_Z:_OFF_DT], w[_OFF_GATE:]], axis=0).astype(BF16)
    w_small = jnp.concatenate([w[_OFF_F:_OFF_Z], w[_OFF_DT:_OFF_GATE]], axis=0)
    w_small = jnp.pad(w_small, ((0, SMALL_WIDTH - w_small.shape[0]), (0, 0))).astype(BF16)
    pad_dt = lambda a: jnp.pad(a, (DT_LANE, SMALL_WIDTH - DT_LANE - SSM_HEADS)).reshape(1, SMALL_WIDTH)
    return {
        'norm1_g': norm1_g[l].reshape(1, D_MODEL), 'w_main': w_main, 'w_small': w_small,
        'b_forget': b_forget[l], 'conv_w': conv_w[l], 'conv_b': conv_b[l].reshape(1, SSM_CONV_DIM),
        'dt_bias': pad_dt(dt_bias[l]), 'a_log': pad_dt(a_log[l]),
        'd_skip': jnp.repeat(d_skip[l], SSM_INNER // SSM_HEADS).reshape(1, SSM_INNER),
        'ssm_norm_g': ssm_norm_g[l].reshape(1, SSM_INNER),
        'w_sb_out': w_sb_out[l].astype(BF16), 'w_ssm_out': w_ssm_out[l].astype(BF16),
        'w_fox_out': w_fox_out[l].astype(BF16), 'w_o': w_o[l].astype(BF16),
        'norm2_g': norm2_g[l].reshape(1, D_MODEL),
        'w_up': w_up[l].astype(BF16), 'w_down': w_down[l].astype(BF16),
    }


def _to_heads(a, bsz, t):
    return a.reshape(bsz, t, N_HEADS, HEAD_DIM)


def _layer(xp, xs, l, depth, lw, gf, kv_prev, caches, state_ssm, state_conv, page_table):
    final = l == depth - 1
    cache_sb_k, cache_sb_v, cache_fox_k, cache_fox_v, cache_lf_t = caches
    bsz, t, d = xp.shape
    n = bsz * t
    (q_sb, q_fx, k_sb, v_sb, k_fx, v_fx, k_sb_h, v_sb_h, k_fx_h, v_fx_h,
     z, xbc, small, gates) = _inproj(xp.reshape(n, d), lw['norm1_g'], lw['w_main'], lw['w_small'],
                                     (bsz, t, l, depth, kv_prev))
    sn, st, _ = xs.shape
    m = sn * st
    (sq_sb, sq_fx, sk_sb, sv_sb, sk_fx, sv_fx, sk_sb_h, sv_sb_h, sk_fx_h, sv_fx_h,
     sz, sxbc, ssmall, sgates) = _inproj(xs.reshape(m, d), lw['norm1_g'], lw['w_main'], lw['w_small'])

    b3 = lambda a: a.reshape(bsz, t, a.shape[-1])
    s3 = lambda a: a.reshape(sn, st, a.shape[-1])
    q_rows = lambda q: jnp.tile(s3(q), (1, N_HEADS, 1))
    pad_keys = lambda a: jnp.pad(s3(a), ((0, 0), (0, PAGE_SIZE - st), (0, 0)))
    y_sb, sy_sb = _sb_attention(l, page_table, (b3(q_sb), b3(k_sb_h), b3(v_sb_h)),
                                (q_rows(sq_sb), pad_keys(sk_sb_h), pad_keys(sv_sb_h)),
                                (cache_sb_k, cache_sb_v), st)

    f_rows = small[:, :N_HEADS].reshape(bsz, t, N_HEADS).transpose(0, 2, 1)
    f_rows = f_rows.reshape(bsz * N_HEADS, t // PAGE_SIZE, PAGE_SIZE)
    logf, cum = _logf_cum(f_rows, lw['b_forget'])
    logf = logf.reshape(bsz, N_HEADS, t).transpose(0, 2, 1)
    blk = ATT_BLOCK
    cum_col = cum.reshape(bsz, N_HEADS, t).transpose(0, 2, 1)
    cum_row = cum.reshape(bsz, N_HEADS, t // blk, blk).transpose(0, 2, 1, 3)
    f_new = ssmall[:, :N_HEADS].reshape(sn, st, N_HEADS).transpose(0, 2, 1)
    sf_rows = jnp.pad(jnp.repeat(f_new, st, axis=1), ((0, 0), (0, 0), (0, PAGE_SIZE - st)))
    b_rows = jnp.broadcast_to(jnp.repeat(lw['b_forget'], st)[:, None], (N_HEADS * st, PAGE_SIZE))
    y_fx, sy_fx, lf_rows = _fox_attention(
        l, page_table, (b3(q_fx), b3(k_fx_h), b3(v_fx_h), cum_col, cum_row),
        (q_rows(sq_fx), pad_keys(sk_fx_h), pad_keys(sv_fx_h), sf_rows, b_rows),
        (cache_fox_k, cache_fox_v, cache_lf_t), st)
    slogf = lf_rows[:, ::st, :st].transpose(0, 2, 1)

    conv0 = jnp.zeros((bsz, SSM_CONV - 1, SSM_CONV_DIM), F32)
    ssm0 = jnp.zeros((bsz, SSM_INNER, SSM_STATE), F32)
    ssm_w = (lw['conv_w'], lw['conv_b'], lw['dt_bias'], lw['a_log'], lw['d_skip'], lw['ssm_norm_g'])
    y_ssm, ssm_new, conv_new = _ssm(b3(z), b3(xbc), b3(small), conv0, ssm0, *ssm_w, SSM_CHUNK)
    pad_rows = lambda a: jnp.pad(s3(a), ((0, 0), (0, SSM_CHUNK - st), (0, 0)))
    sy_ssm, sssm_new, sconv_new = _ssm(pad_rows(sz), pad_rows(sxbc), pad_rows(ssmall), state_conv[l],
                                       state_ssm[l].reshape(sn, SSM_INNER, SSM_STATE), *ssm_w, st)
    sy_ssm = sy_ssm[:, :st]

    xp_new = _merge_ffn(xp.reshape(n, d), y_sb.reshape(n, -1), y_ssm.reshape(n, -1), y_fx.reshape(n, -1),
                        gates, lw, gf, final)
    xs_new = _merge_ffn(xs.reshape(m, d), sy_sb.reshape(m, -1), sy_ssm.reshape(m, -1),
                        sy_fx.reshape(m, -1), sgates, lw, gf, final)
    ssm_shape = (SSM_HEADS, SSM_INNER // SSM_HEADS, SSM_STATE)
    prompt_states = (logf, ssm_new.reshape(bsz, *ssm_shape), conv_new)
    sample_states = (_to_heads(sk_sb, sn, st), _to_heads(sv_sb, sn, st), _to_heads(sk_fx, sn, st),
                     _to_heads(sv_fx, sn, st), slogf, sssm_new.reshape(sn, *ssm_shape), sconv_new)
    return (xp_new.reshape(bsz, t, d), xs_new.reshape(sn, st, d), (k_sb, v_sb, k_fx, v_fx),
            prompt_states, sample_states)


def kernel(x_prompt, x_sample, cache_sb_k, cache_sb_v, cache_fox_k, cache_fox_v, cache_fox_logf, state_ssm, state_conv, page_table, norm1_g, w_in, b_forget, conv_w, conv_b, dt_bias, a_log, d_skip, ssm_norm_g, w_sb_out, w_ssm_out, w_fox_out, w_o, norm2_g, w_up, w_down, final_norm_g):
    depth = w_in.shape[0]
    assert page_table.shape[1] % PAGES_PER_STEP == 0
    assert x_sample.shape[1] >= SSM_CONV - 1 and x_prompt.shape[1] % ATT_BLOCK == 0
    flat = lambda c: c.transpose(0, 1, 3, 4, 2).reshape(c.shape[0], c.shape[1], ATT_WIDTH, PAGE_SIZE)
    caches = (flat(cache_sb_k), flat(cache_sb_v), flat(cache_fox_k), flat(cache_fox_v),
              cache_fox_logf.transpose(0, 1, 3, 2))
    gf = final_norm_g.reshape(1, D_MODEL)
    xp, xs = x_prompt, x_sample
    prompt_states, sample_states = [], []
    kv_stacks = None
    for l in range(depth):
        lw = _layer_weights(l, norm1_g, w_in, b_forget, conv_w, conv_b, dt_bias, a_log, d_skip,
                            ssm_norm_g, w_sb_out, w_ssm_out, w_fox_out, w_o, norm2_g, w_up, w_down)
        xp, xs, kv_stacks, st_p, st_s = _layer(xp, xs, l, depth, lw, gf, kv_stacks, caches,
                                               state_ssm, state_conv, page_table)
        prompt_states.append(st_p)
        sample_states.append(st_s)
    bsz, t = x_prompt.shape[:2]
    from_t = lambda a: a.reshape(depth, bsz, N_HEADS, HEAD_DIM, t).transpose(0, 1, 4, 2, 3)
    stacked_p = [from_t(a) for a in kv_stacks] + [jnp.stack(s) for s in zip(*prompt_states)]
    stacked_s = [jnp.stack(s) for s in zip(*sample_states)]
    return (xp, xs, *stacked_p, *stacked_s)
```

```python
import functools

import jax
import jax.numpy as jnp
from jax import lax
from jax.experimental import pallas as pl
from jax.experimental.pallas import tpu as pltpu

F32 = jnp.float32
BF16 = jnp.bfloat16

D_MODEL = 1024
HEAD_DIM = 64
N_HEADS = 4
ATT_WIDTH = N_HEADS * HEAD_DIM
SSM_HEADS = 8
SSM_INNER = 512
SSM_STATE = 128
SSM_GROUPS = 2
SSM_CONV = 4
SSM_CONV_DIM = 1024
SSM_CHUNK = 128
PAGE_SIZE = 128
N_BRANCH = 3
FFN_HIDDEN = 4 * D_MODEL
RMS_EPS = 1e-6
NEG_INF = -1e30
Q_SCALE = HEAD_DIM ** -0.5
LOG2_E = 1.4426950408889634

_OFF_F = 6 * ATT_WIDTH
_OFF_Z = _OFF_F + N_HEADS
_OFF_XBC = _OFF_Z + SSM_INNER
_OFF_DT = _OFF_XBC + SSM_CONV_DIM
_OFF_GATE = _OFF_DT + SSM_HEADS
IN_WIDTH = _OFF_GATE + N_BRANCH * D_MODEL
MAIN_WIDTH = 6 * ATT_WIDTH + SSM_INNER + SSM_CONV_DIM + N_BRANCH * D_MODEL
SMALL_WIDTH = 128
DT_LANE = N_HEADS

V7X_VMEM_LIMIT_BYTES = 56 * 1024 * 1024
PAGES_PER_STEP = 16
ATT_BLOCK = 256
SB_SPLIT_TERMS = 1
CUMSUM_SEG = 256


def _cparams(*sem):
    return pltpu.CompilerParams(dimension_semantics=sem, vmem_limit_bytes=V7X_VMEM_LIMIT_BYTES)


def _const_spec(shape):
    n = len(shape)
    return pl.BlockSpec(shape, lambda *_: (0,) * n, pipeline_mode=pl.Buffered(1))


def _rms(x, g):
    ms = jnp.mean(x * x, axis=-1, keepdims=True)
    return x * lax.rsqrt(ms + RMS_EPS) * g


def _softplus_tail(z):
    return jnp.log1p(jnp.exp(-jnp.abs(z)))


def _log_sigmoid(z):
    return jnp.minimum(z, 0.0) - _softplus_tail(z)


def _softplus(z):
    return jnp.maximum(z, 0.0) + _softplus_tail(z)


def _dot_nt(a, b):
    return lax.dot_general(a, b, (((1,), (1,)), ((), ())), preferred_element_type=F32)


def _dot(a, b):
    return jnp.dot(a, b, preferred_element_type=F32)


def _dot_exact(a, b):
    return jnp.dot(a, b, preferred_element_type=F32, precision=lax.Precision.HIGHEST)


def _store_kv(ref, a, transposed):
    if not transposed:
        ref[...] = a
    elif len(ref.shape) == 2:
        ref[...] = a.T
    else:
        ref[0] = a.T
        ref[1:] = jnp.zeros((ref.shape[0] - 1,) + ref.shape[1:], ref.dtype)


def _inproj_kernel(x_ref, g_ref, wm_ref, ws_ref, *rest, kv_transposed, n_aliased):
    (qsb_ref, qfx_ref, ksb_ref, vsb_ref, kfx_ref, vfx_ref, ksbh_ref, vsbh_ref, kfxh_ref, vfxh_ref,
     z_ref, xbc_ref, small_ref, gate_ref) = rest[n_aliased:]
    h = _rms(x_ref[...], g_ref[...]).astype(BF16)

    def mm(c0, width):
        return _dot_nt(h, wm_ref[c0:c0 + width, :])

    w = ATT_WIDTH
    qsb_ref[...] = (mm(0, w) * (Q_SCALE * LOG2_E)).astype(BF16)
    for i, (full_ref, half_ref) in enumerate(((ksb_ref, ksbh_ref), (vsb_ref, vsbh_ref))):
        a = mm((1 + i) * w, w)
        _store_kv(full_ref, a, kv_transposed)
        half_ref[...] = a.astype(BF16)
    qfx_ref[...] = (mm(3 * w, w) * (Q_SCALE * LOG2_E)).astype(BF16)
    for i, (full_ref, half_ref) in enumerate(((kfx_ref, kfxh_ref), (vfx_ref, vfxh_ref))):
        a = mm((4 + i) * w, w)
        _store_kv(full_ref, a, kv_transposed)
        half_ref[...] = a.astype(BF16)
    z_ref[...] = mm(6 * w, SSM_INNER)
    c0 = 6 * w + SSM_INNER
    for c in range(SSM_CONV_DIM // 512):
        xbc_ref[:, c * 512:(c + 1) * 512] = mm(c0 + c * 512, 512)
    c0 += SSM_CONV_DIM
    for c in range(N_BRANCH * D_MODEL // 512):
        gate_ref[:, c * 512:(c + 1) * 512] = jax.nn.sigmoid(mm(c0 + c * 512, 512)).astype(BF16)
    small_ref[...] = _dot_nt(h, ws_ref[...])


_HBM_SPEC = pl.BlockSpec(memory_space=pl.ANY)


def _inproj(x2d, g, w_main, w_small, kv_stack=None):
    n = x2d.shape[0]
    tm = min(512, n)
    row = lambda width: pl.BlockSpec((tm, width), lambda i: (i, 0))
    rows = lambda width, dt: (row(width), jax.ShapeDtypeStruct((n, width), dt))
    prev = ()
    if kv_stack is None:
        kv = rows(ATT_WIDTH, F32)
    else:
        bsz, t, layer, depth, prev = kv_stack
        prev = () if prev is None else tuple(prev)
        nt = t // tm
        if prev or depth == 1:
            block = pl.BlockSpec((None, None, ATT_WIDTH, tm), lambda i: (layer, i // nt, 0, i % nt))
        else:
            block = pl.BlockSpec((depth, None, ATT_WIDTH, tm), lambda i: (0, i // nt, 0, i % nt))
        kv = (block, jax.ShapeDtypeStruct((depth, bsz, ATT_WIDTH, t), F32))
    outs = ([rows(ATT_WIDTH, BF16)] * 2 + [kv] * 4 + [rows(ATT_WIDTH, BF16)] * 4
            + [rows(SSM_INNER, F32), rows(SSM_CONV_DIM, F32), rows(SMALL_WIDTH, F32),
               rows(N_BRANCH * D_MODEL, BF16)])
    n_in = 4
    return pl.pallas_call(
        functools.partial(_inproj_kernel, kv_transposed=kv_stack is not None, n_aliased=len(prev)),
        grid=(n // tm,),
        in_specs=[row(D_MODEL), _const_spec((1, D_MODEL)),
                  _const_spec((MAIN_WIDTH, D_MODEL)), _const_spec((SMALL_WIDTH, D_MODEL))]
                 + [_HBM_SPEC] * len(prev),
        out_specs=[spec for spec, _ in outs],
        out_shape=[shape for _, shape in outs],
        input_output_aliases={n_in + j: 2 + j for j in range(len(prev))},
        compiler_params=_cparams("parallel"),
        name="inproj",
    )(x2d, g, w_main, w_small, *prev)


def _lane_head(width=ATT_WIDTH):
    return lax.broadcasted_iota(jnp.int32, (1, width), 1) // HEAD_DIM


def _suffix_matrix(seg, terms):
    r = lax.broadcasted_iota(jnp.int32, (terms * seg, seg), 0) % seg
    c = lax.broadcasted_iota(jnp.int32, (terms * seg, seg), 1)
    return jnp.where(r >= c, 1.0, 0.0).astype(BF16)


def _split_bf16(x, terms):
    out = []
    for _ in range(terms - 1):
        head = x.astype(BF16)
        out.append(head)
        x = x - head.astype(F32)
    out.append(x.astype(BF16))
    return jnp.concatenate(out, axis=1)


def _suffix_sums(x, u, carry):
    m, n = x.shape
    seg = u.shape[1]
    n_seg = n // seg
    terms = u.shape[0] // seg
    if n_seg == 1:
        cs = _dot(_split_bf16(x, terms), u)
        return cs + carry, carry + cs[:, 0:1]
    stacked = jnp.concatenate([x[:, s * seg:(s + 1) * seg] for s in range(n_seg)], axis=0)
    cs = _dot(_split_bf16(stacked, terms), u)
    parts = [None] * n_seg
    for s in reversed(range(n_seg)):
        part = cs[s * m:(s + 1) * m, :]
        parts[s] = part + carry
        carry = carry + part[:, 0:1]
    return jnp.concatenate(parts, axis=1), carry


def _stick_block(z2, carry, u, mask):
    drop = jnp.maximum(z2, 0.0) + jnp.log2(1.0 + jnp.exp2(-jnp.abs(z2)))
    if mask is not None:
        drop = jnp.where(mask, drop, 0.0)
    later, carry = _suffix_sums(drop, u, carry)
    w = jnp.exp2(z2 - later)
    if mask is not None:
        w = jnp.where(mask, w, 0.0)
    return w, carry


def _stack_heads(q):
    lane_head = _lane_head()
    return jnp.concatenate([jnp.where(lane_head == h, q, jnp.zeros_like(q)) for h in range(N_HEADS)], axis=0)


def _fold_heads(acc, rows):
    lane_head = _lane_head()
    out = jnp.zeros((rows, ATT_WIDTH), F32)
    for h in range(N_HEADS):
        out = jnp.where(lane_head == h, acc[h * rows:(h + 1) * rows, :], out)
    return out


def _sb_prompt_kernel(q_ref, k_ref, v_ref, o_ref, acc_ref, *, bq, bk):
    assert bq == bk
    i = pl.program_id(1)
    rows = N_HEADS * bq
    qs = _stack_heads(q_ref[...])
    u = _suffix_matrix(CUMSUM_SEG, SB_SPLIT_TERMS)
    qpos = lax.broadcasted_iota(jnp.int32, (rows, bk), 0) % bq
    causal = lax.broadcasted_iota(jnp.int32, (rows, bk), 1) < qpos

    def block(start, width, carry, mask):
        start = pl.multiple_of(start, width)
        w, carry = _stick_block(_dot_nt(qs, k_ref[pl.ds(start, width), :]), carry, u, mask)
        pv = _dot(w.astype(BF16), v_ref[pl.ds(start, width), :])
        acc_ref[...] = pv if mask is not None else acc_ref[...] + pv
        return carry

    carry = block(i * bk, bk, jnp.zeros((rows, 1), F32), causal)
    carry = lax.fori_loop(0, i % 2, lambda _, c: block((i - 1) * bk, bk, c, None), carry)
    pairs = i // 2
    lax.fori_loop(0, pairs, lambda jj, c: block((pairs - 1 - jj) * 2 * bk, 2 * bk, c, None), carry)
    o_ref[...] = _fold_heads(acc_ref[...], bq).astype(o_ref.dtype)


def _logf_cum_kernel(b_ref, f_ref, logf_ref, cum_ref):
    h = pl.program_id(0) % N_HEADS
    logf = _log_sigmoid(f_ref[...] + b_ref[h])
    logf_ref[...] = logf
    rows = logf.shape[0]
    r = lax.broadcasted_iota(jnp.int32, (PAGE_SIZE, PAGE_SIZE), 0)
    c = lax.broadcasted_iota(jnp.int32, (PAGE_SIZE, PAGE_SIZE), 1)
    within = _dot_exact(logf, jnp.where(r <= c, 1.0, 0.0).astype(F32))
    totals = jnp.broadcast_to(within[:, PAGE_SIZE - 1:PAGE_SIZE], within.shape)
    rr = lax.broadcasted_iota(jnp.int32, (rows, rows), 0)
    cc = lax.broadcasted_iota(jnp.int32, (rows, rows), 1)
    cum_ref[...] = within + _dot_exact(jnp.where(cc < rr, 1.0, 0.0).astype(F32), totals)


def _logf_cum(f_rows, b_forget):
    n, rows, lanes = f_rows.shape
    spec = pl.BlockSpec((None, rows, lanes), lambda i: (i, 0, 0))
    return pl.pallas_call(
        _logf_cum_kernel,
        grid=(n,),
        in_specs=[pl.BlockSpec(memory_space=pltpu.SMEM), spec],
        out_specs=[spec, spec],
        out_shape=[jax.ShapeDtypeStruct(f_rows.shape, F32)] * 2,
        compiler_params=_cparams("parallel"),
        name="logf_cum",
    )(b_forget, f_rows)


def _fox_prompt_kernel(q_ref, k_ref, v_ref, cq_ref, ck_ref, o_ref, acc_ref, *, bq, bk):
    assert bq == bk
    blk = bq
    i = pl.program_id(1)
    rows = N_HEADS * blk
    qs = _stack_heads(q_ref[...])
    cq = cq_ref[...]
    cqs = jnp.concatenate([cq[:, h:h + 1] for h in range(N_HEADS)], axis=0)
    qpos = lax.broadcasted_iota(jnp.int32, (rows, blk), 0) % blk
    causal = lax.broadcasted_iota(jnp.int32, (rows, blk), 1) <= qpos

    def block(j, state, mask):
        start = pl.multiple_of(j * blk, blk)
        s = _dot_nt(qs, k_ref[pl.ds(start, blk), :]) + cqs
        ck = ck_ref[j]
        s = jnp.concatenate([s[h * blk:(h + 1) * blk, :] - ck[h:h + 1, :] for h in range(N_HEADS)], axis=0)
        if mask is not None:
            s = jnp.where(mask, s, NEG_INF)
        m = jnp.max(s, axis=-1, keepdims=True)
        if state is not None:
            m_old, l_old = state
            m = jnp.maximum(m_old, m)
        p = jnp.exp2(s - m)
        pv = _dot(p.astype(BF16), v_ref[pl.ds(start, blk), :])
        l = jnp.sum(p, axis=-1, keepdims=True)
        if state is None:
            acc_ref[...] = pv
            return m, l
        alpha = jnp.exp2(m_old - m)
        acc_ref[...] = acc_ref[...] * alpha + pv
        return m, alpha * l_old + l

    state = block(i, None, causal)
    _, l = lax.fori_loop(0, i, lambda jj, st: block(i - 1 - jj, st, None), state)
    o_ref[...] = _fold_heads(acc_ref[...] / l, blk).astype(o_ref.dtype)


_XBUF_ROW0 = 8


def _ssm_kernel(z_ref, xbc_ref, small_ref, conv0_ref, state0_ref,
                cw_ref, cb_ref, dtb_ref, alog_ref, dskip_ref, ng_ref,
                y_ref, state_out_ref, conv_out_ref, xbuf_ref, state_ref, *, valid):
    c = pl.program_id(1)
    last = pl.num_programs(1) - 1
    L = SSM_CHUNK
    r0 = _XBUF_ROW0

    @pl.when(c == 0)
    def _():
        xbuf_ref[...] = jnp.zeros_like(xbuf_ref)
        xbuf_ref[r0 - 3:r0, :] = conv0_ref[...]
        state_ref[...] = state0_ref[...]

    x_cur = xbc_ref[...]
    window = jnp.concatenate([xbuf_ref[...], x_cur], axis=0)
    cw = cw_ref[...]
    conv = cb_ref[...] + x_cur * cw[3:4, :]
    for i in range(SSM_CONV - 1):
        conv = conv + pltpu.roll(window, 3 - i, 0)[r0:, :] * cw[i:i + 1, :]
    xbuf_ref[...] = x_cur[L - r0:L, :]
    act = conv * jax.nn.sigmoid(conv)
    xs = act[:, :SSM_INNER]
    b_in = act[:, SSM_INNER:SSM_INNER + SSM_GROUPS * SSM_STATE].astype(BF16)
    c_in = act[:, SSM_INNER + SSM_GROUPS * SSM_STATE:].astype(BF16)

    row = lax.broadcasted_iota(jnp.int32, (L, L), 0)
    col = lax.broadcasted_iota(jnp.int32, (L, L), 1)
    tri = row >= col
    dt = _softplus(small_ref[...] + dtb_ref[...])
    if valid < L:
        dt = jnp.where(row < valid, dt, 0.0)
    d_a = dt * (-jnp.exp(alog_ref[...]))
    a_cs = _dot_exact(jnp.where(tri, 1.0, 0.0).astype(F32), d_a)
    a_cs_t = a_cs.T
    a_last = a_cs[L - 1:L, :]
    e_cs = jnp.exp(a_cs)
    wgt = jnp.exp(a_last - a_cs) * dt
    chunk_dec = jnp.exp(a_last)

    half = lax.broadcasted_iota(jnp.int32, (1, 128), 1) // SSM_STATE_HALF
    rhalf = lax.broadcasted_iota(jnp.int32, (128, 1), 0) // SSM_STATE_HALF
    pair_cols = lambda a, p: jnp.where(half == 0, a[:, DT_LANE + 2 * p:DT_LANE + 2 * p + 1],
                                       a[:, DT_LANE + 2 * p + 1:DT_LANE + 2 * p + 2])
    scores = [_dot_nt(c_in[:, g * SSM_STATE:(g + 1) * SSM_STATE],
                      b_in[:, g * SSM_STATE:(g + 1) * SSM_STATE]) for g in range(SSM_GROUPS)]
    ys = []
    for p in range(SSM_HEADS // 2):
        g = (2 * p) // (SSM_HEADS // SSM_GROUPS)
        bg = b_in[:, g * SSM_STATE:(g + 1) * SSM_STATE]
        cg = c_in[:, g * SSM_STATE:(g + 1) * SSM_STATE]
        xs_p = xs[:, 128 * p:128 * (p + 1)]
        xdt = (xs_p * pair_cols(dt, p)).astype(BF16)
        y_diag = jnp.zeros((L, 128), F32)
        for hh in range(2):
            lane = DT_LANE + 2 * p + hh
            seg = a_cs[:, lane:lane + 1] - a_cs_t[lane:lane + 1, :]
            decay = jnp.exp(jnp.where(tri, seg, NEG_INF))
            y_h = _dot((scores[g] * decay).astype(BF16), xdt)
            y_diag = jnp.where(half == hh, y_h, y_diag)
        st = state_ref[128 * p:128 * (p + 1), :]
        y_off = _dot_nt(cg, st.astype(BF16)) * pair_cols(e_cs, p)
        ys.append(y_diag + y_off + dskip_ref[:, 128 * p:128 * (p + 1)] * xs_p)
        xw_t = (xs_p * pair_cols(wgt, p)).T.astype(BF16)
        lane = DT_LANE + 2 * p
        dec = jnp.where(rhalf == 0,
                        jnp.broadcast_to(chunk_dec[:, lane:lane + 1], (128, SSM_STATE)),
                        jnp.broadcast_to(chunk_dec[:, lane + 1:lane + 2], (128, SSM_STATE)))
        state_ref[128 * p:128 * (p + 1), :] = st * dec + _dot(xw_t, bg)

    z = z_ref[...]
    y = jnp.concatenate(ys, axis=1) * (z * jax.nn.sigmoid(z))
    gw = SSM_INNER // SSM_GROUPS
    parts = []
    for g in range(SSM_GROUPS):
        yg = y[:, g * gw:(g + 1) * gw]
        parts.append(yg * lax.rsqrt(jnp.mean(yg * yg, axis=-1, keepdims=True) + RMS_EPS))
    y_ref[...] = (jnp.concatenate(parts, axis=1) * ng_ref[...]).astype(y_ref.dtype)

    @pl.when(c == last)
    def _():
        state_out_ref[...] = state_ref[...]
        conv_out_ref[...] = x_cur[valid - 3:valid, :]


SSM_STATE_HALF = 64


def _ssm(z, xbc, small, conv0, state0, cw, cb, dtb, alog, dskip, ng, valid):
    b, t, _ = z.shape
    nc = t // SSM_CHUNK
    chunk = lambda width: pl.BlockSpec((None, SSM_CHUNK, width), lambda bi, c: (bi, c, 0))
    per_b = lambda shape: pl.BlockSpec((None,) + shape, lambda bi, c: (bi,) + (0,) * len(shape))
    const = lambda shape: pl.BlockSpec(shape, lambda bi, c: (0,) * len(shape))
    state_rows = SSM_INNER
    return pl.pallas_call(
        functools.partial(_ssm_kernel, valid=valid),
        grid=(b, nc),
        in_specs=[chunk(SSM_INNER), chunk(SSM_CONV_DIM), chunk(SMALL_WIDTH),
                  per_b((SSM_CONV - 1, SSM_CONV_DIM)), per_b((state_rows, SSM_STATE)),
                  const((SSM_CONV, SSM_CONV_DIM)), const((1, SSM_CONV_DIM)),
                  const((1, SMALL_WIDTH)), const((1, SMALL_WIDTH)),
                  const((1, SSM_INNER)), const((1, SSM_INNER))],
        out_specs=[chunk(SSM_INNER), per_b((state_rows, SSM_STATE)),
                   per_b((SSM_CONV - 1, SSM_CONV_DIM))],
        out_shape=[jax.ShapeDtypeStruct((b, t, SSM_INNER), BF16),
                   jax.ShapeDtypeStruct((b, state_rows, SSM_STATE), F32),
                   jax.ShapeDtypeStruct((b, SSM_CONV - 1, SSM_CONV_DIM), F32)],
        scratch_shapes=[pltpu.VMEM((_XBUF_ROW0, SSM_CONV_DIM), F32),
                        pltpu.VMEM((state_rows, SSM_STATE), F32)],
        compiler_params=_cparams("parallel", "arbitrary"),
        name="ssm",
    )(z, xbc, small, conv0, state0, cw, cb, dtb, alog, dskip, ng)


def _merge_ffn_kernel(x_ref, ysb_ref, yssm_ref, yfx_ref, gate_ref,
                      wsb_ref, wssm_ref, wfx_ref, wo_ref, g2_ref, wup_ref, wdn_ref, gf_ref,
                      o_ref, *, final):
    d = D_MODEL
    mixed = gate_ref[:, 0:d].astype(F32) * _dot(ysb_ref[...].astype(BF16), wsb_ref[...])
    mixed = mixed + gate_ref[:, d:2 * d].astype(F32) * _dot(yssm_ref[...].astype(BF16), wssm_ref[...])
    mixed = mixed + gate_ref[:, 2 * d:3 * d].astype(F32) * _dot(yfx_ref[...].astype(BF16), wfx_ref[...])
    x = x_ref[...] + _dot(mixed.astype(BF16), wo_ref[...])
    h = _rms(x, g2_ref[...]).astype(BF16)
    hc = FFN_HIDDEN // 2
    for c in range(2):
        u = jnp.maximum(_dot(h, wup_ref[:, c * hc:(c + 1) * hc]), 0.0)
        x = x + _dot((u * u).astype(BF16), wdn_ref[c * hc:(c + 1) * hc, :])
    if final:
        x = _rms(x, gf_ref[...])
    o_ref[...] = x


def _merge_ffn(x2d, y_sb, y_ssm, y_fx, gates, lw, gf, final):
    n = x2d.shape[0]
    tm = min(512, n)
    row = lambda width: pl.BlockSpec((tm, width), lambda i: (i, 0))
    weights = [lw['w_sb_out'], lw['w_ssm_out'], lw['w_fox_out'], lw['w_o'], lw['norm2_g'],
               lw['w_up'], lw['w_down'], gf]
    return pl.pallas_call(
        functools.partial(_merge_ffn_kernel, final=final),
        grid=(n // tm,),
        in_specs=[row(D_MODEL), row(ATT_WIDTH), row(SSM_INNER), row(ATT_WIDTH),
                  row(N_BRANCH * D_MODEL)] + [_const_spec(w.shape) for w in weights],
        out_specs=row(D_MODEL),
        out_shape=jax.ShapeDtypeStruct((n, D_MODEL), F32),
        compiler_params=_cparams("parallel"),
        name="merge_ffn",
    )(x2d, y_sb, y_ssm, y_fx, gates, *weights)


def _block_diag_q(q_ref):
    q = q_ref[...]
    rows = q.shape[0]
    row_head = lax.broadcasted_iota(jnp.int32, (rows, 1), 0) // (rows // N_HEADS)
    return jnp.where(row_head == _lane_head(), q, jnp.zeros_like(q))


def _page_copy(cache_ref, stage_ref, sem_ref, layer, page, slot, p):
    return pltpu.make_async_copy(cache_ref.at[layer, page], stage_ref.at[slot, p], sem_ref.at[slot])


def _grid_step():
    return (pl.program_id(0) * pl.num_programs(1) + pl.program_id(1),
            pl.num_programs(0) * pl.num_programs(1))


def _prefetch_pages(pt_ref, caches, stages, sems, layer, group, n_chunks):
    n_pages = stages[0].shape[1]
    per_seq = n_pages // group
    step, n_steps = _grid_step()
    b, c = step // n_chunks, step % n_chunks
    slot = step % 2

    def start(bb, cc, sl):
        base = (n_chunks - 1 - cc) * per_seq
        for p in range(n_pages):
            page = pt_ref[bb * group + p // per_seq, base + p % per_seq]
            for i, (cache, stage, sem) in enumerate(zip(caches, stages, sems)):
                _page_copy(cache, stage, sem, layer, page, sl, p).start(priority=i % 2)

    @pl.when(step == 0)
    def _():
        start(b, c, slot)

    @pl.when(step + 1 < n_steps)
    def _():
        wrap = c + 1 == n_chunks
        start(jnp.where(wrap, b + 1, b), jnp.where(wrap, 0, c + 1), 1 - slot)

    for p in range(n_pages):
        for cache, stage, sem in zip(caches, stages, sems):
            _page_copy(cache, stage, sem, layer, 0, slot, p).wait()
    return slot


def _pages_to_bf16(stage_ref, slot, g, buf_ref):
    per_seq = buf_ref.shape[-1] // PAGE_SIZE
    for p in range(per_seq):
        buf_ref[g, :, p * PAGE_SIZE:(p + 1) * PAGE_SIZE] = stage_ref[slot, g * per_seq + p].astype(BF16)


def _sb_sample_kernel(pt_ref, q_ref, kn_ref, vn_ref, ck_ref, cv_ref, o_ref,
                      acc_ref, carry_ref, kstage_ref, vstage_ref, ksem, vsem, kbuf_ref, vbuf_ref,
                      *, steps, layer, n_chunks):
    group = q_ref.shape[0]
    slot = _prefetch_pages(pt_ref, (ck_ref, cv_ref), (kstage_ref, vstage_ref), (ksem, vsem), layer, group,
                           n_chunks)
    c = _grid_step()[0] % n_chunks
    rows = N_HEADS * steps
    qbd = [_block_diag_q(q_ref.at[g]) for g in range(group)]
    u = _suffix_matrix(CUMSUM_SEG, SB_SPLIT_TERMS)

    @pl.when(c == 0)
    def _():
        step = lax.broadcasted_iota(jnp.int32, (rows, PAGE_SIZE), 0) % steps
        col = lax.broadcasted_iota(jnp.int32, (rows, PAGE_SIZE), 1)
        for g in range(group):
            w, carry = _stick_block(_dot_nt(qbd[g], kn_ref[g]), jnp.zeros((rows, 1), F32),
                                    _suffix_matrix(PAGE_SIZE, SB_SPLIT_TERMS), col < step)
            acc_ref[g] = _dot(w.astype(BF16), vn_ref[g])
            carry_ref[g] = jnp.broadcast_to(carry, carry_ref.shape[1:])

    for g in range(group):
        _pages_to_bf16(kstage_ref, slot, g, kbuf_ref)
        _pages_to_bf16(vstage_ref, slot, g, vbuf_ref)
        w, carry = _stick_block(_dot(qbd[g], kbuf_ref[g]), carry_ref[g, :, 0:1], u, None)
        acc_ref[g] += _dot_nt(w.astype(BF16), vbuf_ref[g])
        carry_ref[g] = jnp.broadcast_to(carry, carry_ref.shape[1:])

    def finish():
        @pl.when(c == n_chunks - 1)
        def _():
            for g in range(group):
                o_ref[g] = _fold_heads(acc_ref[g], steps)
    return finish


def _page_staging(n_pages, rows, lanes):
    return [pltpu.VMEM((2, n_pages, rows, lanes), F32)], [pltpu.SemaphoreType.DMA((2,))]


def _fox_sample_kernel(pt_ref, q_ref, kn_ref, vn_ref, f_ref, b_ref, ck_ref, cv_ref, clf_ref,
                       o_ref, lf_out_ref, acc_ref, m_ref, l_ref, ncum_ref, rcarry_ref,
                       kstage_ref, vstage_ref, lstage_ref, ksem, vsem, lsem,
                       kbuf_ref, vbuf_ref, lbuf_ref, *, steps, layer, n_chunks):
    group = q_ref.shape[0]
    slot = _prefetch_pages(pt_ref, (ck_ref, cv_ref, clf_ref), (kstage_ref, vstage_ref, lstage_ref),
                           (ksem, vsem, lsem), layer, group, n_chunks)
    c = _grid_step()[0] % n_chunks
    rows = N_HEADS * steps
    qbd = [_block_diag_q(q_ref.at[g]) for g in range(group)]
    stat = lambda a: jnp.broadcast_to(a, m_ref.shape[1:])

    @pl.when(c == 0)
    def _():
        step = lax.broadcasted_iota(jnp.int32, (rows, PAGE_SIZE), 0) % steps
        col = lax.broadcasted_iota(jnp.int32, (rows, PAGE_SIZE), 1)
        visible = col <= step
        lbuf_ref[...] = jnp.zeros_like(lbuf_ref)
        rcarry_ref[...] = jnp.zeros_like(rcarry_ref)
        for g in range(group):
            lf = _log_sigmoid(f_ref[g] + b_ref[...])
            lf_out_ref[g] = lf
            lf = jnp.where(col < steps, lf, 0.0)
            cum = lf
            shift = 1
            while shift < steps:
                cum = cum + jnp.where(col >= shift, pltpu.roll(cum, shift, 1), 0.0)
                shift *= 2
            ncum = jnp.sum(jnp.where(visible, lf, 0.0), axis=-1, keepdims=True)
            s = jnp.where(visible, _dot_nt(qbd[g], kn_ref[g]) + (ncum - cum) * LOG2_E, NEG_INF)
            m = jnp.max(s, axis=-1, keepdims=True)
            p = jnp.exp2(s - m)
            m_ref[g] = stat(m)
            l_ref[g] = stat(jnp.sum(p, axis=-1, keepdims=True))
            acc_ref[g] = _dot(p.astype(BF16), vn_ref[g])
            ncum_ref[g] = stat(ncum)

    per_seq = kbuf_ref.shape[-1] // PAGE_SIZE
    u3 = _suffix_matrix(CUMSUM_SEG, 3)
    for g in range(group):
        _pages_to_bf16(kstage_ref, slot, g, kbuf_ref)
        _pages_to_bf16(vstage_ref, slot, g, vbuf_ref)
        for p in range(per_seq):
            lbuf_ref[g, 0:N_HEADS, p * PAGE_SIZE:(p + 1) * PAGE_SIZE] = lstage_ref[slot, g * per_seq + p]
        lf_pages = lbuf_ref[g]
        incl, carry = _suffix_sums(lf_pages, u3, rcarry_ref[g, :, 0:1])
        suffix = incl - lf_pages
        rcarry_ref[g] = jnp.broadcast_to(carry, rcarry_ref.shape[1:])
        bias = jnp.concatenate(
            [jnp.broadcast_to(suffix[h:h + 1, :], (steps, suffix.shape[1])) for h in range(N_HEADS)], axis=0)
        s = _dot(qbd[g], kbuf_ref[g]) + (bias + ncum_ref[g, :, 0:1]) * LOG2_E
        m_old = m_ref[g, :, 0:1]
        m = jnp.maximum(m_old, jnp.max(s, axis=-1, keepdims=True))
        alpha = jnp.exp2(m_old - m)
        p = jnp.exp2(s - m)
        l_ref[g] = stat(alpha * l_ref[g, :, 0:1] + jnp.sum(p, axis=-1, keepdims=True))
        acc_ref[g] = acc_ref[g] * alpha + _dot_nt(p.astype(BF16), vbuf_ref[g])
        m_ref[g] = stat(m)

    def finish():
        @pl.when(c == n_chunks - 1)
        def _():
            for g in range(group):
                o_ref[g] = _fold_heads(acc_ref[g] / l_ref[g, :, 0:1], steps)
    return finish


def _both_groups_kernel(pt_ref, *refs, prompt_kernel, sample_kernel, counts):
    parts, at = [], 0
    for n in counts:
        parts.append(refs[at:at + n])
        at += n
    p_in, s_in, p_out, s_out, p_scr, s_scr = parts
    finish_sample = sample_kernel(pt_ref, *s_in, *s_out, *s_scr)
    prompt_kernel(*p_in, *p_out, *p_scr)
    finish_sample()


def _attention_both_groups(name, layer, page_table, prompt_kernel, prompt_in, prompt_extra_specs,
                           sample_kernel, sample_in, sample_extra_specs, caches, sample_extra_out,
                           sample_extra_scratch, steps):
    q, k, v = prompt_in[:3]
    b, t, w = q.shape
    blk = ATT_BLOCK
    nq = t // blk
    q_rows = sample_in[0]
    n_seq, rows, _ = q_rows.shape
    n_pages = PAGES_PER_STEP
    n_chunks = page_table.shape[1] // n_pages
    g = n_seq * n_chunks // (b * nq)
    assert g >= 1 and g * b * nq == n_seq * n_chunks, "the two groups must split into equally many grid steps"
    seq_group = lambda bi, i: (bi * nq + i) // n_chunks

    qspec = pl.BlockSpec((None, blk, w), lambda bi, i, pt: (bi, i, 0))
    kvspec = pl.BlockSpec((None, t, w), lambda bi, i, pt: (bi, 0, 0))
    per_g = lambda r, width=w: pl.BlockSpec((g, r, width), lambda bi, i, pt: (seq_group(bi, i), 0, 0))
    prompt_specs = [qspec, kvspec, kvspec] + prompt_extra_specs(b, t, blk)
    sample_specs = ([per_g(rows), per_g(PAGE_SIZE), per_g(PAGE_SIZE)] + sample_extra_specs(per_g, rows)
                    + [_HBM_SPEC] * len(caches))
    stat = pltpu.VMEM((g, rows, 128), F32)
    stages, sems = [], []
    for cache in caches:
        st, se = _page_staging(g * n_pages, cache.shape[2], PAGE_SIZE)
        stages += st
        sems += se
    prompt_scratch = [pltpu.VMEM((N_HEADS * blk, w), F32)]
    sample_scratch = ([pltpu.VMEM((g, rows, w), F32)] + sample_extra_scratch(g, stat) + stages + sems
                      + [pltpu.VMEM((g, w, n_pages * PAGE_SIZE), BF16)] * 2)
    if len(caches) == 3:
        sample_scratch.append(pltpu.VMEM((g, 8, n_pages * PAGE_SIZE), F32))
    out_specs = [qspec, per_g(steps)] + [per_g(rows, PAGE_SIZE)] * len(sample_extra_out)
    out_shape = ([jax.ShapeDtypeStruct((b, t, w), BF16), jax.ShapeDtypeStruct((n_seq, steps, w), F32)]
                 + list(sample_extra_out))
    counts = (len(prompt_specs), len(sample_specs), 1, len(out_specs) - 1,
              len(prompt_scratch), len(sample_scratch))
    grid_spec = pltpu.PrefetchScalarGridSpec(
        num_scalar_prefetch=1,
        grid=(b, nq),
        in_specs=prompt_specs + sample_specs,
        out_specs=out_specs,
        scratch_shapes=prompt_scratch + sample_scratch,
    )
    return pl.pallas_call(
        functools.partial(_both_groups_kernel, counts=counts,
                          prompt_kernel=functools.partial(prompt_kernel, bq=blk, bk=blk),
                          sample_kernel=functools.partial(sample_kernel, steps=steps, layer=layer,
                                                          n_chunks=n_chunks)),
        grid_spec=grid_spec,
        out_shape=out_shape,
        compiler_params=_cparams("arbitrary", "arbitrary"),
        name=name,
    )(page_table, *prompt_in, *sample_in, *caches)


def _sb_attention(layer, page_table, prompt_in, sample_in, caches, steps):
    none = lambda *_: []
    return _attention_both_groups(
        "sb_attention", layer, page_table, _sb_prompt_kernel, prompt_in, none,
        _sb_sample_kernel, sample_in, none, caches, [], lambda g, stat: [stat], steps)


def _fox_attention(layer, page_table, prompt_in, sample_in, caches, steps):
    n_seq, rows, _ = sample_in[0].shape
    prompt_extra = lambda b, t, blk: [
        pl.BlockSpec((None, blk, N_HEADS), lambda bi, i, pt: (bi, i, 0)),
        pl.BlockSpec((None, t // blk, N_HEADS, blk), lambda bi, i, pt: (bi, 0, 0, 0))]
    sample_extra = lambda per_g, rows: [per_g(rows, PAGE_SIZE),
                                        pl.BlockSpec((rows, PAGE_SIZE), lambda bi, i, pt: (0, 0))]
    return _attention_both_groups(
        "fox_attention", layer, page_table, _fox_prompt_kernel, prompt_in, prompt_extra,
        _fox_sample_kernel, sample_in, sample_extra, caches,
        [jax.ShapeDtypeStruct((n_seq, rows, PAGE_SIZE), F32)],
        lambda g, stat: [stat, stat, stat, pltpu.VMEM((g, 8, 128), F32)], steps)


def _layer_weights(l, norm1_g, w_in, b_forget, conv_w, conv_b, dt_bias, a_log, d_skip, ssm_norm_g,
                   w_sb_out, w_ssm_out, w_fox_out, w_o, norm2_g, w_up, w_down):
    w = jnp.transpose(w_in, (2, 0, 1))[:, l, :]
    w_main = jnp.concatenate([w[:_OFF_F], w[_OFF_Z:_OFF_DT], w[_OFF_GATE:]], axis=0).astype(BF16)
    w_small = jnp.concatenate([w[_OFF_F:_OFF_Z], w[_OFF_DT:_OFF_GATE]], axis=0)
    w_small = jnp.pad(w_small, ((0, SMALL_WIDTH - w_small.shape[0]), (0, 0))).astype(BF16)
    pad_dt = lambda a: jnp.pad(a, (DT_LANE, SMALL_WIDTH - DT_LANE - SSM_HEADS)).reshape(1, SMALL_WIDTH)
    return {
        'norm1_g': norm1_g[l].reshape(1, D_MODEL), 'w_main': w_main, 'w_small': w_small,
        'b_forget': b_forget[l], 'conv_w': conv_w[l], 'conv_b': conv_b[l].reshape(1, SSM_CONV_DIM),
        'dt_bias': pad_dt(dt_bias[l]), 'a_log': pad_dt(a_log[l]),
        'd_skip': jnp.repeat(d_skip[l], SSM_INNER // SSM_HEADS).reshape(1, SSM_INNER),
        'ssm_norm_g': ssm_norm_g[l].reshape(1, SSM_INNER),
        'w_sb_out': w_sb_out[l].astype(BF16), 'w_ssm_out': w_ssm_out[l].astype(BF16),
        'w_fox_out': w_fox_out[l].astype(BF16), 'w_o': w_o[l].astype(BF16),
        'norm2_g': norm2_g[l].reshape(1, D_MODEL),
        'w_up': w_up[l].astype(BF16), 'w_down': w_down[l].astype(BF16),
    }


def _to_heads(a, bsz, t):
    return a.reshape(bsz, t, N_HEADS, HEAD_DIM)


def _layer(xp, xs, l, depth, lw, gf, kv_prev, caches, state_ssm, state_conv, page_table):
    final = l == depth - 1
    cache_sb_k, cache_sb_v, cache_fox_k, cache_fox_v, cache_lf_t = caches
    bsz, t, d = xp.shape
    n = bsz * t
    (q_sb, q_fx, k_sb, v_sb, k_fx, v_fx, k_sb_h, v_sb_h, k_fx_h, v_fx_h,
     z, xbc, small, gates) = _inproj(xp.reshape(n, d), lw['norm1_g'], lw['w_main'], lw['w_small'],
                                     (bsz, t, l, depth, kv_prev))
    sn, st, _ = xs.shape
    m = sn * st
    (sq_sb, sq_fx, sk_sb, sv_sb, sk_fx, sv_fx, sk_sb_h, sv_sb_h, sk_fx_h, sv_fx_h,
     sz, sxbc, ssmall, sgates) = _inproj(xs.reshape(m, d), lw['norm1_g'], lw['w_main'], lw['w_small'])

    b3 = lambda a: a.reshape(bsz, t, a.shape[-1])
    s3 = lambda a: a.reshape(sn, st, a.shape[-1])
    q_rows = lambda q: jnp.tile(s3(q), (1, N_HEADS, 1))
    pad_keys = lambda a: jnp.pad(s3(a), ((0, 0), (0, PAGE_SIZE - st), (0, 0)))
    y_sb, sy_sb = _sb_attention(l, page_table, (b3(q_sb), b3(k_sb_h), b3(v_sb_h)),
                                (q_rows(sq_sb), pad_keys(sk_sb_h), pad_keys(sv_sb_h)),
                                (cache_sb_k, cache_sb_v), st)

    f_rows = small[:, :N_HEADS].reshape(bsz, t, N_HEADS).transpose(0, 2, 1)
    f_rows = f_rows.reshape(bsz * N_HEADS, t // PAGE_SIZE, PAGE_SIZE)
    logf, cum = _logf_cum(f_rows, lw['b_forget'])
    logf = logf.reshape(bsz, N_HEADS, t).transpose(0, 2, 1)
    blk = ATT_BLOCK
    cum2 = cum * LOG2_E
    cum_col = cum2.reshape(bsz, N_HEADS, t).transpose(0, 2, 1)
    cum_row = cum2.reshape(bsz, N_HEADS, t // blk, blk).transpose(0, 2, 1, 3)
    f_new = ssmall[:, :N_HEADS].reshape(sn, st, N_HEADS).transpose(0, 2, 1)
    sf_rows = jnp.pad(jnp.repeat(f_new, st, axis=1), ((0, 0), (0, 0), (0, PAGE_SIZE - st)))
    b_rows = jnp.broadcast_to(jnp.repeat(lw['b_forget'], st)[:, None], (N_HEADS * st, PAGE_SIZE))
    y_fx, sy_fx, lf_rows = _fox_attention(
        l, page_table, (b3(q_fx), b3(k_fx_h), b3(v_fx_h), cum_col, cum_row),
        (q_rows(sq_fx), pad_keys(sk_fx_h), pad_keys(sv_fx_h), sf_rows, b_rows),
        (cache_fox_k, cache_fox_v, cache_lf_t), st)
    slogf = lf_rows[:, ::st, :st].transpose(0, 2, 1)

    conv0 = jnp.zeros((bsz, SSM_CONV - 1, SSM_CONV_DIM), F32)
    ssm0 = jnp.zeros((bsz, SSM_INNER, SSM_STATE), F32)
    ssm_w = (lw['conv_w'], lw['conv_b'], lw['dt_bias'], lw['a_log'], lw['d_skip'], lw['ssm_norm_g'])
    y_ssm, ssm_new, conv_new = _ssm(b3(z), b3(xbc), b3(small), conv0, ssm0, *ssm_w, SSM_CHUNK)
    pad_rows = lambda a: jnp.pad(s3(a), ((0, 0), (0, SSM_CHUNK - st), (0, 0)))
    sy_ssm, sssm_new, sconv_new = _ssm(pad_rows(sz), pad_rows(sxbc), pad_rows(ssmall), state_conv[l],
                                       state_ssm[l].reshape(sn, SSM_INNER, SSM_STATE), *ssm_w, st)
    sy_ssm = sy_ssm[:, :st]

    xp_new = _merge_ffn(xp.reshape(n, d), y_sb.reshape(n, -1), y_ssm.reshape(n, -1), y_fx.reshape(n, -1),
                        gates, lw, gf, final)
    xs_new = _merge_ffn(xs.reshape(m, d), sy_sb.reshape(m, -1), sy_ssm.reshape(m, -1),
                        sy_fx.reshape(m, -1), sgates, lw, gf, final)
    ssm_shape = (SSM_HEADS, SSM_INNER // SSM_HEADS, SSM_STATE)
    prompt_states = (logf, ssm_new.reshape(bsz, *ssm_shape), conv_new)
    sample_states = (_to_heads(sk_sb, sn, st), _to_heads(sv_sb, sn, st), _to_heads(sk_fx, sn, st),
                     _to_heads(sv_fx, sn, st), slogf, sssm_new.reshape(sn, *ssm_shape), sconv_new)
    return (xp_new.reshape(bsz, t, d), xs_new.reshape(sn, st, d), (k_sb, v_sb, k_fx, v_fx),
            prompt_states, sample_states)


def kernel(x_prompt, x_sample, cache_sb_k, cache_sb_v, cache_fox_k, cache_fox_v, cache_fox_logf, state_ssm, state_conv, page_table, norm1_g, w_in, b_forget, conv_w, conv_b, dt_bias, a_log, d_skip, ssm_norm_g, w_sb_out, w_ssm_out, w_fox_out, w_o, norm2_g, w_up, w_down, final_norm_g):
    depth = w_in.shape[0]
    assert page_table.shape[1] % PAGES_PER_STEP == 0
    assert x_sample.shape[1] >= SSM_CONV - 1 and x_prompt.shape[1] % ATT_BLOCK == 0
    flat = lambda c: c.transpose(0, 1, 3, 4, 2).reshape(c.shape[0], c.shape[1], ATT_WIDTH, PAGE_SIZE)
    caches = (flat(cache_sb_k), flat(cache_sb_v), flat(cache_fox_k), flat(cache_fox_v),
              cache_fox_logf.transpose(0, 1, 3, 2))
    gf = final_norm_g.reshape(1, D_MODEL)
    xp, xs = x_prompt, x_sample
    prompt_states, sample_states = [], []
    kv_stacks = None
    for l in range(depth):
        lw = _layer_weights(l, norm1_g, w_in, b_forget, conv_w, conv_b, dt_bias, a_log, d_skip,
                            ssm_norm_g, w_sb_out, w_ssm_out, w_fox_out, w_o, norm2_g, w_up, w_down)
        xp, xs, kv_stacks, st_p, st_s = _layer(xp, xs, l, depth, lw, gf, kv_stacks, caches,
                                               state_ssm, state_conv, page_table)
        prompt_states.append(st_p)
        sample_states.append(st_s)
    bsz, t = x_prompt.shape[:2]
    from_t = lambda a: a.reshape(depth, bsz, N_HEADS, HEAD_DIM, t).transpose(0, 1, 4, 2, 3)
    stacked_p = [from_t(a) for a in kv_stacks] + [jnp.stack(s) for s in zip(*prompt_states)]
    stacked_s = [jnp.stack(s) for s in zip(*sample_states)]
    return (xp, xs, *stacked_p, *stacked_s)
```

```python
import functools

import jax
import jax.numpy as jnp
from jax import lax
from jax.experimental import pallas as pl
from jax.experimental.pallas import tpu as pltpu

F32 = jnp.float32
BF16 = jnp.bfloat16

D_MODEL = 1024
HEAD_DIM = 64
N_HEADS = 4
ATT_WIDTH = N_HEADS * HEAD_DIM
SSM_HEADS = 8
SSM_INNER = 512
SSM_STATE = 128
SSM_GROUPS = 2
SSM_CONV = 4
SSM_CONV_DIM = 1024
SSM_CHUNK = 128
PAGE_SIZE = 128
N_BRANCH = 3
FFN_HIDDEN = 4 * D_MODEL
RMS_EPS = 1e-6
NEG_INF = -1e30
Q_SCALE = HEAD_DIM ** -0.5
LOG2_E = 1.4426950408889634

_OFF_F = 6 * ATT_WIDTH
_OFF_Z = _OFF_F + N_HEADS
_OFF_XBC = _OFF_Z + SSM_INNER
_OFF_DT = _OFF_XBC + SSM_CONV_DIM
_OFF_GATE = _OFF_DT + SSM_HEADS
IN_WIDTH = _OFF_GATE + N_BRANCH * D_MODEL
MAIN_WIDTH = 6 * ATT_WIDTH + SSM_INNER + SSM_CONV_DIM + N_BRANCH * D_MODEL
SMALL_WIDTH = 128
DT_LANE = N_HEADS

V7X_VMEM_LIMIT_BYTES = 56 * 1024 * 1024
PAGES_PER_STEP = 16
ATT_BLOCK = 256
SB_SPLIT_TERMS = 1
CUMSUM_SEG = 256


def _cparams(*sem):
    return pltpu.CompilerParams(dimension_semantics=sem, vmem_limit_bytes=V7X_VMEM_LIMIT_BYTES)


def _const_spec(shape):
    n = len(shape)
    return pl.BlockSpec(shape, lambda *_: (0,) * n, pipeline_mode=pl.Buffered(1))


def _rms(x, g):
    ms = jnp.mean(x * x, axis=-1, keepdims=True)
    return x * lax.rsqrt(ms + RMS_EPS) * g


def _softplus_tail(z):
    return jnp.log1p(jnp.exp(-jnp.abs(z)))


def _log_sigmoid(z):
    return jnp.minimum(z, 0.0) - _softplus_tail(z)


def _softplus(z):
    return jnp.maximum(z, 0.0) + _softplus_tail(z)


def _dot_nt(a, b):
    return lax.dot_general(a, b, (((1,), (1,)), ((), ())), preferred_element_type=F32)


def _dot(a, b):
    return jnp.dot(a, b, preferred_element_type=F32)


def _dot_exact(a, b):
    return jnp.dot(a, b, preferred_element_type=F32, precision=lax.Precision.HIGHEST)


def _store_kv(ref, a, transposed):
    if not transposed:
        ref[...] = a
    elif len(ref.shape) == 2:
        ref[...] = a.T
    else:
        ref[0] = a.T
        ref[1:] = jnp.zeros((ref.shape[0] - 1,) + ref.shape[1:], ref.dtype)


def _inproj_kernel(x_ref, g_ref, wm_ref, ws_ref, *rest, kv_transposed, n_aliased):
    (qsb_ref, qfx_ref, ksb_ref, vsb_ref, kfx_ref, vfx_ref, ksbh_ref, vsbh_ref, kfxh_ref, vfxh_ref,
     z_ref, xbc_ref, small_ref, gate_ref) = rest[n_aliased:]
    h = _rms(x_ref[...], g_ref[...]).astype(BF16)

    def mm(c0, width):
        return _dot_nt(h, wm_ref[c0:c0 + width, :])

    w = ATT_WIDTH
    qsb_ref[...] = (mm(0, w) * (Q_SCALE * LOG2_E)).astype(BF16)
    for i, (full_ref, half_ref) in enumerate(((ksb_ref, ksbh_ref), (vsb_ref, vsbh_ref))):
        a = mm((1 + i) * w, w)
        _store_kv(full_ref, a, kv_transposed)
        half_ref[...] = a.astype(BF16)
    qfx_ref[...] = (mm(3 * w, w) * (Q_SCALE * LOG2_E)).astype(BF16)
    for i, (full_ref, half_ref) in enumerate(((kfx_ref, kfxh_ref), (vfx_ref, vfxh_ref))):
        a = mm((4 + i) * w, w)
        _store_kv(full_ref, a, kv_transposed)
        half_ref[...] = a.astype(BF16)
    z_ref[...] = mm(6 * w, SSM_INNER)
    c0 = 6 * w + SSM_INNER
    for c in range(SSM_CONV_DIM // 512):
        xbc_ref[:, c * 512:(c + 1) * 512] = mm(c0 + c * 512, 512)
    c0 += SSM_CONV_DIM
    for c in range(N_BRANCH * D_MODEL // 512):
        gate_ref[:, c * 512:(c + 1) * 512] = jax.nn.sigmoid(mm(c0 + c * 512, 512)).astype(BF16)
    small_ref[...] = _dot_nt(h, ws_ref[...])


_HBM_SPEC = pl.BlockSpec(memory_space=pl.ANY)


def _inproj(x2d, g, w_main, w_small, kv_stack=None):
    n = x2d.shape[0]
    tm = min(512, n)
    row = lambda width: pl.BlockSpec((tm, width), lambda i: (i, 0))
    rows = lambda width, dt: (row(width), jax.ShapeDtypeStruct((n, width), dt))
    prev = ()
    if kv_stack is None:
        kv = rows(ATT_WIDTH, F32)
    else:
        bsz, t, layer, depth, prev = kv_stack
        prev = () if prev is None else tuple(prev)
        nt = t // tm
        if prev or depth == 1:
            block = pl.BlockSpec((None, None, ATT_WIDTH, tm), lambda i: (layer, i // nt, 0, i % nt))
        else:
            block = pl.BlockSpec((depth, None, ATT_WIDTH, tm), lambda i: (0, i // nt, 0, i % nt))
        kv = (block, jax.ShapeDtypeStruct((depth, bsz, ATT_WIDTH, t), F32))
    outs = ([rows(ATT_WIDTH, BF16)] * 2 + [kv] * 4 + [rows(ATT_WIDTH, BF16)] * 4
            + [rows(SSM_INNER, F32), rows(SSM_CONV_DIM, F32), rows(SMALL_WIDTH, F32),
               rows(N_BRANCH * D_MODEL, BF16)])
    n_in = 4
    return pl.pallas_call(
        functools.partial(_inproj_kernel, kv_transposed=kv_stack is not None, n_aliased=len(prev)),
        grid=(n // tm,),
        in_specs=[row(D_MODEL), _const_spec((1, D_MODEL)),
                  _const_spec((MAIN_WIDTH, D_MODEL)), _const_spec((SMALL_WIDTH, D_MODEL))]
                 + [_HBM_SPEC] * len(prev),
        out_specs=[spec for spec, _ in outs],
        out_shape=[shape for _, shape in outs],
        input_output_aliases={n_in + j: 2 + j for j in range(len(prev))},
        compiler_params=_cparams("parallel"),
        name="inproj",
    )(x2d, g, w_main, w_small, *prev)


def _lane_head(width=ATT_WIDTH):
    return lax.broadcasted_iota(jnp.int32, (1, width), 1) // HEAD_DIM


def _suffix_matrix(seg, terms):
    r = lax.broadcasted_iota(jnp.int32, (terms * seg, seg), 0) % seg
    c = lax.broadcasted_iota(jnp.int32, (terms * seg, seg), 1)
    return jnp.where(r >= c, 1.0, 0.0).astype(BF16)


def _split_bf16(x, terms):
    out = []
    for _ in range(terms - 1):
        head = x.astype(BF16)
        out.append(head)
        x = x - head.astype(F32)
    out.append(x.astype(BF16))
    return jnp.concatenate(out, axis=1)


def _suffix_sums(x, u, carry):
    m, n = x.shape
    seg = u.shape[1]
    n_seg = n // seg
    terms = u.shape[0] // seg
    if n_seg == 1:
        cs = _dot(_split_bf16(x, terms), u)
        return cs + carry, carry + cs[:, 0:1]
    stacked = jnp.concatenate([x[:, s * seg:(s + 1) * seg] for s in range(n_seg)], axis=0)
    cs = _dot(_split_bf16(stacked, terms), u)
    parts = [None] * n_seg
    for s in reversed(range(n_seg)):
        part = cs[s * m:(s + 1) * m, :]
        parts[s] = part + carry
        carry = carry + part[:, 0:1]
    return jnp.concatenate(parts, axis=1), carry


def _stick_block(z2, carry, u, mask):
    drop = jnp.maximum(z2, 0.0) + jnp.log2(1.0 + jnp.exp2(-jnp.abs(z2)))
    if mask is not None:
        drop = jnp.where(mask, drop, 0.0)
    later, carry = _suffix_sums(drop, u, carry)
    w = jnp.exp2(z2 - later)
    if mask is not None:
        w = jnp.where(mask, w, 0.0)
    return w, carry


def _stack_heads(q):
    lane_head = _lane_head()
    return jnp.concatenate([jnp.where(lane_head == h, q, jnp.zeros_like(q)) for h in range(N_HEADS)], axis=0)


def _fold_heads(acc, rows):
    lane_head = _lane_head()
    out = jnp.zeros((rows, ATT_WIDTH), F32)
    for h in range(N_HEADS):
        out = jnp.where(lane_head == h, acc[h * rows:(h + 1) * rows, :], out)
    return out


def _sb_prompt_kernel(q_ref, k_ref, v_ref, o_ref, acc_ref, *, bq, bk):
    assert bq == bk
    i = pl.program_id(1)
    rows = N_HEADS * bq
    qs = _stack_heads(q_ref[...])
    u = _suffix_matrix(CUMSUM_SEG, SB_SPLIT_TERMS)
    qpos = lax.broadcasted_iota(jnp.int32, (rows, bk), 0) % bq
    causal = lax.broadcasted_iota(jnp.int32, (rows, bk), 1) < qpos

    def block(start, width, carry, mask):
        start = pl.multiple_of(start, width)
        w, carry = _stick_block(_dot_nt(qs, k_ref[pl.ds(start, width), :]), carry, u, mask)
        pv = _dot(w.astype(BF16), v_ref[pl.ds(start, width), :])
        acc_ref[...] = pv if mask is not None else acc_ref[...] + pv
        return carry

    carry = block(i * bk, bk, jnp.zeros((rows, 1), F32), causal)
    carry = lax.fori_loop(0, i % 2, lambda _, c: block((i - 1) * bk, bk, c, None), carry)
    pairs = i // 2
    lax.fori_loop(0, pairs, lambda jj, c: block((pairs - 1 - jj) * 2 * bk, 2 * bk, c, None), carry)
    o_ref[...] = _fold_heads(acc_ref[...], bq).astype(o_ref.dtype)


def _logf_cum_kernel(b_ref, f_ref, logf_ref, cum_ref, *, rows_per_seq):
    logf = _log_sigmoid(f_ref[...] + b_ref[...])
    logf_ref[...] = logf
    n = logf.shape[0]
    r = lax.broadcasted_iota(jnp.int32, (PAGE_SIZE, PAGE_SIZE), 0)
    c = lax.broadcasted_iota(jnp.int32, (PAGE_SIZE, PAGE_SIZE), 1)
    within = _dot_exact(logf, jnp.where(r <= c, 1.0, 0.0).astype(F32))
    totals = jnp.broadcast_to(within[:, PAGE_SIZE - 1:PAGE_SIZE], within.shape)
    rr = lax.broadcasted_iota(jnp.int32, (n, n), 0)
    cc = lax.broadcasted_iota(jnp.int32, (n, n), 1)
    earlier_rows = jnp.where(cc // rows_per_seq == rr // rows_per_seq, jnp.where(cc < rr, 1.0, 0.0), 0.0)
    cum_ref[...] = within + _dot_exact(earlier_rows, totals)


def _logf_cum(f_rows, b_rows, rows_per_seq):
    spec = pl.BlockSpec(f_rows.shape, lambda i: (0, 0))
    return pl.pallas_call(
        functools.partial(_logf_cum_kernel, rows_per_seq=rows_per_seq),
        grid=(1,),
        in_specs=[pl.BlockSpec(b_rows.shape, lambda i: (0, 0)), spec],
        out_specs=[spec, spec],
        out_shape=[jax.ShapeDtypeStruct(f_rows.shape, F32)] * 2,
        compiler_params=_cparams("arbitrary"),
        name="logf_cum",
    )(b_rows, f_rows)


def _fox_prompt_kernel(q_ref, k_ref, v_ref, cq_ref, ck_ref, o_ref, acc_ref, *, bq, bk):
    assert bq == bk
    blk = bq
    i = pl.program_id(1)
    rows = N_HEADS * blk
    qs = _stack_heads(q_ref[...])
    cq = cq_ref[...]
    cqs = jnp.concatenate([cq[:, h:h + 1] for h in range(N_HEADS)], axis=0)
    qpos = lax.broadcasted_iota(jnp.int32, (rows, blk), 0) % blk
    causal = lax.broadcasted_iota(jnp.int32, (rows, blk), 1) <= qpos

    def block(j, state, mask):
        start = pl.multiple_of(j * blk, blk)
        s = _dot_nt(qs, k_ref[pl.ds(start, blk), :]) + cqs
        ck = ck_ref[j]
        s = jnp.concatenate([s[h * blk:(h + 1) * blk, :] - ck[h:h + 1, :] for h in range(N_HEADS)], axis=0)
        if mask is not None:
            s = jnp.where(mask, s, NEG_INF)
        m = jnp.max(s, axis=-1, keepdims=True)
        if state is not None:
            m_old, l_old = state
            m = jnp.maximum(m_old, m)
        p = jnp.exp2(s - m)
        pv = _dot(p.astype(BF16), v_ref[pl.ds(start, blk), :])
        l = jnp.sum(p, axis=-1, keepdims=True)
        if state is None:
            acc_ref[...] = pv
            return m, l
        alpha = jnp.exp2(m_old - m)
        acc_ref[...] = acc_ref[...] * alpha + pv
        return m, alpha * l_old + l

    state = block(i, None, causal)
    _, l = lax.fori_loop(0, i, lambda jj, st: block(i - 1 - jj, st, None), state)
    o_ref[...] = _fold_heads(acc_ref[...] / l, blk).astype(o_ref.dtype)


_XBUF_ROW0 = 8


def _ssm_kernel(z_ref, xbc_ref, small_ref, conv0_ref, state0_ref,
                cw_ref, cb_ref, dtb_ref, alog_ref, dskip_ref, ng_ref,
                y_ref, state_out_ref, conv_out_ref, xbuf_ref, state_ref, *, valid):
    c = pl.program_id(1)
    last = pl.num_programs(1) - 1
    L = xbc_ref.shape[0]
    S = SSM_CHUNK
    r0 = _XBUF_ROW0

    def pad_time(a):
        return a if L == S else jnp.concatenate([a, jnp.zeros((S - L,) + a.shape[1:], a.dtype)], axis=0)

    @pl.when(c == 0)
    def _():
        xbuf_ref[...] = jnp.zeros_like(xbuf_ref)
        xbuf_ref[r0 - 3:r0, :] = conv0_ref[...]
        state_ref[...] = state0_ref[...]

    x_cur = xbc_ref[...]
    window = jnp.concatenate([xbuf_ref[...], x_cur], axis=0)
    cw = cw_ref[...]
    conv = cb_ref[...] + x_cur * cw[3:4, :]
    for i in range(SSM_CONV - 1):
        conv = conv + pltpu.roll(window, 3 - i, 0)[r0:, :] * cw[i:i + 1, :]
    xbuf_ref[...] = x_cur[L - r0:L, :]
    act = conv * jax.nn.sigmoid(conv)
    xs = act[:, :SSM_INNER]
    b_in = act[:, SSM_INNER:SSM_INNER + SSM_GROUPS * SSM_STATE].astype(BF16)
    c_in = act[:, SSM_INNER + SSM_GROUPS * SSM_STATE:].astype(BF16)

    row = lax.broadcasted_iota(jnp.int32, (L, S), 0)
    col = lax.broadcasted_iota(jnp.int32, (L, S), 1)
    tri = row >= col
    dt = _softplus(small_ref[...] + dtb_ref[...])
    if valid < L:
        dt = jnp.where(lax.broadcasted_iota(jnp.int32, dt.shape, 0) < valid, dt, 0.0)
    d_a = dt * (-jnp.exp(alog_ref[...]))
    a_cs = _dot_exact(jnp.where(tri[:, :L], 1.0, 0.0).astype(F32), d_a)
    a_cs_t = pad_time(a_cs).T
    a_last = a_cs[L - 1:L, :]
    e_cs = jnp.exp(a_cs)
    wgt = jnp.exp(a_last - a_cs) * dt
    chunk_dec = jnp.exp(a_last)

    half = lax.broadcasted_iota(jnp.int32, (1, 128), 1) // SSM_STATE_HALF
    rhalf = lax.broadcasted_iota(jnp.int32, (128, 1), 0) // SSM_STATE_HALF
    pair_cols = lambda a, p: jnp.where(half == 0, a[:, DT_LANE + 2 * p:DT_LANE + 2 * p + 1],
                                       a[:, DT_LANE + 2 * p + 1:DT_LANE + 2 * p + 2])
    b_keys = pad_time(b_in)
    scores = [_dot_nt(c_in[:, g * SSM_STATE:(g + 1) * SSM_STATE],
                      b_keys[:, g * SSM_STATE:(g + 1) * SSM_STATE]) for g in range(SSM_GROUPS)]
    ys = []
    for p in range(SSM_HEADS // 2):
        g = (2 * p) // (SSM_HEADS // SSM_GROUPS)
        bg = b_keys[:, g * SSM_STATE:(g + 1) * SSM_STATE]
        cg = c_in[:, g * SSM_STATE:(g + 1) * SSM_STATE]
        xs_p = xs[:, 128 * p:128 * (p + 1)]
        xdt = pad_time((xs_p * pair_cols(dt, p)).astype(BF16))
        y_diag = jnp.zeros((L, 128), F32)
        for hh in range(2):
            lane = DT_LANE + 2 * p + hh
            seg = a_cs[:, lane:lane + 1] - a_cs_t[lane:lane + 1, :]
            decay = jnp.exp(jnp.where(tri, seg, NEG_INF))
            y_h = _dot((scores[g] * decay).astype(BF16), xdt)
            y_diag = jnp.where(half == hh, y_h, y_diag)
        st = state_ref[128 * p:128 * (p + 1), :]
        y_off = _dot_nt(cg, st.astype(BF16)) * pair_cols(e_cs, p)
        ys.append(y_diag + y_off + dskip_ref[:, 128 * p:128 * (p + 1)] * xs_p)
        xw_t = pad_time(xs_p * pair_cols(wgt, p)).T.astype(BF16)
        lane = DT_LANE + 2 * p
        dec = jnp.where(rhalf == 0,
                        jnp.broadcast_to(chunk_dec[:, lane:lane + 1], (128, SSM_STATE)),
                        jnp.broadcast_to(chunk_dec[:, lane + 1:lane + 2], (128, SSM_STATE)))
        state_ref[128 * p:128 * (p + 1), :] = st * dec + _dot(xw_t, bg)

    z = z_ref[...]
    y = jnp.concatenate(ys, axis=1) * (z * jax.nn.sigmoid(z))
    gw = SSM_INNER // SSM_GROUPS
    parts = []
    for g in range(SSM_GROUPS):
        yg = y[:, g * gw:(g + 1) * gw]
        parts.append(yg * lax.rsqrt(jnp.mean(yg * yg, axis=-1, keepdims=True) + RMS_EPS))
    y_ref[...] = (jnp.concatenate(parts, axis=1) * ng_ref[...]).astype(y_ref.dtype)

    @pl.when(c == last)
    def _():
        state_out_ref[...] = state_ref[...]
        conv_out_ref[...] = x_cur[valid - 3:valid, :]


SSM_STATE_HALF = 64
SSM_SHORT_ROWS = 16


def _ssm(z, xbc, small, conv0, state0, cw, cb, dtb, alog, dskip, ng, valid):
    b, t, _ = z.shape
    rows = min(t, SSM_CHUNK)
    nc = t // rows
    chunk = lambda width: pl.BlockSpec((None, rows, width), lambda bi, c: (bi, c, 0))
    per_b = lambda shape: pl.BlockSpec((None,) + shape, lambda bi, c: (bi,) + (0,) * len(shape))
    const = lambda shape: pl.BlockSpec(shape, lambda bi, c: (0,) * len(shape))
    state_rows = SSM_INNER
    return pl.pallas_call(
        functools.partial(_ssm_kernel, valid=valid),
        grid=(b, nc),
        in_specs=[chunk(SSM_INNER), chunk(SSM_CONV_DIM), chunk(SMALL_WIDTH),
                  per_b((SSM_CONV - 1, SSM_CONV_DIM)), per_b((state_rows, SSM_STATE)),
                  const((SSM_CONV, SSM_CONV_DIM)), const((1, SSM_CONV_DIM)),
                  const((1, SMALL_WIDTH)), const((1, SMALL_WIDTH)),
                  const((1, SSM_INNER)), const((1, SSM_INNER))],
        out_specs=[chunk(SSM_INNER), per_b((state_rows, SSM_STATE)),
                   per_b((SSM_CONV - 1, SSM_CONV_DIM))],
        out_shape=[jax.ShapeDtypeStruct((b, t, SSM_INNER), BF16),
                   jax.ShapeDtypeStruct((b, state_rows, SSM_STATE), F32),
                   jax.ShapeDtypeStruct((b, SSM_CONV - 1, SSM_CONV_DIM), F32)],
        scratch_shapes=[pltpu.VMEM((_XBUF_ROW0, SSM_CONV_DIM), F32),
                        pltpu.VMEM((state_rows, SSM_STATE), F32)],
        compiler_params=_cparams("parallel", "arbitrary"),
        name="ssm",
    )(z, xbc, small, conv0, state0, cw, cb, dtb, alog, dskip, ng)


def _merge_ffn_kernel(x_ref, ysb_ref, yssm_ref, yfx_ref, gate_ref,
                      wsb_ref, wssm_ref, wfx_ref, wo_ref, g2_ref, wup_ref, wdn_ref, gf_ref,
                      o_ref, *, final):
    d = D_MODEL
    mixed = gate_ref[:, 0:d].astype(F32) * _dot(ysb_ref[...].astype(BF16), wsb_ref[...])
    mixed = mixed + gate_ref[:, d:2 * d].astype(F32) * _dot(yssm_ref[...].astype(BF16), wssm_ref[...])
    mixed = mixed + gate_ref[:, 2 * d:3 * d].astype(F32) * _dot(yfx_ref[...].astype(BF16), wfx_ref[...])
    x = x_ref[...] + _dot(mixed.astype(BF16), wo_ref[...])
    h = _rms(x, g2_ref[...]).astype(BF16)
    hc = FFN_HIDDEN // 2
    for c in range(2):
        u = jnp.maximum(_dot(h, wup_ref[:, c * hc:(c + 1) * hc]), 0.0)
        x = x + _dot((u * u).astype(BF16), wdn_ref[c * hc:(c + 1) * hc, :])
    if final:
        x = _rms(x, gf_ref[...])
    o_ref[...] = x


def _merge_ffn(x2d, y_sb, y_ssm, y_fx, gates, lw, gf, final):
    n = x2d.shape[0]
    tm = min(512, n)
    row = lambda width: pl.BlockSpec((tm, width), lambda i: (i, 0))
    weights = [lw['w_sb_out'], lw['w_ssm_out'], lw['w_fox_out'], lw['w_o'], lw['norm2_g'],
               lw['w_up'], lw['w_down'], gf]
    return pl.pallas_call(
        functools.partial(_merge_ffn_kernel, final=final),
        grid=(n // tm,),
        in_specs=[row(D_MODEL), row(ATT_WIDTH), row(SSM_INNER), row(ATT_WIDTH),
                  row(N_BRANCH * D_MODEL)] + [_const_spec(w.shape) for w in weights],
        out_specs=row(D_MODEL),
        out_shape=jax.ShapeDtypeStruct((n, D_MODEL), F32),
        compiler_params=_cparams("parallel"),
        name="merge_ffn",
    )(x2d, y_sb, y_ssm, y_fx, gates, *weights)


def _block_diag_q(q_ref):
    q = q_ref[...]
    rows = q.shape[0]
    row_head = lax.broadcasted_iota(jnp.int32, (rows, 1), 0) // (rows // N_HEADS)
    return jnp.where(row_head == _lane_head(), q, jnp.zeros_like(q))


def _page_copy(cache_ref, stage_ref, sem_ref, layer, page, slot, p):
    return pltpu.make_async_copy(cache_ref.at[layer, page], stage_ref.at[slot, p], sem_ref.at[slot])


def _grid_step():
    return (pl.program_id(0) * pl.num_programs(1) + pl.program_id(1),
            pl.num_programs(0) * pl.num_programs(1))


def _prefetch_pages(pt_ref, caches, stages, sems, layer, group, n_chunks):
    n_pages = stages[0].shape[1]
    per_seq = n_pages // group
    step, n_steps = _grid_step()
    b, c = step // n_chunks, step % n_chunks
    slot = step % 2

    def start(bb, cc, sl):
        base = (n_chunks - 1 - cc) * per_seq
        for p in range(n_pages):
            page = pt_ref[bb * group + p // per_seq, base + p % per_seq]
            for i, (cache, stage, sem) in enumerate(zip(caches, stages, sems)):
                _page_copy(cache, stage, sem, layer, page, sl, p).start(priority=i % 2)

    @pl.when(step == 0)
    def _():
        start(b, c, slot)

    @pl.when(step + 1 < n_steps)
    def _():
        wrap = c + 1 == n_chunks
        start(jnp.where(wrap, b + 1, b), jnp.where(wrap, 0, c + 1), 1 - slot)

    for p in range(n_pages):
        for cache, stage, sem in zip(caches, stages, sems):
            _page_copy(cache, stage, sem, layer, 0, slot, p).wait()
    return slot


def _pages_to_bf16(stage_ref, slot, g, buf_ref):
    per_seq = buf_ref.shape[-1] // PAGE_SIZE
    for p in range(per_seq):
        buf_ref[g, :, p * PAGE_SIZE:(p + 1) * PAGE_SIZE] = stage_ref[slot, g * per_seq + p].astype(BF16)


def _sb_sample_kernel(pt_ref, q_ref, kn_ref, vn_ref, ck_ref, cv_ref, o_ref,
                      acc_ref, carry_ref, kstage_ref, vstage_ref, ksem, vsem, kbuf_ref, vbuf_ref,
                      *, steps, layer, n_chunks):
    group = q_ref.shape[0]
    slot = _prefetch_pages(pt_ref, (ck_ref, cv_ref), (kstage_ref, vstage_ref), (ksem, vsem), layer, group,
                           n_chunks)
    c = _grid_step()[0] % n_chunks
    rows = N_HEADS * steps
    qbd = [_block_diag_q(q_ref.at[g]) for g in range(group)]
    u = _suffix_matrix(CUMSUM_SEG, SB_SPLIT_TERMS)

    @pl.when(c == 0)
    def _():
        step = lax.broadcasted_iota(jnp.int32, (rows, PAGE_SIZE), 0) % steps
        col = lax.broadcasted_iota(jnp.int32, (rows, PAGE_SIZE), 1)
        for g in range(group):
            w, carry = _stick_block(_dot_nt(qbd[g], kn_ref[g]), jnp.zeros((rows, 1), F32),
                                    _suffix_matrix(PAGE_SIZE, SB_SPLIT_TERMS), col < step)
            acc_ref[g] = _dot(w.astype(BF16), vn_ref[g])
            carry_ref[g] = jnp.broadcast_to(carry, carry_ref.shape[1:])

    for g in range(group):
        _pages_to_bf16(kstage_ref, slot, g, kbuf_ref)
        _pages_to_bf16(vstage_ref, slot, g, vbuf_ref)
        w, carry = _stick_block(_dot(qbd[g], kbuf_ref[g]), carry_ref[g, :, 0:1], u, None)
        acc_ref[g] += _dot_nt(w.astype(BF16), vbuf_ref[g])
        carry_ref[g] = jnp.broadcast_to(carry, carry_ref.shape[1:])

    def finish():
        @pl.when(c == n_chunks - 1)
        def _():
            for g in range(group):
                o_ref[g] = _fold_heads(acc_ref[g], steps)
    return finish


def _page_staging(n_pages, rows, lanes):
    return [pltpu.VMEM((2, n_pages, rows, lanes), F32)], [pltpu.SemaphoreType.DMA((2,))]


def _fox_sample_kernel(pt_ref, q_ref, kn_ref, vn_ref, f_ref, b_ref, ck_ref, cv_ref, clf_ref,
                       o_ref, lf_out_ref, acc_ref, m_ref, l_ref, ncum_ref, rcarry_ref,
                       kstage_ref, vstage_ref, lstage_ref, ksem, vsem, lsem,
                       kbuf_ref, vbuf_ref, lbuf_ref, *, steps, layer, n_chunks):
    group = q_ref.shape[0]
    slot = _prefetch_pages(pt_ref, (ck_ref, cv_ref, clf_ref), (kstage_ref, vstage_ref, lstage_ref),
                           (ksem, vsem, lsem), layer, group, n_chunks)
    c = _grid_step()[0] % n_chunks
    rows = N_HEADS * steps
    qbd = [_block_diag_q(q_ref.at[g]) for g in range(group)]
    stat = lambda a: jnp.broadcast_to(a, m_ref.shape[1:])

    @pl.when(c == 0)
    def _():
        step = lax.broadcasted_iota(jnp.int32, (rows, PAGE_SIZE), 0) % steps
        col = lax.broadcasted_iota(jnp.int32, (rows, PAGE_SIZE), 1)
        visible = col <= step
        lbuf_ref[...] = jnp.zeros_like(lbuf_ref)
        rcarry_ref[...] = jnp.zeros_like(rcarry_ref)
        for g in range(group):
            lf = _log_sigmoid(f_ref[g] + b_ref[...])
            lf_out_ref[g] = lf
            lf = jnp.where(col < steps, lf, 0.0)
            cum = lf
            shift = 1
            while shift < steps:
                cum = cum + jnp.where(col >= shift, pltpu.roll(cum, shift, 1), 0.0)
                shift *= 2
            ncum = jnp.sum(jnp.where(visible, lf, 0.0), axis=-1, keepdims=True)
            s = jnp.where(visible, _dot_nt(qbd[g], kn_ref[g]) + (ncum - cum) * LOG2_E, NEG_INF)
            m = jnp.max(s, axis=-1, keepdims=True)
            p = jnp.exp2(s - m)
            m_ref[g] = stat(m)
            l_ref[g] = stat(jnp.sum(p, axis=-1, keepdims=True))
            acc_ref[g] = _dot(p.astype(BF16), vn_ref[g])
            ncum_ref[g] = stat(ncum)

    per_seq = kbuf_ref.shape[-1] // PAGE_SIZE
    u3 = _suffix_matrix(CUMSUM_SEG, 3)
    for g in range(group):
        _pages_to_bf16(kstage_ref, slot, g, kbuf_ref)
        _pages_to_bf16(vstage_ref, slot, g, vbuf_ref)
        for p in range(per_seq):
            lbuf_ref[g, 0:N_HEADS, p * PAGE_SIZE:(p + 1) * PAGE_SIZE] = lstage_ref[slot, g * per_seq + p]
        lf_pages = lbuf_ref[g]
        incl, carry = _suffix_sums(lf_pages, u3, rcarry_ref[g, :, 0:1])
        suffix = incl - lf_pages
        rcarry_ref[g] = jnp.broadcast_to(carry, rcarry_ref.shape[1:])
        bias = jnp.concatenate(
            [jnp.broadcast_to(suffix[h:h + 1, :], (steps, suffix.shape[1])) for h in range(N_HEADS)], axis=0)
        s = _dot(qbd[g], kbuf_ref[g]) + (bias + ncum_ref[g, :, 0:1]) * LOG2_E
        m_old = m_ref[g, :, 0:1]
        m = jnp.maximum(m_old, jnp.max(s, axis=-1, keepdims=True))
        alpha = jnp.exp2(m_old - m)
        p = jnp.exp2(s - m)
        l_ref[g] = stat(alpha * l_ref[g, :, 0:1] + jnp.sum(p, axis=-1, keepdims=True))
        acc_ref[g] = acc_ref[g] * alpha + _dot_nt(p.astype(BF16), vbuf_ref[g])
        m_ref[g] = stat(m)

    def finish():
        @pl.when(c == n_chunks - 1)
        def _():
            for g in range(group):
                o_ref[g] = _fold_heads(acc_ref[g] / l_ref[g, :, 0:1], steps)
    return finish


def _both_groups_kernel(pt_ref, *refs, prompt_kernel, sample_kernel, counts):
    parts, at = [], 0
    for n in counts:
        parts.append(refs[at:at + n])
        at += n
    p_in, s_in, p_out, s_out, p_scr, s_scr = parts
    finish_sample = sample_kernel(pt_ref, *s_in, *s_out, *s_scr)
    prompt_kernel(*p_in, *p_out, *p_scr)
    finish_sample()


def _attention_both_groups(name, layer, page_table, prompt_kernel, prompt_in, prompt_extra_specs,
                           sample_kernel, sample_in, sample_extra_specs, caches, sample_extra_out,
                           sample_extra_scratch, steps):
    q, k, v = prompt_in[:3]
    b, t, w = q.shape
    blk = ATT_BLOCK
    nq = t // blk
    q_rows = sample_in[0]
    n_seq, rows, _ = q_rows.shape
    n_pages = PAGES_PER_STEP
    n_chunks = page_table.shape[1] // n_pages
    g = n_seq * n_chunks // (b * nq)
    assert g >= 1 and g * b * nq == n_seq * n_chunks, "the two groups must split into equally many grid steps"
    seq_group = lambda bi, i: (bi * nq + i) // n_chunks

    qspec = pl.BlockSpec((None, blk, w), lambda bi, i, pt: (bi, i, 0))
    kvspec = pl.BlockSpec((None, t, w), lambda bi, i, pt: (bi, 0, 0))
    per_g = lambda r, width=w: pl.BlockSpec((g, r, width), lambda bi, i, pt: (seq_group(bi, i), 0, 0))
    prompt_specs = [qspec, kvspec, kvspec] + prompt_extra_specs(b, t, blk)
    sample_specs = ([per_g(rows), per_g(PAGE_SIZE), per_g(PAGE_SIZE)] + sample_extra_specs(per_g, rows)
                    + [_HBM_SPEC] * len(caches))
    stat = pltpu.VMEM((g, rows, 128), F32)
    stages, sems = [], []
    for cache in caches:
        st, se = _page_staging(g * n_pages, cache.shape[2], PAGE_SIZE)
        stages += st
        sems += se
    prompt_scratch = [pltpu.VMEM((N_HEADS * blk, w), F32)]
    sample_scratch = ([pltpu.VMEM((g, rows, w), F32)] + sample_extra_scratch(g, stat) + stages + sems
                      + [pltpu.VMEM((g, w, n_pages * PAGE_SIZE), BF16)] * 2)
    if len(caches) == 3:
        sample_scratch.append(pltpu.VMEM((g, 8, n_pages * PAGE_SIZE), F32))
    out_specs = [qspec, per_g(steps)] + [per_g(rows, PAGE_SIZE)] * len(sample_extra_out)
    out_shape = ([jax.ShapeDtypeStruct((b, t, w), BF16), jax.ShapeDtypeStruct((n_seq, steps, w), F32)]
                 + list(sample_extra_out))
    counts = (len(prompt_specs), len(sample_specs), 1, len(out_specs) - 1,
              len(prompt_scratch), len(sample_scratch))
    grid_spec = pltpu.PrefetchScalarGridSpec(
        num_scalar_prefetch=1,
        grid=(b, nq),
        in_specs=prompt_specs + sample_specs,
        out_specs=out_specs,
        scratch_shapes=prompt_scratch + sample_scratch,
    )
    return pl.pallas_call(
        functools.partial(_both_groups_kernel, counts=counts,
                          prompt_kernel=functools.partial(prompt_kernel, bq=blk, bk=blk),
                          sample_kernel=functools.partial(sample_kernel, steps=steps, layer=layer,
                                                          n_chunks=n_chunks)),
        grid_spec=grid_spec,
        out_shape=out_shape,
        compiler_params=_cparams("arbitrary", "arbitrary"),
        name=name,
    )(page_table, *prompt_in, *sample_in, *caches)


def _sb_attention(layer, page_table, prompt_in, sample_in, caches, steps):
    none = lambda *_: []
    return _attention_both_groups(
        "sb_attention", layer, page_table, _sb_prompt_kernel, prompt_in, none,
        _sb_sample_kernel, sample_in, none, caches, [], lambda g, stat: [stat], steps)


def _fox_attention(layer, page_table, prompt_in, sample_in, caches, steps):
    n_seq, rows, _ = sample_in[0].shape
    prompt_extra = lambda b, t, blk: [
        pl.BlockSpec((None, blk, N_HEADS), lambda bi, i, pt: (bi, i, 0)),
        pl.BlockSpec((None, t // blk, N_HEADS, blk), lambda bi, i, pt: (bi, 0, 0, 0))]
    sample_extra = lambda per_g, rows: [per_g(rows, PAGE_SIZE),
                                        pl.BlockSpec((rows, PAGE_SIZE), lambda bi, i, pt: (0, 0))]
    return _attention_both_groups(
        "fox_attention", layer, page_table, _fox_prompt_kernel, prompt_in, prompt_extra,
        _fox_sample_kernel, sample_in, sample_extra, caches,
        [jax.ShapeDtypeStruct((n_seq, rows, PAGE_SIZE), F32)],
        lambda g, stat: [stat, stat, stat, pltpu.VMEM((g, 8, 128), F32)], steps)


def _layer_weights(l, norm1_g, w_in, b_forget, conv_w, conv_b, dt_bias, a_log, d_skip, ssm_norm_g,
                   w_sb_out, w_ssm_out, w_fox_out, w_o, norm2_g, w_up, w_down):
    w = jnp.transpose(w_in, (2, 0, 1))[:, l, :]
    w_main = jnp.concatenate([w[:_OFF_F], w[_OFF_Z:_OFF_DT], w[_OFF_GATE:]], axis=0).astype(BF16)
    w_small = jnp.concatenate([w[_OFF_F:_OFF_Z], w[_OFF_DT:_OFF_GATE]], axis=0)
    w_small = jnp.pad(w_small, ((0, SMALL_WIDTH - w_small.shape[0]), (0, 0))).astype(BF16)
    pad_dt = lambda a: jnp.pad(a, (DT_LANE, SMALL_WIDTH - DT_LANE - SSM_HEADS)).reshape(1, SMALL_WIDTH)
    return {
        'norm1_g': norm1_g[l].reshape(1, D_MODEL), 'w_main': w_main, 'w_small': w_small,
        'b_forget': b_forget[l], 'conv_w': conv_w[l], 'conv_b': conv_b[l].reshape(1, SSM_CONV_DIM),
        'dt_bias': pad_dt(dt_bias[l]), 'a_log': pad_dt(a_log[l]),
        'd_skip': jnp.repeat(d_skip[l], SSM_INNER // SSM_HEADS).reshape(1, SSM_INNER),
        'ssm_norm_g': ssm_norm_g[l].reshape(1, SSM_INNER),
        'w_sb_out': w_sb_out[l].astype(BF16), 'w_ssm_out': w_ssm_out[l].astype(BF16),
        'w_fox_out': w_fox_out[l].astype(BF16), 'w_o': w_o[l].astype(BF16),
        'norm2_g': norm2_g[l].reshape(1, D_MODEL),
        'w_up': w_up[l].astype(BF16), 'w_down': w_down[l].astype(BF16),
    }


def _to_heads(a, bsz, t):
    return a.reshape(bsz, t, N_HEADS, HEAD_DIM)


def _layer(xp, xs, l, depth, lw, gf, kv_prev, caches, state_ssm, state_conv, page_table):
    final = l == depth - 1
    cache_sb_k, cache_sb_v, cache_fox_k, cache_fox_v, cache_lf_t = caches
    bsz, t, d = xp.shape
    n = bsz * t
    (q_sb, q_fx, k_sb, v_sb, k_fx, v_fx, k_sb_h, v_sb_h, k_fx_h, v_fx_h,
     z, xbc, small, gates) = _inproj(xp.reshape(n, d), lw['norm1_g'], lw['w_main'], lw['w_small'],
                                     (bsz, t, l, depth, kv_prev))
    sn, st, _ = xs.shape
    m = sn * st
    (sq_sb, sq_fx, sk_sb, sv_sb, sk_fx, sv_fx, sk_sb_h, sv_sb_h, sk_fx_h, sv_fx_h,
     sz, sxbc, ssmall, sgates) = _inproj(xs.reshape(m, d), lw['norm1_g'], lw['w_main'], lw['w_small'])

    b3 = lambda a: a.reshape(bsz, t, a.shape[-1])
    s3 = lambda a: a.reshape(sn, st, a.shape[-1])
    q_rows = lambda q: jnp.tile(s3(q), (1, N_HEADS, 1))
    pad_keys = lambda a: jnp.pad(s3(a), ((0, 0), (0, PAGE_SIZE - st), (0, 0)))
    y_sb, sy_sb = _sb_attention(l, page_table, (b3(q_sb), b3(k_sb_h), b3(v_sb_h)),
                                (q_rows(sq_sb), pad_keys(sk_sb_h), pad_keys(sv_sb_h)),
                                (cache_sb_k, cache_sb_v), st)

    f_rows = small[:, :N_HEADS].reshape(bsz, t, N_HEADS).transpose(0, 2, 1)
    rows_per_seq = t // PAGE_SIZE
    f_rows = f_rows.reshape(bsz * N_HEADS * rows_per_seq, PAGE_SIZE)
    bias_rows = jnp.tile(jnp.repeat(lw['b_forget'], rows_per_seq), bsz)[:, None]
    logf, cum = _logf_cum(f_rows, bias_rows, rows_per_seq)
    logf = logf.reshape(bsz, N_HEADS, t).transpose(0, 2, 1)
    blk = ATT_BLOCK
    cum2 = cum * LOG2_E
    cum_col = cum2.reshape(bsz, N_HEADS, t).transpose(0, 2, 1)
    cum_row = cum2.reshape(bsz, N_HEADS, t // blk, blk).transpose(0, 2, 1, 3)
    f_new = ssmall[:, :N_HEADS].reshape(sn, st, N_HEADS).transpose(0, 2, 1)
    sf_rows = jnp.pad(jnp.repeat(f_new, st, axis=1), ((0, 0), (0, 0), (0, PAGE_SIZE - st)))
    b_rows = jnp.broadcast_to(jnp.repeat(lw['b_forget'], st)[:, None], (N_HEADS * st, PAGE_SIZE))
    y_fx, sy_fx, lf_rows = _fox_attention(
        l, page_table, (b3(q_fx), b3(k_fx_h), b3(v_fx_h), cum_col, cum_row),
        (q_rows(sq_fx), pad_keys(sk_fx_h), pad_keys(sv_fx_h), sf_rows, b_rows),
        (cache_fox_k, cache_fox_v, cache_lf_t), st)
    slogf = lf_rows[:, ::st, :st].transpose(0, 2, 1)

    conv0 = jnp.zeros((bsz, SSM_CONV - 1, SSM_CONV_DIM), F32)
    ssm0 = jnp.zeros((bsz, SSM_INNER, SSM_STATE), F32)
    ssm_w = (lw['conv_w'], lw['conv_b'], lw['dt_bias'], lw['a_log'], lw['d_skip'], lw['ssm_norm_g'])
    y_ssm, ssm_new, conv_new = _ssm(b3(z), b3(xbc), b3(small), conv0, ssm0, *ssm_w, SSM_CHUNK)
    pad_rows = lambda a: jnp.pad(s3(a), ((0, 0), (0, SSM_SHORT_ROWS - st), (0, 0)))
    sy_ssm, sssm_new, sconv_new = _ssm(pad_rows(sz), pad_rows(sxbc), pad_rows(ssmall), state_conv[l],
                                       state_ssm[l].reshape(sn, SSM_INNER, SSM_STATE), *ssm_w, st)
    sy_ssm = sy_ssm[:, :st]

    xp_new = _merge_ffn(xp.reshape(n, d), y_sb.reshape(n, -1), y_ssm.reshape(n, -1), y_fx.reshape(n, -1),
                        gates, lw, gf, final)
    xs_new = _merge_ffn(xs.reshape(m, d), sy_sb.reshape(m, -1), sy_ssm.reshape(m, -1),
                        sy_fx.reshape(m, -1), sgates, lw, gf, final)
    ssm_shape = (SSM_HEADS, SSM_INNER // SSM_HEADS, SSM_STATE)
    prompt_states = (logf, ssm_new.reshape(bsz, *ssm_shape), conv_new)
    sample_states = (_to_heads(sk_sb, sn, st), _to_heads(sv_sb, sn, st), _to_heads(sk_fx, sn, st),
                     _to_heads(sv_fx, sn, st), slogf, sssm_new.reshape(sn, *ssm_shape), sconv_new)
    return (xp_new.reshape(bsz, t, d), xs_new.reshape(sn, st, d), (k_sb, v_sb, k_fx, v_fx),
            prompt_states, sample_states)


def kernel(x_prompt, x_sample, cache_sb_k, cache_sb_v, cache_fox_k, cache_fox_v, cache_fox_logf, state_ssm, state_conv, page_table, norm1_g, w_in, b_forget, conv_w, conv_b, dt_bias, a_log, d_skip, ssm_norm_g, w_sb_out, w_ssm_out, w_fox_out, w_o, norm2_g, w_up, w_down, final_norm_g):
    depth = w_in.shape[0]
    assert page_table.shape[1] % PAGES_PER_STEP == 0
    assert x_sample.shape[1] >= SSM_CONV - 1 and x_prompt.shape[1] % ATT_BLOCK == 0
    flat = lambda c: c.transpose(0, 1, 3, 4, 2).reshape(c.shape[0], c.shape[1], ATT_WIDTH, PAGE_SIZE)
    caches = (flat(cache_sb_k), flat(cache_sb_v), flat(cache_fox_k), flat(cache_fox_v),
              cache_fox_logf.transpose(0, 1, 3, 2))
    gf = final_norm_g.reshape(1, D_MODEL)
    xp, xs = x_prompt, x_sample
    prompt_states, sample_states = [], []
    kv_stacks = None
    for l in range(depth):
        lw = _layer_weights(l, norm1_g, w_in, b_forget, conv_w, conv_b, dt_bias, a_log, d_skip,
                            ssm_norm_g, w_sb_out, w_ssm_out, w_fox_out, w_o, norm2_g, w_up, w_down)
        xp, xs, kv_stacks, st_p, st_s = _layer(xp, xs, l, depth, lw, gf, kv_stacks, caches,
                                               state_ssm, state_conv, page_table)
        prompt_states.append(st_p)
        sample_states.append(st_s)
    bsz, t = x_prompt.shape[:2]
    from_t = lambda a: a.reshape(depth, bsz, N_HEADS, HEAD_DIM, t).transpose(0, 1, 4, 2, 3)
    stacked_p = [from_t(a) for a in kv_stacks] + [jnp.stack(s) for s in zip(*prompt_states)]
    stacked_s = [jnp.stack(s) for s in zip(*sample_states)]
    return (xp, xs, *stacked_p, *stacked_s)
```

```python
import functools

import jax
import jax.numpy as jnp
from jax import lax
from jax.experimental import pallas as pl
from jax.experimental.pallas import tpu as pltpu

F32 = jnp.float32
BF16 = jnp.bfloat16

D_MODEL = 1024
HEAD_DIM = 64
N_HEADS = 4
ATT_WIDTH = N_HEADS * HEAD_DIM
SSM_HEADS = 8
SSM_INNER = 512
SSM_STATE = 128
SSM_GROUPS = 2
SSM_CONV = 4
SSM_CONV_DIM = 1024
SSM_CHUNK = 128
PAGE_SIZE = 128
N_BRANCH = 3
FFN_HIDDEN = 4 * D_MODEL
RMS_EPS = 1e-6
NEG_INF = -1e30
Q_SCALE = HEAD_DIM ** -0.5
LOG2_E = 1.4426950408889634

_OFF_F = 6 * ATT_WIDTH
_OFF_Z = _OFF_F + N_HEADS
_OFF_XBC = _OFF_Z + SSM_INNER
_OFF_DT = _OFF_XBC + SSM_CONV_DIM
_OFF_GATE = _OFF_DT + SSM_HEADS
IN_WIDTH = _OFF_GATE + N_BRANCH * D_MODEL
MAIN_WIDTH = 6 * ATT_WIDTH + SSM_INNER + SSM_CONV_DIM + N_BRANCH * D_MODEL
SMALL_WIDTH = 128
DT_LANE = N_HEADS

V7X_VMEM_LIMIT_BYTES = 56 * 1024 * 1024
PAGES_PER_STEP = 16
ATT_BLOCK = 256
SB_SPLIT_TERMS = 1
CUMSUM_SEG = 256


def _cparams(*sem):
    return pltpu.CompilerParams(dimension_semantics=sem, vmem_limit_bytes=V7X_VMEM_LIMIT_BYTES)


def _const_spec(shape):
    n = len(shape)
    return pl.BlockSpec(shape, lambda *_: (0,) * n, pipeline_mode=pl.Buffered(1))


def _rms(x, g):
    ms = jnp.mean(x * x, axis=-1, keepdims=True)
    return x * lax.rsqrt(ms + RMS_EPS) * g


def _softplus_tail(z):
    return jnp.log1p(jnp.exp(-jnp.abs(z)))


def _log_sigmoid(z):
    return jnp.minimum(z, 0.0) - _softplus_tail(z)


def _softplus(z):
    return jnp.maximum(z, 0.0) + _softplus_tail(z)


def _dot_nt(a, b):
    return lax.dot_general(a, b, (((1,), (1,)), ((), ())), preferred_element_type=F32)


def _dot(a, b):
    return jnp.dot(a, b, preferred_element_type=F32)


def _dot_exact(a, b):
    return jnp.dot(a, b, preferred_element_type=F32, precision=lax.Precision.HIGHEST)


def _store_kv(ref, a, transposed):
    if not transposed:
        ref[...] = a
    elif len(ref.shape) == 2:
        ref[...] = a.T
    else:
        ref[0] = a.T
        ref[1:] = jnp.zeros((ref.shape[0] - 1,) + ref.shape[1:], ref.dtype)


def _inproj_kernel(x_ref, g_ref, wm_ref, ws_ref, *rest, kv_transposed, n_aliased):
    (qsb_ref, qfx_ref, ksb_ref, vsb_ref, kfx_ref, vfx_ref, ksbh_ref, vsbh_ref, kfxh_ref, vfxh_ref,
     z_ref, xbc_ref, small_ref, gate_ref) = rest[n_aliased:]
    h = _rms(x_ref[...], g_ref[...]).astype(BF16)

    def mm(c0, width):
        return _dot_nt(h, wm_ref[c0:c0 + width, :])

    w = ATT_WIDTH
    qsb_ref[...] = (mm(0, w) * (Q_SCALE * LOG2_E)).astype(BF16)
    for i, (full_ref, half_ref) in enumerate(((ksb_ref, ksbh_ref), (vsb_ref, vsbh_ref))):
        a = mm((1 + i) * w, w)
        _store_kv(full_ref, a, kv_transposed)
        half_ref[...] = a.astype(BF16)
    qfx_ref[...] = (mm(3 * w, w) * (Q_SCALE * LOG2_E)).astype(BF16)
    for i, (full_ref, half_ref) in enumerate(((kfx_ref, kfxh_ref), (vfx_ref, vfxh_ref))):
        a = mm((4 + i) * w, w)
        _store_kv(full_ref, a, kv_transposed)
        half_ref[...] = a.astype(BF16)
    z_ref[...] = mm(6 * w, SSM_INNER)
    c0 = 6 * w + SSM_INNER
    for c in range(SSM_CONV_DIM // 512):
        xbc_ref[:, c * 512:(c + 1) * 512] = mm(c0 + c * 512, 512)
    c0 += SSM_CONV_DIM
    for c in range(N_BRANCH * D_MODEL // 512):
        gate_ref[:, c * 512:(c + 1) * 512] = jax.nn.sigmoid(mm(c0 + c * 512, 512)).astype(BF16)
    small_ref[...] = _dot_nt(h, ws_ref[...])


_HBM_SPEC = pl.BlockSpec(memory_space=pl.ANY)


def _inproj(x2d, g, w_main, w_small, kv_stack=None):
    n = x2d.shape[0]
    tm = min(512, n)
    row = lambda width: pl.BlockSpec((tm, width), lambda i: (i, 0))
    rows = lambda width, dt: (row(width), jax.ShapeDtypeStruct((n, width), dt))
    prev = ()
    if kv_stack is None:
        kv = rows(ATT_WIDTH, F32)
    else:
        bsz, t, layer, depth, prev = kv_stack
        prev = () if prev is None else tuple(prev)
        nt = t // tm
        if prev or depth == 1:
            block = pl.BlockSpec((None, None, ATT_WIDTH, tm), lambda i: (layer, i // nt, 0, i % nt))
        else:
            block = pl.BlockSpec((depth, None, ATT_WIDTH, tm), lambda i: (0, i // nt, 0, i % nt))
        kv = (block, jax.ShapeDtypeStruct((depth, bsz, ATT_WIDTH, t), F32))
    outs = ([rows(ATT_WIDTH, BF16)] * 2 + [kv] * 4 + [rows(ATT_WIDTH, BF16)] * 4
            + [rows(SSM_INNER, F32), rows(SSM_CONV_DIM, F32), rows(SMALL_WIDTH, F32),
               rows(N_BRANCH * D_MODEL, BF16)])
    n_in = 4
    return pl.pallas_call(
        functools.partial(_inproj_kernel, kv_transposed=kv_stack is not None, n_aliased=len(prev)),
        grid=(n // tm,),
        in_specs=[row(D_MODEL), _const_spec((1, D_MODEL)),
                  _const_spec((MAIN_WIDTH, D_MODEL)), _const_spec((SMALL_WIDTH, D_MODEL))]
                 + [_HBM_SPEC] * len(prev),
        out_specs=[spec for spec, _ in outs],
        out_shape=[shape for _, shape in outs],
        input_output_aliases={n_in + j: 2 + j for j in range(len(prev))},
        compiler_params=_cparams("parallel"),
        name="inproj",
    )(x2d, g, w_main, w_small, *prev)


def _lane_head(width=ATT_WIDTH):
    return lax.broadcasted_iota(jnp.int32, (1, width), 1) // HEAD_DIM


def _suffix_matrix(seg, terms):
    r = lax.broadcasted_iota(jnp.int32, (terms * seg, seg), 0) % seg
    c = lax.broadcasted_iota(jnp.int32, (terms * seg, seg), 1)
    return jnp.where(r >= c, 1.0, 0.0).astype(BF16)


def _split_bf16(x, terms):
    out = []
    for _ in range(terms - 1):
        head = x.astype(BF16)
        out.append(head)
        x = x - head.astype(F32)
    out.append(x.astype(BF16))
    return jnp.concatenate(out, axis=1)


def _suffix_sums(x, u, carry):
    m, n = x.shape
    seg = u.shape[1]
    n_seg = n // seg
    terms = u.shape[0] // seg
    if n_seg == 1:
        cs = _dot(_split_bf16(x, terms), u)
        return cs + carry, carry + cs[:, 0:1]
    stacked = jnp.concatenate([x[:, s * seg:(s + 1) * seg] for s in range(n_seg)], axis=0)
    cs = _dot(_split_bf16(stacked, terms), u)
    parts = [None] * n_seg
    for s in reversed(range(n_seg)):
        part = cs[s * m:(s + 1) * m, :]
        parts[s] = part + carry
        carry = carry + part[:, 0:1]
    return jnp.concatenate(parts, axis=1), carry


def _stick_block(z2, carry, u, mask):
    drop = jnp.maximum(z2, 0.0) + jnp.log2(1.0 + jnp.exp2(-jnp.abs(z2)))
    if mask is not None:
        drop = jnp.where(mask, drop, 0.0)
    later, carry = _suffix_sums(drop, u, carry)
    w = jnp.exp2(z2 - later)
    if mask is not None:
        w = jnp.where(mask, w, 0.0)
    return w, carry


def _stack_heads(q):
    lane_head = _lane_head()
    return jnp.concatenate([jnp.where(lane_head == h, q, jnp.zeros_like(q)) for h in range(N_HEADS)], axis=0)


def _fold_heads(acc, rows):
    lane_head = _lane_head()
    out = jnp.zeros((rows, ATT_WIDTH), F32)
    for h in range(N_HEADS):
        out = jnp.where(lane_head == h, acc[h * rows:(h + 1) * rows, :], out)
    return out


def _sb_prompt_kernel(q_ref, k_ref, v_ref, o_ref, acc_ref, *, bq, bk):
    assert bq == bk
    i = pl.program_id(1)
    rows = N_HEADS * bq
    qs = _stack_heads(q_ref[...])
    u = _suffix_matrix(CUMSUM_SEG, SB_SPLIT_TERMS)
    qpos = lax.broadcasted_iota(jnp.int32, (rows, bk), 0) % bq
    causal = lax.broadcasted_iota(jnp.int32, (rows, bk), 1) < qpos

    def block(start, width, carry, mask):
        start = pl.multiple_of(start, width)
        w, carry = _stick_block(_dot_nt(qs, k_ref[pl.ds(start, width), :]), carry, u, mask)
        pv = _dot(w.astype(BF16), v_ref[pl.ds(start, width), :])
        acc_ref[...] = pv if mask is not None else acc_ref[...] + pv
        return carry

    carry = block(i * bk, bk, jnp.zeros((rows, 1), F32), causal)
    carry = lax.fori_loop(0, i % 2, lambda _, c: block((i - 1) * bk, bk, c, None), carry)
    pairs = i // 2
    lax.fori_loop(0, pairs, lambda jj, c: block((pairs - 1 - jj) * 2 * bk, 2 * bk, c, None), carry)
    o_ref[...] = _fold_heads(acc_ref[...], bq).astype(o_ref.dtype)


def _logf_cum_kernel(b_ref, f_ref, logf_ref, cum_ref, *, rows_per_seq):
    logf = _log_sigmoid(f_ref[...] + b_ref[...])
    logf_ref[...] = logf
    n = logf.shape[0]
    r = lax.broadcasted_iota(jnp.int32, (PAGE_SIZE, PAGE_SIZE), 0)
    c = lax.broadcasted_iota(jnp.int32, (PAGE_SIZE, PAGE_SIZE), 1)
    within = _dot_exact(logf, jnp.where(r <= c, 1.0, 0.0).astype(F32))
    totals = jnp.broadcast_to(within[:, PAGE_SIZE - 1:PAGE_SIZE], within.shape)
    rr = lax.broadcasted_iota(jnp.int32, (n, n), 0)
    cc = lax.broadcasted_iota(jnp.int32, (n, n), 1)
    earlier_rows = jnp.where(cc // rows_per_seq == rr // rows_per_seq, jnp.where(cc < rr, 1.0, 0.0), 0.0)
    cum_ref[...] = within + _dot_exact(earlier_rows, totals)


def _logf_cum(f_rows, b_rows, rows_per_seq):
    spec = pl.BlockSpec(f_rows.shape, lambda i: (0, 0))
    return pl.pallas_call(
        functools.partial(_logf_cum_kernel, rows_per_seq=rows_per_seq),
        grid=(1,),
        in_specs=[pl.BlockSpec(b_rows.shape, lambda i: (0, 0)), spec],
        out_specs=[spec, spec],
        out_shape=[jax.ShapeDtypeStruct(f_rows.shape, F32)] * 2,
        compiler_params=_cparams("arbitrary"),
        name="logf_cum",
    )(b_rows, f_rows)


def _fox_prompt_kernel(q_ref, k_ref, v_ref, cq_ref, ck_ref, o_ref, acc_ref, *, bq, bk):
    assert bq == bk
    blk = bq
    i = pl.program_id(1)
    rows = N_HEADS * blk
    qs = _stack_heads(q_ref[...])
    cq = cq_ref[...]
    cqs = jnp.concatenate([cq[:, h:h + 1] for h in range(N_HEADS)], axis=0)
    qpos = lax.broadcasted_iota(jnp.int32, (rows, blk), 0) % blk
    causal = lax.broadcasted_iota(jnp.int32, (rows, blk), 1) <= qpos

    def block(j, state, mask):
        start = pl.multiple_of(j * blk, blk)
        s = _dot_nt(qs, k_ref[pl.ds(start, blk), :]) + cqs
        ck = ck_ref[j]
        s = jnp.concatenate([s[h * blk:(h + 1) * blk, :] - ck[h:h + 1, :] for h in range(N_HEADS)], axis=0)
        if mask is not None:
            s = jnp.where(mask, s, NEG_INF)
        m = jnp.max(s, axis=-1, keepdims=True)
        if state is not None:
            m_old, l_old = state
            m = jnp.maximum(m_old, m)
        p = jnp.exp2(s - m)
        pv = _dot(p.astype(BF16), v_ref[pl.ds(start, blk), :])
        l = jnp.sum(p, axis=-1, keepdims=True)
        if state is None:
            acc_ref[...] = pv
            return m, l
        alpha = jnp.exp2(m_old - m)
        acc_ref[...] = acc_ref[...] * alpha + pv
        return m, alpha * l_old + l

    state = block(i, None, causal)
    _, l = lax.fori_loop(0, i, lambda jj, st: block(i - 1 - jj, st, None), state)
    o_ref[...] = _fold_heads(acc_ref[...] / l, blk).astype(o_ref.dtype)


_XBUF_ROW0 = 8


def _ssm_kernel(z_ref, xbc_ref, small_ref, conv0_ref, state0_ref,
                cw_ref, cb_ref, dtb_ref, alog_ref, dskip_ref, ng_ref,
                y_ref, state_out_ref, conv_out_ref, xbuf_ref, state_ref, *, valid):
    c = pl.program_id(1)
    n_seq = xbc_ref.shape[0]
    r0 = _XBUF_ROW0

    @pl.when(c == 0)
    def _():
        xbuf_ref[...] = jnp.zeros_like(xbuf_ref)
        xbuf_ref[:, r0 - 3:r0, :] = conv0_ref[...]
        state_ref[...] = state0_ref[...]

    x_curs = [_ssm_chunk(z_ref.at[s], xbc_ref.at[s], small_ref.at[s], cw_ref, cb_ref, dtb_ref, alog_ref,
                         dskip_ref, ng_ref, y_ref.at[s], xbuf_ref.at[s], state_ref.at[s], valid)
              for s in range(n_seq)]

    @pl.when(c == pl.num_programs(1) - 1)
    def _():
        state_out_ref[...] = state_ref[...]
        for s in range(n_seq):
            conv_out_ref[s] = x_curs[s][valid - 3:valid, :]


def _ssm_chunk(z_ref, xbc_ref, small_ref, cw_ref, cb_ref, dtb_ref, alog_ref, dskip_ref, ng_ref,
               y_ref, xbuf_ref, state_ref, valid):
    L = xbc_ref.shape[0]
    S = SSM_CHUNK
    r0 = _XBUF_ROW0

    def pad_time(a):
        return a if L == S else jnp.concatenate([a, jnp.zeros((S - L,) + a.shape[1:], a.dtype)], axis=0)

    x_cur = xbc_ref[...]
    window = jnp.concatenate([xbuf_ref[...], x_cur], axis=0)
    cw = cw_ref[...]
    conv = cb_ref[...] + x_cur * cw[3:4, :]
    for i in range(SSM_CONV - 1):
        conv = conv + pltpu.roll(window, 3 - i, 0)[r0:, :] * cw[i:i + 1, :]
    xbuf_ref[...] = x_cur[L - r0:L, :]
    act = conv * jax.nn.sigmoid(conv)
    xs = act[:, :SSM_INNER]
    b_in = act[:, SSM_INNER:SSM_INNER + SSM_GROUPS * SSM_STATE].astype(BF16)
    c_in = act[:, SSM_INNER + SSM_GROUPS * SSM_STATE:].astype(BF16)

    row = lax.broadcasted_iota(jnp.int32, (L, S), 0)
    col = lax.broadcasted_iota(jnp.int32, (L, S), 1)
    tri = row >= col
    dt = _softplus(small_ref[...] + dtb_ref[...])
    if valid < L:
        dt = jnp.where(lax.broadcasted_iota(jnp.int32, dt.shape, 0) < valid, dt, 0.0)
    d_a = dt * (-jnp.exp(alog_ref[...]))
    a_cs = _dot_exact(jnp.where(tri[:, :L], 1.0, 0.0).astype(F32), d_a)
    a_cs_t = pad_time(a_cs).T
    a_last = a_cs[L - 1:L, :]
    e_cs = jnp.exp(a_cs)
    wgt = jnp.exp(a_last - a_cs) * dt
    chunk_dec = jnp.exp(a_last)

    half = lax.broadcasted_iota(jnp.int32, (1, 128), 1) // SSM_STATE_HALF
    rhalf = lax.broadcasted_iota(jnp.int32, (128, 1), 0) // SSM_STATE_HALF
    pair_cols = lambda a, p: jnp.where(half == 0, a[:, DT_LANE + 2 * p:DT_LANE + 2 * p + 1],
                                       a[:, DT_LANE + 2 * p + 1:DT_LANE + 2 * p + 2])
    b_keys = pad_time(b_in)
    scores = [_dot_nt(c_in[:, g * SSM_STATE:(g + 1) * SSM_STATE],
                      b_keys[:, g * SSM_STATE:(g + 1) * SSM_STATE]) for g in range(SSM_GROUPS)]
    ys = []
    for p in range(SSM_HEADS // 2):
        g = (2 * p) // (SSM_HEADS // SSM_GROUPS)
        bg = b_keys[:, g * SSM_STATE:(g + 1) * SSM_STATE]
        cg = c_in[:, g * SSM_STATE:(g + 1) * SSM_STATE]
        xs_p = xs[:, 128 * p:128 * (p + 1)]
        xdt = pad_time((xs_p * pair_cols(dt, p)).astype(BF16))
        y_diag = jnp.zeros((L, 128), F32)
        for hh in range(2):
            lane = DT_LANE + 2 * p + hh
            seg = a_cs[:, lane:lane + 1] - a_cs_t[lane:lane + 1, :]
            decay = jnp.exp(jnp.where(tri, seg, NEG_INF))
            y_h = _dot((scores[g] * decay).astype(BF16), xdt)
            y_diag = jnp.where(half == hh, y_h, y_diag)
        st = state_ref[128 * p:128 * (p + 1), :]
        y_off = _dot_nt(cg, st.astype(BF16)) * pair_cols(e_cs, p)
        ys.append(y_diag + y_off + dskip_ref[:, 128 * p:128 * (p + 1)] * xs_p)
        xw_t = pad_time(xs_p * pair_cols(wgt, p)).T.astype(BF16)
        lane = DT_LANE + 2 * p
        dec = jnp.where(rhalf == 0,
                        jnp.broadcast_to(chunk_dec[:, lane:lane + 1], (128, SSM_STATE)),
                        jnp.broadcast_to(chunk_dec[:, lane + 1:lane + 2], (128, SSM_STATE)))
        state_ref[128 * p:128 * (p + 1), :] = st * dec + _dot(xw_t, bg)

    z = z_ref[...]
    y = jnp.concatenate(ys, axis=1) * (z * jax.nn.sigmoid(z))
    gw = SSM_INNER // SSM_GROUPS
    parts = []
    for g in range(SSM_GROUPS):
        yg = y[:, g * gw:(g + 1) * gw]
        parts.append(yg * lax.rsqrt(jnp.mean(yg * yg, axis=-1, keepdims=True) + RMS_EPS))
    y_ref[...] = (jnp.concatenate(parts, axis=1) * ng_ref[...]).astype(y_ref.dtype)
    return x_cur


SSM_STATE_HALF = 64
SSM_SHORT_ROWS = 16
SSM_SEQS_PER_STEP = 4


def _ssm(z, xbc, small, conv0, state0, cw, cb, dtb, alog, dskip, ng, valid):
    b, t, _ = z.shape
    rows = min(t, SSM_CHUNK)
    nc = t // rows
    par = SSM_SEQS_PER_STEP if b % SSM_SEQS_PER_STEP == 0 else 1
    chunk = lambda width: pl.BlockSpec((par, rows, width), lambda bi, c: (bi, c, 0))
    per_b = lambda shape: pl.BlockSpec((par,) + shape, lambda bi, c: (bi,) + (0,) * len(shape))
    const = lambda shape: pl.BlockSpec(shape, lambda bi, c: (0,) * len(shape))
    state_rows = SSM_INNER
    return pl.pallas_call(
        functools.partial(_ssm_kernel, valid=valid),
        grid=(b // par, nc),
        in_specs=[chunk(SSM_INNER), chunk(SSM_CONV_DIM), chunk(SMALL_WIDTH),
                  per_b((SSM_CONV - 1, SSM_CONV_DIM)), per_b((state_rows, SSM_STATE)),
                  const((SSM_CONV, SSM_CONV_DIM)), const((1, SSM_CONV_DIM)),
                  const((1, SMALL_WIDTH)), const((1, SMALL_WIDTH)),
                  const((1, SSM_INNER)), const((1, SSM_INNER))],
        out_specs=[chunk(SSM_INNER), per_b((state_rows, SSM_STATE)),
                   per_b((SSM_CONV - 1, SSM_CONV_DIM))],
        out_shape=[jax.ShapeDtypeStruct((b, t, SSM_INNER), BF16),
                   jax.ShapeDtypeStruct((b, state_rows, SSM_STATE), F32),
                   jax.ShapeDtypeStruct((b, SSM_CONV - 1, SSM_CONV_DIM), F32)],
        scratch_shapes=[pltpu.VMEM((par, _XBUF_ROW0, SSM_CONV_DIM), F32),
                        pltpu.VMEM((par, state_rows, SSM_STATE), F32)],
        compiler_params=_cparams("parallel", "arbitrary"),
        name="ssm",
    )(z, xbc, small, conv0, state0, cw, cb, dtb, alog, dskip, ng)


def _merge_ffn_kernel(x_ref, ysb_ref, yssm_ref, yfx_ref, gate_ref,
                      wsb_ref, wssm_ref, wfx_ref, wo_ref, g2_ref, wup_ref, wdn_ref, gf_ref,
                      o_ref, *, final):
    d = D_MODEL
    mixed = gate_ref[:, 0:d].astype(F32) * _dot(ysb_ref[...].astype(BF16), wsb_ref[...])
    mixed = mixed + gate_ref[:, d:2 * d].astype(F32) * _dot(yssm_ref[...].astype(BF16), wssm_ref[...])
    mixed = mixed + gate_ref[:, 2 * d:3 * d].astype(F32) * _dot(yfx_ref[...].astype(BF16), wfx_ref[...])
    x = x_ref[...] + _dot(mixed.astype(BF16), wo_ref[...])
    h = _rms(x, g2_ref[...]).astype(BF16)
    hc = FFN_HIDDEN // 2
    for c in range(2):
        u = jnp.maximum(_dot(h, wup_ref[:, c * hc:(c + 1) * hc]), 0.0)
        x = x + _dot((u * u).astype(BF16), wdn_ref[c * hc:(c + 1) * hc, :])
    if final:
        x = _rms(x, gf_ref[...])
    o_ref[...] = x


def _merge_ffn(x2d, y_sb, y_ssm, y_fx, gates, lw, gf, final):
    n = x2d.shape[0]
    tm = min(512, n)
    row = lambda width: pl.BlockSpec((tm, width), lambda i: (i, 0))
    weights = [lw['w_sb_out'], lw['w_ssm_out'], lw['w_fox_out'], lw['w_o'], lw['norm2_g'],
               lw['w_up'], lw['w_down'], gf]
    return pl.pallas_call(
        functools.partial(_merge_ffn_kernel, final=final),
        grid=(n // tm,),
        in_specs=[row(D_MODEL), row(ATT_WIDTH), row(SSM_INNER), row(ATT_WIDTH),
                  row(N_BRANCH * D_MODEL)] + [_const_spec(w.shape) for w in weights],
        out_specs=row(D_MODEL),
        out_shape=jax.ShapeDtypeStruct((n, D_MODEL), F32),
        compiler_params=_cparams("parallel"),
        name="merge_ffn",
    )(x2d, y_sb, y_ssm, y_fx, gates, *weights)


def _block_diag_q(q_ref):
    q = q_ref[...]
    rows = q.shape[0]
    row_head = lax.broadcasted_iota(jnp.int32, (rows, 1), 0) // (rows // N_HEADS)
    return jnp.where(row_head == _lane_head(), q, jnp.zeros_like(q))


def _page_copy(cache_ref, stage_ref, sem_ref, layer, page, slot, p):
    return pltpu.make_async_copy(cache_ref.at[layer, page], stage_ref.at[slot, p], sem_ref.at[slot])


def _grid_step():
    return (pl.program_id(0) * pl.num_programs(1) + pl.program_id(1),
            pl.num_programs(0) * pl.num_programs(1))


def _prefetch_pages(pt_ref, caches, stages, sems, layer, group, n_chunks):
    n_pages = stages[0].shape[1]
    per_seq = n_pages // group
    step, n_steps = _grid_step()
    b, c = step // n_chunks, step % n_chunks
    slot = step % 2

    def start(bb, cc, sl):
        base = (n_chunks - 1 - cc) * per_seq
        for p in range(n_pages):
            page = pt_ref[bb * group + p // per_seq, base + p % per_seq]
            for i, (cache, stage, sem) in enumerate(zip(caches, stages, sems)):
                _page_copy(cache, stage, sem, layer, page, sl, p).start(priority=i % 2)

    @pl.when(step == 0)
    def _():
        start(b, c, slot)

    @pl.when(step + 1 < n_steps)
    def _():
        wrap = c + 1 == n_chunks
        start(jnp.where(wrap, b + 1, b), jnp.where(wrap, 0, c + 1), 1 - slot)

    for p in range(n_pages):
        for cache, stage, sem in zip(caches, stages, sems):
            _page_copy(cache, stage, sem, layer, 0, slot, p).wait()
    return slot


def _pages_to_bf16(stage_ref, slot, g, buf_ref):
    per_seq = buf_ref.shape[-1] // PAGE_SIZE
    for p in range(per_seq):
        buf_ref[g, :, p * PAGE_SIZE:(p + 1) * PAGE_SIZE] = stage_ref[slot, g * per_seq + p].astype(BF16)


def _sb_sample_kernel(pt_ref, q_ref, kn_ref, vn_ref, ck_ref, cv_ref, o_ref,
                      acc_ref, carry_ref, kstage_ref, vstage_ref, ksem, vsem, kbuf_ref, vbuf_ref,
                      *, steps, layer, n_chunks):
    group = q_ref.shape[0]
    slot = _prefetch_pages(pt_ref, (ck_ref, cv_ref), (kstage_ref, vstage_ref), (ksem, vsem), layer, group,
                           n_chunks)
    c = _grid_step()[0] % n_chunks
    rows = N_HEADS * steps
    qbd = [_block_diag_q(q_ref.at[g]) for g in range(group)]
    u = _suffix_matrix(CUMSUM_SEG, SB_SPLIT_TERMS)

    @pl.when(c == 0)
    def _():
        step = lax.broadcasted_iota(jnp.int32, (rows, PAGE_SIZE), 0) % steps
        col = lax.broadcasted_iota(jnp.int32, (rows, PAGE_SIZE), 1)
        for g in range(group):
            w, carry = _stick_block(_dot_nt(qbd[g], kn_ref[g]), jnp.zeros((rows, 1), F32),
                                    _suffix_matrix(PAGE_SIZE, SB_SPLIT_TERMS), col < step)
            acc_ref[g] = _dot(w.astype(BF16), vn_ref[g])
            carry_ref[g] = jnp.broadcast_to(carry, carry_ref.shape[1:])

    for g in range(group):
        _pages_to_bf16(kstage_ref, slot, g, kbuf_ref)
        _pages_to_bf16(vstage_ref, slot, g, vbuf_ref)
        w, carry = _stick_block(_dot(qbd[g], kbuf_ref[g]), carry_ref[g, :, 0:1], u, None)
        acc_ref[g] += _dot_nt(w.astype(BF16), vbuf_ref[g])
        carry_ref[g] = jnp.broadcast_to(carry, carry_ref.shape[1:])

    def finish():
        @pl.when(c == n_chunks - 1)
        def _():
            for g in range(group):
                o_ref[g] = _fold_heads(acc_ref[g], steps)
    return finish


def _page_staging(n_pages, rows, lanes):
    return [pltpu.VMEM((2, n_pages, rows, lanes), F32)], [pltpu.SemaphoreType.DMA((2,))]


def _fox_sample_kernel(pt_ref, q_ref, kn_ref, vn_ref, f_ref, b_ref, ck_ref, cv_ref, clf_ref,
                       o_ref, lf_out_ref, acc_ref, m_ref, l_ref, ncum_ref, rcarry_ref,
                       kstage_ref, vstage_ref, lstage_ref, ksem, vsem, lsem,
                       kbuf_ref, vbuf_ref, lbuf_ref, *, steps, layer, n_chunks):
    group = q_ref.shape[0]
    slot = _prefetch_pages(pt_ref, (ck_ref, cv_ref, clf_ref), (kstage_ref, vstage_ref, lstage_ref),
                           (ksem, vsem, lsem), layer, group, n_chunks)
    c = _grid_step()[0] % n_chunks
    rows = N_HEADS * steps
    qbd = [_block_diag_q(q_ref.at[g]) for g in range(group)]
    stat = lambda a: jnp.broadcast_to(a, m_ref.shape[1:])

    @pl.when(c == 0)
    def _():
        step = lax.broadcasted_iota(jnp.int32, (rows, PAGE_SIZE), 0) % steps
        col = lax.broadcasted_iota(jnp.int32, (rows, PAGE_SIZE), 1)
        visible = col <= step
        lbuf_ref[...] = jnp.zeros_like(lbuf_ref)
        rcarry_ref[...] = jnp.zeros_like(rcarry_ref)
        for g in range(group):
            lf = _log_sigmoid(f_ref[g] + b_ref[...])
            lf_out_ref[g] = lf
            lf = jnp.where(col < steps, lf, 0.0)
            cum = lf
            shift = 1
            while shift < steps:
                cum = cum + jnp.where(col >= shift, pltpu.roll(cum, shift, 1), 0.0)
                shift *= 2
            ncum = jnp.sum(jnp.where(visible, lf, 0.0), axis=-1, keepdims=True)
            s = jnp.where(visible, _dot_nt(qbd[g], kn_ref[g]) + (ncum - cum) * LOG2_E, NEG_INF)
            m = jnp.max(s, axis=-1, keepdims=True)
            p = jnp.exp2(s - m)
            m_ref[g] = stat(m)
            l_ref[g] = stat(jnp.sum(p, axis=-1, keepdims=True))
            acc_ref[g] = _dot(p.astype(BF16), vn_ref[g])
            ncum_ref[g] = stat(ncum)

    per_seq = kbuf_ref.shape[-1] // PAGE_SIZE
    u3 = _suffix_matrix(CUMSUM_SEG, 3)
    for g in range(group):
        _pages_to_bf16(kstage_ref, slot, g, kbuf_ref)
        _pages_to_bf16(vstage_ref, slot, g, vbuf_ref)
        for p in range(per_seq):
            lbuf_ref[g, 0:N_HEADS, p * PAGE_SIZE:(p + 1) * PAGE_SIZE] = lstage_ref[slot, g * per_seq + p]
        lf_pages = lbuf_ref[g]
        incl, carry = _suffix_sums(lf_pages, u3, rcarry_ref[g, :, 0:1])
        suffix = incl - lf_pages
        rcarry_ref[g] = jnp.broadcast_to(carry, rcarry_ref.shape[1:])
        bias = jnp.concatenate(
            [jnp.broadcast_to(suffix[h:h + 1, :], (steps, suffix.shape[1])) for h in range(N_HEADS)], axis=0)
        s = _dot(qbd[g], kbuf_ref[g]) + (bias + ncum_ref[g, :, 0:1]) * LOG2_E
        m_old = m_ref[g, :, 0:1]
        m = jnp.maximum(m_old, jnp.max(s, axis=-1, keepdims=True))
        alpha = jnp.exp2(m_old - m)
        p = jnp.exp2(s - m)
        l_ref[g] = stat(alpha * l_ref[g, :, 0:1] + jnp.sum(p, axis=-1, keepdims=True))
        acc_ref[g] = acc_ref[g] * alpha + _dot_nt(p.astype(BF16), vbuf_ref[g])
        m_ref[g] = stat(m)

    def finish():
        @pl.when(c == n_chunks - 1)
        def _():
            for g in range(group):
                o_ref[g] = _fold_heads(acc_ref[g] / l_ref[g, :, 0:1], steps)
    return finish


def _both_groups_kernel(pt_ref, *refs, prompt_kernel, sample_kernel, counts):
    parts, at = [], 0
    for n in counts:
        parts.append(refs[at:at + n])
        at += n
    p_in, s_in, p_out, s_out, p_scr, s_scr = parts
    finish_sample = sample_kernel(pt_ref, *s_in, *s_out, *s_scr)
    prompt_kernel(*p_in, *p_out, *p_scr)
    finish_sample()


def _attention_both_groups(name, layer, page_table, prompt_kernel, prompt_in, prompt_extra_specs,
                           sample_kernel, sample_in, sample_extra_specs, caches, sample_extra_out,
                           sample_extra_scratch, steps):
    q, k, v = prompt_in[:3]
    b, t, w = q.shape
    blk = ATT_BLOCK
    nq = t // blk
    q_rows = sample_in[0]
    n_seq, rows, _ = q_rows.shape
    n_pages = PAGES_PER_STEP
    n_chunks = page_table.shape[1] // n_pages
    g = n_seq * n_chunks // (b * nq)
    assert g >= 1 and g * b * nq == n_seq * n_chunks, "the two groups must split into equally many grid steps"
    seq_group = lambda bi, i: (bi * nq + i) // n_chunks

    qspec = pl.BlockSpec((None, blk, w), lambda bi, i, pt: (bi, i, 0))
    kvspec = pl.BlockSpec((None, t, w), lambda bi, i, pt: (bi, 0, 0))
    per_g = lambda r, width=w: pl.BlockSpec((g, r, width), lambda bi, i, pt: (seq_group(bi, i), 0, 0))
    prompt_specs = [qspec, kvspec, kvspec] + prompt_extra_specs(b, t, blk)
    sample_specs = ([per_g(rows), per_g(PAGE_SIZE), per_g(PAGE_SIZE)] + sample_extra_specs(per_g, rows)
                    + [_HBM_SPEC] * len(caches))
    stat = pltpu.VMEM((g, rows, 128), F32)
    stages, sems = [], []
    for cache in caches:
        st, se = _page_staging(g * n_pages, cache.shape[2], PAGE_SIZE)
        stages += st
        sems += se
    prompt_scratch = [pltpu.VMEM((N_HEADS * blk, w), F32)]
    sample_scratch = ([pltpu.VMEM((g, rows, w), F32)] + sample_extra_scratch(g, stat) + stages + sems
                      + [pltpu.VMEM((g, w, n_pages * PAGE_SIZE), BF16)] * 2)
    if len(caches) == 3:
        sample_scratch.append(pltpu.VMEM((g, 8, n_pages * PAGE_SIZE), F32))
    out_specs = [qspec, per_g(steps)] + [per_g(rows, PAGE_SIZE)] * len(sample_extra_out)
    out_shape = ([jax.ShapeDtypeStruct((b, t, w), BF16), jax.ShapeDtypeStruct((n_seq, steps, w), F32)]
                 + list(sample_extra_out))
    counts = (len(prompt_specs), len(sample_specs), 1, len(out_specs) - 1,
              len(prompt_scratch), len(sample_scratch))
    grid_spec = pltpu.PrefetchScalarGridSpec(
        num_scalar_prefetch=1,
        grid=(b, nq),
        in_specs=prompt_specs + sample_specs,
        out_specs=out_specs,
        scratch_shapes=prompt_scratch + sample_scratch,
    )
    return pl.pallas_call(
        functools.partial(_both_groups_kernel, counts=counts,
                          prompt_kernel=functools.partial(prompt_kernel, bq=blk, bk=blk),
                          sample_kernel=functools.partial(sample_kernel, steps=steps, layer=layer,
                                                          n_chunks=n_chunks)),
        grid_spec=grid_spec,
        out_shape=out_shape,
        compiler_params=_cparams("arbitrary", "arbitrary"),
        name=name,
    )(page_table, *prompt_in, *sample_in, *caches)


def _sb_attention(layer, page_table, prompt_in, sample_in, caches, steps):
    none = lambda *_: []
    return _attention_both_groups(
        "sb_attention", layer, page_table, _sb_prompt_kernel, prompt_in, none,
        _sb_sample_kernel, sample_in, none, caches, [], lambda g, stat: [stat], steps)


def _fox_attention(layer, page_table, prompt_in, sample_in, caches, steps):
    n_seq, rows, _ = sample_in[0].shape
    prompt_extra = lambda b, t, blk: [
        pl.BlockSpec((None, blk, N_HEADS), lambda bi, i, pt: (bi, i, 0)),
        pl.BlockSpec((None, t // blk, N_HEADS, blk), lambda bi, i, pt: (bi, 0, 0, 0))]
    sample_extra = lambda per_g, rows: [per_g(rows, PAGE_SIZE),
                                        pl.BlockSpec((rows, PAGE_SIZE), lambda bi, i, pt: (0, 0))]
    return _attention_both_groups(
        "fox_attention", layer, page_table, _fox_prompt_kernel, prompt_in, prompt_extra,
        _fox_sample_kernel, sample_in, sample_extra, caches,
        [jax.ShapeDtypeStruct((n_seq, rows, PAGE_SIZE), F32)],
        lambda g, stat: [stat, stat, stat, pltpu.VMEM((g, 8, 128), F32)], steps)


def _layer_weights(l, norm1_g, w_in, b_forget, conv_w, conv_b, dt_bias, a_log, d_skip, ssm_norm_g,
                   w_sb_out, w_ssm_out, w_fox_out, w_o, norm2_g, w_up, w_down):
    w = jnp.transpose(w_in, (2, 0, 1))[:, l, :]
    w_main = jnp.concatenate([w[:_OFF_F], w[_OFF_Z:_OFF_DT], w[_OFF_GATE:]], axis=0).astype(BF16)
    w_small = jnp.concatenate([w[_OFF_F:_OFF_Z], w[_OFF_DT:_OFF_GATE]], axis=0)
    w_small = jnp.pad(w_small, ((0, SMALL_WIDTH - w_small.shape[0]), (0, 0))).astype(BF16)
    pad_dt = lambda a: jnp.pad(a, (DT_LANE, SMALL_WIDTH - DT_LANE - SSM_HEADS)).reshape(1, SMALL_WIDTH)
    return {
        'norm1_g': norm1_g[l].reshape(1, D_MODEL), 'w_main': w_main, 'w_small': w_small,
        'b_forget': b_forget[l], 'conv_w': conv_w[l], 'conv_b': conv_b[l].reshape(1, SSM_CONV_DIM),
        'dt_bias': pad_dt(dt_bias[l]), 'a_log': pad_dt(a_log[l]),
        'd_skip': jnp.repeat(d_skip[l], SSM_INNER // SSM_HEADS).reshape(1, SSM_INNER),
        'ssm_norm_g': ssm_norm_g[l].reshape(1, SSM_INNER),
        'w_sb_out': w_sb_out[l].astype(BF16), 'w_ssm_out': w_ssm_out[l].astype(BF16),
        'w_fox_out': w_fox_out[l].astype(BF16), 'w_o': w_o[l].astype(BF16),
        'norm2_g': norm2_g[l].reshape(1, D_MODEL),
        'w_up': w_up[l].astype(BF16), 'w_down': w_down[l].astype(BF16),
    }


def _to_heads(a, bsz, t):
    return a.reshape(bsz, t, N_HEADS, HEAD_DIM)


def _layer(xp, xs, l, depth, lw, gf, kv_prev, caches, state_ssm, state_conv, page_table):
    final = l == depth - 1
    cache_sb_k, cache_sb_v, cache_fox_k, cache_fox_v, cache_lf_t = caches
    bsz, t, d = xp.shape
    n = bsz * t
    (q_sb, q_fx, k_sb, v_sb, k_fx, v_fx, k_sb_h, v_sb_h, k_fx_h, v_fx_h,
     z, xbc, small, gates) = _inproj(xp.reshape(n, d), lw['norm1_g'], lw['w_main'], lw['w_small'],
                                     (bsz, t, l, depth, kv_prev))
    sn, st, _ = xs.shape
    m = sn * st
    (sq_sb, sq_fx, sk_sb, sv_sb, sk_fx, sv_fx, sk_sb_h, sv_sb_h, sk_fx_h, sv_fx_h,
     sz, sxbc, ssmall, sgates) = _inproj(xs.reshape(m, d), lw['norm1_g'], lw['w_main'], lw['w_small'])

    b3 = lambda a: a.reshape(bsz, t, a.shape[-1])
    s3 = lambda a: a.reshape(sn, st, a.shape[-1])
    q_rows = lambda q: jnp.tile(s3(q), (1, N_HEADS, 1))
    pad_keys = lambda a: jnp.pad(s3(a), ((0, 0), (0, PAGE_SIZE - st), (0, 0)))
    y_sb, sy_sb = _sb_attention(l, page_table, (b3(q_sb), b3(k_sb_h), b3(v_sb_h)),
                                (q_rows(sq_sb), pad_keys(sk_sb_h), pad_keys(sv_sb_h)),
                                (cache_sb_k, cache_sb_v), st)

    f_rows = small[:, :N_HEADS].reshape(bsz, t, N_HEADS).transpose(0, 2, 1)
    rows_per_seq = t // PAGE_SIZE
    f_rows = f_rows.reshape(bsz * N_HEADS * rows_per_seq, PAGE_SIZE)
    bias_rows = jnp.tile(jnp.repeat(lw['b_forget'], rows_per_seq), bsz)[:, None]
    logf, cum = _logf_cum(f_rows, bias_rows, rows_per_seq)
    logf = logf.reshape(bsz, N_HEADS, t).transpose(0, 2, 1)
    blk = ATT_BLOCK
    cum2 = cum * LOG2_E
    cum_col = cum2.reshape(bsz, N_HEADS, t).transpose(0, 2, 1)
    cum_row = cum2.reshape(bsz, N_HEADS, t // blk, blk).transpose(0, 2, 1, 3)
    f_new = ssmall[:, :N_HEADS].reshape(sn, st, N_HEADS).transpose(0, 2, 1)
    sf_rows = jnp.pad(jnp.repeat(f_new, st, axis=1), ((0, 0), (0, 0), (0, PAGE_SIZE - st)))
    b_rows = jnp.broadcast_to(jnp.repeat(lw['b_forget'], st)[:, None], (N_HEADS * st, PAGE_SIZE))
    y_fx, sy_fx, lf_rows = _fox_attention(
        l, page_table, (b3(q_fx), b3(k_fx_h), b3(v_fx_h), cum_col, cum_row),
        (q_rows(sq_fx), pad_keys(sk_fx_h), pad_keys(sv_fx_h), sf_rows, b_rows),
        (cache_fox_k, cache_fox_v, cache_lf_t), st)
    slogf = lf_rows[:, ::st, :st].transpose(0, 2, 1)

    conv0 = jnp.zeros((bsz, SSM_CONV - 1, SSM_CONV_DIM), F32)
    ssm0 = jnp.zeros((bsz, SSM_INNER, SSM_STATE), F32)
    ssm_w = (lw['conv_w'], lw['conv_b'], lw['dt_bias'], lw['a_log'], lw['d_skip'], lw['ssm_norm_g'])
    y_ssm, ssm_new, conv_new = _ssm(b3(z), b3(xbc), b3(small), conv0, ssm0, *ssm_w, SSM_CHUNK)
    pad_rows = lambda a: jnp.pad(s3(a), ((0, 0), (0, SSM_SHORT_ROWS - st), (0, 0)))
    sy_ssm, sssm_new, sconv_new = _ssm(pad_rows(sz), pad_rows(sxbc), pad_rows(ssmall), state_conv[l],
                                       state_ssm[l].reshape(sn, SSM_INNER, SSM_STATE), *ssm_w, st)
    sy_ssm = sy_ssm[:, :st]

    xp_new = _merge_ffn(xp.reshape(n, d), y_sb.reshape(n, -1), y_ssm.reshape(n, -1), y_fx.reshape(n, -1),
                        gates, lw, gf, final)
    xs_new = _merge_ffn(xs.reshape(m, d), sy_sb.reshape(m, -1), sy_ssm.reshape(m, -1),
                        sy_fx.reshape(m, -1), sgates, lw, gf, final)
    ssm_shape = (SSM_HEADS, SSM_INNER // SSM_HEADS, SSM_STATE)
    prompt_states = (logf, ssm_new.reshape(bsz, *ssm_shape), conv_new)
    sample_states = (_to_heads(sk_sb, sn, st), _to_heads(sv_sb, sn, st), _to_heads(sk_fx, sn, st),
                     _to_heads(sv_fx, sn, st), slogf, sssm_new.reshape(sn, *ssm_shape), sconv_new)
    return (xp_new.reshape(bsz, t, d), xs_new.reshape(sn, st, d), (k_sb, v_sb, k_fx, v_fx),
            prompt_states, sample_states)


def kernel(x_prompt, x_sample, cache_sb_k, cache_sb_v, cache_fox_k, cache_fox_v, cache_fox_logf, state_ssm, state_conv, page_table, norm1_g, w_in, b_forget, conv_w, conv_b, dt_bias, a_log, d_skip, ssm_norm_g, w_sb_out, w_ssm_out, w_fox_out, w_o, norm2_g, w_up, w_down, final_norm_g):
    depth = w_in.shape[0]
    assert page_table.shape[1] % PAGES_PER_STEP == 0
    assert x_sample.shape[1] >= SSM_CONV - 1 and x_prompt.shape[1] % ATT_BLOCK == 0
    flat = lambda c: c.transpose(0, 1, 3, 4, 2).reshape(c.shape[0], c.shape[1], ATT_WIDTH, PAGE_SIZE)
    caches = (flat(cache_sb_k), flat(cache_sb_v), flat(cache_fox_k), flat(cache_fox_v),
              cache_fox_logf.transpose(0, 1, 3, 2))
    gf = final_norm_g.reshape(1, D_MODEL)
    xp, xs = x_prompt, x_sample
    prompt_states, sample_states = [], []
    kv_stacks = None
    for l in range(depth):
        lw = _layer_weights(l, norm1_g, w_in, b_forget, conv_w, conv_b, dt_bias, a_log, d_skip,
                            ssm_norm_g, w_sb_out, w_ssm_out, w_fox_out, w_o, norm2_g, w_up, w_down)
        xp, xs, kv_stacks, st_p, st_s = _layer(xp, xs, l, depth, lw, gf, kv_stacks, caches,
                                               state_ssm, state_conv, page_table)
        prompt_states.append(st_p)
        sample_states.append(st_s)
    bsz, t = x_prompt.shape[:2]
    from_t = lambda a: a.reshape(depth, bsz, N_HEADS, HEAD_DIM, t).transpose(0, 1, 4, 2, 3)
    stacked_p = [from_t(a) for a in kv_stacks] + [jnp.stack(s) for s in zip(*prompt_states)]
    stacked_s = [jnp.stack(s) for s in zip(*sample_states)]
    return (xp, xs, *stacked_p, *stacked_s)
```

```python
import functools

import jax
import jax.numpy as jnp
from jax import lax
from jax.experimental import pallas as pl
from jax.experimental.pallas import tpu as pltpu

F32 = jnp.float32
BF16 = jnp.bfloat16

D_MODEL = 1024
HEAD_DIM = 64
N_HEADS = 4
ATT_WIDTH = N_HEADS * HEAD_DIM
SSM_HEADS = 8
SSM_INNER = 512
SSM_STATE = 128
SSM_GROUPS = 2
SSM_CONV = 4
SSM_CONV_DIM = 1024
SSM_CHUNK = 128
PAGE_SIZE = 128
N_BRANCH = 3
FFN_HIDDEN = 4 * D_MODEL
RMS_EPS = 1e-6
NEG_INF = -1e30
Q_SCALE = HEAD_DIM ** -0.5
LOG2_E = 1.4426950408889634

_OFF_F = 6 * ATT_WIDTH
_OFF_Z = _OFF_F + N_HEADS
_OFF_XBC = _OFF_Z + SSM_INNER
_OFF_DT = _OFF_XBC + SSM_CONV_DIM
_OFF_GATE = _OFF_DT + SSM_HEADS
IN_WIDTH = _OFF_GATE + N_BRANCH * D_MODEL
MAIN_WIDTH = 6 * ATT_WIDTH + SSM_INNER + SSM_CONV_DIM + N_BRANCH * D_MODEL
SMALL_WIDTH = 128
DT_LANE = N_HEADS

V7X_VMEM_LIMIT_BYTES = 58 * 1024 * 1024
PAGES_PER_STEP = 16
ATT_BLOCK = 256
PROMPT_SEQS_PER_STEP = 2
SB_SPLIT_TERMS = 1
CUMSUM_SEG = 256


def _cparams(*sem):
    return pltpu.CompilerParams(dimension_semantics=sem, vmem_limit_bytes=V7X_VMEM_LIMIT_BYTES)


def _const_spec(shape):
    n = len(shape)
    return pl.BlockSpec(shape, lambda *_: (0,) * n, pipeline_mode=pl.Buffered(1))


def _rms(x, g):
    ms = jnp.mean(x * x, axis=-1, keepdims=True)
    return x * lax.rsqrt(ms + RMS_EPS) * g


def _softplus_tail(z):
    return jnp.log1p(jnp.exp(-jnp.abs(z)))


def _log_sigmoid(z):
    return jnp.minimum(z, 0.0) - _softplus_tail(z)


def _softplus(z):
    return jnp.maximum(z, 0.0) + _softplus_tail(z)


def _dot_nt(a, b):
    return lax.dot_general(a, b, (((1,), (1,)), ((), ())), preferred_element_type=F32)


def _dot(a, b):
    return jnp.dot(a, b, preferred_element_type=F32)


def _dot_exact(a, b):
    return jnp.dot(a, b, preferred_element_type=F32, precision=lax.Precision.HIGHEST)


def _store_kv(ref, a, transposed):
    if not transposed:
        ref[...] = a
    elif len(ref.shape) == 2:
        ref[...] = a.T
    else:
        ref[0] = a.T
        ref[1:] = jnp.zeros((ref.shape[0] - 1,) + ref.shape[1:], ref.dtype)


def _inproj_kernel(x_ref, g_ref, wm_ref, ws_ref, *rest, kv_transposed, n_aliased):
    (qsb_ref, qfx_ref, ksb_ref, vsb_ref, kfx_ref, vfx_ref, ksbh_ref, vsbh_ref, kfxh_ref, vfxh_ref,
     z_ref, xbc_ref, small_ref, gate_ref) = rest[n_aliased:]
    h = _rms(x_ref[...], g_ref[...]).astype(BF16)

    def mm(c0, width):
        return _dot_nt(h, wm_ref[c0:c0 + width, :])

    w = ATT_WIDTH
    qsb_ref[...] = (mm(0, w) * (Q_SCALE * LOG2_E)).astype(BF16)
    for i, (full_ref, half_ref) in enumerate(((ksb_ref, ksbh_ref), (vsb_ref, vsbh_ref))):
        a = mm((1 + i) * w, w)
        _store_kv(full_ref, a, kv_transposed)
        half_ref[...] = a.astype(BF16)
    qfx_ref[...] = (mm(3 * w, w) * (Q_SCALE * LOG2_E)).astype(BF16)
    for i, (full_ref, half_ref) in enumerate(((kfx_ref, kfxh_ref), (vfx_ref, vfxh_ref))):
        a = mm((4 + i) * w, w)
        _store_kv(full_ref, a, kv_transposed)
        half_ref[...] = a.astype(BF16)
    z_ref[...] = mm(6 * w, SSM_INNER)
    c0 = 6 * w + SSM_INNER
    for c in range(SSM_CONV_DIM // 512):
        xbc_ref[:, c * 512:(c + 1) * 512] = mm(c0 + c * 512, 512)
    c0 += SSM_CONV_DIM
    for c in range(N_BRANCH * D_MODEL // 512):
        gate_ref[:, c * 512:(c + 1) * 512] = jax.nn.sigmoid(mm(c0 + c * 512, 512)).astype(BF16)
    small_ref[...] = _dot_nt(h, ws_ref[...])


_HBM_SPEC = pl.BlockSpec(memory_space=pl.ANY)


def _inproj(x2d, g, w_main, w_small, kv_stack=None):
    n = x2d.shape[0]
    tm = min(512, n)
    row = lambda width: pl.BlockSpec((tm, width), lambda i: (i, 0))
    rows = lambda width, dt: (row(width), jax.ShapeDtypeStruct((n, width), dt))
    prev = ()
    if kv_stack is None:
        kv = rows(ATT_WIDTH, F32)
    else:
        bsz, t, layer, depth, prev = kv_stack
        prev = () if prev is None else tuple(prev)
        nt = t // tm
        if prev or depth == 1:
            block = pl.BlockSpec((None, None, ATT_WIDTH, tm), lambda i: (layer, i // nt, 0, i % nt))
        else:
            block = pl.BlockSpec((depth, None, ATT_WIDTH, tm), lambda i: (0, i // nt, 0, i % nt))
        kv = (block, jax.ShapeDtypeStruct((depth, bsz, ATT_WIDTH, t), F32))
    outs = ([rows(ATT_WIDTH, BF16)] * 2 + [kv] * 4 + [rows(ATT_WIDTH, BF16)] * 4
            + [rows(SSM_INNER, F32), rows(SSM_CONV_DIM, F32), rows(SMALL_WIDTH, F32),
               rows(N_BRANCH * D_MODEL, BF16)])
    n_in = 4
    return pl.pallas_call(
        functools.partial(_inproj_kernel, kv_transposed=kv_stack is not None, n_aliased=len(prev)),
        grid=(n // tm,),
        in_specs=[row(D_MODEL), _const_spec((1, D_MODEL)),
                  _const_spec((MAIN_WIDTH, D_MODEL)), _const_spec((SMALL_WIDTH, D_MODEL))]
                 + [_HBM_SPEC] * len(prev),
        out_specs=[spec for spec, _ in outs],
        out_shape=[shape for _, shape in outs],
        input_output_aliases={n_in + j: 2 + j for j in range(len(prev))},
        compiler_params=_cparams("parallel"),
        name="inproj",
    )(x2d, g, w_main, w_small, *prev)


def _lane_head(width=ATT_WIDTH):
    return lax.broadcasted_iota(jnp.int32, (1, width), 1) // HEAD_DIM


def _suffix_matrix(seg, terms):
    r = lax.broadcasted_iota(jnp.int32, (terms * seg, seg), 0) % seg
    c = lax.broadcasted_iota(jnp.int32, (terms * seg, seg), 1)
    return jnp.where(r >= c, 1.0, 0.0).astype(BF16)


def _split_bf16(x, terms):
    out = []
    for _ in range(terms - 1):
        head = x.astype(BF16)
        out.append(head)
        x = x - head.astype(F32)
    out.append(x.astype(BF16))
    return jnp.concatenate(out, axis=1)


def _suffix_sums(x, u, carry):
    m, n = x.shape
    seg = u.shape[1]
    n_seg = n // seg
    terms = u.shape[0] // seg
    if n_seg == 1:
        cs = _dot(_split_bf16(x, terms), u)
        return cs + carry, carry + cs[:, 0:1]
    stacked = jnp.concatenate([x[:, s * seg:(s + 1) * seg] for s in range(n_seg)], axis=0)
    cs = _dot(_split_bf16(stacked, terms), u)
    parts = [None] * n_seg
    for s in reversed(range(n_seg)):
        part = cs[s * m:(s + 1) * m, :]
        parts[s] = part + carry
        carry = carry + part[:, 0:1]
    return jnp.concatenate(parts, axis=1), carry


def _stick_block(z2, carry, u, mask):
    drop = jnp.maximum(z2, 0.0) + jnp.log2(1.0 + jnp.exp2(-jnp.abs(z2)))
    if mask is not None:
        drop = jnp.where(mask, drop, 0.0)
    later, carry = _suffix_sums(drop, u, carry)
    w = jnp.exp2(z2 - later)
    if mask is not None:
        w = jnp.where(mask, w, 0.0)
    return w, carry


def _stack_heads(q):
    lane_head = _lane_head()
    return jnp.concatenate([jnp.where(lane_head == h, q, jnp.zeros_like(q)) for h in range(N_HEADS)], axis=0)


def _fold_heads(acc, rows):
    lane_head = _lane_head()
    out = jnp.zeros((rows, ATT_WIDTH), F32)
    for h in range(N_HEADS):
        out = jnp.where(lane_head == h, acc[h * rows:(h + 1) * rows, :], out)
    return out


def _sb_prompt_kernel(q_ref, k_ref, v_ref, o_ref, acc_ref, *, bq, bk):
    assert bq == bk
    i = pl.program_id(1)
    n_seq = q_ref.shape[0]
    rows = N_HEADS * bq
    qs = [_stack_heads(q_ref[s]) for s in range(n_seq)]
    u = _suffix_matrix(CUMSUM_SEG, SB_SPLIT_TERMS)
    qpos = lax.broadcasted_iota(jnp.int32, (rows, bk), 0) % bq
    causal = lax.broadcasted_iota(jnp.int32, (rows, bk), 1) < qpos

    def block(start, width, carries, mask):
        start = pl.multiple_of(start, width)
        out = []
        for s in range(n_seq):
            w, carry = _stick_block(_dot_nt(qs[s], k_ref[s, pl.ds(start, width), :]), carries[s], u, mask)
            pv = _dot(w.astype(BF16), v_ref[s, pl.ds(start, width), :])
            acc_ref[s] = pv if mask is not None else acc_ref[s] + pv
            out.append(carry)
        return tuple(out)

    carries = block(i * bk, bk, (jnp.zeros((rows, 1), F32),) * n_seq, causal)
    carries = lax.fori_loop(0, i % 2, lambda _, c: block((i - 1) * bk, bk, c, None), carries)
    pairs = i // 2
    lax.fori_loop(0, pairs, lambda jj, c: block((pairs - 1 - jj) * 2 * bk, 2 * bk, c, None), carries)
    for s in range(n_seq):
        o_ref[s] = _fold_heads(acc_ref[s], bq).astype(o_ref.dtype)


def _logf_cum_kernel(b_ref, f_ref, logf_ref, cum_ref, *, rows_per_seq):
    logf = _log_sigmoid(f_ref[...] + b_ref[...])
    logf_ref[...] = logf
    n = logf.shape[0]
    r = lax.broadcasted_iota(jnp.int32, (PAGE_SIZE, PAGE_SIZE), 0)
    c = lax.broadcasted_iota(jnp.int32, (PAGE_SIZE, PAGE_SIZE), 1)
    within = _dot_exact(logf, jnp.where(r <= c, 1.0, 0.0).astype(F32))
    totals = jnp.broadcast_to(within[:, PAGE_SIZE - 1:PAGE_SIZE], within.shape)
    rr = lax.broadcasted_iota(jnp.int32, (n, n), 0)
    cc = lax.broadcasted_iota(jnp.int32, (n, n), 1)
    earlier_rows = jnp.where(cc // rows_per_seq == rr // rows_per_seq, jnp.where(cc < rr, 1.0, 0.0), 0.0)
    cum_ref[...] = within + _dot_exact(earlier_rows, totals)


def _logf_cum(f_rows, b_rows, rows_per_seq):
    spec = pl.BlockSpec(f_rows.shape, lambda i: (0, 0))
    return pl.pallas_call(
        functools.partial(_logf_cum_kernel, rows_per_seq=rows_per_seq),
        grid=(1,),
        in_specs=[pl.BlockSpec(b_rows.shape, lambda i: (0, 0)), spec],
        out_specs=[spec, spec],
        out_shape=[jax.ShapeDtypeStruct(f_rows.shape, F32)] * 2,
        compiler_params=_cparams("arbitrary"),
        name="logf_cum",
    )(b_rows, f_rows)


def _fox_prompt_kernel(q_ref, k_ref, v_ref, cq_ref, ck_ref, o_ref, acc_ref, *, bq, bk):
    assert bq == bk
    blk = bq
    i = pl.program_id(1)
    n_seq = q_ref.shape[0]
    rows = N_HEADS * blk
    qs = [_stack_heads(q_ref[s]) for s in range(n_seq)]
    cqs = [jnp.concatenate([cq_ref[s, :, h:h + 1] for h in range(N_HEADS)], axis=0) for s in range(n_seq)]
    qpos = lax.broadcasted_iota(jnp.int32, (rows, blk), 0) % blk
    causal = lax.broadcasted_iota(jnp.int32, (rows, blk), 1) <= qpos

    def block(j, states, mask):
        start = pl.multiple_of(j * blk, blk)
        out = []
        for s in range(n_seq):
            sc = _dot_nt(qs[s], k_ref[s, pl.ds(start, blk), :]) + cqs[s]
            ck = ck_ref[s, j]
            sc = jnp.concatenate([sc[h * blk:(h + 1) * blk, :] - ck[h:h + 1, :] for h in range(N_HEADS)], axis=0)
            if mask is not None:
                sc = jnp.where(mask, sc, NEG_INF)
            m = jnp.max(sc, axis=-1, keepdims=True)
            if states is not None:
                m_old, l_old = states[s]
                m = jnp.maximum(m_old, m)
            p = jnp.exp2(sc - m)
            pv = _dot(p.astype(BF16), v_ref[s, pl.ds(start, blk), :])
            l = jnp.sum(p, axis=-1, keepdims=True)
            if states is None:
                acc_ref[s] = pv
            else:
                alpha = jnp.exp2(m_old - m)
                acc_ref[s] = acc_ref[s] * alpha + pv
                l = alpha * l_old + l
            out.append((m, l))
        return tuple(out)

    states = block(i, None, causal)
    states = lax.fori_loop(0, i, lambda jj, st: block(i - 1 - jj, st, None), states)
    for s in range(n_seq):
        o_ref[s] = _fold_heads(acc_ref[s] / states[s][1], blk).astype(o_ref.dtype)


_XBUF_ROW0 = 8


def _ssm_kernel(z_ref, xbc_ref, small_ref, conv0_ref, state0_ref,
                cw_ref, cb_ref, dtb_ref, alog_ref, dskip_ref, ng_ref,
                y_ref, state_out_ref, conv_out_ref, xbuf_ref, state_ref, *, valid):
    c = pl.program_id(1)
    n_seq = xbc_ref.shape[0]
    r0 = _XBUF_ROW0

    @pl.when(c == 0)
    def _():
        xbuf_ref[...] = jnp.zeros_like(xbuf_ref)
        xbuf_ref[:, r0 - 3:r0, :] = conv0_ref[...]
        state_ref[...] = state0_ref[...]

    x_curs = [_ssm_chunk(z_ref.at[s], xbc_ref.at[s], small_ref.at[s], cw_ref, cb_ref, dtb_ref, alog_ref,
                         dskip_ref, ng_ref, y_ref.at[s], xbuf_ref.at[s], state_ref.at[s], valid)
              for s in range(n_seq)]

    @pl.when(c == pl.num_programs(1) - 1)
    def _():
        state_out_ref[...] = state_ref[...]
        for s in range(n_seq):
            conv_out_ref[s] = x_curs[s][valid - 3:valid, :]


def _ssm_chunk(z_ref, xbc_ref, small_ref, cw_ref, cb_ref, dtb_ref, alog_ref, dskip_ref, ng_ref,
               y_ref, xbuf_ref, state_ref, valid):
    L = xbc_ref.shape[0]
    S = SSM_CHUNK
    r0 = _XBUF_ROW0

    def pad_time(a):
        return a if L == S else jnp.concatenate([a, jnp.zeros((S - L,) + a.shape[1:], a.dtype)], axis=0)

    x_cur = xbc_ref[...]
    window = jnp.concatenate([xbuf_ref[...], x_cur], axis=0)
    cw = cw_ref[...]
    conv = cb_ref[...] + x_cur * cw[3:4, :]
    for i in range(SSM_CONV - 1):
        conv = conv + pltpu.roll(window, 3 - i, 0)[r0:, :] * cw[i:i + 1, :]
    xbuf_ref[...] = x_cur[L - r0:L, :]
    act = conv * jax.nn.sigmoid(conv)
    xs = act[:, :SSM_INNER]
    b_in = act[:, SSM_INNER:SSM_INNER + SSM_GROUPS * SSM_STATE].astype(BF16)
    c_in = act[:, SSM_INNER + SSM_GROUPS * SSM_STATE:].astype(BF16)

    row = lax.broadcasted_iota(jnp.int32, (L, S), 0)
    col = lax.broadcasted_iota(jnp.int32, (L, S), 1)
    tri = row >= col
    dt = _softplus(small_ref[...] + dtb_ref[...])
    if valid < L:
        dt = jnp.where(lax.broadcasted_iota(jnp.int32, dt.shape, 0) < valid, dt, 0.0)
    d_a = dt * (-jnp.exp(alog_ref[...]))
    a_cs = _dot_exact(jnp.where(tri[:, :L], 1.0, 0.0).astype(F32), d_a)
    a_cs_t = pad_time(a_cs).T
    a_last = a_cs[L - 1:L, :]
    e_cs = jnp.exp(a_cs)
    wgt = jnp.exp(a_last - a_cs) * dt
    chunk_dec = jnp.exp(a_last)

    half = lax.broadcasted_iota(jnp.int32, (1, 128), 1) // SSM_STATE_HALF
    rhalf = lax.broadcasted_iota(jnp.int32, (128, 1), 0) // SSM_STATE_HALF
    pair_cols = lambda a, p: jnp.where(half == 0, a[:, DT_LANE + 2 * p:DT_LANE + 2 * p + 1],
                                       a[:, DT_LANE + 2 * p + 1:DT_LANE + 2 * p + 2])
    b_keys = pad_time(b_in)
    scores = [_dot_nt(c_in[:, g * SSM_STATE:(g + 1) * SSM_STATE],
                      b_keys[:, g * SSM_STATE:(g + 1) * SSM_STATE]) for g in range(SSM_GROUPS)]
    ys = []
    for p in range(SSM_HEADS // 2):
        g = (2 * p) // (SSM_HEADS // SSM_GROUPS)
        bg = b_keys[:, g * SSM_STATE:(g + 1) * SSM_STATE]
        cg = c_in[:, g * SSM_STATE:(g + 1) * SSM_STATE]
        xs_p = xs[:, 128 * p:128 * (p + 1)]
        xdt = pad_time((xs_p * pair_cols(dt, p)).astype(BF16))
        y_diag = jnp.zeros((L, 128), F32)
        for hh in range(2):
            lane = DT_LANE + 2 * p + hh
            seg = a_cs[:, lane:lane + 1] - a_cs_t[lane:lane + 1, :]
            decay = jnp.exp(jnp.where(tri, seg, NEG_INF))
            y_h = _dot((scores[g] * decay).astype(BF16), xdt)
            y_diag = jnp.where(half == hh, y_h, y_diag)
        st = state_ref[128 * p:128 * (p + 1), :]
        y_off = _dot_nt(cg, st.astype(BF16)) * pair_cols(e_cs, p)
        ys.append(y_diag + y_off + dskip_ref[:, 128 * p:128 * (p + 1)] * xs_p)
        xw_t = pad_time(xs_p * pair_cols(wgt, p)).T.astype(BF16)
        lane = DT_LANE + 2 * p
        dec = jnp.where(rhalf == 0,
                        jnp.broadcast_to(chunk_dec[:, lane:lane + 1], (128, SSM_STATE)),
                        jnp.broadcast_to(chunk_dec[:, lane + 1:lane + 2], (128, SSM_STATE)))
        state_ref[128 * p:128 * (p + 1), :] = st * dec + _dot(xw_t, bg)

    z = z_ref[...]
    y = jnp.concatenate(ys, axis=1) * (z * jax.nn.sigmoid(z))
    gw = SSM_INNER // SSM_GROUPS
    parts = []
    for g in range(SSM_GROUPS):
        yg = y[:, g * gw:(g + 1) * gw]
        parts.append(yg * lax.rsqrt(jnp.mean(yg * yg, axis=-1, keepdims=True) + RMS_EPS))
    y_ref[...] = (jnp.concatenate(parts, axis=1) * ng_ref[...]).astype(y_ref.dtype)
    return x_cur


SSM_STATE_HALF = 64
SSM_SHORT_ROWS = 16
SSM_SEQS_PER_STEP = 4


def _ssm(z, xbc, small, conv0, state0, cw, cb, dtb, alog, dskip, ng, valid):
    b, t, _ = z.shape
    rows = min(t, SSM_CHUNK)
    nc = t // rows
    par = SSM_SEQS_PER_STEP if b % SSM_SEQS_PER_STEP == 0 else 1
    chunk = lambda width: pl.BlockSpec((par, rows, width), lambda bi, c: (bi, c, 0))
    per_b = lambda shape: pl.BlockSpec((par,) + shape, lambda bi, c: (bi,) + (0,) * len(shape))
    const = lambda shape: pl.BlockSpec(shape, lambda bi, c: (0,) * len(shape))
    state_rows = SSM_INNER
    return pl.pallas_call(
        functools.partial(_ssm_kernel, valid=valid),
        grid=(b // par, nc),
        in_specs=[chunk(SSM_INNER), chunk(SSM_CONV_DIM), chunk(SMALL_WIDTH),
                  per_b((SSM_CONV - 1, SSM_CONV_DIM)), per_b((state_rows, SSM_STATE)),
                  const((SSM_CONV, SSM_CONV_DIM)), const((1, SSM_CONV_DIM)),
                  const((1, SMALL_WIDTH)), const((1, SMALL_WIDTH)),
                  const((1, SSM_INNER)), const((1, SSM_INNER))],
        out_specs=[chunk(SSM_INNER), per_b((state_rows, SSM_STATE)),
                   per_b((SSM_CONV - 1, SSM_CONV_DIM))],
        out_shape=[jax.ShapeDtypeStruct((b, t, SSM_INNER), BF16),
                   jax.ShapeDtypeStruct((b, state_rows, SSM_STATE), F32),
                   jax.ShapeDtypeStruct((b, SSM_CONV - 1, SSM_CONV_DIM), F32)],
        scratch_shapes=[pltpu.VMEM((par, _XBUF_ROW0, SSM_CONV_DIM), F32),
                        pltpu.VMEM((par, state_rows, SSM_STATE), F32)],
        compiler_params=_cparams("parallel", "arbitrary"),
        name="ssm",
    )(z, xbc, small, conv0, state0, cw, cb, dtb, alog, dskip, ng)


def _merge_ffn_kernel(x_ref, ysb_ref, yssm_ref, yfx_ref, gate_ref,
                      wsb_ref, wssm_ref, wfx_ref, wo_ref, g2_ref, wup_ref, wdn_ref, gf_ref,
                      o_ref, *, final):
    d = D_MODEL
    mixed = gate_ref[:, 0:d].astype(F32) * _dot(ysb_ref[...].astype(BF16), wsb_ref[...])
    mixed = mixed + gate_ref[:, d:2 * d].astype(F32) * _dot(yssm_ref[...].astype(BF16), wssm_ref[...])
    mixed = mixed + gate_ref[:, 2 * d:3 * d].astype(F32) * _dot(yfx_ref[...].astype(BF16), wfx_ref[...])
    x = x_ref[...] + _dot(mixed.astype(BF16), wo_ref[...])
    h = _rms(x, g2_ref[...]).astype(BF16)
    hc = FFN_HIDDEN // 2
    for c in range(2):
        u = jnp.maximum(_dot(h, wup_ref[:, c * hc:(c + 1) * hc]), 0.0)
        x = x + _dot((u * u).astype(BF16), wdn_ref[c * hc:(c + 1) * hc, :])
    if final:
        x = _rms(x, gf_ref[...])
    o_ref[...] = x


def _merge_ffn(x2d, y_sb, y_ssm, y_fx, gates, lw, gf, final):
    n = x2d.shape[0]
    tm = min(512, n)
    row = lambda width: pl.BlockSpec((tm, width), lambda i: (i, 0))
    weights = [lw['w_sb_out'], lw['w_ssm_out'], lw['w_fox_out'], lw['w_o'], lw['norm2_g'],
               lw['w_up'], lw['w_down'], gf]
    return pl.pallas_call(
        functools.partial(_merge_ffn_kernel, final=final),
        grid=(n // tm,),
        in_specs=[row(D_MODEL), row(ATT_WIDTH), row(SSM_INNER), row(ATT_WIDTH),
                  row(N_BRANCH * D_MODEL)] + [_const_spec(w.shape) for w in weights],
        out_specs=row(D_MODEL),
        out_shape=jax.ShapeDtypeStruct((n, D_MODEL), F32),
        compiler_params=_cparams("parallel"),
        name="merge_ffn",
    )(x2d, y_sb, y_ssm, y_fx, gates, *weights)


def _block_diag_q(q_ref):
    q = q_ref[...]
    rows = q.shape[0]
    row_head = lax.broadcasted_iota(jnp.int32, (rows, 1), 0) // (rows // N_HEADS)
    return jnp.where(row_head == _lane_head(), q, jnp.zeros_like(q))


def _page_copy(cache_ref, stage_ref, sem_ref, layer, page, slot, p):
    return pltpu.make_async_copy(cache_ref.at[layer, page], stage_ref.at[slot, p], sem_ref.at[slot])


def _grid_step():
    return (pl.program_id(0) * pl.num_programs(1) + pl.program_id(1),
            pl.num_programs(0) * pl.num_programs(1))


def _prefetch_pages(pt_ref, caches, stages, sems, layer, group, n_chunks):
    n_pages = stages[0].shape[1]
    per_seq = n_pages // group
    step, n_steps = _grid_step()
    b, c = step // n_chunks, step % n_chunks
    slot = step % 2

    def start(bb, cc, sl):
        base = (n_chunks - 1 - cc) * per_seq
        for p in range(n_pages):
            page = pt_ref[bb * group + p // per_seq, base + p % per_seq]
            for i, (cache, stage, sem) in enumerate(zip(caches, stages, sems)):
                _page_copy(cache, stage, sem, layer, page, sl, p).start(priority=i % 2)

    @pl.when(step == 0)
    def _():
        start(b, c, slot)

    @pl.when(step + 1 < n_steps)
    def _():
        wrap = c + 1 == n_chunks
        start(jnp.where(wrap, b + 1, b), jnp.where(wrap, 0, c + 1), 1 - slot)

    for p in range(n_pages):
        for cache, stage, sem in zip(caches, stages, sems):
            _page_copy(cache, stage, sem, layer, 0, slot, p).wait()
    return slot


def _pages_to_bf16(stage_ref, slot, g, buf_ref):
    per_seq = buf_ref.shape[-1] // PAGE_SIZE
    for p in range(per_seq):
        buf_ref[g, :, p * PAGE_SIZE:(p + 1) * PAGE_SIZE] = stage_ref[slot, g * per_seq + p].astype(BF16)


def _sb_sample_kernel(pt_ref, q_ref, kn_ref, vn_ref, ck_ref, cv_ref, o_ref,
                      acc_ref, carry_ref, kstage_ref, vstage_ref, ksem, vsem, kbuf_ref, vbuf_ref,
                      *, steps, layer, n_chunks):
    group = q_ref.shape[0]
    slot = _prefetch_pages(pt_ref, (ck_ref, cv_ref), (kstage_ref, vstage_ref), (ksem, vsem), layer, group,
                           n_chunks)
    c = _grid_step()[0] % n_chunks
    rows = N_HEADS * steps
    qbd = [_block_diag_q(q_ref.at[g]) for g in range(group)]
    u = _suffix_matrix(CUMSUM_SEG, SB_SPLIT_TERMS)

    @pl.when(c == 0)
    def _():
        step = lax.broadcasted_iota(jnp.int32, (rows, PAGE_SIZE), 0) % steps
        col = lax.broadcasted_iota(jnp.int32, (rows, PAGE_SIZE), 1)
        for g in range(group):
            w, carry = _stick_block(_dot_nt(qbd[g], kn_ref[g]), jnp.zeros((rows, 1), F32),
                                    _suffix_matrix(PAGE_SIZE, SB_SPLIT_TERMS), col < step)
            acc_ref[g] = _dot(w.astype(BF16), vn_ref[g])
            carry_ref[g] = jnp.broadcast_to(carry, carry_ref.shape[1:])

    for g in range(group):
        _pages_to_bf16(kstage_ref, slot, g, kbuf_ref)
        _pages_to_bf16(vstage_ref, slot, g, vbuf_ref)
        w, carry = _stick_block(_dot(qbd[g], kbuf_ref[g]), carry_ref[g, :, 0:1], u, None)
        acc_ref[g] += _dot_nt(w.astype(BF16), vbuf_ref[g])
        carry_ref[g] = jnp.broadcast_to(carry, carry_ref.shape[1:])

    def finish():
        @pl.when(c == n_chunks - 1)
        def _():
            for g in range(group):
                o_ref[g] = _fold_heads(acc_ref[g], steps)
    return finish


def _page_staging(n_pages, rows, lanes):
    return [pltpu.VMEM((2, n_pages, rows, lanes), F32)], [pltpu.SemaphoreType.DMA((2,))]


def _fox_sample_kernel(pt_ref, q_ref, kn_ref, vn_ref, f_ref, b_ref, ck_ref, cv_ref, clf_ref,
                       o_ref, lf_out_ref, acc_ref, m_ref, l_ref, ncum_ref, rcarry_ref,
                       kstage_ref, vstage_ref, lstage_ref, ksem, vsem, lsem,
                       kbuf_ref, vbuf_ref, lbuf_ref, *, steps, layer, n_chunks):
    group = q_ref.shape[0]
    slot = _prefetch_pages(pt_ref, (ck_ref, cv_ref, clf_ref), (kstage_ref, vstage_ref, lstage_ref),
                           (ksem, vsem, lsem), layer, group, n_chunks)
    c = _grid_step()[0] % n_chunks
    rows = N_HEADS * steps
    qbd = [_block_diag_q(q_ref.at[g]) for g in range(group)]
    stat = lambda a: jnp.broadcast_to(a, m_ref.shape[1:])

    @pl.when(c == 0)
    def _():
        step = lax.broadcasted_iota(jnp.int32, (rows, PAGE_SIZE), 0) % steps
        col = lax.broadcasted_iota(jnp.int32, (rows, PAGE_SIZE), 1)
        visible = col <= step
        lbuf_ref[...] = jnp.zeros_like(lbuf_ref)
        rcarry_ref[...] = jnp.zeros_like(rcarry_ref)
        for g in range(group):
            lf = _log_sigmoid(f_ref[g] + b_ref[...])
            lf_out_ref[g] = lf
            lf = jnp.where(col < steps, lf, 0.0)
            cum = lf
            shift = 1
            while shift < steps:
                cum = cum + jnp.where(col >= shift, pltpu.roll(cum, shift, 1), 0.0)
                shift *= 2
            ncum = jnp.sum(jnp.where(visible, lf, 0.0), axis=-1, keepdims=True)
            s = jnp.where(visible, _dot_nt(qbd[g], kn_ref[g]) + (ncum - cum) * LOG2_E, NEG_INF)
            m = jnp.max(s, axis=-1, keepdims=True)
            p = jnp.exp2(s - m)
            m_ref[g] = stat(m)
            l_ref[g] = stat(jnp.sum(p, axis=-1, keepdims=True))
            acc_ref[g] = _dot(p.astype(BF16), vn_ref[g])
            ncum_ref[g] = stat(ncum)

    per_seq = kbuf_ref.shape[-1] // PAGE_SIZE
    u3 = _suffix_matrix(CUMSUM_SEG, 3)
    for g in range(group):
        _pages_to_bf16(kstage_ref, slot, g, kbuf_ref)
        _pages_to_bf16(vstage_ref, slot, g, vbuf_ref)
        for p in range(per_seq):
            lbuf_ref[g, 0:N_HEADS, p * PAGE_SIZE:(p + 1) * PAGE_SIZE] = lstage_ref[slot, g * per_seq + p]
        lf_pages = lbuf_ref[g]
        incl, carry = _suffix_sums(lf_pages, u3, rcarry_ref[g, :, 0:1])
        suffix = incl - lf_pages
        rcarry_ref[g] = jnp.broadcast_to(carry, rcarry_ref.shape[1:])
        bias = jnp.concatenate(
            [jnp.broadcast_to(suffix[h:h + 1, :], (steps, suffix.shape[1])) for h in range(N_HEADS)], axis=0)
        s = _dot(qbd[g], kbuf_ref[g]) + (bias + ncum_ref[g, :, 0:1]) * LOG2_E
        m_old = m_ref[g, :, 0:1]
        m = jnp.maximum(m_old, jnp.max(s, axis=-1, keepdims=True))
        alpha = jnp.exp2(m_old - m)
        p = jnp.exp2(s - m)
        l_ref[g] = stat(alpha * l_ref[g, :, 0:1] + jnp.sum(p, axis=-1, keepdims=True))
        acc_ref[g] = acc_ref[g] * alpha + _dot_nt(p.astype(BF16), vbuf_ref[g])
        m_ref[g] = stat(m)

    def finish():
        @pl.when(c == n_chunks - 1)
        def _():
            for g in range(group):
                o_ref[g] = _fold_heads(acc_ref[g] / l_ref[g, :, 0:1], steps)
    return finish


def _both_groups_kernel(pt_ref, *refs, prompt_kernel, sample_kernel, counts):
    parts, at = [], 0
    for n in counts:
        parts.append(refs[at:at + n])
        at += n
    p_in, s_in, p_out, s_out, p_scr, s_scr = parts
    finish_sample = sample_kernel(pt_ref, *s_in, *s_out, *s_scr)
    prompt_kernel(*p_in, *p_out, *p_scr)
    finish_sample()


def _attention_both_groups(name, layer, page_table, prompt_kernel, prompt_in, prompt_extra_specs,
                           sample_kernel, sample_in, sample_extra_specs, caches, sample_extra_out,
                           sample_extra_scratch, steps):
    q, k, v = prompt_in[:3]
    b, t, w = q.shape
    blk = ATT_BLOCK
    nq = t // blk
    pb = PROMPT_SEQS_PER_STEP if b % PROMPT_SEQS_PER_STEP == 0 else 1
    q_rows = sample_in[0]
    n_seq, rows, _ = q_rows.shape
    n_pages = PAGES_PER_STEP
    n_chunks = page_table.shape[1] // n_pages
    steps_total = (b // pb) * nq
    g = n_seq * n_chunks // steps_total
    assert g >= 1 and g * steps_total == n_seq * n_chunks, "the two groups must split into equally many grid steps"
    seq_group = lambda bi, i: (bi * nq + i) // n_chunks

    qspec = pl.BlockSpec((pb, blk, w), lambda bi, i, pt: (bi, i, 0))
    kvspec = pl.BlockSpec((pb, t, w), lambda bi, i, pt: (bi, 0, 0), pipeline_mode=pl.Buffered(1))
    per_g = lambda r, width=w: pl.BlockSpec((g, r, width), lambda bi, i, pt: (seq_group(bi, i), 0, 0))
    prompt_specs = [qspec, kvspec, kvspec] + prompt_extra_specs(pb, t, blk)
    sample_specs = ([per_g(rows), per_g(PAGE_SIZE), per_g(PAGE_SIZE)] + sample_extra_specs(per_g, rows)
                    + [_HBM_SPEC] * len(caches))
    stat = pltpu.VMEM((g, rows, 128), F32)
    stages, sems = [], []
    for cache in caches:
        st, se = _page_staging(g * n_pages, cache.shape[2], PAGE_SIZE)
        stages += st
        sems += se
    prompt_scratch = [pltpu.VMEM((pb, N_HEADS * blk, w), F32)]
    sample_scratch = ([pltpu.VMEM((g, rows, w), F32)] + sample_extra_scratch(g, stat) + stages + sems
                      + [pltpu.VMEM((g, w, n_pages * PAGE_SIZE), BF16)] * 2)
    if len(caches) == 3:
        sample_scratch.append(pltpu.VMEM((g, 8, n_pages * PAGE_SIZE), F32))
    out_specs = [qspec, per_g(steps)] + [per_g(rows, PAGE_SIZE)] * len(sample_extra_out)
    out_shape = ([jax.ShapeDtypeStruct((b, t, w), BF16), jax.ShapeDtypeStruct((n_seq, steps, w), F32)]
                 + list(sample_extra_out))
    counts = (len(prompt_specs), len(sample_specs), 1, len(out_specs) - 1,
              len(prompt_scratch), len(sample_scratch))
    grid_spec = pltpu.PrefetchScalarGridSpec(
        num_scalar_prefetch=1,
        grid=(b // pb, nq),
        in_specs=prompt_specs + sample_specs,
        out_specs=out_specs,
        scratch_shapes=prompt_scratch + sample_scratch,
    )
    return pl.pallas_call(
        functools.partial(_both_groups_kernel, counts=counts,
                          prompt_kernel=functools.partial(prompt_kernel, bq=blk, bk=blk),
                          sample_kernel=functools.partial(sample_kernel, steps=steps, layer=layer,
                                                          n_chunks=n_chunks)),
        grid_spec=grid_spec,
        out_shape=out_shape,
        compiler_params=_cparams("arbitrary", "arbitrary"),
        name=name,
    )(page_table, *prompt_in, *sample_in, *caches)


def _sb_attention(layer, page_table, prompt_in, sample_in, caches, steps):
    none = lambda *_: []
    return _attention_both_groups(
        "sb_attention", layer, page_table, _sb_prompt_kernel, prompt_in, none,
        _sb_sample_kernel, sample_in, none, caches, [], lambda g, stat: [stat], steps)


def _fox_attention(layer, page_table, prompt_in, sample_in, caches, steps):
    n_seq, rows, _ = sample_in[0].shape
    prompt_extra = lambda pb, t, blk: [
        pl.BlockSpec((pb, blk, N_HEADS), lambda bi, i, pt: (bi, i, 0)),
        pl.BlockSpec((pb, t // blk, N_HEADS, blk), lambda bi, i, pt: (bi, 0, 0, 0))]
    sample_extra = lambda per_g, rows: [per_g(rows, PAGE_SIZE),
                                        pl.BlockSpec((rows, PAGE_SIZE), lambda bi, i, pt: (0, 0))]
    return _attention_both_groups(
        "fox_attention", layer, page_table, _fox_prompt_kernel, prompt_in, prompt_extra,
        _fox_sample_kernel, sample_in, sample_extra, caches,
        [jax.ShapeDtypeStruct((n_seq, rows, PAGE_SIZE), F32)],
        lambda g, stat: [stat, stat, stat, pltpu.VMEM((g, 8, 128), F32)], steps)


def _layer_weights(l, norm1_g, w_in, b_forget, conv_w, conv_b, dt_bias, a_log, d_skip, ssm_norm_g,
                   w_sb_out, w_ssm_out, w_fox_out, w_o, norm2_g, w_up, w_down):
    w = jnp.transpose(w_in, (2, 0, 1))[:, l, :]
    w_main = jnp.concatenate([w[:_OFF_F], w[_OFF_Z:_OFF_DT], w[_OFF_GATE:]], axis=0).astype(BF16)
    w_small = jnp.concatenate([w[_OFF_F:_OFF_Z], w[_OFF_DT:_OFF_GATE]], axis=0)
    w_small = jnp.pad(w_small, ((0, SMALL_WIDTH - w_small.shape[0]), (0, 0))).astype(BF16)
    pad_dt = lambda a: jnp.pad(a, (DT_LANE, SMALL_WIDTH - DT_LANE - SSM_HEADS)).reshape(1, SMALL_WIDTH)
    return {
        'norm1_g': norm1_g[l].reshape(1, D_MODEL), 'w_main': w_main, 'w_small': w_small,
        'b_forget': b_forget[l], 'conv_w': conv_w[l], 'conv_b': conv_b[l].reshape(1, SSM_CONV_DIM),
        'dt_bias': pad_dt(dt_bias[l]), 'a_log': pad_dt(a_log[l]),
        'd_skip': jnp.repeat(d_skip[l], SSM_INNER // SSM_HEADS).reshape(1, SSM_INNER),
        'ssm_norm_g': ssm_norm_g[l].reshape(1, SSM_INNER),
        'w_sb_out': w_sb_out[l].astype(BF16), 'w_ssm_out': w_ssm_out[l].astype(BF16),
        'w_fox_out': w_fox_out[l].astype(BF16), 'w_o': w_o[l].astype(BF16),
        'norm2_g': norm2_g[l].reshape(1, D_MODEL),
        'w_up': w_up[l].astype(BF16), 'w_down': w_down[l].astype(BF16),
    }


def _to_heads(a, bsz, t):
    return a.reshape(bsz, t, N_HEADS, HEAD_DIM)


def _layer(xp, xs, l, depth, lw, gf, kv_prev, caches, state_ssm, state_conv, page_table):
    final = l == depth - 1
    cache_sb_k, cache_sb_v, cache_fox_k, cache_fox_v, cache_lf_t = caches
    bsz, t, d = xp.shape
    n = bsz * t
    (q_sb, q_fx, k_sb, v_sb, k_fx, v_fx, k_sb_h, v_sb_h, k_fx_h, v_fx_h,
     z, xbc, small, gates) = _inproj(xp.reshape(n, d), lw['norm1_g'], lw['w_main'], lw['w_small'],
                                     (bsz, t, l, depth, kv_prev))
    sn, st, _ = xs.shape
    m = sn * st
    (sq_sb, sq_fx, sk_sb, sv_sb, sk_fx, sv_fx, sk_sb_h, sv_sb_h, sk_fx_h, sv_fx_h,
     sz, sxbc, ssmall, sgates) = _inproj(xs.reshape(m, d), lw['norm1_g'], lw['w_main'], lw['w_small'])

    b3 = lambda a: a.reshape(bsz, t, a.shape[-1])
    s3 = lambda a: a.reshape(sn, st, a.shape[-1])
    q_rows = lambda q: jnp.tile(s3(q), (1, N_HEADS, 1))
    pad_keys = lambda a: jnp.pad(s3(a), ((0, 0), (0, PAGE_SIZE - st), (0, 0)))
    y_sb, sy_sb = _sb_attention(l, page_table, (b3(q_sb), b3(k_sb_h), b3(v_sb_h)),
                                (q_rows(sq_sb), pad_keys(sk_sb_h), pad_keys(sv_sb_h)),
                                (cache_sb_k, cache_sb_v), st)

    f_rows = small[:, :N_HEADS].reshape(bsz, t, N_HEADS).transpose(0, 2, 1)
    rows_per_seq = t // PAGE_SIZE
    f_rows = f_rows.reshape(bsz * N_HEADS * rows_per_seq, PAGE_SIZE)
    bias_rows = jnp.tile(jnp.repeat(lw['b_forget'], rows_per_seq), bsz)[:, None]
    logf, cum = _logf_cum(f_rows, bias_rows, rows_per_seq)
    logf = logf.reshape(bsz, N_HEADS, t).transpose(0, 2, 1)
    blk = ATT_BLOCK
    cum2 = cum * LOG2_E
    cum_col = cum2.reshape(bsz, N_HEADS, t).transpose(0, 2, 1)
    cum_row = cum2.reshape(bsz, N_HEADS, t // blk, blk).transpose(0, 2, 1, 3)
    f_new = ssmall[:, :N_HEADS].reshape(sn, st, N_HEADS).transpose(0, 2, 1)
    sf_rows = jnp.pad(jnp.repeat(f_new, st, axis=1), ((0, 0), (0, 0), (0, PAGE_SIZE - st)))
    b_rows = jnp.broadcast_to(jnp.repeat(lw['b_forget'], st)[:, None], (N_HEADS * st, PAGE_SIZE))
    y_fx, sy_fx, lf_rows = _fox_attention(
        l, page_table, (b3(q_fx), b3(k_fx_h), b3(v_fx_h), cum_col, cum_row),
        (q_rows(sq_fx), pad_keys(sk_fx_h), pad_keys(sv_fx_h), sf_rows, b_rows),
        (cache_fox_k, cache_fox_v, cache_lf_t), st)
    slogf = lf_rows[:, ::st, :st].transpose(0, 2, 1)

    conv0 = jnp.zeros((bsz, SSM_CONV - 1, SSM_CONV_DIM), F32)
    ssm0 = jnp.zeros((bsz, SSM_INNER, SSM_STATE), F32)
    ssm_w = (lw['conv_w'], lw['conv_b'], lw['dt_bias'], lw['a_log'], lw['d_skip'], lw['ssm_norm_g'])
    y_ssm, ssm_new, conv_new = _ssm(b3(z), b3(xbc), b3(small), conv0, ssm0, *ssm_w, SSM_CHUNK)
    pad_rows = lambda a: jnp.pad(s3(a), ((0, 0), (0, SSM_SHORT_ROWS - st), (0, 0)))
    sy_ssm, sssm_new, sconv_new = _ssm(pad_rows(sz), pad_rows(sxbc), pad_rows(ssmall), state_conv[l],
                                       state_ssm[l].reshape(sn, SSM_INNER, SSM_STATE), *ssm_w, st)
    sy_ssm = sy_ssm[:, :st]

    xp_new = _merge_ffn(xp.reshape(n, d), y_sb.reshape(n, -1), y_ssm.reshape(n, -1), y_fx.reshape(n, -1),
                        gates, lw, gf, final)
    xs_new = _merge_ffn(xs.reshape(m, d), sy_sb.reshape(m, -1), sy_ssm.reshape(m, -1),
                        sy_fx.reshape(m, -1), sgates, lw, gf, final)
    ssm_shape = (SSM_HEADS, SSM_INNER // SSM_HEADS, SSM_STATE)
    prompt_states = (logf, ssm_new.reshape(bsz, *ssm_shape), conv_new)
    sample_states = (_to_heads(sk_sb, sn, st), _to_heads(sv_sb, sn, st), _to_heads(sk_fx, sn, st),
                     _to_heads(sv_fx, sn, st), slogf, sssm_new.reshape(sn, *ssm_shape), sconv_new)
    return (xp_new.reshape(bsz, t, d), xs_new.reshape(sn, st, d), (k_sb, v_sb, k_fx, v_fx),
            prompt_states, sample_states)


def kernel(x_prompt, x_sample, cache_sb_k, cache_sb_v, cache_fox_k, cache_fox_v, cache_fox_logf, state_ssm, state_conv, page_table, norm1_g, w_in, b_forget, conv_w, conv_b, dt_bias, a_log, d_skip, ssm_norm_g, w_sb_out, w_ssm_out, w_fox_out, w_o, norm2_g, w_up, w_down, final_norm_g):
    depth = w_in.shape[0]
    assert page_table.shape[1] % PAGES_PER_STEP == 0
    assert x_sample.shape[1] >= SSM_CONV - 1 and x_prompt.shape[1] % ATT_BLOCK == 0
    flat = lambda c: c.transpose(0, 1, 3, 4, 2).reshape(c.shape[0], c.shape[1], ATT_WIDTH, PAGE_SIZE)
    caches = (flat(cache_sb_k), flat(cache_sb_v), flat(cache_fox_k), flat(cache_fox_v),
              cache_fox_logf.transpose(0, 1, 3, 2))
    gf = final_norm_g.reshape(1, D_MODEL)
    xp, xs = x_prompt, x_sample
    prompt_states, sample_states = [], []
    kv_stacks = None
    for l in range(depth):
        lw = _layer_weights(l, norm1_g, w_in, b_forget, conv_w, conv_b, dt_bias, a_log, d_skip,
                            ssm_norm_g, w_sb_out, w_ssm_out, w_fox_out, w_o, norm2_g, w_up, w_down)
        xp, xs, kv_stacks, st_p, st_s = _layer(xp, xs, l, depth, lw, gf, kv_stacks, caches,
                                               state_ssm, state_conv, page_table)
        prompt_states.append(st_p)
        sample_states.append(st_s)
    bsz, t = x_prompt.shape[:2]
    from_t = lambda a: a.reshape(depth, bsz, N_HEADS, HEAD_DIM, t).transpose(0, 1, 4, 2, 3)
    stacked_p = [from_t(a) for a in kv_stacks] + [jnp.stack(s) for s in zip(*prompt_states)]
    stacked_s = [jnp.stack(s) for s in zip(*sample_states)]
    return (xp, xs, *stacked_p, *stacked_s)
```

```python
import functools

import jax
import jax.numpy as jnp
from jax import lax
from jax.experimental import pallas as pl
from jax.experimental.pallas import tpu as pltpu

F32 = jnp.float32
BF16 = jnp.bfloat16

D_MODEL = 1024
HEAD_DIM = 64
N_HEADS = 4
ATT_WIDTH = N_HEADS * HEAD_DIM
SSM_HEADS = 8
SSM_INNER = 512
SSM_STATE = 128
SSM_GROUPS = 2
SSM_CONV = 4
SSM_CONV_DIM = 1024
SSM_CHUNK = 128
PAGE_SIZE = 128
N_BRANCH = 3
FFN_HIDDEN = 4 * D_MODEL
RMS_EPS = 1e-6
NEG_INF = -1e30
Q_SCALE = HEAD_DIM ** -0.5
LOG2_E = 1.4426950408889634

_OFF_F = 6 * ATT_WIDTH
_OFF_Z = _OFF_F + N_HEADS
_OFF_XBC = _OFF_Z + SSM_INNER
_OFF_DT = _OFF_XBC + SSM_CONV_DIM
_OFF_GATE = _OFF_DT + SSM_HEADS
MAIN_WIDTH = 6 * ATT_WIDTH + SSM_INNER + SSM_CONV_DIM + N_BRANCH * D_MODEL
SMALL_WIDTH = 128
DT_LANE = N_HEADS

V7X_VMEM_LIMIT_BYTES = 58 * 1024 * 1024
PAGES_PER_STEP = 16
ATT_BLOCK = 256
PROMPT_SEQS_PER_STEP = 2
SB_SPLIT_TERMS = 1
CUMSUM_SEG = 256


def _cparams(*sem):
    return pltpu.CompilerParams(dimension_semantics=sem, vmem_limit_bytes=V7X_VMEM_LIMIT_BYTES)


def _const_spec(shape):
    n = len(shape)
    return pl.BlockSpec(shape, lambda *_: (0,) * n, pipeline_mode=pl.Buffered(1))


def _rms(x, g):
    ms = jnp.mean(x * x, axis=-1, keepdims=True)
    return x * lax.rsqrt(ms + RMS_EPS) * g


def _softplus_tail(z):
    return jnp.log1p(jnp.exp(-jnp.abs(z)))


def _log_sigmoid(z):
    return jnp.minimum(z, 0.0) - _softplus_tail(z)


def _softplus(z):
    return jnp.maximum(z, 0.0) + _softplus_tail(z)


def _dot_nt(a, b):
    return lax.dot_general(a, b, (((1,), (1,)), ((), ())), preferred_element_type=F32)


def _dot(a, b):
    return jnp.dot(a, b, preferred_element_type=F32)


def _dot_exact(a, b):
    return jnp.dot(a, b, preferred_element_type=F32, precision=lax.Precision.HIGHEST)


def _store_kv(ref, a, transposed):
    if not transposed:
        ref[...] = a
    elif len(ref.shape) == 2:
        ref[...] = a.T
    else:
        ref[0] = a.T
        ref[1:] = jnp.zeros((ref.shape[0] - 1,) + ref.shape[1:], ref.dtype)


def _inproj_kernel(x_ref, g_ref, wm_ref, ws_ref, *rest, kv_transposed, n_aliased):
    (qsb_ref, qfx_ref, ksb_ref, vsb_ref, kfx_ref, vfx_ref, ksbh_ref, vsbh_ref, kfxh_ref, vfxh_ref,
     z_ref, xbc_ref, small_ref, gate_ref) = rest[n_aliased:]
    h = _rms(x_ref[...], g_ref[...]).astype(BF16)

    def mm(c0, width):
        return _dot_nt(h, wm_ref[c0:c0 + width, :])

    w = ATT_WIDTH
    qsb_ref[...] = (mm(0, w) * (Q_SCALE * LOG2_E)).astype(BF16)
    for i, (full_ref, half_ref) in enumerate(((ksb_ref, ksbh_ref), (vsb_ref, vsbh_ref))):
        a = mm((1 + i) * w, w)
        _store_kv(full_ref, a, kv_transposed)
        half_ref[...] = a.astype(BF16)
    qfx_ref[...] = (mm(3 * w, w) * (Q_SCALE * LOG2_E)).astype(BF16)
    for i, (full_ref, half_ref) in enumerate(((kfx_ref, kfxh_ref), (vfx_ref, vfxh_ref))):
        a = mm((4 + i) * w, w)
        _store_kv(full_ref, a, kv_transposed)
        half_ref[...] = a.astype(BF16)
    z_ref[...] = mm(6 * w, SSM_INNER)
    c0 = 6 * w + SSM_INNER
    for c in range(SSM_CONV_DIM // 512):
        xbc_ref[:, c * 512:(c + 1) * 512] = mm(c0 + c * 512, 512)
    c0 += SSM_CONV_DIM
    for c in range(N_BRANCH * D_MODEL // 512):
        gate_ref[:, c * 512:(c + 1) * 512] = jax.nn.sigmoid(mm(c0 + c * 512, 512)).astype(BF16)
    small_ref[...] = _dot_nt(h, ws_ref[...])


_HBM_SPEC = pl.BlockSpec(memory_space=pl.ANY)


def _inproj(x2d, g, w_main, w_small, kv_stack=None):
    n = x2d.shape[0]
    tm = min(512, n)
    row = lambda width: pl.BlockSpec((tm, width), lambda i: (i, 0))
    rows = lambda width, dt: (row(width), jax.ShapeDtypeStruct((n, width), dt))
    prev = ()
    if kv_stack is None:
        kv = rows(ATT_WIDTH, F32)
    else:
        bsz, t, layer, depth, prev = kv_stack
        prev = () if prev is None else tuple(prev)
        nt = t // tm
        if prev or depth == 1:
            block = pl.BlockSpec((None, None, ATT_WIDTH, tm), lambda i: (layer, i // nt, 0, i % nt))
        else:
            block = pl.BlockSpec((depth, None, ATT_WIDTH, tm), lambda i: (0, i // nt, 0, i % nt))
        kv = (block, jax.ShapeDtypeStruct((depth, bsz, ATT_WIDTH, t), F32))
    outs = ([rows(ATT_WIDTH, BF16)] * 2 + [kv] * 4 + [rows(ATT_WIDTH, BF16)] * 4
            + [rows(SSM_INNER, F32), rows(SSM_CONV_DIM, F32), rows(SMALL_WIDTH, F32),
               rows(N_BRANCH * D_MODEL, BF16)])
    n_in = 4
    return pl.pallas_call(
        functools.partial(_inproj_kernel, kv_transposed=kv_stack is not None, n_aliased=len(prev)),
        grid=(n // tm,),
        in_specs=[row(D_MODEL), _const_spec((1, D_MODEL)),
                  _const_spec((MAIN_WIDTH, D_MODEL)), _const_spec((SMALL_WIDTH, D_MODEL))]
                 + [_HBM_SPEC] * len(prev),
        out_specs=[spec for spec, _ in outs],
        out_shape=[shape for _, shape in outs],
        input_output_aliases={n_in + j: 2 + j for j in range(len(prev))},
        compiler_params=_cparams("parallel"),
        name="inproj",
    )(x2d, g, w_main, w_small, *prev)


def _lane_head(width=ATT_WIDTH):
    return lax.broadcasted_iota(jnp.int32, (1, width), 1) // HEAD_DIM


def _suffix_matrix(seg, terms):
    r = lax.broadcasted_iota(jnp.int32, (terms * seg, seg), 0) % seg
    c = lax.broadcasted_iota(jnp.int32, (terms * seg, seg), 1)
    return jnp.where(r >= c, 1.0, 0.0).astype(BF16)


def _split_bf16(x, terms):
    out = []
    for _ in range(terms - 1):
        head = x.astype(BF16)
        out.append(head)
        x = x - head.astype(F32)
    out.append(x.astype(BF16))
    return jnp.concatenate(out, axis=1)


def _suffix_sums(x, u, carry):
    m, n = x.shape
    seg = u.shape[1]
    n_seg = n // seg
    terms = u.shape[0] // seg
    if n_seg == 1:
        cs = _dot(_split_bf16(x, terms), u)
        return cs + carry, carry + cs[:, 0:1]
    stacked = jnp.concatenate([x[:, s * seg:(s + 1) * seg] for s in range(n_seg)], axis=0)
    cs = _dot(_split_bf16(stacked, terms), u)
    parts = [None] * n_seg
    for s in reversed(range(n_seg)):
        part = cs[s * m:(s + 1) * m, :]
        parts[s] = part + carry
        carry = carry + part[:, 0:1]
    return jnp.concatenate(parts, axis=1), carry


def _stick_block(z2, carry, u, mask):
    drop = jnp.maximum(z2, 0.0) + jnp.log2(1.0 + jnp.exp2(-jnp.abs(z2)))
    if mask is not None:
        drop = jnp.where(mask, drop, 0.0)
    later, carry = _suffix_sums(drop, u, carry)
    w = jnp.exp2(z2 - later)
    if mask is not None:
        w = jnp.where(mask, w, 0.0)
    return w, carry


def _stack_heads(q):
    lane_head = _lane_head()
    return jnp.concatenate([jnp.where(lane_head == h, q, jnp.zeros_like(q)) for h in range(N_HEADS)], axis=0)


def _fold_heads(acc, rows):
    lane_head = _lane_head()
    out = jnp.zeros((rows, ATT_WIDTH), F32)
    for h in range(N_HEADS):
        out = jnp.where(lane_head == h, acc[h * rows:(h + 1) * rows, :], out)
    return out


def _sb_prompt_kernel(q_ref, k_ref, v_ref, o_ref, acc_ref, *, bq, bk):
    assert bq == bk
    i = pl.program_id(1)
    n_seq = q_ref.shape[0]
    rows = N_HEADS * bq
    qs = [_stack_heads(q_ref[s]) for s in range(n_seq)]
    u = _suffix_matrix(CUMSUM_SEG, SB_SPLIT_TERMS)
    qpos = lax.broadcasted_iota(jnp.int32, (rows, bk), 0) % bq
    causal = lax.broadcasted_iota(jnp.int32, (rows, bk), 1) < qpos

    def block(start, width, carries, mask):
        start = pl.multiple_of(start, width)
        out = []
        for s in range(n_seq):
            w, carry = _stick_block(_dot_nt(qs[s], k_ref[s, pl.ds(start, width), :]), carries[s], u, mask)
            pv = _dot(w.astype(BF16), v_ref[s, pl.ds(start, width), :])
            acc_ref[s] = pv if mask is not None else acc_ref[s] + pv
            out.append(carry)
        return tuple(out)

    carries = block(i * bk, bk, (jnp.zeros((rows, 1), F32),) * n_seq, causal)
    carries = lax.fori_loop(0, i % 2, lambda _, c: block((i - 1) * bk, bk, c, None), carries)
    pairs = i // 2
    lax.fori_loop(0, pairs, lambda jj, c: block((pairs - 1 - jj) * 2 * bk, 2 * bk, c, None), carries)
    for s in range(n_seq):
        o_ref[s] = _fold_heads(acc_ref[s], bq).astype(o_ref.dtype)


def _logf_cum_kernel(b_ref, f_ref, logf_ref, cum_ref, *, rows_per_seq):
    logf = _log_sigmoid(f_ref[...] + b_ref[...])
    logf_ref[...] = logf
    n = logf.shape[0]
    r = lax.broadcasted_iota(jnp.int32, (PAGE_SIZE, PAGE_SIZE), 0)
    c = lax.broadcasted_iota(jnp.int32, (PAGE_SIZE, PAGE_SIZE), 1)
    within = _dot_exact(logf, jnp.where(r <= c, 1.0, 0.0).astype(F32))
    totals = jnp.broadcast_to(within[:, PAGE_SIZE - 1:PAGE_SIZE], within.shape)
    rr = lax.broadcasted_iota(jnp.int32, (n, n), 0)
    cc = lax.broadcasted_iota(jnp.int32, (n, n), 1)
    earlier_rows = jnp.where(cc // rows_per_seq == rr // rows_per_seq, jnp.where(cc < rr, 1.0, 0.0), 0.0)
    cum_ref[...] = within + _dot_exact(earlier_rows, totals)


def _logf_cum(f_rows, b_rows, rows_per_seq):
    spec = pl.BlockSpec(f_rows.shape, lambda i: (0, 0))
    return pl.pallas_call(
        functools.partial(_logf_cum_kernel, rows_per_seq=rows_per_seq),
        grid=(1,),
        in_specs=[pl.BlockSpec(b_rows.shape, lambda i: (0, 0)), spec],
        out_specs=[spec, spec],
        out_shape=[jax.ShapeDtypeStruct(f_rows.shape, F32)] * 2,
        compiler_params=_cparams("arbitrary"),
        name="logf_cum",
    )(b_rows, f_rows)


def _fox_prompt_kernel(q_ref, k_ref, v_ref, cq_ref, ck_ref, o_ref, acc_ref, *, bq, bk):
    assert bq == bk
    blk = bq
    i = pl.program_id(1)
    n_seq = q_ref.shape[0]
    rows = N_HEADS * blk
    qs = [_stack_heads(q_ref[s]) for s in range(n_seq)]
    cqs = [jnp.concatenate([cq_ref[s, :, h:h + 1] for h in range(N_HEADS)], axis=0) for s in range(n_seq)]
    qpos = lax.broadcasted_iota(jnp.int32, (rows, blk), 0) % blk
    causal = lax.broadcasted_iota(jnp.int32, (rows, blk), 1) <= qpos

    def block(j, states, mask):
        start = pl.multiple_of(j * blk, blk)
        out = []
        for s in range(n_seq):
            sc = _dot_nt(qs[s], k_ref[s, pl.ds(start, blk), :]) + cqs[s]
            ck = ck_ref[s, j]
            sc = jnp.concatenate([sc[h * blk:(h + 1) * blk, :] - ck[h:h + 1, :] for h in range(N_HEADS)], axis=0)
            if mask is not None:
                sc = jnp.where(mask, sc, NEG_INF)
            m = jnp.max(sc, axis=-1, keepdims=True)
            if states is not None:
                m_old, l_old = states[s]
                m = jnp.maximum(m_old, m)
            p = jnp.exp2(sc - m)
            pv = _dot(p.astype(BF16), v_ref[s, pl.ds(start, blk), :])
            l = jnp.sum(p, axis=-1, keepdims=True)
            if states is None:
                acc_ref[s] = pv
            else:
                alpha = jnp.exp2(m_old - m)
                acc_ref[s] = acc_ref[s] * alpha + pv
                l = alpha * l_old + l
            out.append((m, l))
        return tuple(out)

    states = block(i, None, causal)
    states = lax.fori_loop(0, i, lambda jj, st: block(i - 1 - jj, st, None), states)
    for s in range(n_seq):
        o_ref[s] = _fold_heads(acc_ref[s] / states[s][1], blk).astype(o_ref.dtype)


_XBUF_ROW0 = 8


def _ssm_kernel(z_ref, xbc_ref, small_ref, conv0_ref, state0_ref,
                cw_ref, cb_ref, dtb_ref, alog_ref, dskip_ref, ng_ref, *rest, valid, n_aliased):
    y_ref, state_out_ref, conv_out_ref, xbuf_ref, state_ref = rest[n_aliased:]
    c = pl.program_id(1)
    n_seq = xbc_ref.shape[0]
    r0 = _XBUF_ROW0

    @pl.when(c == 0)
    def _():
        xbuf_ref[...] = jnp.zeros_like(xbuf_ref)
        xbuf_ref[:, r0 - 3:r0, :] = conv0_ref[...]
        state_ref[...] = state0_ref[...]

    x_curs = [_ssm_chunk(z_ref.at[s], xbc_ref.at[s], small_ref.at[s], cw_ref, cb_ref, dtb_ref, alog_ref,
                         dskip_ref, ng_ref, y_ref.at[s], xbuf_ref.at[s], state_ref.at[s], valid)
              for s in range(n_seq)]

    @pl.when(c == pl.num_programs(1) - 1)
    def _():
        if len(state_out_ref.shape) == len(state_ref.shape):
            state_out_ref[...] = state_ref[...]
        else:
            state_out_ref[0] = state_ref[...]
            state_out_ref[1:] = jnp.zeros((state_out_ref.shape[0] - 1,) + state_ref.shape, state_ref.dtype)
        for s in range(n_seq):
            conv_out_ref[s] = x_curs[s][valid - 3:valid, :]


def _ssm_chunk(z_ref, xbc_ref, small_ref, cw_ref, cb_ref, dtb_ref, alog_ref, dskip_ref, ng_ref,
               y_ref, xbuf_ref, state_ref, valid):
    L = xbc_ref.shape[0]
    S = SSM_CHUNK
    r0 = _XBUF_ROW0

    def pad_time(a):
        return a if L == S else jnp.concatenate([a, jnp.zeros((S - L,) + a.shape[1:], a.dtype)], axis=0)

    x_cur = xbc_ref[...]
    window = jnp.concatenate([xbuf_ref[...], x_cur], axis=0)
    cw = cw_ref[...]
    conv = cb_ref[...] + x_cur * cw[3:4, :]
    for i in range(SSM_CONV - 1):
        conv = conv + pltpu.roll(window, 3 - i, 0)[r0:, :] * cw[i:i + 1, :]
    xbuf_ref[...] = x_cur[L - r0:L, :]
    act = conv * jax.nn.sigmoid(conv)
    xs = act[:, :SSM_INNER]
    b_in = act[:, SSM_INNER:SSM_INNER + SSM_GROUPS * SSM_STATE].astype(BF16)
    c_in = act[:, SSM_INNER + SSM_GROUPS * SSM_STATE:].astype(BF16)

    row = lax.broadcasted_iota(jnp.int32, (L, S), 0)
    col = lax.broadcasted_iota(jnp.int32, (L, S), 1)
    tri = row >= col
    dt = _softplus(small_ref[...] + dtb_ref[...])
    if valid < L:
        dt = jnp.where(lax.broadcasted_iota(jnp.int32, dt.shape, 0) < valid, dt, 0.0)
    d_a = dt * (-jnp.exp(alog_ref[...]))
    a_cs = _dot_exact(jnp.where(tri[:, :L], 1.0, 0.0).astype(F32), d_a)
    a_cs_t = pad_time(a_cs).T
    a_last = a_cs[L - 1:L, :]
    e_cs = jnp.exp(a_cs)
    wgt = jnp.exp(a_last - a_cs) * dt
    chunk_dec = jnp.exp(a_last)

    half = lax.broadcasted_iota(jnp.int32, (1, 128), 1) // SSM_STATE_HALF
    rhalf = lax.broadcasted_iota(jnp.int32, (128, 1), 0) // SSM_STATE_HALF
    pair_cols = lambda a, p: jnp.where(half == 0, a[:, DT_LANE + 2 * p:DT_LANE + 2 * p + 1],
                                       a[:, DT_LANE + 2 * p + 1:DT_LANE + 2 * p + 2])
    b_keys = pad_time(b_in)
    scores = [_dot_nt(c_in[:, g * SSM_STATE:(g + 1) * SSM_STATE],
                      b_keys[:, g * SSM_STATE:(g + 1) * SSM_STATE]) for g in range(SSM_GROUPS)]
    ys = []
    for p in range(SSM_HEADS // 2):
        g = (2 * p) // (SSM_HEADS // SSM_GROUPS)
        bg = b_keys[:, g * SSM_STATE:(g + 1) * SSM_STATE]
        cg = c_in[:, g * SSM_STATE:(g + 1) * SSM_STATE]
        xs_p = xs[:, 128 * p:128 * (p + 1)]
        xdt = pad_time((xs_p * pair_cols(dt, p)).astype(BF16))
        y_diag = jnp.zeros((L, 128), F32)
        for hh in range(2):
            lane = DT_LANE + 2 * p + hh
            seg = a_cs[:, lane:lane + 1] - a_cs_t[lane:lane + 1, :]
            decay = jnp.exp(jnp.where(tri, seg, NEG_INF))
            y_h = _dot((scores[g] * decay).astype(BF16), xdt)
            y_diag = jnp.where(half == hh, y_h, y_diag)
        st = state_ref[128 * p:128 * (p + 1), :]
        y_off = _dot_nt(cg, st.astype(BF16)) * pair_cols(e_cs, p)
        ys.append(y_diag + y_off + dskip_ref[:, 128 * p:128 * (p + 1)] * xs_p)
        xw_t = pad_time(xs_p * pair_cols(wgt, p)).T.astype(BF16)
        lane = DT_LANE + 2 * p
        dec = jnp.where(rhalf == 0,
                        jnp.broadcast_to(chunk_dec[:, lane:lane + 1], (128, SSM_STATE)),
                        jnp.broadcast_to(chunk_dec[:, lane + 1:lane + 2], (128, SSM_STATE)))
        state_ref[128 * p:128 * (p + 1), :] = st * dec + _dot(xw_t, bg)

    z = z_ref[...]
    y = jnp.concatenate(ys, axis=1) * (z * jax.nn.sigmoid(z))
    gw = SSM_INNER // SSM_GROUPS
    parts = []
    for g in range(SSM_GROUPS):
        yg = y[:, g * gw:(g + 1) * gw]
        parts.append(yg * lax.rsqrt(jnp.mean(yg * yg, axis=-1, keepdims=True) + RMS_EPS))
    y_ref[...] = (jnp.concatenate(parts, axis=1) * ng_ref[...]).astype(y_ref.dtype)
    return x_cur


SSM_STATE_HALF = 64
SSM_SHORT_ROWS = 16
SSM_SEQS_PER_STEP = 4


def _ssm(z, xbc, small, conv0, state0, cw, cb, dtb, alog, dskip, ng, valid, layer, depth, prev):
    b, t, _ = z.shape
    rows = min(t, SSM_CHUNK)
    nc = t // rows
    par = SSM_SEQS_PER_STEP if b % SSM_SEQS_PER_STEP == 0 else 1
    chunk = lambda width: pl.BlockSpec((par, rows, width), lambda bi, c: (bi, c, 0))
    per_b = lambda shape: pl.BlockSpec((par,) + shape, lambda bi, c: (bi,) + (0,) * len(shape))
    const = lambda shape: pl.BlockSpec(shape, lambda bi, c: (0,) * len(shape))
    state_rows = SSM_INNER
    prev = () if prev is None else (prev,)
    if prev or depth == 1:
        state_out = pl.BlockSpec((None, par, state_rows, SSM_STATE), lambda bi, c: (layer, bi, 0, 0))
    else:
        state_out = pl.BlockSpec((depth, par, state_rows, SSM_STATE), lambda bi, c: (0, bi, 0, 0))
    n_in = 11
    return pl.pallas_call(
        functools.partial(_ssm_kernel, valid=valid, n_aliased=len(prev)),
        grid=(b // par, nc),
        in_specs=[chunk(SSM_INNER), chunk(SSM_CONV_DIM), chunk(SMALL_WIDTH),
                  per_b((SSM_CONV - 1, SSM_CONV_DIM)), per_b((state_rows, SSM_STATE)),
                  const((SSM_CONV, SSM_CONV_DIM)), const((1, SSM_CONV_DIM)),
                  const((1, SMALL_WIDTH)), const((1, SMALL_WIDTH)),
                  const((1, SSM_INNER)), const((1, SSM_INNER))] + [_HBM_SPEC] * len(prev),
        out_specs=[chunk(SSM_INNER), state_out, per_b((SSM_CONV - 1, SSM_CONV_DIM))],
        out_shape=[jax.ShapeDtypeStruct((b, t, SSM_INNER), BF16),
                   jax.ShapeDtypeStruct((depth, b, state_rows, SSM_STATE), F32),
                   jax.ShapeDtypeStruct((b, SSM_CONV - 1, SSM_CONV_DIM), F32)],
        input_output_aliases={n_in + j: 1 + j for j in range(len(prev))},
        scratch_shapes=[pltpu.VMEM((par, _XBUF_ROW0, SSM_CONV_DIM), F32),
                        pltpu.VMEM((par, state_rows, SSM_STATE), F32)],
        compiler_params=_cparams("parallel", "arbitrary"),
        name="ssm",
    )(z, xbc, small, conv0, state0, cw, cb, dtb, alog, dskip, ng, *prev)


def _merge_ffn_kernel(x_ref, ysb_ref, yssm_ref, yfx_ref, gate_ref,
                      wsb_ref, wssm_ref, wfx_ref, wo_ref, g2_ref, wup_ref, wdn_ref, gf_ref,
                      o_ref, *, final):
    d = D_MODEL
    mixed = gate_ref[:, 0:d].astype(F32) * _dot(ysb_ref[...].astype(BF16), wsb_ref[...])
    mixed = mixed + gate_ref[:, d:2 * d].astype(F32) * _dot(yssm_ref[...].astype(BF16), wssm_ref[...])
    mixed = mixed + gate_ref[:, 2 * d:3 * d].astype(F32) * _dot(yfx_ref[...].astype(BF16), wfx_ref[...])
    x = x_ref[...] + _dot(mixed.astype(BF16), wo_ref[...])
    h = _rms(x, g2_ref[...]).astype(BF16)
    hc = FFN_HIDDEN // 2
    for c in range(2):
        u = jnp.maximum(_dot(h, wup_ref[:, c * hc:(c + 1) * hc]), 0.0)
        x = x + _dot((u * u).astype(BF16), wdn_ref[c * hc:(c + 1) * hc, :])
    if final:
        x = _rms(x, gf_ref[...])
    o_ref[...] = x


def _merge_ffn(x2d, y_sb, y_ssm, y_fx, gates, lw, gf, final):
    n = x2d.shape[0]
    tm = min(512, n)
    row = lambda width: pl.BlockSpec((tm, width), lambda i: (i, 0))
    weights = [lw['w_sb_out'], lw['w_ssm_out'], lw['w_fox_out'], lw['w_o'], lw['norm2_g'],
               lw['w_up'], lw['w_down'], gf]
    return pl.pallas_call(
        functools.partial(_merge_ffn_kernel, final=final),
        grid=(n // tm,),
        in_specs=[row(D_MODEL), row(ATT_WIDTH), row(SSM_INNER), row(ATT_WIDTH),
                  row(N_BRANCH * D_MODEL)] + [_const_spec(w.shape) for w in weights],
        out_specs=row(D_MODEL),
        out_shape=jax.ShapeDtypeStruct((n, D_MODEL), F32),
        compiler_params=_cparams("parallel"),
        name="merge_ffn",
    )(x2d, y_sb, y_ssm, y_fx, gates, *weights)


def _block_diag_q(q_ref):
    q = q_ref[...]
    rows = q.shape[0]
    row_head = lax.broadcasted_iota(jnp.int32, (rows, 1), 0) // (rows // N_HEADS)
    return jnp.where(row_head == _lane_head(), q, jnp.zeros_like(q))


def _page_copy(cache_ref, stage_ref, sem_ref, layer, page, slot, p):
    return pltpu.make_async_copy(cache_ref.at[layer, page], stage_ref.at[slot, p], sem_ref.at[slot])


def _grid_step():
    return (pl.program_id(0) * pl.num_programs(1) + pl.program_id(1),
            pl.num_programs(0) * pl.num_programs(1))


def _prefetch_pages(pt_ref, caches, stages, sems, layer, group, n_chunks):
    n_pages = stages[0].shape[1]
    per_seq = n_pages // group
    step, n_steps = _grid_step()
    b, c = step // n_chunks, step % n_chunks
    slot = step % 2

    def start(bb, cc, sl):
        base = (n_chunks - 1 - cc) * per_seq
        for p in range(n_pages):
            page = pt_ref[bb * group + p // per_seq, base + p % per_seq]
            for i, (cache, stage, sem) in enumerate(zip(caches, stages, sems)):
                _page_copy(cache, stage, sem, layer, page, sl, p).start(priority=i % 2)

    @pl.when(step == 0)
    def _():
        start(b, c, slot)

    @pl.when(step + 1 < n_steps)
    def _():
        wrap = c + 1 == n_chunks
        start(jnp.where(wrap, b + 1, b), jnp.where(wrap, 0, c + 1), 1 - slot)

    for p in range(n_pages):
        for cache, stage, sem in zip(caches, stages, sems):
            _page_copy(cache, stage, sem, layer, 0, slot, p).wait()
    return slot


def _pages_to_bf16(stage_ref, slot, g, buf_ref):
    per_seq = buf_ref.shape[-1] // PAGE_SIZE
    for p in range(per_seq):
        buf_ref[g, :, p * PAGE_SIZE:(p + 1) * PAGE_SIZE] = stage_ref[slot, g * per_seq + p].astype(BF16)


def _sb_sample_kernel(pt_ref, q_ref, kn_ref, vn_ref, ck_ref, cv_ref, o_ref,
                      acc_ref, carry_ref, kstage_ref, vstage_ref, ksem, vsem, kbuf_ref, vbuf_ref,
                      *, steps, layer, n_chunks):
    group = q_ref.shape[0]
    slot = _prefetch_pages(pt_ref, (ck_ref, cv_ref), (kstage_ref, vstage_ref), (ksem, vsem), layer, group,
                           n_chunks)
    c = _grid_step()[0] % n_chunks
    rows = N_HEADS * steps
    qbd = [_block_diag_q(q_ref.at[g]) for g in range(group)]
    u = _suffix_matrix(CUMSUM_SEG, SB_SPLIT_TERMS)

    @pl.when(c == 0)
    def _():
        step = lax.broadcasted_iota(jnp.int32, (rows, PAGE_SIZE), 0) % steps
        col = lax.broadcasted_iota(jnp.int32, (rows, PAGE_SIZE), 1)
        for g in range(group):
            w, carry = _stick_block(_dot_nt(qbd[g], kn_ref[g]), jnp.zeros((rows, 1), F32),
                                    _suffix_matrix(PAGE_SIZE, SB_SPLIT_TERMS), col < step)
            acc_ref[g] = _dot(w.astype(BF16), vn_ref[g])
            carry_ref[g] = jnp.broadcast_to(carry, carry_ref.shape[1:])

    for g in range(group):
        _pages_to_bf16(kstage_ref, slot, g, kbuf_ref)
        _pages_to_bf16(vstage_ref, slot, g, vbuf_ref)
        w, carry = _stick_block(_dot(qbd[g], kbuf_ref[g]), carry_ref[g, :, 0:1], u, None)
        acc_ref[g] += _dot_nt(w.astype(BF16), vbuf_ref[g])
        carry_ref[g] = jnp.broadcast_to(carry, carry_ref.shape[1:])

    def finish():
        @pl.when(c == n_chunks - 1)
        def _():
            for g in range(group):
                o_ref[g] = _fold_heads(acc_ref[g], steps)
    return finish


def _page_staging(n_pages, rows, lanes):
    return [pltpu.VMEM((2, n_pages, rows, lanes), F32)], [pltpu.SemaphoreType.DMA((2,))]


def _fox_sample_kernel(pt_ref, q_ref, kn_ref, vn_ref, f_ref, b_ref, ck_ref, cv_ref, clf_ref,
                       o_ref, lf_out_ref, acc_ref, m_ref, l_ref, ncum_ref, rcarry_ref,
                       kstage_ref, vstage_ref, lstage_ref, ksem, vsem, lsem,
                       kbuf_ref, vbuf_ref, lbuf_ref, *, steps, layer, n_chunks):
    group = q_ref.shape[0]
    slot = _prefetch_pages(pt_ref, (ck_ref, cv_ref, clf_ref), (kstage_ref, vstage_ref, lstage_ref),
                           (ksem, vsem, lsem), layer, group, n_chunks)
    c = _grid_step()[0] % n_chunks
    rows = N_HEADS * steps
    qbd = [_block_diag_q(q_ref.at[g]) for g in range(group)]
    stat = lambda a: jnp.broadcast_to(a, m_ref.shape[1:])

    @pl.when(c == 0)
    def _():
        step = lax.broadcasted_iota(jnp.int32, (rows, PAGE_SIZE), 0) % steps
        col = lax.broadcasted_iota(jnp.int32, (rows, PAGE_SIZE), 1)
        visible = col <= step
        lbuf_ref[...] = jnp.zeros_like(lbuf_ref)
        rcarry_ref[...] = jnp.zeros_like(rcarry_ref)
        for g in range(group):
            lf = _log_sigmoid(f_ref[g] + b_ref[...])
            lf_out_ref[g] = lf
            lf = jnp.where(col < steps, lf, 0.0)
            cum = lf
            shift = 1
            while shift < steps:
                cum = cum + jnp.where(col >= shift, pltpu.roll(cum, shift, 1), 0.0)
                shift *= 2
            ncum = jnp.sum(jnp.where(visible, lf, 0.0), axis=-1, keepdims=True)
            s = jnp.where(visible, _dot_nt(qbd[g], kn_ref[g]) + (ncum - cum) * LOG2_E, NEG_INF)
            m = jnp.max(s, axis=-1, keepdims=True)
            p = jnp.exp2(s - m)
            m_ref[g] = stat(m)
            l_ref[g] = stat(jnp.sum(p, axis=-1, keepdims=True))
            acc_ref[g] = _dot(p.astype(BF16), vn_ref[g])
            ncum_ref[g] = stat(ncum)

    per_seq = kbuf_ref.shape[-1] // PAGE_SIZE
    u3 = _suffix_matrix(CUMSUM_SEG, 3)
    for g in range(group):
        _pages_to_bf16(kstage_ref, slot, g, kbuf_ref)
        _pages_to_bf16(vstage_ref, slot, g, vbuf_ref)
        for p in range(per_seq):
            lbuf_ref[g, 0:N_HEADS, p * PAGE_SIZE:(p + 1) * PAGE_SIZE] = lstage_ref[slot, g * per_seq + p]
        lf_pages = lbuf_ref[g]
        incl, carry = _suffix_sums(lf_pages, u3, rcarry_ref[g, :, 0:1])
        suffix = incl - lf_pages
        rcarry_ref[g] = jnp.broadcast_to(carry, rcarry_ref.shape[1:])
        bias = jnp.concatenate(
            [jnp.broadcast_to(suffix[h:h + 1, :], (steps, suffix.shape[1])) for h in range(N_HEADS)], axis=0)
        s = _dot(qbd[g], kbuf_ref[g]) + (bias + ncum_ref[g, :, 0:1]) * LOG2_E
        m_old = m_ref[g, :, 0:1]
        m = jnp.maximum(m_old, jnp.max(s, axis=-1, keepdims=True))
        alpha = jnp.exp2(m_old - m)
        p = jnp.exp2(s - m)
        l_ref[g] = stat(alpha * l_ref[g, :, 0:1] + jnp.sum(p, axis=-1, keepdims=True))
        acc_ref[g] = acc_ref[g] * alpha + _dot_nt(p.astype(BF16), vbuf_ref[g])
        m_ref[g] = stat(m)

    def finish():
        @pl.when(c == n_chunks - 1)
        def _():
            for g in range(group):
                o_ref[g] = _fold_heads(acc_ref[g] / l_ref[g, :, 0:1], steps)
    return finish


def _both_groups_kernel(pt_ref, *refs, prompt_kernel, sample_kernel, counts):
    parts, at = [], 0
    for n in counts:
        parts.append(refs[at:at + n])
        at += n
    p_in, s_in, p_out, s_out, p_scr, s_scr = parts
    finish_sample = sample_kernel(pt_ref, *s_in, *s_out, *s_scr)
    prompt_kernel(*p_in, *p_out, *p_scr)
    finish_sample()


def _attention_both_groups(name, layer, page_table, prompt_kernel, prompt_in, prompt_extra_specs,
                           sample_kernel, sample_in, sample_extra_specs, caches, sample_extra_out,
                           sample_extra_scratch, steps):
    q, k, v = prompt_in[:3]
    b, t, w = q.shape
    blk = ATT_BLOCK
    nq = t // blk
    pb = PROMPT_SEQS_PER_STEP if b % PROMPT_SEQS_PER_STEP == 0 else 1
    q_rows = sample_in[0]
    n_seq, rows, _ = q_rows.shape
    n_pages = PAGES_PER_STEP
    n_chunks = page_table.shape[1] // n_pages
    steps_total = (b // pb) * nq
    g = n_seq * n_chunks // steps_total
    assert g >= 1 and g * steps_total == n_seq * n_chunks, "the two groups must split into equally many grid steps"
    seq_group = lambda bi, i: (bi * nq + i) // n_chunks

    qspec = pl.BlockSpec((pb, blk, w), lambda bi, i, pt: (bi, i, 0))
    kvspec = pl.BlockSpec((pb, t, w), lambda bi, i, pt: (bi, 0, 0), pipeline_mode=pl.Buffered(1))
    per_g = lambda r, width=w: pl.BlockSpec((g, r, width), lambda bi, i, pt: (seq_group(bi, i), 0, 0))
    prompt_specs = [qspec, kvspec, kvspec] + prompt_extra_specs(pb, t, blk)
    sample_specs = ([per_g(rows), per_g(PAGE_SIZE), per_g(PAGE_SIZE)] + sample_extra_specs(per_g, rows)
                    + [_HBM_SPEC] * len(caches))
    stat = pltpu.VMEM((g, rows, 128), F32)
    stages, sems = [], []
    for cache in caches:
        st, se = _page_staging(g * n_pages, cache.shape[2], PAGE_SIZE)
        stages += st
        sems += se
    prompt_scratch = [pltpu.VMEM((pb, N_HEADS * blk, w), F32)]
    sample_scratch = ([pltpu.VMEM((g, rows, w), F32)] + sample_extra_scratch(g, stat) + stages + sems
                      + [pltpu.VMEM((g, w, n_pages * PAGE_SIZE), BF16)] * 2)
    if len(caches) == 3:
        sample_scratch.append(pltpu.VMEM((g, 8, n_pages * PAGE_SIZE), F32))
    out_specs = [qspec, per_g(steps)] + [per_g(rows, PAGE_SIZE)] * len(sample_extra_out)
    out_shape = ([jax.ShapeDtypeStruct((b, t, w), BF16), jax.ShapeDtypeStruct((n_seq, steps, w), F32)]
                 + list(sample_extra_out))
    counts = (len(prompt_specs), len(sample_specs), 1, len(out_specs) - 1,
              len(prompt_scratch), len(sample_scratch))
    grid_spec = pltpu.PrefetchScalarGridSpec(
        num_scalar_prefetch=1,
        grid=(b // pb, nq),
        in_specs=prompt_specs + sample_specs,
        out_specs=out_specs,
        scratch_shapes=prompt_scratch + sample_scratch,
    )
    return pl.pallas_call(
        functools.partial(_both_groups_kernel, counts=counts,
                          prompt_kernel=functools.partial(prompt_kernel, bq=blk, bk=blk),
                          sample_kernel=functools.partial(sample_kernel, steps=steps, layer=layer,
                                                          n_chunks=n_chunks)),
        grid_spec=grid_spec,
        out_shape=out_shape,
        compiler_params=_cparams("arbitrary", "arbitrary"),
        name=name,
    )(page_table, *prompt_in, *sample_in, *caches)


def _sb_attention(layer, page_table, prompt_in, sample_in, caches, steps):
    none = lambda *_: []
    return _attention_both_groups(
        "sb_attention", layer, page_table, _sb_prompt_kernel, prompt_in, none,
        _sb_sample_kernel, sample_in, none, caches, [], lambda g, stat: [stat], steps)


def _fox_attention(layer, page_table, prompt_in, sample_in, caches, steps):
    n_seq, rows, _ = sample_in[0].shape
    prompt_extra = lambda pb, t, blk: [
        pl.BlockSpec((pb, blk, N_HEADS), lambda bi, i, pt: (bi, i, 0)),
        pl.BlockSpec((pb, t // blk, N_HEADS, blk), lambda bi, i, pt: (bi, 0, 0, 0))]
    sample_extra = lambda per_g, rows: [per_g(rows, PAGE_SIZE),
                                        pl.BlockSpec((rows, PAGE_SIZE), lambda bi, i, pt: (0, 0))]
    return _attention_both_groups(
        "fox_attention", layer, page_table, _fox_prompt_kernel, prompt_in, prompt_extra,
        _fox_sample_kernel, sample_in, sample_extra, caches,
        [jax.ShapeDtypeStruct((n_seq, rows, PAGE_SIZE), F32)],
        lambda g, stat: [stat, stat, stat, pltpu.VMEM((g, 8, 128), F32)], steps)


def _layer_weights(l, norm1_g, w_in, b_forget, conv_w, conv_b, dt_bias, a_log, d_skip, ssm_norm_g,
                   w_sb_out, w_ssm_out, w_fox_out, w_o, norm2_g, w_up, w_down):
    w = jnp.transpose(w_in, (2, 0, 1))[:, l, :]
    w_main = jnp.concatenate([w[:_OFF_F], w[_OFF_Z:_OFF_DT], w[_OFF_GATE:]], axis=0).astype(BF16)
    w_small = jnp.concatenate([w[_OFF_F:_OFF_Z], w[_OFF_DT:_OFF_GATE]], axis=0)
    w_small = jnp.pad(w_small, ((0, SMALL_WIDTH - w_small.shape[0]), (0, 0))).astype(BF16)
    pad_dt = lambda a: jnp.pad(a, (DT_LANE, SMALL_WIDTH - DT_LANE - SSM_HEADS)).reshape(1, SMALL_WIDTH)
    return {
        'norm1_g': norm1_g[l].reshape(1, D_MODEL), 'w_main': w_main, 'w_small': w_small,
        'b_forget': b_forget[l], 'conv_w': conv_w[l], 'conv_b': conv_b[l].reshape(1, SSM_CONV_DIM),
        'dt_bias': pad_dt(dt_bias[l]), 'a_log': pad_dt(a_log[l]),
        'd_skip': jnp.repeat(d_skip[l], SSM_INNER // SSM_HEADS).reshape(1, SSM_INNER),
        'ssm_norm_g': ssm_norm_g[l].reshape(1, SSM_INNER),
        'w_sb_out': w_sb_out[l].astype(BF16), 'w_ssm_out': w_ssm_out[l].astype(BF16),
        'w_fox_out': w_fox_out[l].astype(BF16), 'w_o': w_o[l].astype(BF16),
        'norm2_g': norm2_g[l].reshape(1, D_MODEL),
        'w_up': w_up[l].astype(BF16), 'w_down': w_down[l].astype(BF16),
    }


def _to_heads(a, bsz, t):
    return a.reshape(bsz, t, N_HEADS, HEAD_DIM)


def _layer(xp, xs, l, depth, lw, gf, stacks, caches, state_ssm, state_conv, page_table):
    kv_prev, ssm_prev, sssm_prev = (None, None, None) if stacks is None else stacks
    final = l == depth - 1
    cache_sb_k, cache_sb_v, cache_fox_k, cache_fox_v, cache_lf_t = caches
    bsz, t, d = xp.shape
    n = bsz * t
    (q_sb, q_fx, k_sb, v_sb, k_fx, v_fx, k_sb_h, v_sb_h, k_fx_h, v_fx_h,
     z, xbc, small, gates) = _inproj(xp.reshape(n, d), lw['norm1_g'], lw['w_main'], lw['w_small'],
                                     (bsz, t, l, depth, kv_prev))
    sn, st, _ = xs.shape
    m = sn * st
    (sq_sb, sq_fx, sk_sb, sv_sb, sk_fx, sv_fx, sk_sb_h, sv_sb_h, sk_fx_h, sv_fx_h,
     sz, sxbc, ssmall, sgates) = _inproj(xs.reshape(m, d), lw['norm1_g'], lw['w_main'], lw['w_small'])

    b3 = lambda a: a.reshape(bsz, t, a.shape[-1])
    s3 = lambda a: a.reshape(sn, st, a.shape[-1])
    q_rows = lambda q: jnp.tile(s3(q), (1, N_HEADS, 1))
    pad_keys = lambda a: jnp.pad(s3(a), ((0, 0), (0, PAGE_SIZE - st), (0, 0)))
    y_sb, sy_sb = _sb_attention(l, page_table, (b3(q_sb), b3(k_sb_h), b3(v_sb_h)),
                                (q_rows(sq_sb), pad_keys(sk_sb_h), pad_keys(sv_sb_h)),
                                (cache_sb_k, cache_sb_v), st)

    f_rows = small[:, :N_HEADS].reshape(bsz, t, N_HEADS).transpose(0, 2, 1)
    rows_per_seq = t // PAGE_SIZE
    f_rows = f_rows.reshape(bsz * N_HEADS * rows_per_seq, PAGE_SIZE)
    bias_rows = jnp.tile(jnp.repeat(lw['b_forget'], rows_per_seq), bsz)[:, None]
    logf, cum = _logf_cum(f_rows, bias_rows, rows_per_seq)
    logf = logf.reshape(bsz, N_HEADS, t).transpose(0, 2, 1)
    blk = ATT_BLOCK
    cum2 = cum * LOG2_E
    cum_col = cum2.reshape(bsz, N_HEADS, t).transpose(0, 2, 1)
    cum_row = cum2.reshape(bsz, N_HEADS, t // blk, blk).transpose(0, 2, 1, 3)
    f_new = ssmall[:, :N_HEADS].reshape(sn, st, N_HEADS).transpose(0, 2, 1)
    sf_rows = jnp.pad(jnp.repeat(f_new, st, axis=1), ((0, 0), (0, 0), (0, PAGE_SIZE - st)))
    b_rows = jnp.broadcast_to(jnp.repeat(lw['b_forget'], st)[:, None], (N_HEADS * st, PAGE_SIZE))
    y_fx, sy_fx, lf_rows = _fox_attention(
        l, page_table, (b3(q_fx), b3(k_fx_h), b3(v_fx_h), cum_col, cum_row),
        (q_rows(sq_fx), pad_keys(sk_fx_h), pad_keys(sv_fx_h), sf_rows, b_rows),
        (cache_fox_k, cache_fox_v, cache_lf_t), st)
    slogf = lf_rows[:, ::st, :st].transpose(0, 2, 1)

    conv0 = jnp.zeros((bsz, SSM_CONV - 1, SSM_CONV_DIM), F32)
    ssm0 = jnp.zeros((bsz, SSM_INNER, SSM_STATE), F32)
    ssm_w = (lw['conv_w'], lw['conv_b'], lw['dt_bias'], lw['a_log'], lw['d_skip'], lw['ssm_norm_g'])
    y_ssm, ssm_stack, conv_new = _ssm(b3(z), b3(xbc), b3(small), conv0, ssm0, *ssm_w, SSM_CHUNK,
                                      l, depth, ssm_prev)
    pad_rows = lambda a: jnp.pad(s3(a), ((0, 0), (0, SSM_SHORT_ROWS - st), (0, 0)))
    sy_ssm, sssm_stack, sconv_new = _ssm(pad_rows(sz), pad_rows(sxbc), pad_rows(ssmall), state_conv[l],
                                         state_ssm[l].reshape(sn, SSM_INNER, SSM_STATE), *ssm_w, st,
                                         l, depth, sssm_prev)
    sy_ssm = sy_ssm[:, :st]

    xp_new = _merge_ffn(xp.reshape(n, d), y_sb.reshape(n, -1), y_ssm.reshape(n, -1), y_fx.reshape(n, -1),
                        gates, lw, gf, final)
    xs_new = _merge_ffn(xs.reshape(m, d), sy_sb.reshape(m, -1), sy_ssm.reshape(m, -1),
                        sy_fx.reshape(m, -1), sgates, lw, gf, final)
    prompt_states = (logf, conv_new)
    sample_states = (_to_heads(sk_sb, sn, st), _to_heads(sv_sb, sn, st), _to_heads(sk_fx, sn, st),
                     _to_heads(sv_fx, sn, st), slogf, sconv_new)
    return (xp_new.reshape(bsz, t, d), xs_new.reshape(sn, st, d),
            ((k_sb, v_sb, k_fx, v_fx), ssm_stack, sssm_stack), prompt_states, sample_states)


def kernel(x_prompt, x_sample, cache_sb_k, cache_sb_v, cache_fox_k, cache_fox_v, cache_fox_logf, state_ssm, state_conv, page_table, norm1_g, w_in, b_forget, conv_w, conv_b, dt_bias, a_log, d_skip, ssm_norm_g, w_sb_out, w_ssm_out, w_fox_out, w_o, norm2_g, w_up, w_down, final_norm_g):
    depth = w_in.shape[0]
    assert page_table.shape[1] % PAGES_PER_STEP == 0
    assert x_sample.shape[1] >= SSM_CONV - 1 and x_prompt.shape[1] % ATT_BLOCK == 0
    flat = lambda c: c.transpose(0, 1, 3, 4, 2).reshape(c.shape[0], c.shape[1], ATT_WIDTH, PAGE_SIZE)
    caches = (flat(cache_sb_k), flat(cache_sb_v), flat(cache_fox_k), flat(cache_fox_v),
              cache_fox_logf.transpose(0, 1, 3, 2))
    gf = final_norm_g.reshape(1, D_MODEL)
    xp, xs = x_prompt, x_sample
    prompt_states, sample_states = [], []
    stacks = None
    for l in range(depth):
        lw = _layer_weights(l, norm1_g, w_in, b_forget, conv_w, conv_b, dt_bias, a_log, d_skip,
                            ssm_norm_g, w_sb_out, w_ssm_out, w_fox_out, w_o, norm2_g, w_up, w_down)
        xp, xs, stacks, st_p, st_s = _layer(xp, xs, l, depth, lw, gf, stacks, caches,
                                            state_ssm, state_conv, page_table)
        prompt_states.append(st_p)
        sample_states.append(st_s)
    bsz, t = x_prompt.shape[:2]
    kv_stacks, ssm_stack, sssm_stack = stacks
    from_t = lambda a: a.reshape(depth, bsz, N_HEADS, HEAD_DIM, t).transpose(0, 1, 4, 2, 3)
    ssm_heads = lambda a: a.reshape(depth, a.shape[1], SSM_HEADS, SSM_INNER // SSM_HEADS, SSM_STATE)
    p_logf, p_conv = [jnp.stack(s) for s in zip(*prompt_states)]
    s_sb_k, s_sb_v, s_fox_k, s_fox_v, s_logf, s_conv = [jnp.stack(s) for s in zip(*sample_states)]
    return (xp, xs, *[from_t(a) for a in kv_stacks], p_logf, ssm_heads(ssm_stack), p_conv,
            s_sb_k, s_sb_v, s_fox_k, s_fox_v, s_logf, ssm_heads(sssm_stack), s_conv)
```

```python
import functools

import jax
import jax.numpy as jnp
from jax import lax
from jax.experimental import pallas as pl
from jax.experimental.pallas import tpu as pltpu

F32 = jnp.float32
BF16 = jnp.bfloat16

D_MODEL = 1024
HEAD_DIM = 64
N_HEADS = 4
ATT_WIDTH = N_HEADS * HEAD_DIM
SSM_HEADS = 8
SSM_INNER = 512
SSM_STATE = 128
SSM_GROUPS = 2
SSM_CONV = 4
SSM_CONV_DIM = 1024
SSM_CHUNK = 128
PAGE_SIZE = 128
N_BRANCH = 3
FFN_HIDDEN = 4 * D_MODEL
RMS_EPS = 1e-6
NEG_INF = -1e30
Q_SCALE = HEAD_DIM ** -0.5
LOG2_E = 1.4426950408889634

_OFF_F = 6 * ATT_WIDTH
_OFF_Z = _OFF_F + N_HEADS
_OFF_XBC = _OFF_Z + SSM_INNER
_OFF_DT = _OFF_XBC + SSM_CONV_DIM
_OFF_GATE = _OFF_DT + SSM_HEADS
MAIN_WIDTH = 6 * ATT_WIDTH + SSM_INNER + SSM_CONV_DIM + N_BRANCH * D_MODEL
SMALL_WIDTH = 128
DT_LANE = N_HEADS

V7X_VMEM_LIMIT_BYTES = 58 * 1024 * 1024
PAGES_PER_STEP = 32
ATT_BLOCK = 256
PROMPT_SEQS_PER_STEP = 2
SB_SPLIT_TERMS = 1
CUMSUM_SEG = 256


def _cparams(*sem):
    return pltpu.CompilerParams(dimension_semantics=sem, vmem_limit_bytes=V7X_VMEM_LIMIT_BYTES)


def _const_spec(shape):
    n = len(shape)
    return pl.BlockSpec(shape, lambda *_: (0,) * n, pipeline_mode=pl.Buffered(1))


def _rms(x, g):
    ms = jnp.mean(x * x, axis=-1, keepdims=True)
    return x * lax.rsqrt(ms + RMS_EPS) * g


def _softplus_tail(z):
    return jnp.log1p(jnp.exp(-jnp.abs(z)))


def _log_sigmoid(z):
    return jnp.minimum(z, 0.0) - _softplus_tail(z)


def _softplus(z):
    return jnp.maximum(z, 0.0) + _softplus_tail(z)


def _dot_nt(a, b):
    return lax.dot_general(a, b, (((1,), (1,)), ((), ())), preferred_element_type=F32)


def _dot(a, b):
    return jnp.dot(a, b, preferred_element_type=F32)


def _dot_exact(a, b):
    return jnp.dot(a, b, preferred_element_type=F32, precision=lax.Precision.HIGHEST)


def _store_kv(ref, a, transposed):
    if not transposed:
        ref[...] = a
    elif len(ref.shape) == 2:
        ref[...] = a.T
    else:
        ref[0] = a.T
        ref[1:] = jnp.zeros((ref.shape[0] - 1,) + ref.shape[1:], ref.dtype)


def _inproj_kernel(x_ref, g_ref, wm_ref, ws_ref, *rest, kv_transposed, n_aliased):
    (qsb_ref, qfx_ref, ksb_ref, vsb_ref, kfx_ref, vfx_ref, ksbh_ref, vsbh_ref, kfxh_ref, vfxh_ref,
     z_ref, xbc_ref, small_ref, gate_ref) = rest[n_aliased:]
    h = _rms(x_ref[...], g_ref[...]).astype(BF16)

    def mm(c0, width):
        return _dot_nt(h, wm_ref[c0:c0 + width, :])

    w = ATT_WIDTH
    qsb_ref[...] = (mm(0, w) * (Q_SCALE * LOG2_E)).astype(BF16)
    for i, (full_ref, half_ref) in enumerate(((ksb_ref, ksbh_ref), (vsb_ref, vsbh_ref))):
        a = mm((1 + i) * w, w)
        _store_kv(full_ref, a, kv_transposed)
        half_ref[...] = a.astype(BF16)
    qfx_ref[...] = (mm(3 * w, w) * (Q_SCALE * LOG2_E)).astype(BF16)
    for i, (full_ref, half_ref) in enumerate(((kfx_ref, kfxh_ref), (vfx_ref, vfxh_ref))):
        a = mm((4 + i) * w, w)
        _store_kv(full_ref, a, kv_transposed)
        half_ref[...] = a.astype(BF16)
    z_ref[...] = mm(6 * w, SSM_INNER)
    c0 = 6 * w + SSM_INNER
    for c in range(SSM_CONV_DIM // 512):
        xbc_ref[:, c * 512:(c + 1) * 512] = mm(c0 + c * 512, 512)
    c0 += SSM_CONV_DIM
    for c in range(N_BRANCH * D_MODEL // 512):
        gate_ref[:, c * 512:(c + 1) * 512] = jax.nn.sigmoid(mm(c0 + c * 512, 512)).astype(BF16)
    small_ref[...] = _dot_nt(h, ws_ref[...])


_HBM_SPEC = pl.BlockSpec(memory_space=pl.ANY)


def _inproj(x2d, g, w_main, w_small, kv_stack=None):
    n = x2d.shape[0]
    tm = min(512, n)
    row = lambda width: pl.BlockSpec((tm, width), lambda i: (i, 0))
    rows = lambda width, dt: (row(width), jax.ShapeDtypeStruct((n, width), dt))
    prev = ()
    if kv_stack is None:
        kv = rows(ATT_WIDTH, F32)
    else:
        bsz, t, layer, depth, prev = kv_stack
        prev = () if prev is None else tuple(prev)
        nt = t // tm
        if prev or depth == 1:
            block = pl.BlockSpec((None, None, ATT_WIDTH, tm), lambda i: (layer, i // nt, 0, i % nt))
        else:
            block = pl.BlockSpec((depth, None, ATT_WIDTH, tm), lambda i: (0, i // nt, 0, i % nt))
        kv = (block, jax.ShapeDtypeStruct((depth, bsz, ATT_WIDTH, t), F32))
    outs = ([rows(ATT_WIDTH, BF16)] * 2 + [kv] * 4 + [rows(ATT_WIDTH, BF16)] * 4
            + [rows(SSM_INNER, F32), rows(SSM_CONV_DIM, F32), rows(SMALL_WIDTH, F32),
               rows(N_BRANCH * D_MODEL, BF16)])
    n_in = 4
    return pl.pallas_call(
        functools.partial(_inproj_kernel, kv_transposed=kv_stack is not None, n_aliased=len(prev)),
        grid=(n // tm,),
        in_specs=[row(D_MODEL), _const_spec((1, D_MODEL)),
                  _const_spec((MAIN_WIDTH, D_MODEL)), _const_spec((SMALL_WIDTH, D_MODEL))]
                 + [_HBM_SPEC] * len(prev),
        out_specs=[spec for spec, _ in outs],
        out_shape=[shape for _, shape in outs],
        input_output_aliases={n_in + j: 2 + j for j in range(len(prev))},
        compiler_params=_cparams("parallel"),
        name="inproj",
    )(x2d, g, w_main, w_small, *prev)


def _lane_head(width=ATT_WIDTH):
    return lax.broadcasted_iota(jnp.int32, (1, width), 1) // HEAD_DIM


def _suffix_matrix(seg, terms):
    r = lax.broadcasted_iota(jnp.int32, (terms * seg, seg), 0) % seg
    c = lax.broadcasted_iota(jnp.int32, (terms * seg, seg), 1)
    return jnp.where(r >= c, 1.0, 0.0).astype(BF16)


def _split_bf16(x, terms):
    out = []
    for _ in range(terms - 1):
        head = x.astype(BF16)
        out.append(head)
        x = x - head.astype(F32)
    out.append(x.astype(BF16))
    return jnp.concatenate(out, axis=1)


def _suffix_sums(x, u, carry):
    m, n = x.shape
    seg = u.shape[1]
    n_seg = n // seg
    terms = u.shape[0] // seg
    if n_seg == 1:
        cs = _dot(_split_bf16(x, terms), u)
        return cs + carry, carry + cs[:, 0:1]
    stacked = jnp.concatenate([x[:, s * seg:(s + 1) * seg] for s in range(n_seg)], axis=0)
    cs = _dot(_split_bf16(stacked, terms), u)
    parts = [None] * n_seg
    for s in reversed(range(n_seg)):
        part = cs[s * m:(s + 1) * m, :]
        parts[s] = part + carry
        carry = carry + part[:, 0:1]
    return jnp.concatenate(parts, axis=1), carry


def _stick_block(z2, carry, u, mask):
    drop = jnp.maximum(z2, 0.0) + jnp.log2(1.0 + jnp.exp2(-jnp.abs(z2)))
    if mask is not None:
        drop = jnp.where(mask, drop, 0.0)
    later, carry = _suffix_sums(drop, u, carry)
    w = jnp.exp2(z2 - later)
    if mask is not None:
        w = jnp.where(mask, w, 0.0)
    return w, carry


def _stack_heads(q):
    lane_head = _lane_head()
    return jnp.concatenate([jnp.where(lane_head == h, q, jnp.zeros_like(q)) for h in range(N_HEADS)], axis=0)


def _fold_heads(acc, rows):
    lane_head = _lane_head()
    out = jnp.zeros((rows, ATT_WIDTH), F32)
    for h in range(N_HEADS):
        out = jnp.where(lane_head == h, acc[h * rows:(h + 1) * rows, :], out)
    return out


def _sb_prompt_kernel(q_ref, k_ref, v_ref, o_ref, acc_ref, *, bq, bk):
    assert bq == bk
    i = pl.program_id(1)
    n_seq = q_ref.shape[0]
    rows = N_HEADS * bq
    qs = [_stack_heads(q_ref[s]) for s in range(n_seq)]
    u = _suffix_matrix(CUMSUM_SEG, SB_SPLIT_TERMS)
    qpos = lax.broadcasted_iota(jnp.int32, (rows, bk), 0) % bq
    causal = lax.broadcasted_iota(jnp.int32, (rows, bk), 1) < qpos

    def block(start, width, carries, mask):
        start = pl.multiple_of(start, width)
        out = []
        for s in range(n_seq):
            w, carry = _stick_block(_dot_nt(qs[s], k_ref[s, pl.ds(start, width), :]), carries[s], u, mask)
            pv = _dot(w.astype(BF16), v_ref[s, pl.ds(start, width), :])
            acc_ref[s] = pv if mask is not None else acc_ref[s] + pv
            out.append(carry)
        return tuple(out)

    carries = block(i * bk, bk, (jnp.zeros((rows, 1), F32),) * n_seq, causal)
    carries = lax.fori_loop(0, i % 2, lambda _, c: block((i - 1) * bk, bk, c, None), carries)
    pairs = i // 2
    lax.fori_loop(0, pairs, lambda jj, c: block((pairs - 1 - jj) * 2 * bk, 2 * bk, c, None), carries)
    for s in range(n_seq):
        o_ref[s] = _fold_heads(acc_ref[s], bq).astype(o_ref.dtype)


def _logf_cum_kernel(b_ref, f_ref, logf_ref, cum_ref, *, rows_per_seq):
    logf = _log_sigmoid(f_ref[...] + b_ref[...])
    logf_ref[...] = logf
    n = logf.shape[0]
    r = lax.broadcasted_iota(jnp.int32, (PAGE_SIZE, PAGE_SIZE), 0)
    c = lax.broadcasted_iota(jnp.int32, (PAGE_SIZE, PAGE_SIZE), 1)
    within = _dot_exact(logf, jnp.where(r <= c, 1.0, 0.0).astype(F32))
    totals = jnp.broadcast_to(within[:, PAGE_SIZE - 1:PAGE_SIZE], within.shape)
    rr = lax.broadcasted_iota(jnp.int32, (n, n), 0)
    cc = lax.broadcasted_iota(jnp.int32, (n, n), 1)
    earlier_rows = jnp.where(cc // rows_per_seq == rr // rows_per_seq, jnp.where(cc < rr, 1.0, 0.0), 0.0)
    cum_ref[...] = within + _dot_exact(earlier_rows, totals)


def _logf_cum(f_rows, b_rows, rows_per_seq):
    spec = pl.BlockSpec(f_rows.shape, lambda i: (0, 0))
    return pl.pallas_call(
        functools.partial(_logf_cum_kernel, rows_per_seq=rows_per_seq),
        grid=(1,),
        in_specs=[pl.BlockSpec(b_rows.shape, lambda i: (0, 0)), spec],
        out_specs=[spec, spec],
        out_shape=[jax.ShapeDtypeStruct(f_rows.shape, F32)] * 2,
        compiler_params=_cparams("arbitrary"),
        name="logf_cum",
    )(b_rows, f_rows)


def _fox_prompt_kernel(q_ref, k_ref, v_ref, cq_ref, ck_ref, o_ref, acc_ref, *, bq, bk):
    assert bq == bk
    blk = bq
    i = pl.program_id(1)
    n_seq = q_ref.shape[0]
    rows = N_HEADS * blk
    qs = [_stack_heads(q_ref[s]) for s in range(n_seq)]
    cqs = [jnp.concatenate([cq_ref[s, :, h:h + 1] for h in range(N_HEADS)], axis=0) for s in range(n_seq)]
    qpos = lax.broadcasted_iota(jnp.int32, (rows, blk), 0) % blk
    causal = lax.broadcasted_iota(jnp.int32, (rows, blk), 1) <= qpos

    def block(j, states, mask):
        start = pl.multiple_of(j * blk, blk)
        out = []
        for s in range(n_seq):
            sc = _dot_nt(qs[s], k_ref[s, pl.ds(start, blk), :]) + cqs[s]
            ck = ck_ref[s, j]
            sc = jnp.concatenate([sc[h * blk:(h + 1) * blk, :] - ck[h:h + 1, :] for h in range(N_HEADS)], axis=0)
            if mask is not None:
                sc = jnp.where(mask, sc, NEG_INF)
            m = jnp.max(sc, axis=-1, keepdims=True)
            if states is not None:
                m_old, l_old = states[s]
                m = jnp.maximum(m_old, m)
            p = jnp.exp2(sc - m)
            pv = _dot(p.astype(BF16), v_ref[s, pl.ds(start, blk), :])
            l = jnp.sum(p, axis=-1, keepdims=True)
            if states is None:
                acc_ref[s] = pv
            else:
                alpha = jnp.exp2(m_old - m)
                acc_ref[s] = acc_ref[s] * alpha + pv
                l = alpha * l_old + l
            out.append((m, l))
        return tuple(out)

    states = block(i, None, causal)
    states = lax.fori_loop(0, i, lambda jj, st: block(i - 1 - jj, st, None), states)
    for s in range(n_seq):
        o_ref[s] = _fold_heads(acc_ref[s] / states[s][1], blk).astype(o_ref.dtype)


_XBUF_ROW0 = 8


def _ssm_kernel(z_ref, xbc_ref, small_ref, conv0_ref, state0_ref,
                cw_ref, cb_ref, dtb_ref, alog_ref, dskip_ref, ng_ref, *rest, valid, n_aliased):
    y_ref, state_out_ref, conv_out_ref, xbuf_ref, state_ref = rest[n_aliased:]
    c = pl.program_id(1)
    n_seq = xbc_ref.shape[0]
    r0 = _XBUF_ROW0

    @pl.when(c == 0)
    def _():
        xbuf_ref[...] = jnp.zeros_like(xbuf_ref)
        xbuf_ref[:, r0 - 3:r0, :] = conv0_ref[...]
        state_ref[...] = state0_ref[...]

    x_curs = [_ssm_chunk(z_ref.at[s], xbc_ref.at[s], small_ref.at[s], cw_ref, cb_ref, dtb_ref, alog_ref,
                         dskip_ref, ng_ref, y_ref.at[s], xbuf_ref.at[s], state_ref.at[s], valid)
              for s in range(n_seq)]

    @pl.when(c == pl.num_programs(1) - 1)
    def _():
        if len(state_out_ref.shape) == len(state_ref.shape):
            state_out_ref[...] = state_ref[...]
        else:
            state_out_ref[0] = state_ref[...]
            state_out_ref[1:] = jnp.zeros((state_out_ref.shape[0] - 1,) + state_ref.shape, state_ref.dtype)
        for s in range(n_seq):
            conv_out_ref[s] = x_curs[s][valid - 3:valid, :]


def _ssm_chunk(z_ref, xbc_ref, small_ref, cw_ref, cb_ref, dtb_ref, alog_ref, dskip_ref, ng_ref,
               y_ref, xbuf_ref, state_ref, valid):
    L = xbc_ref.shape[0]
    S = SSM_CHUNK
    r0 = _XBUF_ROW0

    def pad_time(a):
        return a if L == S else jnp.concatenate([a, jnp.zeros((S - L,) + a.shape[1:], a.dtype)], axis=0)

    x_cur = xbc_ref[...]
    window = jnp.concatenate([xbuf_ref[...], x_cur], axis=0)
    cw = cw_ref[...]
    conv = cb_ref[...] + x_cur * cw[3:4, :]
    for i in range(SSM_CONV - 1):
        conv = conv + pltpu.roll(window, 3 - i, 0)[r0:, :] * cw[i:i + 1, :]
    xbuf_ref[...] = x_cur[L - r0:L, :]
    act = conv * jax.nn.sigmoid(conv)
    xs = act[:, :SSM_INNER]
    b_in = act[:, SSM_INNER:SSM_INNER + SSM_GROUPS * SSM_STATE].astype(BF16)
    c_in = act[:, SSM_INNER + SSM_GROUPS * SSM_STATE:].astype(BF16)

    row = lax.broadcasted_iota(jnp.int32, (L, S), 0)
    col = lax.broadcasted_iota(jnp.int32, (L, S), 1)
    tri = row >= col
    dt = _softplus(small_ref[...] + dtb_ref[...])
    if valid < L:
        dt = jnp.where(lax.broadcasted_iota(jnp.int32, dt.shape, 0) < valid, dt, 0.0)
    d_a = dt * (-jnp.exp(alog_ref[...]))
    a_cs = _dot_exact(jnp.where(tri[:, :L], 1.0, 0.0).astype(F32), d_a)
    a_cs_t = pad_time(a_cs).T
    a_last = a_cs[L - 1:L, :]
    e_cs = jnp.exp(a_cs)
    wgt = jnp.exp(a_last - a_cs) * dt
    chunk_dec = jnp.exp(a_last)

    half = lax.broadcasted_iota(jnp.int32, (1, 128), 1) // SSM_STATE_HALF
    rhalf = lax.broadcasted_iota(jnp.int32, (128, 1), 0) // SSM_STATE_HALF
    pair_cols = lambda a, p: jnp.where(half == 0, a[:, DT_LANE + 2 * p:DT_LANE + 2 * p + 1],
                                       a[:, DT_LANE + 2 * p + 1:DT_LANE + 2 * p + 2])
    b_keys = pad_time(b_in)
    scores = [_dot_nt(c_in[:, g * SSM_STATE:(g + 1) * SSM_STATE],
                      b_keys[:, g * SSM_STATE:(g + 1) * SSM_STATE]) for g in range(SSM_GROUPS)]
    ys = []
    for p in range(SSM_HEADS // 2):
        g = (2 * p) // (SSM_HEADS // SSM_GROUPS)
        bg = b_keys[:, g * SSM_STATE:(g + 1) * SSM_STATE]
        cg = c_in[:, g * SSM_STATE:(g + 1) * SSM_STATE]
        xs_p = xs[:, 128 * p:128 * (p + 1)]
        xdt = pad_time((xs_p * pair_cols(dt, p)).astype(BF16))
        y_diag = jnp.zeros((L, 128), F32)
        for hh in range(2):
            lane = DT_LANE + 2 * p + hh
            seg = a_cs[:, lane:lane + 1] - a_cs_t[lane:lane + 1, :]
            decay = jnp.exp(jnp.where(tri, seg, NEG_INF))
            y_h = _dot((scores[g] * decay).astype(BF16), xdt)
            y_diag = jnp.where(half == hh, y_h, y_diag)
        st = state_ref[128 * p:128 * (p + 1), :]
        y_off = _dot_nt(cg, st.astype(BF16)) * pair_cols(e_cs, p)
        ys.append(y_diag + y_off + dskip_ref[:, 128 * p:128 * (p + 1)] * xs_p)
        xw_t = pad_time(xs_p * pair_cols(wgt, p)).T.astype(BF16)
        lane = DT_LANE + 2 * p
        dec = jnp.where(rhalf == 0,
                        jnp.broadcast_to(chunk_dec[:, lane:lane + 1], (128, SSM_STATE)),
                        jnp.broadcast_to(chunk_dec[:, lane + 1:lane + 2], (128, SSM_STATE)))
        state_ref[128 * p:128 * (p + 1), :] = st * dec + _dot(xw_t, bg)

    z = z_ref[...]
    y = jnp.concatenate(ys, axis=1) * (z * jax.nn.sigmoid(z))
    gw = SSM_INNER // SSM_GROUPS
    parts = []
    for g in range(SSM_GROUPS):
        yg = y[:, g * gw:(g + 1) * gw]
        parts.append(yg * lax.rsqrt(jnp.mean(yg * yg, axis=-1, keepdims=True) + RMS_EPS))
    y_ref[...] = (jnp.concatenate(parts, axis=1) * ng_ref[...]).astype(y_ref.dtype)
    return x_cur


SSM_STATE_HALF = 64
SSM_SHORT_ROWS = 16
SSM_SEQS_PER_STEP = 4


def _ssm(z, xbc, small, conv0, state0, cw, cb, dtb, alog, dskip, ng, valid, layer, depth, prev):
    b, t, _ = z.shape
    rows = min(t, SSM_CHUNK)
    nc = t // rows
    par = SSM_SEQS_PER_STEP if b % SSM_SEQS_PER_STEP == 0 else 1
    chunk = lambda width: pl.BlockSpec((par, rows, width), lambda bi, c: (bi, c, 0))
    per_b = lambda shape: pl.BlockSpec((par,) + shape, lambda bi, c: (bi,) + (0,) * len(shape))
    const = lambda shape: pl.BlockSpec(shape, lambda bi, c: (0,) * len(shape))
    state_rows = SSM_INNER
    prev = () if prev is None else (prev,)
    if prev or depth == 1:
        state_out = pl.BlockSpec((None, par, state_rows, SSM_STATE), lambda bi, c: (layer, bi, 0, 0))
    else:
        state_out = pl.BlockSpec((depth, par, state_rows, SSM_STATE), lambda bi, c: (0, bi, 0, 0))
    n_in = 11
    return pl.pallas_call(
        functools.partial(_ssm_kernel, valid=valid, n_aliased=len(prev)),
        grid=(b // par, nc),
        in_specs=[chunk(SSM_INNER), chunk(SSM_CONV_DIM), chunk(SMALL_WIDTH),
                  per_b((SSM_CONV - 1, SSM_CONV_DIM)), per_b((state_rows, SSM_STATE)),
                  const((SSM_CONV, SSM_CONV_DIM)), const((1, SSM_CONV_DIM)),
                  const((1, SMALL_WIDTH)), const((1, SMALL_WIDTH)),
                  const((1, SSM_INNER)), const((1, SSM_INNER))] + [_HBM_SPEC] * len(prev),
        out_specs=[chunk(SSM_INNER), state_out, per_b((SSM_CONV - 1, SSM_CONV_DIM))],
        out_shape=[jax.ShapeDtypeStruct((b, t, SSM_INNER), BF16),
                   jax.ShapeDtypeStruct((depth, b, state_rows, SSM_STATE), F32),
                   jax.ShapeDtypeStruct((b, SSM_CONV - 1, SSM_CONV_DIM), F32)],
        input_output_aliases={n_in + j: 1 + j for j in range(len(prev))},
        scratch_shapes=[pltpu.VMEM((par, _XBUF_ROW0, SSM_CONV_DIM), F32),
                        pltpu.VMEM((par, state_rows, SSM_STATE), F32)],
        compiler_params=_cparams("parallel", "arbitrary"),
        name="ssm",
    )(z, xbc, small, conv0, state0, cw, cb, dtb, alog, dskip, ng, *prev)


def _merge_ffn_kernel(x_ref, ysb_ref, yssm_ref, yfx_ref, gate_ref,
                      wsb_ref, wssm_ref, wfx_ref, wo_ref, g2_ref, wup_ref, wdn_ref, gf_ref,
                      o_ref, *, final):
    d = D_MODEL
    mixed = gate_ref[:, 0:d].astype(F32) * _dot(ysb_ref[...].astype(BF16), wsb_ref[...])
    mixed = mixed + gate_ref[:, d:2 * d].astype(F32) * _dot(yssm_ref[...].astype(BF16), wssm_ref[...])
    mixed = mixed + gate_ref[:, 2 * d:3 * d].astype(F32) * _dot(yfx_ref[...].astype(BF16), wfx_ref[...])
    x = x_ref[...] + _dot(mixed.astype(BF16), wo_ref[...])
    h = _rms(x, g2_ref[...]).astype(BF16)
    hc = FFN_HIDDEN // 2
    for c in range(2):
        u = jnp.maximum(_dot(h, wup_ref[:, c * hc:(c + 1) * hc]), 0.0)
        x = x + _dot((u * u).astype(BF16), wdn_ref[c * hc:(c + 1) * hc, :])
    if final:
        x = _rms(x, gf_ref[...])
    o_ref[...] = x


def _merge_ffn(x2d, y_sb, y_ssm, y_fx, gates, lw, gf, final):
    n = x2d.shape[0]
    tm = min(512, n)
    row = lambda width: pl.BlockSpec((tm, width), lambda i: (i, 0))
    weights = [lw['w_sb_out'], lw['w_ssm_out'], lw['w_fox_out'], lw['w_o'], lw['norm2_g'],
               lw['w_up'], lw['w_down'], gf]
    return pl.pallas_call(
        functools.partial(_merge_ffn_kernel, final=final),
        grid=(n // tm,),
        in_specs=[row(D_MODEL), row(ATT_WIDTH), row(SSM_INNER), row(ATT_WIDTH),
                  row(N_BRANCH * D_MODEL)] + [_const_spec(w.shape) for w in weights],
        out_specs=row(D_MODEL),
        out_shape=jax.ShapeDtypeStruct((n, D_MODEL), F32),
        compiler_params=_cparams("parallel"),
        name="merge_ffn",
    )(x2d, y_sb, y_ssm, y_fx, gates, *weights)


def _block_diag_q(q_ref):
    q = q_ref[...]
    rows = q.shape[0]
    row_head = lax.broadcasted_iota(jnp.int32, (rows, 1), 0) // (rows // N_HEADS)
    return jnp.where(row_head == _lane_head(), q, jnp.zeros_like(q))


def _page_copy(cache_ref, stage_ref, sem_ref, layer, page, slot, p):
    return pltpu.make_async_copy(cache_ref.at[layer, page], stage_ref.at[slot, p], sem_ref.at[slot])


def _grid_step():
    return (pl.program_id(0) * pl.num_programs(1) + pl.program_id(1),
            pl.num_programs(0) * pl.num_programs(1))


def _prefetch_pages(pt_ref, caches, stages, sems, layer, group, n_chunks):
    n_pages = stages[0].shape[1]
    per_seq = n_pages // group
    step, n_steps = _grid_step()
    b, c = step // n_chunks, step % n_chunks
    slot = step % 2

    def start(bb, cc, sl):
        base = (n_chunks - 1 - cc) * per_seq
        for p in range(n_pages):
            page = pt_ref[bb * group + p // per_seq, base + p % per_seq]
            for i, (cache, stage, sem) in enumerate(zip(caches, stages, sems)):
                _page_copy(cache, stage, sem, layer, page, sl, p).start(priority=i % 2)

    @pl.when(step == 0)
    def _():
        start(b, c, slot)

    @pl.when(step + 1 < n_steps)
    def _():
        wrap = c + 1 == n_chunks
        start(jnp.where(wrap, b + 1, b), jnp.where(wrap, 0, c + 1), 1 - slot)

    for p in range(n_pages):
        for cache, stage, sem in zip(caches, stages, sems):
            _page_copy(cache, stage, sem, layer, 0, slot, p).wait()
    return slot


def _pages_to_bf16(stage_ref, slot, g, buf_ref):
    per_seq = buf_ref.shape[-1] // PAGE_SIZE
    for p in range(per_seq):
        buf_ref[g, :, p * PAGE_SIZE:(p + 1) * PAGE_SIZE] = stage_ref[slot, g * per_seq + p].astype(BF16)


def _sb_sample_kernel(pt_ref, q_ref, kn_ref, vn_ref, ck_ref, cv_ref, o_ref,
                      acc_ref, carry_ref, kstage_ref, vstage_ref, ksem, vsem, kbuf_ref, vbuf_ref,
                      *, steps, layer, n_chunks):
    group = q_ref.shape[0]
    slot = _prefetch_pages(pt_ref, (ck_ref, cv_ref), (kstage_ref, vstage_ref), (ksem, vsem), layer, group,
                           n_chunks)
    c = _grid_step()[0] % n_chunks
    rows = N_HEADS * steps
    qbd = [_block_diag_q(q_ref.at[g]) for g in range(group)]
    u = _suffix_matrix(CUMSUM_SEG, SB_SPLIT_TERMS)

    @pl.when(c == 0)
    def _():
        step = lax.broadcasted_iota(jnp.int32, (rows, PAGE_SIZE), 0) % steps
        col = lax.broadcasted_iota(jnp.int32, (rows, PAGE_SIZE), 1)
        for g in range(group):
            w, carry = _stick_block(_dot_nt(qbd[g], kn_ref[g]), jnp.zeros((rows, 1), F32),
                                    _suffix_matrix(PAGE_SIZE, SB_SPLIT_TERMS), col < step)
            acc_ref[g] = _dot(w.astype(BF16), vn_ref[g])
            carry_ref[g] = jnp.broadcast_to(carry, carry_ref.shape[1:])

    for g in range(group):
        _pages_to_bf16(kstage_ref, slot, g, kbuf_ref)
        _pages_to_bf16(vstage_ref, slot, g, vbuf_ref)
        w, carry = _stick_block(_dot(qbd[g], kbuf_ref[g]), carry_ref[g, :, 0:1], u, None)
        acc_ref[g] += _dot_nt(w.astype(BF16), vbuf_ref[g])
        carry_ref[g] = jnp.broadcast_to(carry, carry_ref.shape[1:])

    def finish():
        @pl.when(c == n_chunks - 1)
        def _():
            for g in range(group):
                o_ref[g] = _fold_heads(acc_ref[g], steps)
    return finish


def _page_staging(n_pages, rows, lanes):
    return [pltpu.VMEM((2, n_pages, rows, lanes), F32)], [pltpu.SemaphoreType.DMA((2,))]


def _fox_sample_kernel(pt_ref, q_ref, kn_ref, vn_ref, f_ref, b_ref, ck_ref, cv_ref, clf_ref,
                       o_ref, lf_out_ref, acc_ref, m_ref, l_ref, ncum_ref, rcarry_ref,
                       kstage_ref, vstage_ref, lstage_ref, ksem, vsem, lsem,
                       kbuf_ref, vbuf_ref, lbuf_ref, *, steps, layer, n_chunks):
    group = q_ref.shape[0]
    slot = _prefetch_pages(pt_ref, (ck_ref, cv_ref, clf_ref), (kstage_ref, vstage_ref, lstage_ref),
                           (ksem, vsem, lsem), layer, group, n_chunks)
    c = _grid_step()[0] % n_chunks
    rows = N_HEADS * steps
    qbd = [_block_diag_q(q_ref.at[g]) for g in range(group)]
    stat = lambda a: jnp.broadcast_to(a, m_ref.shape[1:])

    @pl.when(c == 0)
    def _():
        step = lax.broadcasted_iota(jnp.int32, (rows, PAGE_SIZE), 0) % steps
        col = lax.broadcasted_iota(jnp.int32, (rows, PAGE_SIZE), 1)
        visible = col <= step
        lbuf_ref[...] = jnp.zeros_like(lbuf_ref)
        rcarry_ref[...] = jnp.zeros_like(rcarry_ref)
        for g in range(group):
            lf = _log_sigmoid(f_ref[g] + b_ref[...])
            lf_out_ref[g] = lf
            lf = jnp.where(col < steps, lf, 0.0)
            cum = lf
            shift = 1
            while shift < steps:
                cum = cum + jnp.where(col >= shift, pltpu.roll(cum, shift, 1), 0.0)
                shift *= 2
            ncum = jnp.sum(jnp.where(visible, lf, 0.0), axis=-1, keepdims=True)
            s = jnp.where(visible, _dot_nt(qbd[g], kn_ref[g]) + (ncum - cum) * LOG2_E, NEG_INF)
            m = jnp.max(s, axis=-1, keepdims=True)
            p = jnp.exp2(s - m)
            m_ref[g] = stat(m)
            l_ref[g] = stat(jnp.sum(p, axis=-1, keepdims=True))
            acc_ref[g] = _dot(p.astype(BF16), vn_ref[g])
            ncum_ref[g] = stat(ncum)

    per_seq = kbuf_ref.shape[-1] // PAGE_SIZE
    u3 = _suffix_matrix(CUMSUM_SEG, 3)
    for g in range(group):
        _pages_to_bf16(kstage_ref, slot, g, kbuf_ref)
        _pages_to_bf16(vstage_ref, slot, g, vbuf_ref)
        for p in range(per_seq):
            lbuf_ref[g, 0:N_HEADS, p * PAGE_SIZE:(p + 1) * PAGE_SIZE] = lstage_ref[slot, g * per_seq + p]
        lf_pages = lbuf_ref[g]
        incl, carry = _suffix_sums(lf_pages, u3, rcarry_ref[g, :, 0:1])
        suffix = incl - lf_pages
        rcarry_ref[g] = jnp.broadcast_to(carry, rcarry_ref.shape[1:])
        bias = jnp.concatenate(
            [jnp.broadcast_to(suffix[h:h + 1, :], (steps, suffix.shape[1])) for h in range(N_HEADS)], axis=0)
        s = _dot(qbd[g], kbuf_ref[g]) + (bias + ncum_ref[g, :, 0:1]) * LOG2_E
        m_old = m_ref[g, :, 0:1]
        m = jnp.maximum(m_old, jnp.max(s, axis=-1, keepdims=True))
        alpha = jnp.exp2(m_old - m)
        p = jnp.exp2(s - m)
        l_ref[g] = stat(alpha * l_ref[g, :, 0:1] + jnp.sum(p, axis=-1, keepdims=True))
        acc_ref[g] = acc_ref[g] * alpha + _dot_nt(p.astype(BF16), vbuf_ref[g])
        m_ref[g] = stat(m)

    def finish():
        @pl.when(c == n_chunks - 1)
        def _():
            for g in range(group):
                o_ref[g] = _fold_heads(acc_ref[g] / l_ref[g, :, 0:1], steps)
    return finish


def _both_groups_kernel(pt_ref, *refs, prompt_kernel, sample_kernel, counts):
    parts, at = [], 0
    for n in counts:
        parts.append(refs[at:at + n])
        at += n
    p_in, s_in, p_out, s_out, p_scr, s_scr = parts
    finish_sample = sample_kernel(pt_ref, *s_in, *s_out, *s_scr)
    prompt_kernel(*p_in, *p_out, *p_scr)
    finish_sample()


def _attention_both_groups(name, layer, page_table, prompt_kernel, prompt_in, prompt_extra_specs,
                           sample_kernel, sample_in, sample_extra_specs, caches, sample_extra_out,
                           sample_extra_scratch, steps):
    q, k, v = prompt_in[:3]
    b, t, w = q.shape
    blk = ATT_BLOCK
    nq = t // blk
    pb = PROMPT_SEQS_PER_STEP if b % PROMPT_SEQS_PER_STEP == 0 else 1
    q_rows = sample_in[0]
    n_seq, rows, _ = q_rows.shape
    n_pages = PAGES_PER_STEP
    n_chunks = page_table.shape[1] // n_pages
    steps_total = (b // pb) * nq
    g = n_seq * n_chunks // steps_total
    assert g >= 1 and g * steps_total == n_seq * n_chunks, "the two groups must split into equally many grid steps"
    seq_group = lambda bi, i: (bi * nq + i) // n_chunks

    qspec = pl.BlockSpec((pb, blk, w), lambda bi, i, pt: (bi, i, 0))
    kvspec = pl.BlockSpec((pb, t, w), lambda bi, i, pt: (bi, 0, 0), pipeline_mode=pl.Buffered(1))
    per_g = lambda r, width=w: pl.BlockSpec((g, r, width), lambda bi, i, pt: (seq_group(bi, i), 0, 0))
    prompt_specs = [qspec, kvspec, kvspec] + prompt_extra_specs(pb, t, blk)
    sample_specs = ([per_g(rows), per_g(PAGE_SIZE), per_g(PAGE_SIZE)] + sample_extra_specs(per_g, rows)
                    + [_HBM_SPEC] * len(caches))
    stat = pltpu.VMEM((g, rows, 128), F32)
    stages, sems = [], []
    for cache in caches:
        st, se = _page_staging(g * n_pages, cache.shape[2], PAGE_SIZE)
        stages += st
        sems += se
    prompt_scratch = [pltpu.VMEM((pb, N_HEADS * blk, w), F32)]
    sample_scratch = ([pltpu.VMEM((g, rows, w), F32)] + sample_extra_scratch(g, stat) + stages + sems
                      + [pltpu.VMEM((g, w, n_pages * PAGE_SIZE), BF16)] * 2)
    if len(caches) == 3:
        sample_scratch.append(pltpu.VMEM((g, 8, n_pages * PAGE_SIZE), F32))
    out_specs = [qspec, per_g(steps)] + [per_g(rows, PAGE_SIZE)] * len(sample_extra_out)
    out_shape = ([jax.ShapeDtypeStruct((b, t, w), BF16), jax.ShapeDtypeStruct((n_seq, steps, w), F32)]
                 + list(sample_extra_out))
    counts = (len(prompt_specs), len(sample_specs), 1, len(out_specs) - 1,
              len(prompt_scratch), len(sample_scratch))
    grid_spec = pltpu.PrefetchScalarGridSpec(
        num_scalar_prefetch=1,
        grid=(b // pb, nq),
        in_specs=prompt_specs + sample_specs,
        out_specs=out_specs,
        scratch_shapes=prompt_scratch + sample_scratch,
    )
    return pl.pallas_call(
        functools.partial(_both_groups_kernel, counts=counts,
                          prompt_kernel=functools.partial(prompt_kernel, bq=blk, bk=blk),
                          sample_kernel=functools.partial(sample_kernel, steps=steps, layer=layer,
                                                          n_chunks=n_chunks)),
        grid_spec=grid_spec,
        out_shape=out_shape,
        compiler_params=_cparams("arbitrary", "arbitrary"),
        name=name,
    )(page_table, *prompt_in, *sample_in, *caches)


def _sb_attention(layer, page_table, prompt_in, sample_in, caches, steps):
    none = lambda *_: []
    return _attention_both_groups(
        "sb_attention", layer, page_table, _sb_prompt_kernel, prompt_in, none,
        _sb_sample_kernel, sample_in, none, caches, [], lambda g, stat: [stat], steps)


def _fox_attention(layer, page_table, prompt_in, sample_in, caches, steps):
    n_seq, rows, _ = sample_in[0].shape
    prompt_extra = lambda pb, t, blk: [
        pl.BlockSpec((pb, blk, N_HEADS), lambda bi, i, pt: (bi, i, 0)),
        pl.BlockSpec((pb, t // blk, N_HEADS, blk), lambda bi, i, pt: (bi, 0, 0, 0))]
    sample_extra = lambda per_g, rows: [per_g(rows, PAGE_SIZE),
                                        pl.BlockSpec((rows, PAGE_SIZE), lambda bi, i, pt: (0, 0))]
    return _attention_both_groups(
        "fox_attention", layer, page_table, _fox_prompt_kernel, prompt_in, prompt_extra,
        _fox_sample_kernel, sample_in, sample_extra, caches,
        [jax.ShapeDtypeStruct((n_seq, rows, PAGE_SIZE), F32)],
        lambda g, stat: [stat, stat, stat, pltpu.VMEM((g, 8, 128), F32)], steps)


def _layer_weights(l, norm1_g, w_in, b_forget, conv_w, conv_b, dt_bias, a_log, d_skip, ssm_norm_g,
                   w_sb_out, w_ssm_out, w_fox_out, w_o, norm2_g, w_up, w_down):
    w = jnp.transpose(w_in, (2, 0, 1))[:, l, :]
    w_main = jnp.concatenate([w[:_OFF_F], w[_OFF_Z:_OFF_DT], w[_OFF_GATE:]], axis=0).astype(BF16)
    w_small = jnp.concatenate([w[_OFF_F:_OFF_Z], w[_OFF_DT:_OFF_GATE]], axis=0)
    w_small = jnp.pad(w_small, ((0, SMALL_WIDTH - w_small.shape[0]), (0, 0))).astype(BF16)
    pad_dt = lambda a: jnp.pad(a, (DT_LANE, SMALL_WIDTH - DT_LANE - SSM_HEADS)).reshape(1, SMALL_WIDTH)
    return {
        'norm1_g': norm1_g[l].reshape(1, D_MODEL), 'w_main': w_main, 'w_small': w_small,
        'b_forget': b_forget[l], 'conv_w': conv_w[l], 'conv_b': conv_b[l].reshape(1, SSM_CONV_DIM),
        'dt_bias': pad_dt(dt_bias[l]), 'a_log': pad_dt(a_log[l]),
        'd_skip': jnp.repeat(d_skip[l], SSM_INNER // SSM_HEADS).reshape(1, SSM_INNER),
        'ssm_norm_g': ssm_norm_g[l].reshape(1, SSM_INNER),
        'w_sb_out': w_sb_out[l].astype(BF16), 'w_ssm_out': w_ssm_out[l].astype(BF16),
        'w_fox_out': w_fox_out[l].astype(BF16), 'w_o': w_o[l].astype(BF16),
        'norm2_g': norm2_g[l].reshape(1, D_MODEL),
        'w_up': w_up[l].astype(BF16), 'w_down': w_down[l].astype(BF16),
    }


def _to_heads(a, bsz, t):
    return a.reshape(bsz, t, N_HEADS, HEAD_DIM)


def _layer(xp, xs, l, depth, lw, gf, stacks, caches, state_ssm, state_conv, page_table):
    kv_prev, ssm_prev, sssm_prev = (None, None, None) if stacks is None else stacks
    final = l == depth - 1
    cache_sb_k, cache_sb_v, cache_fox_k, cache_fox_v, cache_lf_t = caches
    bsz, t, d = xp.shape
    n = bsz * t
    (q_sb, q_fx, k_sb, v_sb, k_fx, v_fx, k_sb_h, v_sb_h, k_fx_h, v_fx_h,
     z, xbc, small, gates) = _inproj(xp.reshape(n, d), lw['norm1_g'], lw['w_main'], lw['w_small'],
                                     (bsz, t, l, depth, kv_prev))
    sn, st, _ = xs.shape
    m = sn * st
    (sq_sb, sq_fx, sk_sb, sv_sb, sk_fx, sv_fx, sk_sb_h, sv_sb_h, sk_fx_h, sv_fx_h,
     sz, sxbc, ssmall, sgates) = _inproj(xs.reshape(m, d), lw['norm1_g'], lw['w_main'], lw['w_small'])

    b3 = lambda a: a.reshape(bsz, t, a.shape[-1])
    s3 = lambda a: a.reshape(sn, st, a.shape[-1])
    q_rows = lambda q: jnp.tile(s3(q), (1, N_HEADS, 1))
    pad_keys = lambda a: jnp.pad(s3(a), ((0, 0), (0, PAGE_SIZE - st), (0, 0)))
    y_sb, sy_sb = _sb_attention(l, page_table, (b3(q_sb), b3(k_sb_h), b3(v_sb_h)),
                                (q_rows(sq_sb), pad_keys(sk_sb_h), pad_keys(sv_sb_h)),
                                (cache_sb_k, cache_sb_v), st)

    f_rows = small[:, :N_HEADS].reshape(bsz, t, N_HEADS).transpose(0, 2, 1)
    rows_per_seq = t // PAGE_SIZE
    f_rows = f_rows.reshape(bsz * N_HEADS * rows_per_seq, PAGE_SIZE)
    bias_rows = jnp.tile(jnp.repeat(lw['b_forget'], rows_per_seq), bsz)[:, None]
    logf, cum = _logf_cum(f_rows, bias_rows, rows_per_seq)
    logf = logf.reshape(bsz, N_HEADS, t).transpose(0, 2, 1)
    blk = ATT_BLOCK
    cum2 = cum * LOG2_E
    cum_col = cum2.reshape(bsz, N_HEADS, t).transpose(0, 2, 1)
    cum_row = cum2.reshape(bsz, N_HEADS, t // blk, blk).transpose(0, 2, 1, 3)
    f_new = ssmall[:, :N_HEADS].reshape(sn, st, N_HEADS).transpose(0, 2, 1)
    sf_rows = jnp.pad(jnp.repeat(f_new, st, axis=1), ((0, 0), (0, 0), (0, PAGE_SIZE - st)))
    b_rows = jnp.broadcast_to(jnp.repeat(lw['b_forget'], st)[:, None], (N_HEADS * st, PAGE_SIZE))
    y_fx, sy_fx, lf_rows = _fox_attention(
        l, page_table, (b3(q_fx), b3(k_fx_h), b3(v_fx_h), cum_col, cum_row),
        (q_rows(sq_fx), pad_keys(sk_fx_h), pad_keys(sv_fx_h), sf_rows, b_rows),
        (cache_fox_k, cache_fox_v, cache_lf_t), st)
    slogf = lf_rows[:, ::st, :st].transpose(0, 2, 1)

    conv0 = jnp.zeros((bsz, SSM_CONV - 1, SSM_CONV_DIM), F32)
    ssm0 = jnp.zeros((bsz, SSM_INNER, SSM_STATE), F32)
    ssm_w = (lw['conv_w'], lw['conv_b'], lw['dt_bias'], lw['a_log'], lw['d_skip'], lw['ssm_norm_g'])
    y_ssm, ssm_stack, conv_new = _ssm(b3(z), b3(xbc), b3(small), conv0, ssm0, *ssm_w, SSM_CHUNK,
                                      l, depth, ssm_prev)
    pad_rows = lambda a: jnp.pad(s3(a), ((0, 0), (0, SSM_SHORT_ROWS - st), (0, 0)))
    sy_ssm, sssm_stack, sconv_new = _ssm(pad_rows(sz), pad_rows(sxbc), pad_rows(ssmall), state_conv[l],
                                         state_ssm[l].reshape(sn, SSM_INNER, SSM_STATE), *ssm_w, st,
                                         l, depth, sssm_prev)
    sy_ssm = sy_ssm[:, :st]

    xp_new = _merge_ffn(xp.reshape(n, d), y_sb.reshape(n, -1), y_ssm.reshape(n, -1), y_fx.reshape(n, -1),
                        gates, lw, gf, final)
    xs_new = _merge_ffn(xs.reshape(m, d), sy_sb.reshape(m, -1), sy_ssm.reshape(m, -1),
                        sy_fx.reshape(m, -1), sgates, lw, gf, final)
    prompt_states = (logf, conv_new)
    sample_states = (_to_heads(sk_sb, sn, st), _to_heads(sv_sb, sn, st), _to_heads(sk_fx, sn, st),
                     _to_heads(sv_fx, sn, st), slogf, sconv_new)
    return (xp_new.reshape(bsz, t, d), xs_new.reshape(sn, st, d),
            ((k_sb, v_sb, k_fx, v_fx), ssm_stack, sssm_stack), prompt_states, sample_states)


def kernel(x_prompt, x_sample, cache_sb_k, cache_sb_v, cache_fox_k, cache_fox_v, cache_fox_logf, state_ssm, state_conv, page_table, norm1_g, w_in, b_forget, conv_w, conv_b, dt_bias, a_log, d_skip, ssm_norm_g, w_sb_out, w_ssm_out, w_fox_out, w_o, norm2_g, w_up, w_down, final_norm_g):
    depth = w_in.shape[0]
    assert page_table.shape[1] % PAGES_PER_STEP == 0
    assert x_sample.shape[1] >= SSM_CONV - 1 and x_prompt.shape[1] % ATT_BLOCK == 0
    flat = lambda c: c.transpose(0, 1, 3, 4, 2).reshape(c.shape[0], c.shape[1], ATT_WIDTH, PAGE_SIZE)
    caches = (flat(cache_sb_k), flat(cache_sb_v), flat(cache_fox_k), flat(cache_fox_v),
              cache_fox_logf.transpose(0, 1, 3, 2))
    gf = final_norm_g.reshape(1, D_MODEL)
    xp, xs = x_prompt, x_sample
    prompt_states, sample_states = [], []
    stacks = None
    for l in range(depth):
        lw = _layer_weights(l, norm1_g, w_in, b_forget, conv_w, conv_b, dt_bias, a_log, d_skip,
                            ssm_norm_g, w_sb_out, w_ssm_out, w_fox_out, w_o, norm2_g, w_up, w_down)
        xp, xs, stacks, st_p, st_s = _layer(xp, xs, l, depth, lw, gf, stacks, caches,
                                            state_ssm, state_conv, page_table)
        prompt_states.append(st_p)
        sample_states.append(st_s)
    bsz, t = x_prompt.shape[:2]
    kv_stacks, ssm_stack, sssm_stack = stacks
    from_t = lambda a: a.reshape(depth, bsz, N_HEADS, HEAD_DIM, t).transpose(0, 1, 4, 2, 3)
    ssm_heads = lambda a: a.reshape(depth, a.shape[1], SSM_HEADS, SSM_INNER // SSM_HEADS, SSM_STATE)
    p_logf, p_conv = [jnp.stack(s) for s in zip(*prompt_states)]
    s_sb_k, s_sb_v, s_fox_k, s_fox_v, s_logf, s_conv = [jnp.stack(s) for s in zip(*sample_states)]
    return (xp, xs, *[from_t(a) for a in kv_stacks], p_logf, ssm_heads(ssm_stack), p_conv,
            s_sb_k, s_sb_v, s_fox_k, s_fox_v, s_logf, ssm_heads(sssm_stack), s_conv)
```

```python
import functools

import jax
import jax.numpy as jnp
from jax import lax
from jax.experimental import pallas as pl
from jax.experimental.pallas import tpu as pltpu

F32 = jnp.float32
BF16 = jnp.bfloat16

D_MODEL = 1024
HEAD_DIM = 64
N_HEADS = 4
ATT_WIDTH = N_HEADS * HEAD_DIM
SSM_HEADS = 8
SSM_INNER = 512
SSM_STATE = 128
SSM_GROUPS = 2
SSM_CONV = 4
SSM_CONV_DIM = 1024
SSM_CHUNK = 128
PAGE_SIZE = 128
N_BRANCH = 3
FFN_HIDDEN = 4 * D_MODEL
RMS_EPS = 1e-6
NEG_INF = -1e30
Q_SCALE = HEAD_DIM ** -0.5
LOG2_E = 1.4426950408889634

_OFF_F = 6 * ATT_WIDTH
_OFF_Z = _OFF_F + N_HEADS
_OFF_XBC = _OFF_Z + SSM_INNER
_OFF_DT = _OFF_XBC + SSM_CONV_DIM
_OFF_GATE = _OFF_DT + SSM_HEADS
MAIN_WIDTH = 6 * ATT_WIDTH + SSM_INNER + SSM_CONV_DIM + N_BRANCH * D_MODEL
SMALL_WIDTH = 128
DT_LANE = N_HEADS

V7X_VMEM_LIMIT_BYTES = 58 * 1024 * 1024
PAGES_PER_STEP = 16
ATT_BLOCK = 256
PROMPT_SEQS_PER_STEP = 2
SB_SPLIT_TERMS = 1
CUMSUM_SEG = 256


def _cparams(*sem):
    return pltpu.CompilerParams(dimension_semantics=sem, vmem_limit_bytes=V7X_VMEM_LIMIT_BYTES)


def _const_spec(shape):
    n = len(shape)
    return pl.BlockSpec(shape, lambda *_: (0,) * n, pipeline_mode=pl.Buffered(1))


def _rms(x, g):
    ms = jnp.mean(x * x, axis=-1, keepdims=True)
    return x * lax.rsqrt(ms + RMS_EPS) * g


def _softplus_tail(z):
    return jnp.log1p(jnp.exp(-jnp.abs(z)))


def _log_sigmoid(z):
    return jnp.minimum(z, 0.0) - _softplus_tail(z)


def _softplus(z):
    return jnp.maximum(z, 0.0) + _softplus_tail(z)


def _dot_nt(a, b):
    return lax.dot_general(a, b, (((1,), (1,)), ((), ())), preferred_element_type=F32)


def _dot(a, b):
    return jnp.dot(a, b, preferred_element_type=F32)


def _dot_exact(a, b):
    return jnp.dot(a, b, preferred_element_type=F32, precision=lax.Precision.HIGHEST)


def _store_kv(ref, a, transposed):
    if not transposed:
        ref[...] = a
    elif len(ref.shape) == 2:
        ref[...] = a.T
    else:
        ref[0] = a.T
        ref[1:] = jnp.zeros((ref.shape[0] - 1,) + ref.shape[1:], ref.dtype)


def _inproj_kernel(x_ref, g_ref, wm_ref, ws_ref, *rest, kv_transposed, n_aliased):
    (qsb_ref, qfx_ref, ksb_ref, vsb_ref, kfx_ref, vfx_ref, ksbh_ref, vsbh_ref, kfxh_ref, vfxh_ref,
     z_ref, xbc_ref, small_ref, gate_ref) = rest[n_aliased:]
    h = _rms(x_ref[...], g_ref[...]).astype(BF16)

    def mm(c0, width):
        return _dot_nt(h, wm_ref[c0:c0 + width, :])

    w = ATT_WIDTH
    qsb_ref[...] = (mm(0, w) * (Q_SCALE * LOG2_E)).astype(BF16)
    for i, (full_ref, half_ref) in enumerate(((ksb_ref, ksbh_ref), (vsb_ref, vsbh_ref))):
        a = mm((1 + i) * w, w)
        _store_kv(full_ref, a, kv_transposed)
        half_ref[...] = a.astype(BF16)
    qfx_ref[...] = (mm(3 * w, w) * (Q_SCALE * LOG2_E)).astype(BF16)
    for i, (full_ref, half_ref) in enumerate(((kfx_ref, kfxh_ref), (vfx_ref, vfxh_ref))):
        a = mm((4 + i) * w, w)
        _store_kv(full_ref, a, kv_transposed)
        half_ref[...] = a.astype(BF16)
    z_ref[...] = mm(6 * w, SSM_INNER)
    c0 = 6 * w + SSM_INNER
    for c in range(SSM_CONV_DIM // 512):
        xbc_ref[:, c * 512:(c + 1) * 512] = mm(c0 + c * 512, 512)
    c0 += SSM_CONV_DIM
    for c in range(N_BRANCH * D_MODEL // 512):
        gate_ref[:, c * 512:(c + 1) * 512] = jax.nn.sigmoid(mm(c0 + c * 512, 512)).astype(BF16)
    small_ref[...] = _dot_nt(h, ws_ref[...])


_HBM_SPEC = pl.BlockSpec(memory_space=pl.ANY)


def _inproj(x2d, g, w_main, w_small, kv_stack=None):
    n = x2d.shape[0]
    tm = min(512, n)
    row = lambda width: pl.BlockSpec((tm, width), lambda i: (i, 0))
    rows = lambda width, dt: (row(width), jax.ShapeDtypeStruct((n, width), dt))
    prev = ()
    if kv_stack is None:
        kv = rows(ATT_WIDTH, F32)
    else:
        bsz, t, layer, depth, prev = kv_stack
        prev = () if prev is None else tuple(prev)
        nt = t // tm
        if prev or depth == 1:
            block = pl.BlockSpec((None, None, ATT_WIDTH, tm), lambda i: (layer, i // nt, 0, i % nt))
        else:
            block = pl.BlockSpec((depth, None, ATT_WIDTH, tm), lambda i: (0, i // nt, 0, i % nt))
        kv = (block, jax.ShapeDtypeStruct((depth, bsz, ATT_WIDTH, t), F32))
    outs = ([rows(ATT_WIDTH, BF16)] * 2 + [kv] * 4 + [rows(ATT_WIDTH, BF16)] * 4
            + [rows(SSM_INNER, F32), rows(SSM_CONV_DIM, F32), rows(SMALL_WIDTH, F32),
               rows(N_BRANCH * D_MODEL, BF16)])
    n_in = 4
    return pl.pallas_call(
        functools.partial(_inproj_kernel, kv_transposed=kv_stack is not None, n_aliased=len(prev)),
        grid=(n // tm,),
        in_specs=[row(D_MODEL), _const_spec((1, D_MODEL)),
                  _const_spec((MAIN_WIDTH, D_MODEL)), _const_spec((SMALL_WIDTH, D_MODEL))]
                 + [_HBM_SPEC] * len(prev),
        out_specs=[spec for spec, _ in outs],
        out_shape=[shape for _, shape in outs],
        input_output_aliases={n_in + j: 2 + j for j in range(len(prev))},
        compiler_params=_cparams("parallel"),
        name="inproj",
    )(x2d, g, w_main, w_small, *prev)


def _lane_head(width=ATT_WIDTH):
    return lax.broadcasted_iota(jnp.int32, (1, width), 1) // HEAD_DIM


def _suffix_matrix(seg, terms):
    r = lax.broadcasted_iota(jnp.int32, (terms * seg, seg), 0) % seg
    c = lax.broadcasted_iota(jnp.int32, (terms * seg, seg), 1)
    return jnp.where(r >= c, 1.0, 0.0).astype(BF16)


def _split_bf16(x, terms):
    out = []
    for _ in range(terms - 1):
        head = x.astype(BF16)
        out.append(head)
        x = x - head.astype(F32)
    out.append(x.astype(BF16))
    return jnp.concatenate(out, axis=1)


def _suffix_sums(x, u, carry):
    m, n = x.shape
    seg = u.shape[1]
    n_seg = n // seg
    terms = u.shape[0] // seg
    if n_seg == 1:
        cs = _dot(_split_bf16(x, terms), u)
        return cs + carry, carry + cs[:, 0:1]
    stacked = jnp.concatenate([x[:, s * seg:(s + 1) * seg] for s in range(n_seg)], axis=0)
    cs = _dot(_split_bf16(stacked, terms), u)
    parts = [None] * n_seg
    for s in reversed(range(n_seg)):
        part = cs[s * m:(s + 1) * m, :]
        parts[s] = part + carry
        carry = carry + part[:, 0:1]
    return jnp.concatenate(parts, axis=1), carry


def _stick_block(z2, carry, u, mask):
    drop = jnp.maximum(z2, 0.0) + jnp.log2(1.0 + jnp.exp2(-jnp.abs(z2)))
    if mask is not None:
        drop = jnp.where(mask, drop, 0.0)
    later, carry = _suffix_sums(drop, u, carry)
    w = jnp.exp2(z2 - later)
    if mask is not None:
        w = jnp.where(mask, w, 0.0)
    return w, carry


def _stack_heads(q):
    lane_head = _lane_head()
    return jnp.concatenate([jnp.where(lane_head == h, q, jnp.zeros_like(q)) for h in range(N_HEADS)], axis=0)


def _fold_heads(acc, rows):
    lane_head = _lane_head()
    out = jnp.zeros((rows, ATT_WIDTH), F32)
    for h in range(N_HEADS):
        out = jnp.where(lane_head == h, acc[h * rows:(h + 1) * rows, :], out)
    return out


def _sb_prompt_kernel(q_ref, k_ref, v_ref, o_ref, acc_ref, *, bq, bk):
    assert bq == bk
    i = pl.program_id(1)
    n_seq = q_ref.shape[0]
    rows = N_HEADS * bq
    qs = [_stack_heads(q_ref[s]) for s in range(n_seq)]
    u = _suffix_matrix(CUMSUM_SEG, SB_SPLIT_TERMS)
    qpos = lax.broadcasted_iota(jnp.int32, (rows, bk), 0) % bq
    causal = lax.broadcasted_iota(jnp.int32, (rows, bk), 1) < qpos

    def block(start, width, carries, mask):
        start = pl.multiple_of(start, width)
        out = []
        for s in range(n_seq):
            w, carry = _stick_block(_dot_nt(qs[s], k_ref[s, pl.ds(start, width), :]), carries[s], u, mask)
            pv = _dot(w.astype(BF16), v_ref[s, pl.ds(start, width), :])
            acc_ref[s] = pv if mask is not None else acc_ref[s] + pv
            out.append(carry)
        return tuple(out)

    carries = block(i * bk, bk, (jnp.zeros((rows, 1), F32),) * n_seq, causal)
    carries = lax.fori_loop(0, i % 2, lambda _, c: block((i - 1) * bk, bk, c, None), carries)
    pairs = i // 2
    lax.fori_loop(0, pairs, lambda jj, c: block((pairs - 1 - jj) * 2 * bk, 2 * bk, c, None), carries)
    for s in range(n_seq):
        o_ref[s] = _fold_heads(acc_ref[s], bq).astype(o_ref.dtype)


def _logf_cum_kernel(b_ref, f_ref, logf_ref, cum_ref, *, rows_per_seq):
    logf = _log_sigmoid(f_ref[...] + b_ref[...])
    logf_ref[...] = logf
    n = logf.shape[0]
    r = lax.broadcasted_iota(jnp.int32, (PAGE_SIZE, PAGE_SIZE), 0)
    c = lax.broadcasted_iota(jnp.int32, (PAGE_SIZE, PAGE_SIZE), 1)
    within = _dot_exact(logf, jnp.where(r <= c, 1.0, 0.0).astype(F32))
    totals = jnp.broadcast_to(within[:, PAGE_SIZE - 1:PAGE_SIZE], within.shape)
    rr = lax.broadcasted_iota(jnp.int32, (n, n), 0)
    cc = lax.broadcasted_iota(jnp.int32, (n, n), 1)
    earlier_rows = jnp.where(cc // rows_per_seq == rr // rows_per_seq, jnp.where(cc < rr, 1.0, 0.0), 0.0)
    cum_ref[...] = within + _dot_exact(earlier_rows, totals)


def _logf_cum(f_rows, b_rows, rows_per_seq):
    spec = pl.BlockSpec(f_rows.shape, lambda i: (0, 0))
    return pl.pallas_call(
        functools.partial(_logf_cum_kernel, rows_per_seq=rows_per_seq),
        grid=(1,),
        in_specs=[pl.BlockSpec(b_rows.shape, lambda i: (0, 0)), spec],
        out_specs=[spec, spec],
        out_shape=[jax.ShapeDtypeStruct(f_rows.shape, F32)] * 2,
        compiler_params=_cparams("arbitrary"),
        name="logf_cum",
    )(b_rows, f_rows)


def _fox_prompt_kernel(q_ref, k_ref, v_ref, cq_ref, ck_ref, o_ref, acc_ref, *, bq, bk):
    assert bq == bk
    blk = bq
    i = pl.program_id(1)
    n_seq = q_ref.shape[0]
    rows = N_HEADS * blk
    qs = [_stack_heads(q_ref[s]) for s in range(n_seq)]
    cqs = [jnp.concatenate([cq_ref[s, :, h:h + 1] for h in range(N_HEADS)], axis=0) for s in range(n_seq)]
    qpos = lax.broadcasted_iota(jnp.int32, (rows, blk), 0) % blk
    causal = lax.broadcasted_iota(jnp.int32, (rows, blk), 1) <= qpos

    def block(j, states, mask):
        start = pl.multiple_of(j * blk, blk)
        out = []
        for s in range(n_seq):
            sc = _dot_nt(qs[s], k_ref[s, pl.ds(start, blk), :]) + cqs[s]
            ck = ck_ref[s, j]
            sc = jnp.concatenate([sc[h * blk:(h + 1) * blk, :] - ck[h:h + 1, :] for h in range(N_HEADS)], axis=0)
            if mask is not None:
                sc = jnp.where(mask, sc, NEG_INF)
            m = jnp.max(sc, axis=-1, keepdims=True)
            if states is not None:
                m_old, l_old = states[s]
                m = jnp.maximum(m_old, m)
            p = jnp.exp2(sc - m)
            pv = _dot(p.astype(BF16), v_ref[s, pl.ds(start, blk), :])
            l = jnp.sum(p, axis=-1, keepdims=True)
            if states is None:
                acc_ref[s] = pv
            else:
                alpha = jnp.exp2(m_old - m)
                acc_ref[s] = acc_ref[s] * alpha + pv
                l = alpha * l_old + l
            out.append((m, l))
        return tuple(out)

    states = block(i, None, causal)
    states = lax.fori_loop(0, i, lambda jj, st: block(i - 1 - jj, st, None), states)
    for s in range(n_seq):
        o_ref[s] = _fold_heads(acc_ref[s] / states[s][1], blk).astype(o_ref.dtype)


_XBUF_ROW0 = 8


def _ssm_kernel(z_ref, xbc_ref, small_ref, conv0_ref, state0_ref,
                cw_ref, cb_ref, dtb_ref, alog_ref, dskip_ref, ng_ref, *rest, valid, n_aliased):
    y_ref, state_out_ref, conv_out_ref, xbuf_ref, state_ref = rest[n_aliased:]
    c = pl.program_id(1)
    n_seq = xbc_ref.shape[0]
    r0 = _XBUF_ROW0

    @pl.when(c == 0)
    def _():
        xbuf_ref[...] = jnp.zeros_like(xbuf_ref)
        xbuf_ref[:, r0 - 3:r0, :] = conv0_ref[...]
        state_ref[...] = state0_ref[...]

    x_curs = [_ssm_chunk(z_ref.at[s], xbc_ref.at[s], small_ref.at[s], cw_ref, cb_ref, dtb_ref, alog_ref,
                         dskip_ref, ng_ref, y_ref.at[s], xbuf_ref.at[s], state_ref.at[s], valid)
              for s in range(n_seq)]

    @pl.when(c == pl.num_programs(1) - 1)
    def _():
        if len(state_out_ref.shape) == len(state_ref.shape):
            state_out_ref[...] = state_ref[...]
        else:
            state_out_ref[0] = state_ref[...]
            state_out_ref[1:] = jnp.zeros((state_out_ref.shape[0] - 1,) + state_ref.shape, state_ref.dtype)
        for s in range(n_seq):
            conv_out_ref[s] = x_curs[s][valid - 3:valid, :]


def _ssm_chunk(z_ref, xbc_ref, small_ref, cw_ref, cb_ref, dtb_ref, alog_ref, dskip_ref, ng_ref,
               y_ref, xbuf_ref, state_ref, valid):
    L = xbc_ref.shape[0]
    S = SSM_CHUNK
    r0 = _XBUF_ROW0

    def pad_time(a):
        return a if L == S else jnp.concatenate([a, jnp.zeros((S - L,) + a.shape[1:], a.dtype)], axis=0)

    x_cur = xbc_ref[...]
    window = jnp.concatenate([xbuf_ref[...], x_cur], axis=0)
    cw = cw_ref[...]
    conv = cb_ref[...] + x_cur * cw[3:4, :]
    for i in range(SSM_CONV - 1):
        conv = conv + pltpu.roll(window, 3 - i, 0)[r0:, :] * cw[i:i + 1, :]
    xbuf_ref[...] = x_cur[L - r0:L, :]
    act = conv * jax.nn.sigmoid(conv)
    xs = act[:, :SSM_INNER]
    b_in = act[:, SSM_INNER:SSM_INNER + SSM_GROUPS * SSM_STATE].astype(BF16)
    c_in = act[:, SSM_INNER + SSM_GROUPS * SSM_STATE:].astype(BF16)

    row = lax.broadcasted_iota(jnp.int32, (L, S), 0)
    col = lax.broadcasted_iota(jnp.int32, (L, S), 1)
    tri = row >= col
    dt = _softplus(small_ref[...] + dtb_ref[...])
    if valid < L:
        dt = jnp.where(lax.broadcasted_iota(jnp.int32, dt.shape, 0) < valid, dt, 0.0)
    d_a = dt * (-jnp.exp(alog_ref[...]))
    a_cs = _dot_exact(jnp.where(tri[:, :L], 1.0, 0.0).astype(F32), d_a)
    a_cs_t = pad_time(a_cs).T
    a_last = a_cs[L - 1:L, :]
    e_cs = jnp.exp(a_cs)
    wgt = jnp.exp(a_last - a_cs) * dt
    chunk_dec = jnp.exp(a_last)

    half = lax.broadcasted_iota(jnp.int32, (1, 128), 1) // SSM_STATE_HALF
    rhalf = lax.broadcasted_iota(jnp.int32, (128, 1), 0) // SSM_STATE_HALF
    pair_cols = lambda a, p: jnp.where(half == 0, a[:, DT_LANE + 2 * p:DT_LANE + 2 * p + 1],
                                       a[:, DT_LANE + 2 * p + 1:DT_LANE + 2 * p + 2])
    b_keys = pad_time(b_in)
    scores = [_dot_nt(c_in[:, g * SSM_STATE:(g + 1) * SSM_STATE],
                      b_keys[:, g * SSM_STATE:(g + 1) * SSM_STATE]) for g in range(SSM_GROUPS)]
    ys = []
    for p in range(SSM_HEADS // 2):
        g = (2 * p) // (SSM_HEADS // SSM_GROUPS)
        bg = b_keys[:, g * SSM_STATE:(g + 1) * SSM_STATE]
        cg = c_in[:, g * SSM_STATE:(g + 1) * SSM_STATE]
        xs_p = xs[:, 128 * p:128 * (p + 1)]
        xdt = pad_time((xs_p * pair_cols(dt, p)).astype(BF16))
        y_diag = jnp.zeros((L, 128), F32)
        for hh in range(2):
            lane = DT_LANE + 2 * p + hh
            seg = a_cs[:, lane:lane + 1] - a_cs_t[lane:lane + 1, :]
            decay = jnp.exp(jnp.where(tri, seg, NEG_INF))
            y_h = _dot((scores[g] * decay).astype(BF16), xdt)
            y_diag = jnp.where(half == hh, y_h, y_diag)
        st = state_ref[128 * p:128 * (p + 1), :]
        y_off = _dot_nt(cg, st.astype(BF16)) * pair_cols(e_cs, p)
        ys.append(y_diag + y_off + dskip_ref[:, 128 * p:128 * (p + 1)] * xs_p)
        xw_t = pad_time(xs_p * pair_cols(wgt, p)).T.astype(BF16)
        lane = DT_LANE + 2 * p
        dec = jnp.where(rhalf == 0,
                        jnp.broadcast_to(chunk_dec[:, lane:lane + 1], (128, SSM_STATE)),
                        jnp.broadcast_to(chunk_dec[:, lane + 1:lane + 2], (128, SSM_STATE)))
        state_ref[128 * p:128 * (p + 1), :] = st * dec + _dot(xw_t, bg)

    z = z_ref[...]
    y = jnp.concatenate(ys, axis=1) * (z * jax.nn.sigmoid(z))
    gw = SSM_INNER // SSM_GROUPS
    parts = []
    for g in range(SSM_GROUPS):
        yg = y[:, g * gw:(g + 1) * gw]
        parts.append(yg * lax.rsqrt(jnp.mean(yg * yg, axis=-1, keepdims=True) + RMS_EPS))
    y_ref[...] = (jnp.concatenate(parts, axis=1) * ng_ref[...]).astype(y_ref.dtype)
    return x_cur


SSM_STATE_HALF = 64
SSM_SHORT_ROWS = 16
SSM_SEQS_PER_STEP = 4


def _ssm(z, xbc, small, conv0, state0, cw, cb, dtb, alog, dskip, ng, valid, layer, depth, prev):
    b, t, _ = z.shape
    rows = min(t, SSM_CHUNK)
    nc = t // rows
    par = SSM_SEQS_PER_STEP if b % SSM_SEQS_PER_STEP == 0 else 1
    chunk = lambda width: pl.BlockSpec((par, rows, width), lambda bi, c: (bi, c, 0))
    per_b = lambda shape: pl.BlockSpec((par,) + shape, lambda bi, c: (bi,) + (0,) * len(shape))
    const = lambda shape: pl.BlockSpec(shape, lambda bi, c: (0,) * len(shape))
    state_rows = SSM_INNER
    prev = () if prev is None else (prev,)
    if prev or depth == 1:
        state_out = pl.BlockSpec((None, par, state_rows, SSM_STATE), lambda bi, c: (layer, bi, 0, 0))
    else:
        state_out = pl.BlockSpec((depth, par, state_rows, SSM_STATE), lambda bi, c: (0, bi, 0, 0))
    n_in = 11
    return pl.pallas_call(
        functools.partial(_ssm_kernel, valid=valid, n_aliased=len(prev)),
        grid=(b // par, nc),
        in_specs=[chunk(SSM_INNER), chunk(SSM_CONV_DIM), chunk(SMALL_WIDTH),
                  per_b((SSM_CONV - 1, SSM_CONV_DIM)), per_b((state_rows, SSM_STATE)),
                  const((SSM_CONV, SSM_CONV_DIM)), const((1, SSM_CONV_DIM)),
                  const((1, SMALL_WIDTH)), const((1, SMALL_WIDTH)),
                  const((1, SSM_INNER)), const((1, SSM_INNER))] + [_HBM_SPEC] * len(prev),
        out_specs=[chunk(SSM_INNER), state_out, per_b((SSM_CONV - 1, SSM_CONV_DIM))],
        out_shape=[jax.ShapeDtypeStruct((b, t, SSM_INNER), BF16),
                   jax.ShapeDtypeStruct((depth, b, state_rows, SSM_STATE), F32),
                   jax.ShapeDtypeStruct((b, SSM_CONV - 1, SSM_CONV_DIM), F32)],
        input_output_aliases={n_in + j: 1 + j for j in range(len(prev))},
        scratch_shapes=[pltpu.VMEM((par, _XBUF_ROW0, SSM_CONV_DIM), F32),
                        pltpu.VMEM((par, state_rows, SSM_STATE), F32)],
        compiler_params=_cparams("parallel", "arbitrary"),
        name="ssm",
    )(z, xbc, small, conv0, state0, cw, cb, dtb, alog, dskip, ng, *prev)


def _merge_ffn_kernel(x_ref, ysb_ref, yssm_ref, yfx_ref, gate_ref,
                      wsb_ref, wssm_ref, wfx_ref, wo_ref, g2_ref, wup_ref, wdn_ref, gf_ref,
                      o_ref, *, final):
    d = D_MODEL
    mixed = gate_ref[:, 0:d].astype(F32) * _dot(ysb_ref[...].astype(BF16), wsb_ref[...])
    mixed = mixed + gate_ref[:, d:2 * d].astype(F32) * _dot(yssm_ref[...].astype(BF16), wssm_ref[...])
    mixed = mixed + gate_ref[:, 2 * d:3 * d].astype(F32) * _dot(yfx_ref[...].astype(BF16), wfx_ref[...])
    x = x_ref[...] + _dot(mixed.astype(BF16), wo_ref[...])
    h = _rms(x, g2_ref[...]).astype(BF16)
    hc = FFN_HIDDEN // 2
    for c in range(2):
        u = jnp.maximum(_dot(h, wup_ref[:, c * hc:(c + 1) * hc]), 0.0)
        x = x + _dot((u * u).astype(BF16), wdn_ref[c * hc:(c + 1) * hc, :])
    if final:
        x = _rms(x, gf_ref[...])
    o_ref[...] = x


def _merge_ffn(x2d, y_sb, y_ssm, y_fx, gates, lw, gf, final):
    n = x2d.shape[0]
    tm = min(512, n)
    row = lambda width: pl.BlockSpec((tm, width), lambda i: (i, 0))
    weights = [lw['w_sb_out'], lw['w_ssm_out'], lw['w_fox_out'], lw['w_o'], lw['norm2_g'],
               lw['w_up'], lw['w_down'], gf]
    return pl.pallas_call(
        functools.partial(_merge_ffn_kernel, final=final),
        grid=(n // tm,),
        in_specs=[row(D_MODEL), row(ATT_WIDTH), row(SSM_INNER), row(ATT_WIDTH),
                  row(N_BRANCH * D_MODEL)] + [_const_spec(w.shape) for w in weights],
        out_specs=row(D_MODEL),
        out_shape=jax.ShapeDtypeStruct((n, D_MODEL), F32),
        compiler_params=_cparams("parallel"),
        name="merge_ffn",
    )(x2d, y_sb, y_ssm, y_fx, gates, *weights)


def _block_diag_q(q_ref):
    q = q_ref[...]
    rows = q.shape[0]
    row_head = lax.broadcasted_iota(jnp.int32, (rows, 1), 0) // (rows // N_HEADS)
    return jnp.where(row_head == _lane_head(), q, jnp.zeros_like(q))


def _page_copy(cache_ref, stage_ref, sem_ref, layer, page, slot, p):
    return pltpu.make_async_copy(cache_ref.at[layer, page], stage_ref.at[slot, p], sem_ref.at[slot])


def _grid_step():
    return (pl.program_id(0) * pl.num_programs(1) + pl.program_id(1),
            pl.num_programs(0) * pl.num_programs(1))


def _prefetch_pages(pt_ref, caches, stages, sems, layer, group, n_chunks):
    n_pages = stages[0].shape[1]
    per_seq = n_pages // group
    step, n_steps = _grid_step()
    b, c = step // n_chunks, step % n_chunks
    slot = step % 2

    def start(bb, cc, sl):
        base = (n_chunks - 1 - cc) * per_seq
        for p in range(n_pages):
            page = pt_ref[bb * group + p // per_seq, base + p % per_seq]
            for i, (cache, stage, sem) in enumerate(zip(caches, stages, sems)):
                _page_copy(cache, stage, sem, layer, page, sl, p).start(priority=i % 2)

    @pl.when(step == 0)
    def _():
        start(b, c, slot)

    @pl.when(step + 1 < n_steps)
    def _():
        wrap = c + 1 == n_chunks
        start(jnp.where(wrap, b + 1, b), jnp.where(wrap, 0, c + 1), 1 - slot)

    for p in range(n_pages):
        for cache, stage, sem in zip(caches, stages, sems):
            _page_copy(cache, stage, sem, layer, 0, slot, p).wait()
    return slot


def _seq_pages(stage_ref, slot, g, group):
    per_seq = stage_ref.shape[1] // group
    return jnp.concatenate([stage_ref[slot, g * per_seq + p] for p in range(per_seq)], axis=1)


def _sb_sample_kernel(pt_ref, q_ref, kn_ref, vn_ref, ck_ref, cv_ref, o_ref,
                      acc_ref, carry_ref, kstage_ref, vstage_ref, ksem, vsem,
                      *, steps, layer, n_chunks):
    group = q_ref.shape[0]
    slot = _prefetch_pages(pt_ref, (ck_ref, cv_ref), (kstage_ref, vstage_ref), (ksem, vsem), layer, group,
                           n_chunks)
    c = _grid_step()[0] % n_chunks
    rows = N_HEADS * steps
    qbd = [_block_diag_q(q_ref.at[g]) for g in range(group)]
    u = _suffix_matrix(CUMSUM_SEG, SB_SPLIT_TERMS)

    @pl.when(c == 0)
    def _():
        step = lax.broadcasted_iota(jnp.int32, (rows, PAGE_SIZE), 0) % steps
        col = lax.broadcasted_iota(jnp.int32, (rows, PAGE_SIZE), 1)
        for g in range(group):
            w, carry = _stick_block(_dot_nt(qbd[g], kn_ref[g]), jnp.zeros((rows, 1), F32),
                                    _suffix_matrix(PAGE_SIZE, SB_SPLIT_TERMS), col < step)
            acc_ref[g] = _dot(w.astype(BF16), vn_ref[g])
            carry_ref[g] = jnp.broadcast_to(carry, carry_ref.shape[1:])

    for g in range(group):
        scores = _dot(qbd[g].astype(F32), _seq_pages(kstage_ref, slot, g, group))
        w, carry = _stick_block(scores, carry_ref[g, :, 0:1], u, None)
        acc_ref[g] += _dot_nt(w, _seq_pages(vstage_ref, slot, g, group))
        carry_ref[g] = jnp.broadcast_to(carry, carry_ref.shape[1:])

    def finish():
        @pl.when(c == n_chunks - 1)
        def _():
            for g in range(group):
                o_ref[g] = _fold_heads(acc_ref[g], steps)
    return finish


def _page_staging(n_pages, rows, lanes):
    return [pltpu.VMEM((2, n_pages, rows, lanes), F32)], [pltpu.SemaphoreType.DMA((2,))]


def _fox_sample_kernel(pt_ref, q_ref, kn_ref, vn_ref, f_ref, b_ref, ck_ref, cv_ref, clf_ref,
                       o_ref, lf_out_ref, acc_ref, m_ref, l_ref, ncum_ref, rcarry_ref,
                       kstage_ref, vstage_ref, lstage_ref, ksem, vsem, lsem,
                       lbuf_ref, *, steps, layer, n_chunks):
    group = q_ref.shape[0]
    slot = _prefetch_pages(pt_ref, (ck_ref, cv_ref, clf_ref), (kstage_ref, vstage_ref, lstage_ref),
                           (ksem, vsem, lsem), layer, group, n_chunks)
    c = _grid_step()[0] % n_chunks
    rows = N_HEADS * steps
    qbd = [_block_diag_q(q_ref.at[g]) for g in range(group)]
    stat = lambda a: jnp.broadcast_to(a, m_ref.shape[1:])

    @pl.when(c == 0)
    def _():
        step = lax.broadcasted_iota(jnp.int32, (rows, PAGE_SIZE), 0) % steps
        col = lax.broadcasted_iota(jnp.int32, (rows, PAGE_SIZE), 1)
        visible = col <= step
        lbuf_ref[...] = jnp.zeros_like(lbuf_ref)
        rcarry_ref[...] = jnp.zeros_like(rcarry_ref)
        for g in range(group):
            lf = _log_sigmoid(f_ref[g] + b_ref[...])
            lf_out_ref[g] = lf
            lf = jnp.where(col < steps, lf, 0.0)
            cum = lf
            shift = 1
            while shift < steps:
                cum = cum + jnp.where(col >= shift, pltpu.roll(cum, shift, 1), 0.0)
                shift *= 2
            ncum = jnp.sum(jnp.where(visible, lf, 0.0), axis=-1, keepdims=True)
            s = jnp.where(visible, _dot_nt(qbd[g], kn_ref[g]) + (ncum - cum) * LOG2_E, NEG_INF)
            m = jnp.max(s, axis=-1, keepdims=True)
            p = jnp.exp2(s - m)
            m_ref[g] = stat(m)
            l_ref[g] = stat(jnp.sum(p, axis=-1, keepdims=True))
            acc_ref[g] = _dot(p.astype(BF16), vn_ref[g])
            ncum_ref[g] = stat(ncum)

    per_seq = lbuf_ref.shape[-1] // PAGE_SIZE
    u3 = _suffix_matrix(CUMSUM_SEG, 3)
    for g in range(group):
        for p in range(per_seq):
            lbuf_ref[g, 0:N_HEADS, p * PAGE_SIZE:(p + 1) * PAGE_SIZE] = lstage_ref[slot, g * per_seq + p]
        lf_pages = lbuf_ref[g]
        incl, carry = _suffix_sums(lf_pages, u3, rcarry_ref[g, :, 0:1])
        suffix = incl - lf_pages
        rcarry_ref[g] = jnp.broadcast_to(carry, rcarry_ref.shape[1:])
        bias = jnp.concatenate(
            [jnp.broadcast_to(suffix[h:h + 1, :], (steps, suffix.shape[1])) for h in range(N_HEADS)], axis=0)
        s = (_dot(qbd[g].astype(F32), _seq_pages(kstage_ref, slot, g, group))
             + (bias + ncum_ref[g, :, 0:1]) * LOG2_E)
        m_old = m_ref[g, :, 0:1]
        m = jnp.maximum(m_old, jnp.max(s, axis=-1, keepdims=True))
        alpha = jnp.exp2(m_old - m)
        p = jnp.exp2(s - m)
        l_ref[g] = stat(alpha * l_ref[g, :, 0:1] + jnp.sum(p, axis=-1, keepdims=True))
        acc_ref[g] = acc_ref[g] * alpha + _dot_nt(p, _seq_pages(vstage_ref, slot, g, group))
        m_ref[g] = stat(m)

    def finish():
        @pl.when(c == n_chunks - 1)
        def _():
            for g in range(group):
                o_ref[g] = _fold_heads(acc_ref[g] / l_ref[g, :, 0:1], steps)
    return finish


def _both_groups_kernel(pt_ref, *refs, prompt_kernel, sample_kernel, counts):
    parts, at = [], 0
    for n in counts:
        parts.append(refs[at:at + n])
        at += n
    p_in, s_in, p_out, s_out, p_scr, s_scr = parts
    finish_sample = sample_kernel(pt_ref, *s_in, *s_out, *s_scr)
    prompt_kernel(*p_in, *p_out, *p_scr)
    finish_sample()


def _attention_both_groups(name, layer, page_table, prompt_kernel, prompt_in, prompt_extra_specs,
                           sample_kernel, sample_in, sample_extra_specs, caches, sample_extra_out,
                           sample_extra_scratch, steps):
    q, k, v = prompt_in[:3]
    b, t, w = q.shape
    blk = ATT_BLOCK
    nq = t // blk
    pb = PROMPT_SEQS_PER_STEP if b % PROMPT_SEQS_PER_STEP == 0 else 1
    q_rows = sample_in[0]
    n_seq, rows, _ = q_rows.shape
    n_pages = PAGES_PER_STEP
    n_chunks = page_table.shape[1] // n_pages
    steps_total = (b // pb) * nq
    g = n_seq * n_chunks // steps_total
    assert g >= 1 and g * steps_total == n_seq * n_chunks, "the two groups must split into equally many grid steps"
    seq_group = lambda bi, i: (bi * nq + i) // n_chunks

    qspec = pl.BlockSpec((pb, blk, w), lambda bi, i, pt: (bi, i, 0))
    kvspec = pl.BlockSpec((pb, t, w), lambda bi, i, pt: (bi, 0, 0), pipeline_mode=pl.Buffered(1))
    per_g = lambda r, width=w: pl.BlockSpec((g, r, width), lambda bi, i, pt: (seq_group(bi, i), 0, 0))
    prompt_specs = [qspec, kvspec, kvspec] + prompt_extra_specs(pb, t, blk)
    sample_specs = ([per_g(rows), per_g(PAGE_SIZE), per_g(PAGE_SIZE)] + sample_extra_specs(per_g, rows)
                    + [_HBM_SPEC] * len(caches))
    stat = pltpu.VMEM((g, rows, 128), F32)
    stages, sems = [], []
    for cache in caches:
        st, se = _page_staging(g * n_pages, cache.shape[2], PAGE_SIZE)
        stages += st
        sems += se
    prompt_scratch = [pltpu.VMEM((pb, N_HEADS * blk, w), F32)]
    sample_scratch = [pltpu.VMEM((g, rows, w), F32)] + sample_extra_scratch(g, stat) + stages + sems
    if len(caches) == 3:
        sample_scratch.append(pltpu.VMEM((g, 8, n_pages * PAGE_SIZE), F32))
    out_specs = [qspec, per_g(steps)] + [per_g(rows, PAGE_SIZE)] * len(sample_extra_out)
    out_shape = ([jax.ShapeDtypeStruct((b, t, w), BF16), jax.ShapeDtypeStruct((n_seq, steps, w), F32)]
                 + list(sample_extra_out))
    counts = (len(prompt_specs), len(sample_specs), 1, len(out_specs) - 1,
              len(prompt_scratch), len(sample_scratch))
    grid_spec = pltpu.PrefetchScalarGridSpec(
        num_scalar_prefetch=1,
        grid=(b // pb, nq),
        in_specs=prompt_specs + sample_specs,
        out_specs=out_specs,
        scratch_shapes=prompt_scratch + sample_scratch,
    )
    return pl.pallas_call(
        functools.partial(_both_groups_kernel, counts=counts,
                          prompt_kernel=functools.partial(prompt_kernel, bq=blk, bk=blk),
                          sample_kernel=functools.partial(sample_kernel, steps=steps, layer=layer,
                                                          n_chunks=n_chunks)),
        grid_spec=grid_spec,
        out_shape=out_shape,
        compiler_params=_cparams("arbitrary", "arbitrary"),
        name=name,
    )(page_table, *prompt_in, *sample_in, *caches)


def _sb_attention(layer, page_table, prompt_in, sample_in, caches, steps):
    none = lambda *_: []
    return _attention_both_groups(
        "sb_attention", layer, page_table, _sb_prompt_kernel, prompt_in, none,
        _sb_sample_kernel, sample_in, none, caches, [], lambda g, stat: [stat], steps)


def _fox_attention(layer, page_table, prompt_in, sample_in, caches, steps):
    n_seq, rows, _ = sample_in[0].shape
    prompt_extra = lambda pb, t, blk: [
        pl.BlockSpec((pb, blk, N_HEADS), lambda bi, i, pt: (bi, i, 0)),
        pl.BlockSpec((pb, t // blk, N_HEADS, blk), lambda bi, i, pt: (bi, 0, 0, 0))]
    sample_extra = lambda per_g, rows: [per_g(rows, PAGE_SIZE),
                                        pl.BlockSpec((rows, PAGE_SIZE), lambda bi, i, pt: (0, 0))]
    return _attention_both_groups(
        "fox_attention", layer, page_table, _fox_prompt_kernel, prompt_in, prompt_extra,
        _fox_sample_kernel, sample_in, sample_extra, caches,
        [jax.ShapeDtypeStruct((n_seq, rows, PAGE_SIZE), F32)],
        lambda g, stat: [stat, stat, stat, pltpu.VMEM((g, 8, 128), F32)], steps)


def _layer_weights(l, norm1_g, w_in, b_forget, conv_w, conv_b, dt_bias, a_log, d_skip, ssm_norm_g,
                   w_sb_out, w_ssm_out, w_fox_out, w_o, norm2_g, w_up, w_down):
    w = jnp.transpose(w_in, (2, 0, 1))[:, l, :]
    w_main = jnp.concatenate([w[:_OFF_F], w[_OFF_Z:_OFF_DT], w[_OFF_GATE:]], axis=0).astype(BF16)
    w_small = jnp.concatenate([w[_OFF_F:_OFF_Z], w[_OFF_DT:_OFF_GATE]], axis=0)
    w_small = jnp.pad(w_small, ((0, SMALL_WIDTH - w_small.shape[0]), (0, 0))).astype(BF16)
    pad_dt = lambda a: jnp.pad(a, (DT_LANE, SMALL_WIDTH - DT_LANE - SSM_HEADS)).reshape(1, SMALL_WIDTH)
    return {
        'norm1_g': norm1_g[l].reshape(1, D_MODEL), 'w_main': w_main, 'w_small': w_small,
        'b_forget': b_forget[l], 'conv_w': conv_w[l], 'conv_b': conv_b[l].reshape(1, SSM_CONV_DIM),
        'dt_bias': pad_dt(dt_bias[l]), 'a_log': pad_dt(a_log[l]),
        'd_skip': jnp.repeat(d_skip[l], SSM_INNER // SSM_HEADS).reshape(1, SSM_INNER),
        'ssm_norm_g': ssm_norm_g[l].reshape(1, SSM_INNER),
        'w_sb_out': w_sb_out[l].astype(BF16), 'w_ssm_out': w_ssm_out[l].astype(BF16),
        'w_fox_out': w_fox_out[l].astype(BF16), 'w_o': w_o[l].astype(BF16),
        'norm2_g': norm2_g[l].reshape(1, D_MODEL),
        'w_up': w_up[l].astype(BF16), 'w_down': w_down[l].astype(BF16),
    }


def _to_heads(a, bsz, t):
    return a.reshape(bsz, t, N_HEADS, HEAD_DIM)


def _layer(xp, xs, l, depth, lw, gf, stacks, caches, state_ssm, state_conv, page_table):
    kv_prev, ssm_prev, sssm_prev = (None, None, None) if stacks is None else stacks
    final = l == depth - 1
    cache_sb_k, cache_sb_v, cache_fox_k, cache_fox_v, cache_lf_t = caches
    bsz, t, d = xp.shape
    n = bsz * t
    (q_sb, q_fx, k_sb, v_sb, k_fx, v_fx, k_sb_h, v_sb_h, k_fx_h, v_fx_h,
     z, xbc, small, gates) = _inproj(xp.reshape(n, d), lw['norm1_g'], lw['w_main'], lw['w_small'],
                                     (bsz, t, l, depth, kv_prev))
    sn, st, _ = xs.shape
    m = sn * st
    (sq_sb, sq_fx, sk_sb, sv_sb, sk_fx, sv_fx, sk_sb_h, sv_sb_h, sk_fx_h, sv_fx_h,
     sz, sxbc, ssmall, sgates) = _inproj(xs.reshape(m, d), lw['norm1_g'], lw['w_main'], lw['w_small'])

    b3 = lambda a: a.reshape(bsz, t, a.shape[-1])
    s3 = lambda a: a.reshape(sn, st, a.shape[-1])
    q_rows = lambda q: jnp.tile(s3(q), (1, N_HEADS, 1))
    pad_keys = lambda a: jnp.pad(s3(a), ((0, 0), (0, PAGE_SIZE - st), (0, 0)))
    y_sb, sy_sb = _sb_attention(l, page_table, (b3(q_sb), b3(k_sb_h), b3(v_sb_h)),
                                (q_rows(sq_sb), pad_keys(sk_sb_h), pad_keys(sv_sb_h)),
                                (cache_sb_k, cache_sb_v), st)

    f_rows = small[:, :N_HEADS].reshape(bsz, t, N_HEADS).transpose(0, 2, 1)
    rows_per_seq = t // PAGE_SIZE
    f_rows = f_rows.reshape(bsz * N_HEADS * rows_per_seq, PAGE_SIZE)
    bias_rows = jnp.tile(jnp.repeat(lw['b_forget'], rows_per_seq), bsz)[:, None]
    logf, cum = _logf_cum(f_rows, bias_rows, rows_per_seq)
    logf = logf.reshape(bsz, N_HEADS, t).transpose(0, 2, 1)
    blk = ATT_BLOCK
    cum2 = cum * LOG2_E
    cum_col = cum2.reshape(bsz, N_HEADS, t).transpose(0, 2, 1)
    cum_row = cum2.reshape(bsz, N_HEADS, t // blk, blk).transpose(0, 2, 1, 3)
    f_new = ssmall[:, :N_HEADS].reshape(sn, st, N_HEADS).transpose(0, 2, 1)
    sf_rows = jnp.pad(jnp.repeat(f_new, st, axis=1), ((0, 0), (0, 0), (0, PAGE_SIZE - st)))
    b_rows = jnp.broadcast_to(jnp.repeat(lw['b_forget'], st)[:, None], (N_HEADS * st, PAGE_SIZE))
    y_fx, sy_fx, lf_rows = _fox_attention(
        l, page_table, (b3(q_fx), b3(k_fx_h), b3(v_fx_h), cum_col, cum_row),
        (q_rows(sq_fx), pad_keys(sk_fx_h), pad_keys(sv_fx_h), sf_rows, b_rows),
        (cache_fox_k, cache_fox_v, cache_lf_t), st)
    slogf = lf_rows[:, ::st, :st].transpose(0, 2, 1)

    conv0 = jnp.zeros((bsz, SSM_CONV - 1, SSM_CONV_DIM), F32)
    ssm0 = jnp.zeros((bsz, SSM_INNER, SSM_STATE), F32)
    ssm_w = (lw['conv_w'], lw['conv_b'], lw['dt_bias'], lw['a_log'], lw['d_skip'], lw['ssm_norm_g'])
    y_ssm, ssm_stack, conv_new = _ssm(b3(z), b3(xbc), b3(small), conv0, ssm0, *ssm_w, SSM_CHUNK,
                                      l, depth, ssm_prev)
    pad_rows = lambda a: jnp.pad(s3(a), ((0, 0), (0, SSM_SHORT_ROWS - st), (0, 0)))
    sy_ssm, sssm_stack, sconv_new = _ssm(pad_rows(sz), pad_rows(sxbc), pad_rows(ssmall), state_conv[l],
                                         state_ssm[l].reshape(sn, SSM_INNER, SSM_STATE), *ssm_w, st,
                                         l, depth, sssm_prev)
    sy_ssm = sy_ssm[:, :st]

    xp_new = _merge_ffn(xp.reshape(n, d), y_sb.reshape(n, -1), y_ssm.reshape(n, -1), y_fx.reshape(n, -1),
                        gates, lw, gf, final)
    xs_new = _merge_ffn(xs.reshape(m, d), sy_sb.reshape(m, -1), sy_ssm.reshape(m, -1),
                        sy_fx.reshape(m, -1), sgates, lw, gf, final)
    prompt_states = (logf, conv_new)
    sample_states = (_to_heads(sk_sb, sn, st), _to_heads(sv_sb, sn, st), _to_heads(sk_fx, sn, st),
                     _to_heads(sv_fx, sn, st), slogf, sconv_new)
    return (xp_new.reshape(bsz, t, d), xs_new.reshape(sn, st, d),
            ((k_sb, v_sb, k_fx, v_fx), ssm_stack, sssm_stack), prompt_states, sample_states)


def kernel(x_prompt, x_sample, cache_sb_k, cache_sb_v, cache_fox_k, cache_fox_v, cache_fox_logf, state_ssm, state_conv, page_table, norm1_g, w_in, b_forget, conv_w, conv_b, dt_bias, a_log, d_skip, ssm_norm_g, w_sb_out, w_ssm_out, w_fox_out, w_o, norm2_g, w_up, w_down, final_norm_g):
    depth = w_in.shape[0]
    assert page_table.shape[1] % PAGES_PER_STEP == 0
    assert x_sample.shape[1] >= SSM_CONV - 1 and x_prompt.shape[1] % ATT_BLOCK == 0
    flat = lambda c: c.transpose(0, 1, 3, 4, 2).reshape(c.shape[0], c.shape[1], ATT_WIDTH, PAGE_SIZE)
    caches = (flat(cache_sb_k), flat(cache_sb_v), flat(cache_fox_k), flat(cache_fox_v),
              cache_fox_logf.transpose(0, 1, 3, 2))
    gf = final_norm_g.reshape(1, D_MODEL)
    xp, xs = x_prompt, x_sample
    prompt_states, sample_states = [], []
    stacks = None
    for l in range(depth):
        lw = _layer_weights(l, norm1_g, w_in, b_forget, conv_w, conv_b, dt_bias, a_log, d_skip,
                            ssm_norm_g, w_sb_out, w_ssm_out, w_fox_out, w_o, norm2_g, w_up, w_down)
        xp, xs, stacks, st_p, st_s = _layer(xp, xs, l, depth, lw, gf, stacks, caches,
                                            state_ssm, state_conv, page_table)
        prompt_states.append(st_p)
        sample_states.append(st_s)
    bsz, t = x_prompt.shape[:2]
    kv_stacks, ssm_stack, sssm_stack = stacks
    from_t = lambda a: a.reshape(depth, bsz, N_HEADS, HEAD_DIM, t).transpose(0, 1, 4, 2, 3)
    ssm_heads = lambda a: a.reshape(depth, a.shape[1], SSM_HEADS, SSM_INNER // SSM_HEADS, SSM_STATE)
    p_logf, p_conv = [jnp.stack(s) for s in zip(*prompt_states)]
    s_sb_k, s_sb_v, s_fox_k, s_fox_v, s_logf, s_conv = [jnp.stack(s) for s in zip(*sample_states)]
    return (xp, xs, *[from_t(a) for a in kv_stacks], p_logf, ssm_heads(ssm_stack), p_conv,
            s_sb_k, s_sb_v, s_fox_k, s_fox_v, s_logf, ssm_heads(sssm_stack), s_conv)
```

```python
import functools

import jax
import jax.numpy as jnp
from jax import lax
from jax.experimental import pallas as pl
from jax.experimental.pallas import tpu as pltpu

F32 = jnp.float32
BF16 = jnp.bfloat16

D_MODEL = 1024
HEAD_DIM = 64
N_HEADS = 4
ATT_WIDTH = N_HEADS * HEAD_DIM
SSM_HEADS = 8
SSM_INNER = 512
SSM_STATE = 128
SSM_GROUPS = 2
SSM_CONV = 4
SSM_CONV_DIM = 1024
SSM_CHUNK = 128
PAGE_SIZE = 128
N_BRANCH = 3
FFN_HIDDEN = 4 * D_MODEL
RMS_EPS = 1e-6
NEG_INF = -1e30
Q_SCALE = HEAD_DIM ** -0.5
LOG2_E = 1.4426950408889634

_OFF_F = 6 * ATT_WIDTH
_OFF_Z = _OFF_F + N_HEADS
_OFF_XBC = _OFF_Z + SSM_INNER
_OFF_DT = _OFF_XBC + SSM_CONV_DIM
_OFF_GATE = _OFF_DT + SSM_HEADS
MAIN_WIDTH = 6 * ATT_WIDTH + SSM_INNER + SSM_CONV_DIM + N_BRANCH * D_MODEL
SMALL_WIDTH = 128
DT_LANE = N_HEADS

V7X_VMEM_LIMIT_BYTES = 58 * 1024 * 1024
PAGES_PER_STEP = 16
ATT_BLOCK = 256
PROMPT_SEQS_PER_STEP = 2
SB_SPLIT_TERMS = 1
CUMSUM_SEG = 256


def _cparams(*sem):
    return pltpu.CompilerParams(dimension_semantics=sem, vmem_limit_bytes=V7X_VMEM_LIMIT_BYTES)


def _const_spec(shape):
    n = len(shape)
    return pl.BlockSpec(shape, lambda *_: (0,) * n, pipeline_mode=pl.Buffered(1))


def _rms(x, g):
    ms = jnp.mean(x * x, axis=-1, keepdims=True)
    return x * lax.rsqrt(ms + RMS_EPS) * g


def _softplus_tail(z):
    return jnp.log1p(jnp.exp(-jnp.abs(z)))


def _log_sigmoid(z):
    return jnp.minimum(z, 0.0) - _softplus_tail(z)


def _softplus(z):
    return jnp.maximum(z, 0.0) + _softplus_tail(z)


def _dot_nt(a, b):
    return lax.dot_general(a, b, (((1,), (1,)), ((), ())), preferred_element_type=F32)


def _dot(a, b):
    return jnp.dot(a, b, preferred_element_type=F32)


def _dot_exact(a, b):
    return jnp.dot(a, b, preferred_element_type=F32, precision=lax.Precision.HIGHEST)


def _store_kv(ref, a, transposed):
    if not transposed:
        ref[...] = a
    elif len(ref.shape) == 2:
        ref[...] = a.T
    else:
        ref[0] = a.T
        ref[1:] = jnp.zeros((ref.shape[0] - 1,) + ref.shape[1:], ref.dtype)


def _inproj_kernel(x_ref, g_ref, wm_ref, ws_ref, *rest, kv_transposed, n_aliased):
    (qsb_ref, qfx_ref, ksb_ref, vsb_ref, kfx_ref, vfx_ref, ksbh_ref, vsbh_ref, kfxh_ref, vfxh_ref,
     z_ref, xbc_ref, small_ref, gate_ref) = rest[n_aliased:]
    h = _rms(x_ref[...], g_ref[...]).astype(BF16)

    def mm(c0, width):
        return _dot_nt(h, wm_ref[c0:c0 + width, :])

    w = ATT_WIDTH
    qsb_ref[...] = (mm(0, w) * (Q_SCALE * LOG2_E)).astype(BF16)
    for i, (full_ref, half_ref) in enumerate(((ksb_ref, ksbh_ref), (vsb_ref, vsbh_ref))):
        a = mm((1 + i) * w, w)
        _store_kv(full_ref, a, kv_transposed)
        half_ref[...] = a.astype(BF16)
    qfx_ref[...] = (mm(3 * w, w) * (Q_SCALE * LOG2_E)).astype(BF16)
    for i, (full_ref, half_ref) in enumerate(((kfx_ref, kfxh_ref), (vfx_ref, vfxh_ref))):
        a = mm((4 + i) * w, w)
        _store_kv(full_ref, a, kv_transposed)
        half_ref[...] = a.astype(BF16)
    z_ref[...] = mm(6 * w, SSM_INNER)
    c0 = 6 * w + SSM_INNER
    for c in range(SSM_CONV_DIM // 512):
        xbc_ref[:, c * 512:(c + 1) * 512] = mm(c0 + c * 512, 512)
    c0 += SSM_CONV_DIM
    for c in range(N_BRANCH * D_MODEL // 512):
        gate_ref[:, c * 512:(c + 1) * 512] = jax.nn.sigmoid(mm(c0 + c * 512, 512)).astype(BF16)
    small_ref[...] = _dot_nt(h, ws_ref[...])


_HBM_SPEC = pl.BlockSpec(memory_space=pl.ANY)


def _inproj(x2d, g, w_main, w_small, kv_stack=None):
    n = x2d.shape[0]
    tm = min(512, n)
    row = lambda width: pl.BlockSpec((tm, width), lambda i: (i, 0))
    rows = lambda width, dt: (row(width), jax.ShapeDtypeStruct((n, width), dt))
    prev = ()
    if kv_stack is None:
        kv = rows(ATT_WIDTH, F32)
    else:
        bsz, t, layer, depth, prev = kv_stack
        prev = () if prev is None else tuple(prev)
        nt = t // tm
        if prev or depth == 1:
            block = pl.BlockSpec((None, None, ATT_WIDTH, tm), lambda i: (layer, i // nt, 0, i % nt))
        else:
            block = pl.BlockSpec((depth, None, ATT_WIDTH, tm), lambda i: (0, i // nt, 0, i % nt))
        kv = (block, jax.ShapeDtypeStruct((depth, bsz, ATT_WIDTH, t), F32))
    outs = ([rows(ATT_WIDTH, BF16)] * 2 + [kv] * 4 + [rows(ATT_WIDTH, BF16)] * 4
            + [rows(SSM_INNER, F32), rows(SSM_CONV_DIM, F32), rows(SMALL_WIDTH, F32),
               rows(N_BRANCH * D_MODEL, BF16)])
    n_in = 4
    return pl.pallas_call(
        functools.partial(_inproj_kernel, kv_transposed=kv_stack is not None, n_aliased=len(prev)),
        grid=(n // tm,),
        in_specs=[row(D_MODEL), _const_spec((1, D_MODEL)),
                  _const_spec((MAIN_WIDTH, D_MODEL)), _const_spec((SMALL_WIDTH, D_MODEL))]
                 + [_HBM_SPEC] * len(prev),
        out_specs=[spec for spec, _ in outs],
        out_shape=[shape for _, shape in outs],
        input_output_aliases={n_in + j: 2 + j for j in range(len(prev))},
        compiler_params=_cparams("parallel"),
        name="inproj",
    )(x2d, g, w_main, w_small, *prev)


def _lane_head(width=ATT_WIDTH):
    return lax.broadcasted_iota(jnp.int32, (1, width), 1) // HEAD_DIM


def _suffix_matrix(seg, terms):
    r = lax.broadcasted_iota(jnp.int32, (terms * seg, seg), 0) % seg
    c = lax.broadcasted_iota(jnp.int32, (terms * seg, seg), 1)
    return jnp.where(r >= c, 1.0, 0.0).astype(BF16)


def _split_bf16(x, terms):
    out = []
    for _ in range(terms - 1):
        head = x.astype(BF16)
        out.append(head)
        x = x - head.astype(F32)
    out.append(x.astype(BF16))
    return jnp.concatenate(out, axis=1)


def _suffix_sums(x, u, carry):
    m, n = x.shape
    seg = u.shape[1]
    n_seg = n // seg
    terms = u.shape[0] // seg
    if n_seg == 1:
        cs = _dot(_split_bf16(x, terms), u)
        return cs + carry, carry + cs[:, 0:1]
    stacked = jnp.concatenate([x[:, s * seg:(s + 1) * seg] for s in range(n_seg)], axis=0)
    cs = _dot(_split_bf16(stacked, terms), u)
    parts = [None] * n_seg
    for s in reversed(range(n_seg)):
        part = cs[s * m:(s + 1) * m, :]
        parts[s] = part + carry
        carry = carry + part[:, 0:1]
    return jnp.concatenate(parts, axis=1), carry


def _stick_block(z2, carry, u, mask):
    drop = jnp.maximum(z2, 0.0) + jnp.log2(1.0 + jnp.exp2(-jnp.abs(z2)))
    if mask is not None:
        drop = jnp.where(mask, drop, 0.0)
    later, carry = _suffix_sums(drop, u, carry)
    w = jnp.exp2(z2 - later)
    if mask is not None:
        w = jnp.where(mask, w, 0.0)
    return w, carry


def _stack_heads(q):
    lane_head = _lane_head()
    return jnp.concatenate([jnp.where(lane_head == h, q, jnp.zeros_like(q)) for h in range(N_HEADS)], axis=0)


def _fold_heads(acc, rows):
    lane_head = _lane_head()
    out = jnp.zeros((rows, ATT_WIDTH), F32)
    for h in range(N_HEADS):
        out = jnp.where(lane_head == h, acc[h * rows:(h + 1) * rows, :], out)
    return out


def _sb_prompt_kernel(q_ref, k_ref, v_ref, o_ref, acc_ref, *, bq, bk):
    assert bq == bk
    i = pl.program_id(1)
    n_seq = q_ref.shape[0]
    rows = N_HEADS * bq
    qs = [_stack_heads(q_ref[s]) for s in range(n_seq)]
    u = _suffix_matrix(CUMSUM_SEG, SB_SPLIT_TERMS)
    qpos = lax.broadcasted_iota(jnp.int32, (rows, bk), 0) % bq
    causal = lax.broadcasted_iota(jnp.int32, (rows, bk), 1) < qpos

    def block(start, width, carries, mask):
        start = pl.multiple_of(start, width)
        out = []
        for s in range(n_seq):
            w, carry = _stick_block(_dot_nt(qs[s], k_ref[s, pl.ds(start, width), :]), carries[s], u, mask)
            pv = _dot(w.astype(BF16), v_ref[s, pl.ds(start, width), :])
            acc_ref[s] = pv if mask is not None else acc_ref[s] + pv
            out.append(carry)
        return tuple(out)

    carries = block(i * bk, bk, (jnp.zeros((rows, 1), F32),) * n_seq, causal)
    carries = lax.fori_loop(0, i % 2, lambda _, c: block((i - 1) * bk, bk, c, None), carries)
    pairs = i // 2
    lax.fori_loop(0, pairs, lambda jj, c: block((pairs - 1 - jj) * 2 * bk, 2 * bk, c, None), carries)
    for s in range(n_seq):
        o_ref[s] = _fold_heads(acc_ref[s], bq).astype(o_ref.dtype)


def _logf_cum_kernel(b_ref, f_ref, logf_ref, cum_ref, *, rows_per_seq):
    logf = _log_sigmoid(f_ref[...] + b_ref[...])
    logf_ref[...] = logf
    n = logf.shape[0]
    r = lax.broadcasted_iota(jnp.int32, (PAGE_SIZE, PAGE_SIZE), 0)
    c = lax.broadcasted_iota(jnp.int32, (PAGE_SIZE, PAGE_SIZE), 1)
    within = _dot_exact(logf, jnp.where(r <= c, 1.0, 0.0).astype(F32))
    totals = jnp.broadcast_to(within[:, PAGE_SIZE - 1:PAGE_SIZE], within.shape)
    rr = lax.broadcasted_iota(jnp.int32, (n, n), 0)
    cc = lax.broadcasted_iota(jnp.int32, (n, n), 1)
    earlier_rows = jnp.where(cc // rows_per_seq == rr // rows_per_seq, jnp.where(cc < rr, 1.0, 0.0), 0.0)
    cum_ref[...] = within + _dot_exact(earlier_rows, totals)


def _logf_cum(f_rows, b_rows, rows_per_seq):
    spec = pl.BlockSpec(f_rows.shape, lambda i: (0, 0))
    return pl.pallas_call(
        functools.partial(_logf_cum_kernel, rows_per_seq=rows_per_seq),
        grid=(1,),
        in_specs=[pl.BlockSpec(b_rows.shape, lambda i: (0, 0)), spec],
        out_specs=[spec, spec],
        out_shape=[jax.ShapeDtypeStruct(f_rows.shape, F32)] * 2,
        compiler_params=_cparams("arbitrary"),
        name="logf_cum",
    )(b_rows, f_rows)


def _fox_prompt_kernel(q_ref, k_ref, v_ref, cq_ref, ck_ref, o_ref, acc_ref, *, bq, bk):
    assert bq == bk
    blk = bq
    i = pl.program_id(1)
    n_seq = q_ref.shape[0]
    rows = N_HEADS * blk
    qs = [_stack_heads(q_ref[s]) for s in range(n_seq)]
    cqs = [jnp.concatenate([cq_ref[s, :, h:h + 1] for h in range(N_HEADS)], axis=0) for s in range(n_seq)]
    qpos = lax.broadcasted_iota(jnp.int32, (rows, blk), 0) % blk
    causal = lax.broadcasted_iota(jnp.int32, (rows, blk), 1) <= qpos

    def block(j, states, mask):
        start = pl.multiple_of(j * blk, blk)
        out = []
        for s in range(n_seq):
            sc = _dot_nt(qs[s], k_ref[s, pl.ds(start, blk), :]) + cqs[s]
            ck = ck_ref[s, j]
            sc = jnp.concatenate([sc[h * blk:(h + 1) * blk, :] - ck[h:h + 1, :] for h in range(N_HEADS)], axis=0)
            if mask is not None:
                sc = jnp.where(mask, sc, NEG_INF)
            m = jnp.max(sc, axis=-1, keepdims=True)
            if states is not None:
                m_old, l_old = states[s]
                m = jnp.maximum(m_old, m)
            p = jnp.exp2(sc - m)
            pv = _dot(p.astype(BF16), v_ref[s, pl.ds(start, blk), :])
            l = jnp.sum(p, axis=-1, keepdims=True)
            if states is None:
                acc_ref[s] = pv
            else:
                alpha = jnp.exp2(m_old - m)
                acc_ref[s] = acc_ref[s] * alpha + pv
                l = alpha * l_old + l
            out.append((m, l))
        return tuple(out)

    states = block(i, None, causal)
    states = lax.fori_loop(0, i, lambda jj, st: block(i - 1 - jj, st, None), states)
    for s in range(n_seq):
        o_ref[s] = _fold_heads(acc_ref[s] / states[s][1], blk).astype(o_ref.dtype)


_XBUF_ROW0 = 8


def _ssm_kernel(z_ref, xbc_ref, small_ref, conv0_ref, state0_ref,
                cw_ref, cb_ref, dtb_ref, alog_ref, dskip_ref, ng_ref, *rest, valid, n_aliased):
    y_ref, state_out_ref, conv_out_ref, xbuf_ref, state_ref = rest[n_aliased:]
    c = pl.program_id(1)
    n_seq = xbc_ref.shape[0]
    r0 = _XBUF_ROW0

    @pl.when(c == 0)
    def _():
        xbuf_ref[...] = jnp.zeros_like(xbuf_ref)
        xbuf_ref[:, r0 - 3:r0, :] = conv0_ref[...]
        state_ref[...] = state0_ref[...]

    x_curs = [_ssm_chunk(z_ref.at[s], xbc_ref.at[s], small_ref.at[s], cw_ref, cb_ref, dtb_ref, alog_ref,
                         dskip_ref, ng_ref, y_ref.at[s], xbuf_ref.at[s], state_ref.at[s], valid)
              for s in range(n_seq)]

    @pl.when(c == pl.num_programs(1) - 1)
    def _():
        if len(state_out_ref.shape) == len(state_ref.shape):
            state_out_ref[...] = state_ref[...]
        else:
            state_out_ref[0] = state_ref[...]
            state_out_ref[1:] = jnp.zeros((state_out_ref.shape[0] - 1,) + state_ref.shape, state_ref.dtype)
        for s in range(n_seq):
            conv_out_ref[s] = x_curs[s][valid - 3:valid, :]


def _ssm_chunk(z_ref, xbc_ref, small_ref, cw_ref, cb_ref, dtb_ref, alog_ref, dskip_ref, ng_ref,
               y_ref, xbuf_ref, state_ref, valid):
    L = xbc_ref.shape[0]
    S = SSM_CHUNK
    r0 = _XBUF_ROW0

    def pad_time(a):
        return a if L == S else jnp.concatenate([a, jnp.zeros((S - L,) + a.shape[1:], a.dtype)], axis=0)

    x_cur = xbc_ref[...]
    window = jnp.concatenate([xbuf_ref[...], x_cur], axis=0)
    cw = cw_ref[...]
    conv = cb_ref[...] + x_cur * cw[3:4, :]
    for i in range(SSM_CONV - 1):
        conv = conv + pltpu.roll(window, 3 - i, 0)[r0:, :] * cw[i:i + 1, :]
    xbuf_ref[...] = x_cur[L - r0:L, :]
    act = conv * jax.nn.sigmoid(conv)
    xs = act[:, :SSM_INNER]
    b_in = act[:, SSM_INNER:SSM_INNER + SSM_GROUPS * SSM_STATE]
    c_in = act[:, SSM_INNER + SSM_GROUPS * SSM_STATE:]

    row = lax.broadcasted_iota(jnp.int32, (L, S), 0)
    col = lax.broadcasted_iota(jnp.int32, (L, S), 1)
    tri = row >= col
    dt = _softplus(small_ref[...] + dtb_ref[...])
    if valid < L:
        dt = jnp.where(lax.broadcasted_iota(jnp.int32, dt.shape, 0) < valid, dt, 0.0)
    d_a = dt * (-jnp.exp(alog_ref[...]))
    a_cs = _dot_exact(jnp.where(tri[:, :L], 1.0, 0.0).astype(F32), d_a)
    a_cs_t = pad_time(a_cs).T
    a_last = a_cs[L - 1:L, :]
    e_cs = jnp.exp(a_cs)
    wgt = jnp.exp(a_last - a_cs) * dt
    chunk_dec = jnp.exp(a_last)

    half = lax.broadcasted_iota(jnp.int32, (1, 128), 1) // SSM_STATE_HALF
    rhalf = lax.broadcasted_iota(jnp.int32, (128, 1), 0) // SSM_STATE_HALF
    pair_cols = lambda a, p: jnp.where(half == 0, a[:, DT_LANE + 2 * p:DT_LANE + 2 * p + 1],
                                       a[:, DT_LANE + 2 * p + 1:DT_LANE + 2 * p + 2])
    b_keys = pad_time(b_in)
    scores = [_dot_nt(c_in[:, g * SSM_STATE:(g + 1) * SSM_STATE],
                      b_keys[:, g * SSM_STATE:(g + 1) * SSM_STATE]) for g in range(SSM_GROUPS)]
    ys = []
    for p in range(SSM_HEADS // 2):
        g = (2 * p) // (SSM_HEADS // SSM_GROUPS)
        bg = b_keys[:, g * SSM_STATE:(g + 1) * SSM_STATE]
        cg = c_in[:, g * SSM_STATE:(g + 1) * SSM_STATE]
        xs_p = xs[:, 128 * p:128 * (p + 1)]
        xdt = pad_time(xs_p * pair_cols(dt, p))
        y_diag = jnp.zeros((L, 128), F32)
        for hh in range(2):
            lane = DT_LANE + 2 * p + hh
            seg = a_cs[:, lane:lane + 1] - a_cs_t[lane:lane + 1, :]
            decay = jnp.exp(jnp.where(tri, seg, NEG_INF))
            y_h = _dot(scores[g] * decay, xdt)
            y_diag = jnp.where(half == hh, y_h, y_diag)
        st = state_ref[128 * p:128 * (p + 1), :]
        y_off = _dot_nt(cg, st) * pair_cols(e_cs, p)
        ys.append(y_diag + y_off + dskip_ref[:, 128 * p:128 * (p + 1)] * xs_p)
        xw_t = pad_time(xs_p * pair_cols(wgt, p)).T
        lane = DT_LANE + 2 * p
        dec = jnp.where(rhalf == 0,
                        jnp.broadcast_to(chunk_dec[:, lane:lane + 1], (128, SSM_STATE)),
                        jnp.broadcast_to(chunk_dec[:, lane + 1:lane + 2], (128, SSM_STATE)))
        state_ref[128 * p:128 * (p + 1), :] = st * dec + _dot(xw_t, bg)

    z = z_ref[...]
    y = jnp.concatenate(ys, axis=1) * (z * jax.nn.sigmoid(z))
    gw = SSM_INNER // SSM_GROUPS
    parts = []
    for g in range(SSM_GROUPS):
        yg = y[:, g * gw:(g + 1) * gw]
        parts.append(yg * lax.rsqrt(jnp.mean(yg * yg, axis=-1, keepdims=True) + RMS_EPS))
    y_ref[...] = (jnp.concatenate(parts, axis=1) * ng_ref[...]).astype(y_ref.dtype)
    return x_cur


SSM_STATE_HALF = 64
SSM_SHORT_ROWS = 16
SSM_SEQS_PER_STEP = 4


def _ssm(z, xbc, small, conv0, state0, cw, cb, dtb, alog, dskip, ng, valid, layer, depth, prev):
    b, t, _ = z.shape
    rows = min(t, SSM_CHUNK)
    nc = t // rows
    par = SSM_SEQS_PER_STEP if b % SSM_SEQS_PER_STEP == 0 else 1
    chunk = lambda width: pl.BlockSpec((par, rows, width), lambda bi, c: (bi, c, 0))
    per_b = lambda shape: pl.BlockSpec((par,) + shape, lambda bi, c: (bi,) + (0,) * len(shape))
    const = lambda shape: pl.BlockSpec(shape, lambda bi, c: (0,) * len(shape))
    state_rows = SSM_INNER
    prev = () if prev is None else (prev,)
    if prev or depth == 1:
        state_out = pl.BlockSpec((None, par, state_rows, SSM_STATE), lambda bi, c: (layer, bi, 0, 0))
    else:
        state_out = pl.BlockSpec((depth, par, state_rows, SSM_STATE), lambda bi, c: (0, bi, 0, 0))
    n_in = 11
    return pl.pallas_call(
        functools.partial(_ssm_kernel, valid=valid, n_aliased=len(prev)),
        grid=(b // par, nc),
        in_specs=[chunk(SSM_INNER), chunk(SSM_CONV_DIM), chunk(SMALL_WIDTH),
                  per_b((SSM_CONV - 1, SSM_CONV_DIM)), per_b((state_rows, SSM_STATE)),
                  const((SSM_CONV, SSM_CONV_DIM)), const((1, SSM_CONV_DIM)),
                  const((1, SMALL_WIDTH)), const((1, SMALL_WIDTH)),
                  const((1, SSM_INNER)), const((1, SSM_INNER))] + [_HBM_SPEC] * len(prev),
        out_specs=[chunk(SSM_INNER), state_out, per_b((SSM_CONV - 1, SSM_CONV_DIM))],
        out_shape=[jax.ShapeDtypeStruct((b, t, SSM_INNER), BF16),
                   jax.ShapeDtypeStruct((depth, b, state_rows, SSM_STATE), F32),
                   jax.ShapeDtypeStruct((b, SSM_CONV - 1, SSM_CONV_DIM), F32)],
        input_output_aliases={n_in + j: 1 + j for j in range(len(prev))},
        scratch_shapes=[pltpu.VMEM((par, _XBUF_ROW0, SSM_CONV_DIM), F32),
                        pltpu.VMEM((par, state_rows, SSM_STATE), F32)],
        compiler_params=_cparams("parallel", "arbitrary"),
        name="ssm",
    )(z, xbc, small, conv0, state0, cw, cb, dtb, alog, dskip, ng, *prev)


def _merge_ffn_kernel(x_ref, ysb_ref, yssm_ref, yfx_ref, gate_ref,
                      wsb_ref, wssm_ref, wfx_ref, wo_ref, g2_ref, wup_ref, wdn_ref, gf_ref,
                      o_ref, *, final):
    d = D_MODEL
    mixed = gate_ref[:, 0:d].astype(F32) * _dot(ysb_ref[...].astype(BF16), wsb_ref[...])
    mixed = mixed + gate_ref[:, d:2 * d].astype(F32) * _dot(yssm_ref[...].astype(BF16), wssm_ref[...])
    mixed = mixed + gate_ref[:, 2 * d:3 * d].astype(F32) * _dot(yfx_ref[...].astype(BF16), wfx_ref[...])
    x = x_ref[...] + _dot(mixed.astype(BF16), wo_ref[...])
    h = _rms(x, g2_ref[...]).astype(BF16)
    hc = FFN_HIDDEN // 2
    for c in range(2):
        u = jnp.maximum(_dot(h, wup_ref[:, c * hc:(c + 1) * hc]), 0.0)
        x = x + _dot((u * u).astype(BF16), wdn_ref[c * hc:(c + 1) * hc, :])
    if final:
        x = _rms(x, gf_ref[...])
    o_ref[...] = x


def _merge_ffn(x2d, y_sb, y_ssm, y_fx, gates, lw, gf, final):
    n = x2d.shape[0]
    tm = min(512, n)
    row = lambda width: pl.BlockSpec((tm, width), lambda i: (i, 0))
    weights = [lw['w_sb_out'], lw['w_ssm_out'], lw['w_fox_out'], lw['w_o'], lw['norm2_g'],
               lw['w_up'], lw['w_down'], gf]
    return pl.pallas_call(
        functools.partial(_merge_ffn_kernel, final=final),
        grid=(n // tm,),
        in_specs=[row(D_MODEL), row(ATT_WIDTH), row(SSM_INNER), row(ATT_WIDTH),
                  row(N_BRANCH * D_MODEL)] + [_const_spec(w.shape) for w in weights],
        out_specs=row(D_MODEL),
        out_shape=jax.ShapeDtypeStruct((n, D_MODEL), F32),
        compiler_params=_cparams("parallel"),
        name="merge_ffn",
    )(x2d, y_sb, y_ssm, y_fx, gates, *weights)


def _block_diag_q(q_ref):
    q = q_ref[...]
    rows = q.shape[0]
    row_head = lax.broadcasted_iota(jnp.int32, (rows, 1), 0) // (rows // N_HEADS)
    return jnp.where(row_head == _lane_head(), q, jnp.zeros_like(q))


def _page_copy(cache_ref, stage_ref, sem_ref, layer, page, slot, p):
    return pltpu.make_async_copy(cache_ref.at[layer, page], stage_ref.at[slot, p], sem_ref.at[slot])


def _grid_step():
    return (pl.program_id(0) * pl.num_programs(1) + pl.program_id(1),
            pl.num_programs(0) * pl.num_programs(1))


def _prefetch_pages(pt_ref, caches, stages, sems, layer, group, n_chunks):
    n_pages = stages[0].shape[1]
    per_seq = n_pages // group
    step, n_steps = _grid_step()
    b, c = step // n_chunks, step % n_chunks
    slot = step % 2

    def start(bb, cc, sl):
        base = (n_chunks - 1 - cc) * per_seq
        for p in range(n_pages):
            page = pt_ref[bb * group + p // per_seq, base + p % per_seq]
            for i, (cache, stage, sem) in enumerate(zip(caches, stages, sems)):
                _page_copy(cache, stage, sem, layer, page, sl, p).start(priority=i % 2)

    @pl.when(step == 0)
    def _():
        start(b, c, slot)

    @pl.when(step + 1 < n_steps)
    def _():
        wrap = c + 1 == n_chunks
        start(jnp.where(wrap, b + 1, b), jnp.where(wrap, 0, c + 1), 1 - slot)

    for p in range(n_pages):
        for cache, stage, sem in zip(caches, stages, sems):
            _page_copy(cache, stage, sem, layer, 0, slot, p).wait()
    return slot


def _seq_pages(stage_ref, slot, g, group):
    per_seq = stage_ref.shape[1] // group
    return jnp.concatenate([stage_ref[slot, g * per_seq + p] for p in range(per_seq)], axis=1)


def _sb_sample_kernel(pt_ref, q_ref, kn_ref, vn_ref, ck_ref, cv_ref, o_ref,
                      acc_ref, carry_ref, kstage_ref, vstage_ref, ksem, vsem,
                      *, steps, layer, n_chunks):
    group = q_ref.shape[0]
    slot = _prefetch_pages(pt_ref, (ck_ref, cv_ref), (kstage_ref, vstage_ref), (ksem, vsem), layer, group,
                           n_chunks)
    c = _grid_step()[0] % n_chunks
    rows = N_HEADS * steps
    qbd = [_block_diag_q(q_ref.at[g]) for g in range(group)]
    u = _suffix_matrix(CUMSUM_SEG, SB_SPLIT_TERMS)

    @pl.when(c == 0)
    def _():
        step = lax.broadcasted_iota(jnp.int32, (rows, PAGE_SIZE), 0) % steps
        col = lax.broadcasted_iota(jnp.int32, (rows, PAGE_SIZE), 1)
        for g in range(group):
            w, carry = _stick_block(_dot_nt(qbd[g], kn_ref[g]), jnp.zeros((rows, 1), F32),
                                    _suffix_matrix(PAGE_SIZE, SB_SPLIT_TERMS), col < step)
            acc_ref[g] = _dot(w.astype(BF16), vn_ref[g])
            carry_ref[g] = jnp.broadcast_to(carry, carry_ref.shape[1:])

    for g in range(group):
        scores = _dot(qbd[g].astype(F32), _seq_pages(kstage_ref, slot, g, group))
        w, carry = _stick_block(scores, carry_ref[g, :, 0:1], u, None)
        acc_ref[g] += _dot_nt(w, _seq_pages(vstage_ref, slot, g, group))
        carry_ref[g] = jnp.broadcast_to(carry, carry_ref.shape[1:])

    def finish():
        @pl.when(c == n_chunks - 1)
        def _():
            for g in range(group):
                o_ref[g] = _fold_heads(acc_ref[g], steps)
    return finish


def _page_staging(n_pages, rows, lanes):
    return [pltpu.VMEM((2, n_pages, rows, lanes), F32)], [pltpu.SemaphoreType.DMA((2,))]


def _fox_sample_kernel(pt_ref, q_ref, kn_ref, vn_ref, f_ref, b_ref, ck_ref, cv_ref, clf_ref,
                       o_ref, lf_out_ref, acc_ref, m_ref, l_ref, ncum_ref, rcarry_ref,
                       kstage_ref, vstage_ref, lstage_ref, ksem, vsem, lsem,
                       lbuf_ref, *, steps, layer, n_chunks):
    group = q_ref.shape[0]
    slot = _prefetch_pages(pt_ref, (ck_ref, cv_ref, clf_ref), (kstage_ref, vstage_ref, lstage_ref),
                           (ksem, vsem, lsem), layer, group, n_chunks)
    c = _grid_step()[0] % n_chunks
    rows = N_HEADS * steps
    qbd = [_block_diag_q(q_ref.at[g]) for g in range(group)]
    stat = lambda a: jnp.broadcast_to(a, m_ref.shape[1:])

    @pl.when(c == 0)
    def _():
        step = lax.broadcasted_iota(jnp.int32, (rows, PAGE_SIZE), 0) % steps
        col = lax.broadcasted_iota(jnp.int32, (rows, PAGE_SIZE), 1)
        visible = col <= step
        lbuf_ref[...] = jnp.zeros_like(lbuf_ref)
        rcarry_ref[...] = jnp.zeros_like(rcarry_ref)
        for g in range(group):
            lf = _log_sigmoid(f_ref[g] + b_ref[...])
            lf_out_ref[g] = lf
            lf = jnp.where(col < steps, lf, 0.0)
            cum = lf
            shift = 1
            while shift < steps:
                cum = cum + jnp.where(col >= shift, pltpu.roll(cum, shift, 1), 0.0)
                shift *= 2
            ncum = jnp.sum(jnp.where(visible, lf, 0.0), axis=-1, keepdims=True)
            s = jnp.where(visible, _dot_nt(qbd[g], kn_ref[g]) + (ncum - cum) * LOG2_E, NEG_INF)
            m = jnp.max(s, axis=-1, keepdims=True)
            p = jnp.exp2(s - m)
            m_ref[g] = stat(m)
            l_ref[g] = stat(jnp.sum(p, axis=-1, keepdims=True))
            acc_ref[g] = _dot(p.astype(BF16), vn_ref[g])
            ncum_ref[g] = stat(ncum)

    per_seq = lbuf_ref.shape[-1] // PAGE_SIZE
    u3 = _suffix_matrix(CUMSUM_SEG, 3)
    for g in range(group):
        for p in range(per_seq):
            lbuf_ref[g, 0:N_HEADS, p * PAGE_SIZE:(p + 1) * PAGE_SIZE] = lstage_ref[slot, g * per_seq + p]
        lf_pages = lbuf_ref[g]
        incl, carry = _suffix_sums(lf_pages, u3, rcarry_ref[g, :, 0:1])
        suffix = incl - lf_pages
        rcarry_ref[g] = jnp.broadcast_to(carry, rcarry_ref.shape[1:])
        bias = jnp.concatenate(
            [jnp.broadcast_to(suffix[h:h + 1, :], (steps, suffix.shape[1])) for h in range(N_HEADS)], axis=0)
        s = (_dot(qbd[g].astype(F32), _seq_pages(kstage_ref, slot, g, group))
             + (bias + ncum_ref[g, :, 0:1]) * LOG2_E)
        m_old = m_ref[g, :, 0:1]
        m = jnp.maximum(m_old, jnp.max(s, axis=-1, keepdims=True))
        alpha = jnp.exp2(m_old - m)
        p = jnp.exp2(s - m)
        l_ref[g] = stat(alpha * l_ref[g, :, 0:1] + jnp.sum(p, axis=-1, keepdims=True))
        acc_ref[g] = acc_ref[g] * alpha + _dot_nt(p, _seq_pages(vstage_ref, slot, g, group))
        m_ref[g] = stat(m)

    def finish():
        @pl.when(c == n_chunks - 1)
        def _():
            for g in range(group):
                o_ref[g] = _fold_heads(acc_ref[g] / l_ref[g, :, 0:1], steps)
    return finish


def _both_groups_kernel(pt_ref, *refs, prompt_kernel, sample_kernel, counts):
    parts, at = [], 0
    for n in counts:
        parts.append(refs[at:at + n])
        at += n
    p_in, s_in, p_out, s_out, p_scr, s_scr = parts
    finish_sample = sample_kernel(pt_ref, *s_in, *s_out, *s_scr)
    prompt_kernel(*p_in, *p_out, *p_scr)
    finish_sample()


def _attention_both_groups(name, layer, page_table, prompt_kernel, prompt_in, prompt_extra_specs,
                           sample_kernel, sample_in, sample_extra_specs, caches, sample_extra_out,
                           sample_extra_scratch, steps):
    q, k, v = prompt_in[:3]
    b, t, w = q.shape
    blk = ATT_BLOCK
    nq = t // blk
    pb = PROMPT_SEQS_PER_STEP if b % PROMPT_SEQS_PER_STEP == 0 else 1
    q_rows = sample_in[0]
    n_seq, rows, _ = q_rows.shape
    n_pages = PAGES_PER_STEP
    n_chunks = page_table.shape[1] // n_pages
    steps_total = (b // pb) * nq
    g = n_seq * n_chunks // steps_total
    assert g >= 1 and g * steps_total == n_seq * n_chunks, "the two groups must split into equally many grid steps"
    seq_group = lambda bi, i: (bi * nq + i) // n_chunks

    qspec = pl.BlockSpec((pb, blk, w), lambda bi, i, pt: (bi, i, 0))
    kvspec = pl.BlockSpec((pb, t, w), lambda bi, i, pt: (bi, 0, 0), pipeline_mode=pl.Buffered(1))
    per_g = lambda r, width=w: pl.BlockSpec((g, r, width), lambda bi, i, pt: (seq_group(bi, i), 0, 0))
    prompt_specs = [qspec, kvspec, kvspec] + prompt_extra_specs(pb, t, blk)
    sample_specs = ([per_g(rows), per_g(PAGE_SIZE), per_g(PAGE_SIZE)] + sample_extra_specs(per_g, rows)
                    + [_HBM_SPEC] * len(caches))
    stat = pltpu.VMEM((g, rows, 128), F32)
    stages, sems = [], []
    for cache in caches:
        st, se = _page_staging(g * n_pages, cache.shape[2], PAGE_SIZE)
        stages += st
        sems += se
    prompt_scratch = [pltpu.VMEM((pb, N_HEADS * blk, w), F32)]
    sample_scratch = [pltpu.VMEM((g, rows, w), F32)] + sample_extra_scratch(g, stat) + stages + sems
    if len(caches) == 3:
        sample_scratch.append(pltpu.VMEM((g, 8, n_pages * PAGE_SIZE), F32))
    out_specs = [qspec, per_g(steps)] + [per_g(rows, PAGE_SIZE)] * len(sample_extra_out)
    out_shape = ([jax.ShapeDtypeStruct((b, t, w), BF16), jax.ShapeDtypeStruct((n_seq, steps, w), F32)]
                 + list(sample_extra_out))
    counts = (len(prompt_specs), len(sample_specs), 1, len(out_specs) - 1,
              len(prompt_scratch), len(sample_scratch))
    grid_spec = pltpu.PrefetchScalarGridSpec(
        num_scalar_prefetch=1,
        grid=(b // pb, nq),
        in_specs=prompt_specs + sample_specs,
        out_specs=out_specs,
        scratch_shapes=prompt_scratch + sample_scratch,
    )
    return pl.pallas_call(
        functools.partial(_both_groups_kernel, counts=counts,
                          prompt_kernel=functools.partial(prompt_kernel, bq=blk, bk=blk),
                          sample_kernel=functools.partial(sample_kernel, steps=steps, layer=layer,
                                                          n_chunks=n_chunks)),
        grid_spec=grid_spec,
        out_shape=out_shape,
        compiler_params=_cparams("arbitrary", "arbitrary"),
        name=name,
    )(page_table, *prompt_in, *sample_in, *caches)


def _sb_attention(layer, page_table, prompt_in, sample_in, caches, steps):
    none = lambda *_: []
    return _attention_both_groups(
        "sb_attention", layer, page_table, _sb_prompt_kernel, prompt_in, none,
        _sb_sample_kernel, sample_in, none, caches, [], lambda g, stat: [stat], steps)


def _fox_attention(layer, page_table, prompt_in, sample_in, caches, steps):
    n_seq, rows, _ = sample_in[0].shape
    prompt_extra = lambda pb, t, blk: [
        pl.BlockSpec((pb, blk, N_HEADS), lambda bi, i, pt: (bi, i, 0)),
        pl.BlockSpec((pb, t // blk, N_HEADS, blk), lambda bi, i, pt: (bi, 0, 0, 0))]
    sample_extra = lambda per_g, rows: [per_g(rows, PAGE_SIZE),
                                        pl.BlockSpec((rows, PAGE_SIZE), lambda bi, i, pt: (0, 0))]
    return _attention_both_groups(
        "fox_attention", layer, page_table, _fox_prompt_kernel, prompt_in, prompt_extra,
        _fox_sample_kernel, sample_in, sample_extra, caches,
        [jax.ShapeDtypeStruct((n_seq, rows, PAGE_SIZE), F32)],
        lambda g, stat: [stat, stat, stat, pltpu.VMEM((g, 8, 128), F32)], steps)


def _layer_weights(l, norm1_g, w_in, b_forget, conv_w, conv_b, dt_bias, a_log, d_skip, ssm_norm_g,
                   w_sb_out, w_ssm_out, w_fox_out, w_o, norm2_g, w_up, w_down):
    w = jnp.transpose(w_in, (2, 0, 1))[:, l, :]
    w_main = jnp.concatenate([w[:_OFF_F], w[_OFF_Z:_OFF_DT], w[_OFF_GATE:]], axis=0).astype(BF16)
    w_small = jnp.concatenate([w[_OFF_F:_OFF_Z], w[_OFF_DT:_OFF_GATE]], axis=0)
    w_small = jnp.pad(w_small, ((0, SMALL_WIDTH - w_small.shape[0]), (0, 0))).astype(BF16)
    pad_dt = lambda a: jnp.pad(a, (DT_LANE, SMALL_WIDTH - DT_LANE - SSM_HEADS)).reshape(1, SMALL_WIDTH)
    return {
        'norm1_g': norm1_g[l].reshape(1, D_MODEL), 'w_main': w_main, 'w_small': w_small,
        'b_forget': b_forget[l], 'conv_w': conv_w[l], 'conv_b': conv_b[l].reshape(1, SSM_CONV_DIM),
        'dt_bias': pad_dt(dt_bias[l]), 'a_log': pad_dt(a_log[l]),
        'd_skip': jnp.repeat(d_skip[l], SSM_INNER // SSM_HEADS).reshape(1, SSM_INNER),
        'ssm_norm_g': ssm_norm_g[l].reshape(1, SSM_INNER),
        'w_sb_out': w_sb_out[l].astype(BF16), 'w_ssm_out': w_ssm_out[l].astype(BF16),
        'w_fox_out': w_fox_out[l].astype(BF16), 'w_o': w_o[l].astype(BF16),
        'norm2_g': norm2_g[l].reshape(1, D_MODEL),
        'w_up': w_up[l].astype(BF16), 'w_down': w_down[l].astype(BF16),
    }


def _to_heads(a, bsz, t):
    return a.reshape(bsz, t, N_HEADS, HEAD_DIM)


def _layer(xp, xs, l, depth, lw, gf, stacks, caches, state_ssm, state_conv, page_table):
    kv_prev, ssm_prev, sssm_prev = (None, None, None) if stacks is None else stacks
    final = l == depth - 1
    cache_sb_k, cache_sb_v, cache_fox_k, cache_fox_v, cache_lf_t = caches
    bsz, t, d = xp.shape
    n = bsz * t
    (q_sb, q_fx, k_sb, v_sb, k_fx, v_fx, k_sb_h, v_sb_h, k_fx_h, v_fx_h,
     z, xbc, small, gates) = _inproj(xp.reshape(n, d), lw['norm1_g'], lw['w_main'], lw['w_small'],
                                     (bsz, t, l, depth, kv_prev))
    sn, st, _ = xs.shape
    m = sn * st
    (sq_sb, sq_fx, sk_sb, sv_sb, sk_fx, sv_fx, sk_sb_h, sv_sb_h, sk_fx_h, sv_fx_h,
     sz, sxbc, ssmall, sgates) = _inproj(xs.reshape(m, d), lw['norm1_g'], lw['w_main'], lw['w_small'])

    b3 = lambda a: a.reshape(bsz, t, a.shape[-1])
    s3 = lambda a: a.reshape(sn, st, a.shape[-1])
    q_rows = lambda q: jnp.tile(s3(q), (1, N_HEADS, 1))
    pad_keys = lambda a: jnp.pad(s3(a), ((0, 0), (0, PAGE_SIZE - st), (0, 0)))
    y_sb, sy_sb = _sb_attention(l, page_table, (b3(q_sb), b3(k_sb_h), b3(v_sb_h)),
                                (q_rows(sq_sb), pad_keys(sk_sb_h), pad_keys(sv_sb_h)),
                                (cache_sb_k, cache_sb_v), st)

    f_rows = small[:, :N_HEADS].reshape(bsz, t, N_HEADS).transpose(0, 2, 1)
    rows_per_seq = t // PAGE_SIZE
    f_rows = f_rows.reshape(bsz * N_HEADS * rows_per_seq, PAGE_SIZE)
    bias_rows = jnp.tile(jnp.repeat(lw['b_forget'], rows_per_seq), bsz)[:, None]
    logf, cum = _logf_cum(f_rows, bias_rows, rows_per_seq)
    logf = logf.reshape(bsz, N_HEADS, t).transpose(0, 2, 1)
    blk = ATT_BLOCK
    cum2 = cum * LOG2_E
    cum_col = cum2.reshape(bsz, N_HEADS, t).transpose(0, 2, 1)
    cum_row = cum2.reshape(bsz, N_HEADS, t // blk, blk).transpose(0, 2, 1, 3)
    f_new = ssmall[:, :N_HEADS].reshape(sn, st, N_HEADS).transpose(0, 2, 1)
    sf_rows = jnp.pad(jnp.repeat(f_new, st, axis=1), ((0, 0), (0, 0), (0, PAGE_SIZE - st)))
    b_rows = jnp.broadcast_to(jnp.repeat(lw['b_forget'], st)[:, None], (N_HEADS * st, PAGE_SIZE))
    y_fx, sy_fx, lf_rows = _fox_attention(
        l, page_table, (b3(q_fx), b3(k_fx_h), b3(v_fx_h), cum_col, cum_row),
        (q_rows(sq_fx), pad_keys(sk_fx_h), pad_keys(sv_fx_h), sf_rows, b_rows),
        (cache_fox_k, cache_fox_v, cache_lf_t), st)
    slogf = lf_rows[:, ::st, :st].transpose(0, 2, 1)

    conv0 = jnp.zeros((bsz, SSM_CONV - 1, SSM_CONV_DIM), F32)
    ssm0 = jnp.zeros((bsz, SSM_INNER, SSM_STATE), F32)
    ssm_w = (lw['conv_w'], lw['conv_b'], lw['dt_bias'], lw['a_log'], lw['d_skip'], lw['ssm_norm_g'])
    y_ssm, ssm_stack, conv_new = _ssm(b3(z), b3(xbc), b3(small), conv0, ssm0, *ssm_w, SSM_CHUNK,
                                      l, depth, ssm_prev)
    pad_rows = lambda a: jnp.pad(s3(a), ((0, 0), (0, SSM_SHORT_ROWS - st), (0, 0)))
    sy_ssm, sssm_stack, sconv_new = _ssm(pad_rows(sz), pad_rows(sxbc), pad_rows(ssmall), state_conv[l],
                                         state_ssm[l].reshape(sn, SSM_INNER, SSM_STATE), *ssm_w, st,
                                         l, depth, sssm_prev)
    sy_ssm = sy_ssm[:, :st]

    xp_new = _merge_ffn(xp.reshape(n, d), y_sb.reshape(n, -1), y_ssm.reshape(n, -1), y_fx.reshape(n, -1),
                        gates, lw, gf, final)
    xs_new = _merge_ffn(xs.reshape(m, d), sy_sb.reshape(m, -1), sy_ssm.reshape(m, -1),
                        sy_fx.reshape(m, -1), sgates, lw, gf, final)
    prompt_states = (logf, conv_new)
    sample_states = (_to_heads(sk_sb, sn, st), _to_heads(sv_sb, sn, st), _to_heads(sk_fx, sn, st),
                     _to_heads(sv_fx, sn, st), slogf, sconv_new)
    return (xp_new.reshape(bsz, t, d), xs_new.reshape(sn, st, d),
            ((k_sb, v_sb, k_fx, v_fx), ssm_stack, sssm_stack), prompt_states, sample_states)


def kernel(x_prompt, x_sample, cache_sb_k, cache_sb_v, cache_fox_k, cache_fox_v, cache_fox_logf, state_ssm, state_conv, page_table, norm1_g, w_in, b_forget, conv_w, conv_b, dt_bias, a_log, d_skip, ssm_norm_g, w_sb_out, w_ssm_out, w_fox_out, w_o, norm2_g, w_up, w_down, final_norm_g):
    depth = w_in.shape[0]
    assert page_table.shape[1] % PAGES_PER_STEP == 0
    assert x_sample.shape[1] >= SSM_CONV - 1 and x_prompt.shape[1] % ATT_BLOCK == 0
    flat = lambda c: c.transpose(0, 1, 3, 4, 2).reshape(c.shape[0], c.shape[1], ATT_WIDTH, PAGE_SIZE)
    caches = (flat(cache_sb_k), flat(cache_sb_v), flat(cache_fox_k), flat(cache_fox_v),
              cache_fox_logf.transpose(0, 1, 3, 2))
    gf = final_norm_g.reshape(1, D_MODEL)
    xp, xs = x_prompt, x_sample
    prompt_states, sample_states = [], []
    stacks = None
    for l in range(depth):
        lw = _layer_weights(l, norm1_g, w_in, b_forget, conv_w, conv_b, dt_bias, a_log, d_skip,
                            ssm_norm_g, w_sb_out, w_ssm_out, w_fox_out, w_o, norm2_g, w_up, w_down)
        xp, xs, stacks, st_p, st_s = _layer(xp, xs, l, depth, lw, gf, stacks, caches,
                                            state_ssm, state_conv, page_table)
        prompt_states.append(st_p)
        sample_states.append(st_s)
    bsz, t = x_prompt.shape[:2]
    kv_stacks, ssm_stack, sssm_stack = stacks
    from_t = lambda a: a.reshape(depth, bsz, N_HEADS, HEAD_DIM, t).transpose(0, 1, 4, 2, 3)
    ssm_heads = lambda a: a.reshape(depth, a.shape[1], SSM_HEADS, SSM_INNER // SSM_HEADS, SSM_STATE)
    p_logf, p_conv = [jnp.stack(s) for s in zip(*prompt_states)]
    s_sb_k, s_sb_v, s_fox_k, s_fox_v, s_logf, s_conv = [jnp.stack(s) for s in zip(*sample_states)]
    return (xp, xs, *[from_t(a) for a in kv_stacks], p_logf, ssm_heads(ssm_stack), p_conv,
            s_sb_k, s_sb_v, s_fox_k, s_fox_v, s_logf, ssm_heads(sssm_stack), s_conv)
```
